```python
import jax, jax.numpy as jnp
from jax import lax
import numpy as np

D_MODEL = 2048
BATCH = 1
SEQ = 8192
DEPTH = 2

N_MIXERS = 2
N_FOX_LAYERS = (DEPTH + 1) // 2
N_GLA_LAYERS = DEPTH // 2
FOX_HEADS = 16
FOX_HEAD_DIM = D_MODEL // FOX_HEADS
Q_BLOCK = 128
FOX_BIAS_OFFSET = 4.0
GLA_HEADS = 4
GLA_DK = D_MODEL // 2
GLA_DV = D_MODEL
GLA_DK_HEAD = GLA_DK // GLA_HEADS
GLA_DV_HEAD = GLA_DV // GLA_HEADS
GLA_GATE_RANK = 16
GLA_GATE_TAU = 16.0
GLA_CHUNK = 64
N_GROUPS = 8
EXPERTS_PER_GROUP = 8
N_EXPERTS = N_GROUPS * EXPERTS_PER_GROUP
TOP_K_INNER = 2
EXPERT_FF = D_MODEL // 4
MOE_BLOCK = 128
DEEPNORM_ALPHA = (2 * DEPTH) ** 0.25
DEEPNORM_BETA = (8 * DEPTH) ** -0.25
LN_EPS = 1e-5
RMS_EPS = 1e-6

kernel_name = "hybrid_fox_gla_hier_moe_deepnorm"


def layer_norm(x, g, b):
    xf = x.astype(jnp.float32)
    mu = jnp.mean(xf, axis=-1, keepdims=True)
    var = jnp.mean(jnp.square(xf - mu), axis=-1, keepdims=True)
    return ((xf - mu) * lax.rsqrt(var + LN_EPS) * g + b).astype(x.dtype)


def fox_mixer(x, w_in, b_f, w_o):
    B, S, D = x.shape
    H, Dh = FOX_HEADS, FOX_HEAD_DIM
    proj = x @ w_in
    q = proj[..., :D].reshape(B, S, H, Dh).transpose(0, 2, 1, 3)
    k = proj[..., D:2 * D].reshape(B, S, H, Dh).transpose(0, 2, 1, 3)
    v = proj[..., 2 * D:3 * D].reshape(B, S, H, Dh).transpose(0, 2, 1, 3)
    log_f = jax.nn.log_sigmoid((proj[..., 3 * D:] + b_f).astype(jnp.float32))
    c = jnp.cumsum(log_f, axis=1).transpose(0, 2, 1)
    n_blk = S // Q_BLOCK
    q_blocks = q.reshape(B, H, n_blk, Q_BLOCK, Dh).transpose(2, 0, 1, 3, 4)
    c_q = c.reshape(B, H, n_blk, Q_BLOCK).transpose(2, 0, 1, 3)
    key_pos = jnp.arange(S)
    scale = FOX_HEAD_DIM ** -0.5

    def attend(args):
        blk, qb, cq = args
        q_pos = blk * Q_BLOCK + jnp.arange(Q_BLOCK)
        s = jnp.einsum('bhqd,bhkd->bhqk', qb, k).astype(jnp.float32) * scale
        s = s + (cq[..., :, None] - c[..., None, :])
        s = jnp.where(key_pos[None, :] <= q_pos[:, None], s, -jnp.inf)
        p = jax.nn.softmax(s, axis=-1).astype(v.dtype)
        return jnp.einsum('bhqk,bhkd->bhqd', p, v)

    o = lax.map(attend, (jnp.arange(n_blk), q_blocks, c_q))
    o = o.transpose(1, 0, 3, 2, 4).reshape(B, S, D)
    return o @ w_o


def gla_mixer(x, w_in, w_gate_up, b_gate, norm_g, w_o):
    B, S, D = x.shape
    H, dk, dv, C = GLA_HEADS, GLA_DK_HEAD, GLA_DV_HEAD, GLA_CHUNK
    N = S // C
    proj = x @ w_in
    o1, o2, o3, o4 = GLA_DK, 2 * GLA_DK, 2 * GLA_DK + GLA_DV, 2 * GLA_DK + 2 * GLA_DV
    q, k, v, r, g_low = proj[..., :o1], proj[..., o1:o2], proj[..., o2:o3], proj[..., o3:o4], proj[..., o4:]
    log_alpha = jax.nn.log_sigmoid((g_low @ w_gate_up + b_gate).astype(jnp.float32)) / GLA_GATE_TAU

    def heads_chunks(t, dh):
        return t.astype(jnp.float32).reshape(B, N, C, H, dh).transpose(0, 3, 1, 2, 4)

    qf = heads_chunks(q, dk) * (dk ** -0.5)
    kf = heads_chunks(k, dk)
    vf = heads_chunks(v, dv)
    b = jnp.cumsum(heads_chunks(log_alpha, dk), axis=3)
    b_ref = b[:, :, :, C // 2 - 1:C // 2, :]
    a = jnp.einsum('bhncd,bhnjd->bhncj', qf * jnp.exp(b - b_ref), kf * jnp.exp(b_ref - b))
    causal = jnp.tril(jnp.ones((C, C), dtype=bool))
    a = jnp.where(causal, a, 0.0)
    o_intra = jnp.einsum('bhncj,bhnjv->bhncv', a, vf)
    b_last = b[:, :, :, -1, :]
    q_inter = qf * jnp.exp(b)
    k_end = kf * jnp.exp(b_last[:, :, :, None, :] - b)

    def step(state, inp):
        qi, ki, vi, bl = inp
        out = jnp.einsum('bhcd,bhdv->bhcv', qi, state)
        state = jnp.exp(bl)[..., None] * state + jnp.einsum('bhcd,bhcv->bhdv', ki, vi)
        return state, out

    xs = (jnp.moveaxis(q_inter, 2, 0), jnp.moveaxis(k_end, 2, 0), jnp.moveaxis(vf, 2, 0), jnp.moveaxis(b_last, 2, 0))
    _, o_inter = lax.scan(step, jnp.zeros((B, H, dk, dv), jnp.float32), xs)
    o = o_intra + jnp.moveaxis(o_inter, 0, 2)
    o = o.transpose(0, 2, 3, 1, 4).reshape(B, S, H, dv)
    o = o * lax.rsqrt(jnp.mean(jnp.square(o), axis=-1, keepdims=True) + RMS_EPS) * norm_g
    o = o.reshape(B, S, GLA_DV).astype(x.dtype) * jax.nn.silu(r)
    return o @ w_o


def hier_moe(x, w_group, b_group, w_expert, b_expert, w_gate, w_up, w_down):
    B, S, D = x.shape
    T = B * S
    K = TOP_K_INNER
    xt = x.reshape(T, D)
    grp_prob = jax.nn.softmax((xt @ w_group + b_group).astype(jnp.float32), axis=-1)
    grp_p, grp_idx = lax.top_k(grp_prob, 1)
    exp_logits = (xt @ w_expert + b_expert).astype(jnp.float32).reshape(T, N_GROUPS, EXPERTS_PER_GROUP)
    exp_logits = jnp.take_along_axis(exp_logits, grp_idx[:, :, None], axis=1)[:, 0]
    in_p, in_idx = lax.top_k(jax.nn.softmax(exp_logits, axis=-1), K)
    in_p = in_p / jnp.sum(in_p, axis=-1, keepdims=True)
    gates = grp_p * in_p
    expert_id = grp_idx * EXPERTS_PER_GROUP + in_idx
    A = T * K
    flat_e = expert_id.reshape(A).astype(jnp.int32)
    flat_tok = jnp.broadcast_to(jnp.arange(T, dtype=jnp.int32)[:, None], (T, K)).reshape(A)
    flat_gate = gates.reshape(A)
    order = jnp.argsort(flat_e)
    sorted_e = flat_e[order]
    counts = jnp.zeros((N_EXPERTS,), jnp.int32).at[flat_e].add(1)
    start = jnp.cumsum(counts) - counts
    padded = ((counts + MOE_BLOCK - 1) // MOE_BLOCK) * MOE_BLOCK
    pad_end = jnp.cumsum(padded)
    pad_start = pad_end - padded
    dest = pad_start[sorted_e] + (jnp.arange(A, dtype=jnp.int32) - start[sorted_e])
    P = ((A + MOE_BLOCK - 1) // MOE_BLOCK) * MOE_BLOCK + N_EXPERTS * MOE_BLOCK
    slot_tok = jnp.zeros((P,), jnp.int32).at[dest].set(flat_tok[order])
    slot_gate = jnp.zeros((P,), x.dtype).at[dest].set(flat_gate[order].astype(x.dtype))
    n_blocks = P // MOE_BLOCK
    blk_start = jnp.arange(n_blocks, dtype=jnp.int32) * MOE_BLOCK
    blk_expert = jnp.minimum(jnp.searchsorted(pad_end, blk_start, side='right'), N_EXPERTS - 1)

    def run_block(args):
        e, toks = args
        xb = xt[toks]
        h = jax.nn.silu(xb @ w_gate[e]) * (xb @ w_up[e])
        return h @ w_down[e]

    y = lax.map(run_block, (blk_expert, slot_tok.reshape(n_blocks, MOE_BLOCK)))
    y = y.reshape(P, D) * slot_gate[:, None]
    out = jnp.zeros((T, D), x.dtype).at[slot_tok].add(y)
    return out.reshape(B, S, D)


def setup_inputs(seed: int = 0) -> dict:
    key = jax.random.key(seed)
    ks = jax.random.split(key, 24)
    f32 = jnp.float32

    def nrm(k, shape, scale):
        return jax.random.normal(k, shape, f32) * scale

    D = D_MODEL
    x = jax.random.normal(ks[0], (BATCH, SEQ, D), f32)
    fox_w_in = nrm(ks[1], (N_FOX_LAYERS, D, 3 * D + FOX_HEADS), D ** -0.5)
    fox_w_in = fox_w_in.at[:, :, 2 * D:3 * D].multiply(DEEPNORM_BETA)
    fox_b_f = FOX_BIAS_OFFSET + nrm(ks[2], (N_FOX_LAYERS, FOX_HEADS), 0.5)
    fox_w_o = nrm(ks[3], (N_FOX_LAYERS, D, D), D ** -0.5 * DEEPNORM_BETA)
    gla_w_in = nrm(ks[4], (N_GLA_LAYERS, D, 2 * GLA_DK + 2 * GLA_DV + GLA_GATE_RANK), D ** -0.5)
    gla_w_in = gla_w_in.at[:, :, 2 * GLA_DK:2 * GLA_DK + GLA_DV].multiply(DEEPNORM_BETA)
    gla_w_gate_up = nrm(ks[5], (N_GLA_LAYERS, GLA_GATE_RANK, GLA_DK), GLA_GATE_RANK ** -0.5)
    gla_b_gate = nrm(ks[6], (N_GLA_LAYERS, GLA_DK), 0.1)
    gla_norm_g = 1.0 + nrm(ks[7], (N_GLA_LAYERS, GLA_DV_HEAD), 0.01)
    gla_w_o = nrm(ks[8], (N_GLA_LAYERS, GLA_DV, D), GLA_DV ** -0.5 * DEEPNORM_BETA)
    ln_mix_g = 1.0 + nrm(ks[9], (DEPTH, D), 0.01)
    ln_mix_b = nrm(ks[10], (DEPTH, D), 0.01)
    ln_ffn_g = 1.0 + nrm(ks[11], (DEPTH, D), 0.01)
    ln_ffn_b = nrm(ks[12], (DEPTH, D), 0.01)
    moe_w_group = nrm(ks[13], (DEPTH, D, N_GROUPS), D ** -0.5)
    moe_b_group = nrm(ks[14], (DEPTH, N_GROUPS), 0.01)
    moe_w_expert = nrm(ks[15], (DEPTH, D, N_EXPERTS), D ** -0.5)
    moe_b_expert = nrm(ks[16], (DEPTH, N_EXPERTS), 0.01)
    moe_w_gate = nrm(ks[17], (DEPTH, N_EXPERTS, D, EXPERT_FF), D ** -0.5)
    moe_w_up = nrm(ks[18], (DEPTH, N_EXPERTS, D, EXPERT_FF), D ** -0.5)
    moe_w_down = nrm(ks[19], (DEPTH, N_EXPERTS, EXPERT_FF, D), EXPERT_FF ** -0.5 * DEEPNORM_BETA)
    return {"x": x, "fox_w_in": fox_w_in, "fox_b_f": fox_b_f, "fox_w_o": fox_w_o,
            "gla_w_in": gla_w_in, "gla_w_gate_up": gla_w_gate_up, "gla_b_gate": gla_b_gate,
            "gla_norm_g": gla_norm_g, "gla_w_o": gla_w_o,
            "ln_mix_g": ln_mix_g, "ln_mix_b": ln_mix_b, "ln_ffn_g": ln_ffn_g, "ln_ffn_b": ln_ffn_b,
            "moe_w_group": moe_w_group, "moe_b_group": moe_b_group,
            "moe_w_expert": moe_w_expert, "moe_b_expert": moe_b_expert,
            "moe_w_gate": moe_w_gate, "moe_w_up": moe_w_up, "moe_w_down": moe_w_down}


def reference(x, fox_w_in, fox_b_f, fox_w_o, gla_w_in, gla_w_gate_up, gla_b_gate, gla_norm_g, gla_w_o,
              ln_mix_g, ln_mix_b, ln_ffn_g, ln_ffn_b, moe_w_group, moe_b_group, moe_w_expert,
              moe_b_expert, moe_w_gate, moe_w_up, moe_w_down):
    for i in range(DEPTH):
        j = i // N_MIXERS
        if i % N_MIXERS == 0:
            mix = fox_mixer(x, fox_w_in[j], fox_b_f[j], fox_w_o[j])
        else:
            mix = gla_mixer(x, gla_w_in[j], gla_w_gate_up[j], gla_b_gate[j], gla_norm_g[j], gla_w_o[j])
        x = layer_norm(DEEPNORM_ALPHA * x + mix, ln_mix_g[i], ln_mix_b[i])
        ffn = hier_moe(x, moe_w_group[i], moe_b_group[i], moe_w_expert[i], moe_b_expert[i],
                       moe_w_gate[i], moe_w_up[i], moe_w_down[i])
        x = layer_norm(DEEPNORM_ALPHA * x + ffn, ln_ffn_g[i], ln_ffn_b[i])
    return x
```

```python
import functools

import jax
import jax.numpy as jnp
from jax import lax
from jax.experimental import pallas as pl
from jax.experimental.pallas import tpu as pltpu

F32 = jnp.float32
BF16 = jnp.bfloat16

DEPTH = 2
FOX_HEADS = 16
FOX_HEAD_DIM = 128
GLA_HEADS = 4
GLA_CHUNK = 64
GLA_GATE_TAU = 16.0
N_GROUPS = 8
EXPERTS_PER_GROUP = 8
N_EXPERTS = N_GROUPS * EXPERTS_PER_GROUP
TOP_K = 2
DEEPNORM_ALPHA = (2 * DEPTH) ** 0.25
LN_EPS = 1e-5
RMS_EPS = 1e-6

LANES = 128
VMEM_LIMIT = 56 * 2**20

MM_TM, MM_TN = 1024, 1024
ROW_TILE = 512
ATT_TQ, ATT_TK = 512, 512
GLA_ROWS = 512
MOE_BLOCK = 128
COMB_TM = 256


def _params(*sem):
    return pltpu.CompilerParams(dimension_semantics=sem, vmem_limit_bytes=VMEM_LIMIT)


def _split2(a):
    hi = a.astype(BF16)
    lo = (a - hi.astype(F32)).astype(BF16)
    return hi, lo


def _split3(a):
    hi = a.astype(BF16)
    r = a - hi.astype(F32)
    mid = r.astype(BF16)
    lo = (r - mid.astype(F32)).astype(BF16)
    return hi, mid, lo


def _pack_hi_lo(w, n_pad=LANES):
    k, n = w.shape
    wp = jnp.zeros((k, n_pad), F32).at[:, :n].set(w)
    hi, lo = _split2(wp)
    return jnp.concatenate([hi, lo], axis=1)


def _dot_hi_lo(x, whl_ref, n_pad=LANES):
    xh, xl = _split2(x)
    a = jnp.dot(xh, whl_ref[...], preferred_element_type=F32)
    b = jnp.dot(xl, whl_ref[:, :n_pad], preferred_element_type=F32)
    return a[:, :n_pad] + a[:, n_pad:] + b


def _log_sigmoid(x):
    return -(jnp.maximum(-x, 0.0) + jnp.log1p(jnp.exp(-jnp.abs(x))))


def _cumsum_rows(a, incl_tri):
    n = a.shape[1]
    parts = jnp.concatenate(_split3(a), axis=1)
    c = jnp.dot(incl_tri, parts, preferred_element_type=F32)
    return c[:, :n] + c[:, n:2 * n] + c[:, 2 * n:]


def _tri(n, strict=False):
    r = lax.broadcasted_iota(jnp.int32, (n, n), 0)
    c = lax.broadcasted_iota(jnp.int32, (n, n), 1)
    return jnp.where((r > c) if strict else (r >= c), 1.0, 0.0).astype(BF16)


def _layer_norm_rows(z, g, b):
    mu = jnp.mean(z, axis=-1, keepdims=True)
    d = z - mu
    var = jnp.mean(d * d, axis=-1, keepdims=True)
    return d * lax.rsqrt(var + LN_EPS) * g + b


def _mm_kernel(x_ref, w_ref, o_ref, xb_ref):
    @pl.when(pl.program_id(1) == 0)
    def _():
        xb_ref[...] = x_ref[...].astype(BF16)

    o_ref[...] = jnp.dot(xb_ref[...], w_ref[...], preferred_element_type=F32).astype(o_ref.dtype)


def _matmul(x, w, out_dtype):
    m, k = x.shape
    n = w.shape[1]
    tm, tn = min(MM_TM, m), min(MM_TN, n)
    return pl.pallas_call(
        _mm_kernel,
        grid=(m // tm, n // tn),
        in_specs=[pl.BlockSpec((tm, k), lambda i, j: (i, 0)),
                  pl.BlockSpec((k, tn), lambda i, j: (0, j))],
        out_specs=pl.BlockSpec((tm, tn), lambda i, j: (i, j)),
        out_shape=jax.ShapeDtypeStruct((m, n), out_dtype),
        scratch_shapes=[pltpu.VMEM((tm, k), BF16)],
        compiler_params=_params("parallel", "arbitrary"),
        name="dense_proj",
    )(x, w)


def _fox_gate_kernel(x_ref, w_ref, b_ref, c_ref, ct_ref, carry_ref):
    @pl.when(pl.program_id(0) == 0)
    def _():
        carry_ref[...] = jnp.zeros_like(carry_ref)

    ts = x_ref.shape[0]
    logits = _dot_hi_lo(x_ref[...], w_ref) + b_ref[...]
    log_f = _log_sigmoid(logits)
    c = _cumsum_rows(log_f, _tri(ts)) + carry_ref[...]
    carry_ref[...] = c[ts - 1:ts, :]
    c_ref[...] = c
    ct_ref[...] = c.T[:ct_ref.shape[0], :]


def _fox_gates(x, w_f, b_f):
    s, d = x.shape
    h = w_f.shape[1]
    ts = min(ROW_TILE, s)
    whl = _pack_hi_lo(w_f)
    bias = jnp.zeros((1, LANES), F32).at[0, :h].set(b_f)
    return pl.pallas_call(
        _fox_gate_kernel,
        grid=(s // ts,),
        in_specs=[pl.BlockSpec((ts, d), lambda i: (i, 0)),
                  pl.BlockSpec((d, 2 * LANES), lambda i: (0, 0)),
                  pl.BlockSpec((1, LANES), lambda i: (0, 0))],
        out_specs=[pl.BlockSpec((ts, LANES), lambda i: (i, 0)),
                   pl.BlockSpec((h, ts), lambda i: (0, i))],
        out_shape=[jax.ShapeDtypeStruct((s, LANES), F32),
                   jax.ShapeDtypeStruct((h, s), F32)],
        scratch_shapes=[pltpu.VMEM((1, LANES), F32)],
        compiler_params=_params("arbitrary"),
        name="fox_gates",
    )(x, whl, bias)


def _fox_attn_kernel(q_ref, k_ref, v_ref, c_ref, ct_ref, o_ref, acc_ref, m_ref, l_ref):
    h = pl.program_id(0)
    qi = pl.program_id(1)
    tq, dh = q_ref.shape
    tk = ATT_TK
    scale = dh ** -0.5
    q = (q_ref[...].astype(F32) * scale).astype(BF16)
    lane = lax.broadcasted_iota(jnp.int32, c_ref.shape, 1)
    cq = jnp.sum(jnp.where(lane == h, c_ref[...], 0.0), axis=1, keepdims=True)

    acc_ref[...] = jnp.zeros_like(acc_ref)
    m_ref[...] = jnp.full_like(m_ref, -jnp.inf)
    l_ref[...] = jnp.zeros_like(l_ref)

    def block(k_start, diag_offset):
        kb = k_ref[pl.ds(k_start, tk), :]
        vb = v_ref[pl.ds(k_start, tk), :]
        ck = ct_ref[pl.ds(h, 1), pl.ds(k_start, tk)]
        s = lax.dot_general(q, kb, (((1,), (1,)), ((), ())), preferred_element_type=F32)
        s = s + (cq - ck)
        if diag_offset is not None:
            r = lax.broadcasted_iota(jnp.int32, (tq, tk), 0)
            c = lax.broadcasted_iota(jnp.int32, (tq, tk), 1) + diag_offset
            s = jnp.where(c <= r, s, -jnp.inf)
        m_prev = m_ref[...]
        m_new = jnp.maximum(m_prev, jnp.max(s, axis=1, keepdims=True))
        p = jnp.exp(s - m_new)
        alpha = jnp.exp(m_prev - m_new)
        l_ref[...] = alpha * l_ref[...] + jnp.sum(p, axis=1, keepdims=True)
        acc_ref[...] = alpha * acc_ref[...] + jnp.dot(p.astype(BF16), vb, preferred_element_type=F32)
        m_ref[...] = m_new

    n_full = qi * (tq // tk)

    def body(ki, carry):
        block(pl.multiple_of(ki * tk, tk), None)
        return carry

    lax.fori_loop(0, n_full, body, 0)
    for j in range(tq // tk):
        block(pl.multiple_of(qi * tq + j * tk, tk), j * tk)

    o_ref[...] = (acc_ref[...] / l_ref[...]).astype(o_ref.dtype)


def _fox_attention(qkv, c, ct, n_heads):
    s = qkv.shape[0]
    dh = FOX_HEAD_DIM
    tq = min(ATT_TQ, s)
    hh = n_heads
    return pl.pallas_call(
        _fox_attn_kernel,
        grid=(hh, s // tq),
        in_specs=[pl.BlockSpec((tq, dh), lambda h, i: (i, h)),
                  pl.BlockSpec((s, dh), lambda h, i: (0, hh + h)),
                  pl.BlockSpec((s, dh), lambda h, i: (0, 2 * hh + h)),
                  pl.BlockSpec((tq, LANES), lambda h, i: (i, 0)),
                  pl.BlockSpec((hh, s), lambda h, i: (0, 0))],
        out_specs=pl.BlockSpec((tq, dh), lambda h, i: (i, h)),
        out_shape=jax.ShapeDtypeStruct((s, hh * dh), BF16),
        scratch_shapes=[pltpu.VMEM((tq, dh), F32),
                        pltpu.VMEM((tq, 1), F32),
                        pltpu.VMEM((tq, 1), F32)],
        compiler_params=_params("parallel", "arbitrary"),
        name="fox_attention",
    )(qkv, qkv, qkv, c, ct)


def _proj_ln_kernel(o_ref, w_ref, x_ref, g_ref, b_ref, y_ref):
    mix = jnp.dot(o_ref[...], w_ref[...], preferred_element_type=F32)
    z = DEEPNORM_ALPHA * x_ref[...] + mix
    y_ref[...] = _layer_norm_rows(z, g_ref[...], b_ref[...])


def _proj_ln(o, w, x, g, b):
    s, d = x.shape
    kd = o.shape[1]
    tm = min(ROW_TILE, s)
    return pl.pallas_call(
        _proj_ln_kernel,
        grid=(s // tm,),
        in_specs=[pl.BlockSpec((tm, kd), lambda i: (i, 0)),
                  pl.BlockSpec((kd, d), lambda i: (0, 0)),
                  pl.BlockSpec((tm, d), lambda i: (i, 0)),
                  pl.BlockSpec((1, d), lambda i: (0, 0)),
                  pl.BlockSpec((1, d), lambda i: (0, 0))],
        out_specs=pl.BlockSpec((tm, d), lambda i: (i, 0)),
        out_shape=jax.ShapeDtypeStruct((s, d), F32),
        compiler_params=_params("parallel"),
        name="proj_ln",
    )(o, w, x, g.reshape(1, d), b.reshape(1, d))


def _first_lane_eq(vals, target, lane):
    return jnp.min(jnp.where(vals == target, lane, LANES), axis=1, keepdims=True)


def _router_kernel(x_ref, w_ref, b_ref, route_ref, cnt_ref, carry_ref):
    @pl.when(pl.program_id(0) == 0)
    def _():
        carry_ref[...] = jnp.zeros_like(carry_ref)

    tm = x_ref.shape[0]
    neg = -jnp.inf
    logits = _dot_hi_lo(x_ref[...], w_ref) + b_ref[...]
    lane = lax.broadcasted_iota(jnp.int32, (tm, LANES), 1)

    gl = jnp.where(lane < N_GROUPS, logits, neg)
    gmax = jnp.max(gl, axis=1, keepdims=True)
    gsum = jnp.sum(jnp.exp(gl - gmax), axis=1, keepdims=True)
    grp_p = 1.0 / gsum
    grp = _first_lane_eq(gl, gmax, lane)

    lo = N_GROUPS + grp * EXPERTS_PER_GROUP
    el = jnp.where((lane >= lo) & (lane < lo + EXPERTS_PER_GROUP), logits, neg)
    emax = jnp.max(el, axis=1, keepdims=True)
    esum = jnp.sum(jnp.exp(el - emax), axis=1, keepdims=True)
    idx1 = _first_lane_eq(el, emax, lane)
    el2 = jnp.where(lane == idx1, neg, el)
    emax2 = jnp.max(el2, axis=1, keepdims=True)
    idx2 = _first_lane_eq(el2, emax2, lane)
    p1 = 1.0 / esum
    p2 = jnp.exp(emax2 - emax) / esum
    psum = p1 + p2
    g1 = grp_p * (p1 / psum)
    g2 = grp_p * (p2 / psum)

    oh1 = lane == idx1
    oh2 = lane == idx2
    both = jnp.where(oh1 | oh2, 1.0, 0.0)
    before = jnp.dot(_tri(tm, strict=True), both.astype(BF16), preferred_element_type=F32)
    before = before + carry_ref[...]
    r1 = jnp.sum(jnp.where(oh1, before, 0.0), axis=1, keepdims=True)
    r2 = jnp.sum(jnp.where(oh2, before, 0.0), axis=1, keepdims=True)
    carry_ref[...] = carry_ref[...] + jnp.sum(both, axis=0, keepdims=True)
    cnt_ref[...] = carry_ref[...]

    e1 = (idx1 - N_GROUPS).astype(F32)
    e2 = (idx2 - N_GROUPS).astype(F32)
    out = jnp.where(lane == 0, e1, 0.0)
    out = jnp.where(lane == 1, e2, out)
    out = jnp.where(lane == 2, g1, out)
    out = jnp.where(lane == 3, g2, out)
    out = jnp.where(lane == 4, r1, out)
    out = jnp.where(lane == 5, r2, out)
    route_ref[...] = out


def _router(x, w_group, b_group, w_expert, b_expert):
    s, d = x.shape
    tm = min(ROW_TILE, s)
    whl = _pack_hi_lo(jnp.concatenate([w_group, w_expert], axis=1))
    nb = N_GROUPS + N_EXPERTS
    bias = jnp.zeros((1, LANES), F32).at[0, :nb].set(jnp.concatenate([b_group, b_expert]))
    return pl.pallas_call(
        _router_kernel,
        grid=(s // tm,),
        in_specs=[pl.BlockSpec((tm, d), lambda i: (i, 0)),
                  pl.BlockSpec((d, 2 * LANES), lambda i: (0, 0)),
                  pl.BlockSpec((1, LANES), lambda i: (0, 0))],
        out_specs=[pl.BlockSpec((tm, LANES), lambda i: (i, 0)),
                   pl.BlockSpec((1, LANES), lambda i: (0, 0))],
        out_shape=[jax.ShapeDtypeStruct((s, LANES), F32),
                   jax.ShapeDtypeStruct((1, LANES), F32)],
        scratch_shapes=[pltpu.VMEM((1, LANES), F32)],
        compiler_params=_params("arbitrary"),
        name="router",
    )(x, whl, bias)


def _expert_kernel(be_ref, nv_ref, tok_ref, x_hbm, wg_ref, wu_ref, wd_ref, y_ref,
                   xg_ref, wgb_ref, wub_ref, wdb_ref, sem):
    b = pl.program_id(0)
    n_valid = nv_ref[0]
    bm = xg_ref.shape[1]

    def row_copy(blk, slot, r):
        tok = tok_ref[blk * bm + r]
        return pltpu.make_async_copy(x_hbm.at[pl.ds(tok, 1), :],
                                     xg_ref.at[slot, pl.ds(r, 1), :], sem.at[slot])

    def start_gather(blk, slot):
        def body(r, carry):
            row_copy(blk, slot, r).start()
            return carry
        lax.fori_loop(0, bm, body, 0)

    def wait_gather(blk, slot):
        def body(r, carry):
            row_copy(blk, slot, r).wait()
            return carry
        lax.fori_loop(0, bm, body, 0)

    @pl.when((b == 0) & (n_valid > 0))
    def _():
        start_gather(0, 0)

    @pl.when(b + 1 < n_valid)
    def _():
        start_gather(b + 1, (b + 1) % 2)

    @pl.when(b < n_valid)
    def _():
        prev = be_ref[jnp.maximum(b - 1, 0)]

        @pl.when((b == 0) | (be_ref[b] != prev))
        def _():
            wgb_ref[...] = wg_ref[0].astype(BF16)
            wub_ref[...] = wu_ref[0].astype(BF16)
            wdb_ref[...] = wd_ref[0].astype(BF16)

        slot = b % 2
        wait_gather(b, slot)
        xb = xg_ref[slot].astype(BF16)
        g = jnp.dot(xb, wgb_ref[...], preferred_element_type=F32)
        u = jnp.dot(xb, wub_ref[...], preferred_element_type=F32)
        hid = (g * jax.nn.sigmoid(g) * u).astype(BF16)
        y_ref[...] = jnp.dot(hid, wdb_ref[...], preferred_element_type=F32)

    @pl.when(b >= n_valid)
    def _():
        y_ref[...] = jnp.zeros_like(y_ref)


def _experts(x, slot_tok, blk_expert, n_valid, w_gate, w_up, w_down):
    t, d = x.shape
    e, _, f = w_gate.shape
    p = slot_tok.shape[0]
    bm = MOE_BLOCK
    nb = p // bm

    def blk_idx(b, be, nv, tok):
        return jnp.minimum(b, jnp.maximum(nv[0] - 1, 0))

    grid_spec = pltpu.PrefetchScalarGridSpec(
        num_scalar_prefetch=3,
        grid=(nb,),
        in_specs=[pl.BlockSpec(memory_space=pl.ANY),
                  pl.BlockSpec((1, d, f), lambda b, be, nv, tok: (be[blk_idx(b, be, nv, tok)], 0, 0)),
                  pl.BlockSpec((1, d, f), lambda b, be, nv, tok: (be[blk_idx(b, be, nv, tok)], 0, 0)),
                  pl.BlockSpec((1, f, d), lambda b, be, nv, tok: (be[blk_idx(b, be, nv, tok)], 0, 0))],
        out_specs=pl.BlockSpec((bm, d), lambda b, be, nv, tok: (b, 0)),
        scratch_shapes=[pltpu.VMEM((2, bm, d), F32),
                        pltpu.VMEM((d, f), BF16),
                        pltpu.VMEM((d, f), BF16),
                        pltpu.VMEM((f, d), BF16),
                        pltpu.SemaphoreType.DMA((2,))],
    )
    return pl.pallas_call(
        _expert_kernel,
        grid_spec=grid_spec,
        out_shape=jax.ShapeDtypeStruct((p, d), F32),
        compiler_params=_params("arbitrary"),
        name="moe_experts",
    )(blk_expert, n_valid, slot_tok, x, w_gate, w_up, w_down)


def _combine_ln_kernel(dest_ref, y_hbm, x_ref, route_ref, g_ref, b_ref, o_ref, yg_ref, sem):
    i = pl.program_id(0)
    n = pl.num_programs(0)
    tm = x_ref.shape[0]

    def row_copy(step, slot, r):
        src = dest_ref[step * (2 * tm) + r]
        return pltpu.make_async_copy(y_hbm.at[pl.ds(src, 1), :],
                                     yg_ref.at[slot, pl.ds(r, 1), :], sem.at[slot])

    def start_gather(step, slot):
        def body(r, carry):
            row_copy(step, slot, r).start()
            return carry
        lax.fori_loop(0, 2 * tm, body, 0)

    def wait_gather(step, slot):
        def body(r, carry):
            row_copy(step, slot, r).wait()
            return carry
        lax.fori_loop(0, 2 * tm, body, 0)

    @pl.when(i == 0)
    def _():
        start_gather(0, 0)

    @pl.when(i + 1 < n)
    def _():
        start_gather(i + 1, (i + 1) % 2)

    slot = i % 2
    wait_gather(i, slot)
    route = route_ref[...]
    lane = lax.broadcasted_iota(jnp.int32, route.shape, 1)
    g1 = jnp.sum(jnp.where(lane == 2, route, 0.0), axis=1, keepdims=True)
    g2 = jnp.sum(jnp.where(lane == 3, route, 0.0), axis=1, keepdims=True)
    y1 = yg_ref[slot, pl.ds(0, tm), :]
    y2 = yg_ref[slot, pl.ds(tm, tm), :]
    z = DEEPNORM_ALPHA * x_ref[...] + (y1 * g1 + y2 * g2)
    o_ref[...] = _layer_norm_rows(z, g_ref[...], b_ref[...])


def _combine_ln(y, dest, x, route, g, b):
    t, d = x.shape
    tm = min(COMB_TM, t)
    grid_spec = pltpu.PrefetchScalarGridSpec(
        num_scalar_prefetch=1,
        grid=(t // tm,),
        in_specs=[pl.BlockSpec(memory_space=pl.ANY),
                  pl.BlockSpec((tm, d), lambda i, dest: (i, 0)),
                  pl.BlockSpec((tm, LANES), lambda i, dest: (i, 0)),
                  pl.BlockSpec((1, d), lambda i, dest: (0, 0)),
                  pl.BlockSpec((1, d), lambda i, dest: (0, 0))],
        out_specs=pl.BlockSpec((tm, d), lambda i, dest: (i, 0)),
        scratch_shapes=[pltpu.VMEM((2, 2 * tm, d), F32),
                        pltpu.SemaphoreType.DMA((2,))],
    )
    return pl.pallas_call(
        _combine_ln_kernel,
        grid_spec=grid_spec,
        out_shape=jax.ShapeDtypeStruct((t, d), F32),
        compiler_params=_params("arbitrary"),
        name="moe_combine_ln",
    )(dest, y, x, route, g.reshape(1, d), b.reshape(1, d))


def _hier_moe_ln(x, w_group, b_group, w_expert, b_expert, w_gate, w_up, w_down, ln_g, ln_b):
    t, d = x.shape
    bm = MOE_BLOCK
    route, counts = _router(x, w_group, b_group, w_expert, b_expert)
    eid = route[:, 0:2].astype(jnp.int32)
    rank = route[:, 4:6].astype(jnp.int32)
    counts = counts[0, N_GROUPS:N_GROUPS + N_EXPERTS].astype(jnp.int32)
    padded = ((counts + bm - 1) // bm) * bm
    pad_end = jnp.cumsum(padded)
    pad_start = pad_end - padded
    dest = pad_start[eid] + rank
    p = ((t * TOP_K + bm - 1) // bm) * bm + N_EXPERTS * bm
    tok = jnp.broadcast_to(jnp.arange(t, dtype=jnp.int32)[:, None], (t, TOP_K))
    slot_tok = jnp.zeros((p,), jnp.int32).at[dest.reshape(-1)].set(tok.reshape(-1))
    nb = p // bm
    blk_start = jnp.arange(nb, dtype=jnp.int32) * bm
    blk_expert = jnp.minimum(jnp.searchsorted(pad_end, blk_start, side="right"),
                             N_EXPERTS - 1).astype(jnp.int32)
    n_valid = (pad_end[-1] // bm).astype(jnp.int32).reshape(1)
    y = _experts(x, slot_tok, blk_expert, n_valid, w_gate, w_up, w_down)
    tm = min(COMB_TM, t)
    dest_tiles = dest.reshape(t // tm, tm, TOP_K).transpose(0, 2, 1).reshape(-1)
    return _combine_ln(y, dest_tiles, x, route, ln_g, ln_b)


def _gla_gate_kernel(x_ref, wl_ref, wu_ref, b_ref, la_ref):
    g_low = _dot_hi_lo(x_ref[...], wl_ref)
    n = wu_ref.shape[1] // 2
    gh, gl = _split2(g_low)
    a = jnp.dot(gh, wu_ref[...], preferred_element_type=F32)
    c = jnp.dot(gl, wu_ref[:, :n], preferred_element_type=F32)
    logit = a[:, :n] + a[:, n:] + c + b_ref[...]
    la_ref[...] = _log_sigmoid(logit) / GLA_GATE_TAU


def _gla_gates(x, w_low, w_gate_up, b_gate):
    s, d = x.shape
    rank, dk = w_gate_up.shape
    ts = min(ROW_TILE, s)
    wl = _pack_hi_lo(w_low)
    wu_pad = jnp.zeros((LANES, dk), F32).at[:rank].set(w_gate_up)
    wu = jnp.concatenate(_split2(wu_pad), axis=1)
    return pl.pallas_call(
        _gla_gate_kernel,
        grid=(s // ts,),
        in_specs=[pl.BlockSpec((ts, d), lambda i: (i, 0)),
                  pl.BlockSpec((d, 2 * LANES), lambda i: (0, 0)),
                  pl.BlockSpec((LANES, 2 * dk), lambda i: (0, 0)),
                  pl.BlockSpec((1, dk), lambda i: (0, 0))],
        out_specs=pl.BlockSpec((ts, dk), lambda i: (i, 0)),
        out_shape=jax.ShapeDtypeStruct((s, dk), F32),
        compiler_params=_params("parallel"),
        name="gla_gates",
    )(x, wl, wu, b_gate.reshape(1, dk))


def _gla_kernel(q_ref, k_ref, v_ref, la_ref, r_ref, g_ref, o_ref, state_ref):
    @pl.when(pl.program_id(1) == 0)
    def _():
        state_ref[...] = jnp.zeros_like(state_ref)

    rows, dk = q_ref.shape
    cs = GLA_CHUNK
    tri = _tri(cs)
    rr = lax.broadcasted_iota(jnp.int32, (cs, cs), 0)
    cc = lax.broadcasted_iota(jnp.int32, (cs, cs), 1)
    causal = rr >= cc
    for ci in range(rows // cs):
        sl = pl.ds(ci * cs, cs)
        b = _cumsum_rows(la_ref[sl, :], tri)
        b_mid = b[cs // 2 - 1:cs // 2, :]
        b_last = b[cs - 1:cs, :]
        q = q_ref[sl, :].astype(F32) * (dk ** -0.5)
        k = k_ref[sl, :].astype(F32)
        v = v_ref[sl, :]
        qa = (q * jnp.exp(b - b_mid)).astype(BF16)
        ka = (k * jnp.exp(b_mid - b)).astype(BF16)
        a = lax.dot_general(qa, ka, (((1,), (1,)), ((), ())), preferred_element_type=F32)
        a = jnp.where(causal, a, 0.0)
        o = jnp.dot(a.astype(BF16), v, preferred_element_type=F32)
        state = state_ref[...]
        o = o + jnp.dot((q * jnp.exp(b)).astype(BF16), state.astype(BF16), preferred_element_type=F32)
        k_end_t = (k * jnp.exp(b_last - b)).T.astype(BF16)
        decay = jnp.exp(b.T[:, cs - 1:cs])
        state_ref[...] = decay * state + jnp.dot(k_end_t, v, preferred_element_type=F32)
        o = o * lax.rsqrt(jnp.mean(o * o, axis=-1, keepdims=True) + RMS_EPS) * g_ref[...]
        r = r_ref[sl, :].astype(F32)
        o_ref[sl, :] = (o * (r * jax.nn.sigmoid(r))).astype(o_ref.dtype)


def _gla(proj, la, norm_g):
    s = proj.shape[0]
    dk_all = la.shape[1]
    nh = GLA_HEADS
    dk = dk_all // nh
    dv = norm_g.shape[0]
    rows = min(GLA_ROWS, s)
    kq, kv = dk_all // dk, 2 * dk_all // dv
    kr = kv + nh
    return pl.pallas_call(
        _gla_kernel,
        grid=(nh, s // rows),
        in_specs=[pl.BlockSpec((rows, dk), lambda h, i: (i, h)),
                  pl.BlockSpec((rows, dk), lambda h, i: (i, kq + h)),
                  pl.BlockSpec((rows, dv), lambda h, i: (i, kv + h)),
                  pl.BlockSpec((rows, dk), lambda h, i: (i, h)),
                  pl.BlockSpec((rows, dv), lambda h, i: (i, kr + h)),
                  pl.BlockSpec((1, dv), lambda h, i: (0, 0))],
        out_specs=pl.BlockSpec((rows, dv), lambda h, i: (i, h)),
        out_shape=jax.ShapeDtypeStruct((s, nh * dv), BF16),
        scratch_shapes=[pltpu.VMEM((dk, dv), F32)],
        compiler_params=_params("parallel", "arbitrary"),
        name="gla_chunks",
    )(proj, proj, proj, la, proj, norm_g.reshape(1, dv))


def kernel(x, fox_w_in, fox_b_f, fox_w_o, gla_w_in, gla_w_gate_up, gla_b_gate, gla_norm_g, gla_w_o,
           ln_mix_g, ln_mix_b, ln_ffn_g, ln_ffn_b, moe_w_group, moe_b_group, moe_w_expert,
           moe_b_expert, moe_w_gate, moe_w_up, moe_w_down):
    bsz, s, d = x.shape
    outs = []
    for bi in range(bsz):
        xt = x[bi]
        for i in range(DEPTH):
            j = i // 2
            if i % 2 == 0:
                w_in = fox_w_in[j]
                qkv = _matmul(xt, w_in[:, :3 * d].astype(BF16), BF16)
                c, ct = _fox_gates(xt, w_in[:, 3 * d:], fox_b_f[j])
                o = _fox_attention(qkv, c, ct, FOX_HEADS)
                w_o = fox_w_o[j]
            else:
                w_in = gla_w_in[j]
                n_main = w_in.shape[1] - gla_w_gate_up.shape[1]
                proj = _matmul(xt, w_in[:, :n_main].astype(BF16), BF16)
                la = _gla_gates(xt, w_in[:, n_main:], gla_w_gate_up[j], gla_b_gate[j])
                o = _gla(proj, la, gla_norm_g[j])
                w_o = gla_w_o[j]
            xt = _proj_ln(o, w_o.astype(BF16), xt, ln_mix_g[i], ln_mix_b[i])
            xt = _hier_moe_ln(xt, moe_w_group[i], moe_b_group[i], moe_w_expert[i], moe_b_expert[i],
                              moe_w_gate[i], moe_w_up[i], moe_w_down[i], ln_ffn_g[i], ln_ffn_b[i])
        outs.append(xt)
    return jnp.stack(outs, axis=0)
```

```python
import functools

import jax
import jax.numpy as jnp
from jax import lax
from jax.experimental import pallas as pl
from jax.experimental.pallas import tpu as pltpu

F32 = jnp.float32
BF16 = jnp.bfloat16

DEPTH = 2
FOX_HEADS = 16
FOX_HEAD_DIM = 128
GLA_HEADS = 4
GLA_CHUNK = 64
GLA_GATE_TAU = 16.0
N_GROUPS = 8
EXPERTS_PER_GROUP = 8
N_EXPERTS = N_GROUPS * EXPERTS_PER_GROUP
TOP_K = 2
DEEPNORM_ALPHA = (2 * DEPTH) ** 0.25
LN_EPS = 1e-5
RMS_EPS = 1e-6

LANES = 128
VMEM_LIMIT = 56 * 2**20

MM_TM, MM_TN = 1024, 1024
ROW_TILE = 512
ATT_TQ, ATT_TK = 1024, 512
GLA_ROWS = 512
MOE_BLOCK = 128
COMB_TM = 256


def _params(*sem):
    return pltpu.CompilerParams(dimension_semantics=sem, vmem_limit_bytes=VMEM_LIMIT)


def _split2(a):
    hi = a.astype(BF16)
    lo = (a - hi.astype(F32)).astype(BF16)
    return hi, lo


def _split3(a):
    hi = a.astype(BF16)
    r = a - hi.astype(F32)
    mid = r.astype(BF16)
    lo = (r - mid.astype(F32)).astype(BF16)
    return hi, mid, lo


def _pack_hi_lo(w, n_pad=LANES):
    k, n = w.shape
    wp = jnp.zeros((k, n_pad), F32).at[:, :n].set(w)
    hi, lo = _split2(wp)
    return jnp.concatenate([hi, lo], axis=1)


def _dot_hi_lo(x, whl_ref, n_pad=LANES):
    xh, xl = _split2(x)
    a = jnp.dot(xh, whl_ref[...], preferred_element_type=F32)
    b = jnp.dot(xl, whl_ref[:, :n_pad], preferred_element_type=F32)
    return a[:, :n_pad] + a[:, n_pad:] + b


def _log_sigmoid(x):
    return -(jnp.maximum(-x, 0.0) + jnp.log1p(jnp.exp(-jnp.abs(x))))


def _cumsum_rows(a, incl_tri):
    n = a.shape[1]
    parts = jnp.concatenate(_split3(a), axis=1)
    c = jnp.dot(incl_tri, parts, preferred_element_type=F32)
    return c[:, :n] + c[:, n:2 * n] + c[:, 2 * n:]


def _tri(n, strict=False):
    r = lax.broadcasted_iota(jnp.int32, (n, n), 0)
    c = lax.broadcasted_iota(jnp.int32, (n, n), 1)
    return jnp.where((r > c) if strict else (r >= c), 1.0, 0.0).astype(BF16)


def _layer_norm_rows(z, g, b):
    mu = jnp.mean(z, axis=-1, keepdims=True)
    d = z - mu
    var = jnp.mean(d * d, axis=-1, keepdims=True)
    return d * lax.rsqrt(var + LN_EPS) * g + b


def _mm_kernel(x_ref, w_ref, o_ref, xb_ref):
    @pl.when(pl.program_id(1) == 0)
    def _():
        xb_ref[...] = x_ref[...].astype(BF16)

    o_ref[...] = jnp.dot(xb_ref[...], w_ref[...], preferred_element_type=F32).astype(o_ref.dtype)


def _matmul(x, w, out_dtype):
    m, k = x.shape
    n = w.shape[1]
    tm, tn = min(MM_TM, m), min(MM_TN, n)
    return pl.pallas_call(
        _mm_kernel,
        grid=(m // tm, n // tn),
        in_specs=[pl.BlockSpec((tm, k), lambda i, j: (i, 0)),
                  pl.BlockSpec((k, tn), lambda i, j: (0, j))],
        out_specs=pl.BlockSpec((tm, tn), lambda i, j: (i, j)),
        out_shape=jax.ShapeDtypeStruct((m, n), out_dtype),
        scratch_shapes=[pltpu.VMEM((tm, k), BF16)],
        compiler_params=_params("parallel", "arbitrary"),
        name="dense_proj",
    )(x, w)


def _fox_gate_kernel(x_ref, w_ref, b_ref, c_ref, carry_ref):
    @pl.when(pl.program_id(0) == 0)
    def _():
        carry_ref[...] = jnp.zeros_like(carry_ref)

    ts = x_ref.shape[0]
    logits = _dot_hi_lo(x_ref[...], w_ref) + b_ref[...]
    log_f = _log_sigmoid(logits)
    c = _cumsum_rows(log_f, _tri(ts)) + carry_ref[...]
    carry_ref[...] = c[ts - 1:ts, :]
    c_ref[...] = c


def _fox_gates(x, w_f, b_f):
    s, d = x.shape
    h = w_f.shape[1]
    ts = min(ROW_TILE, s)
    whl = _pack_hi_lo(w_f)
    bias = jnp.zeros((1, LANES), F32).at[0, :h].set(b_f)
    return pl.pallas_call(
        _fox_gate_kernel,
        grid=(s // ts,),
        in_specs=[pl.BlockSpec((ts, d), lambda i: (i, 0)),
                  pl.BlockSpec((d, 2 * LANES), lambda i: (0, 0)),
                  pl.BlockSpec((1, LANES), lambda i: (0, 0))],
        out_specs=pl.BlockSpec((ts, LANES), lambda i: (i, 0)),
        out_shape=jax.ShapeDtypeStruct((s, LANES), F32),
        scratch_shapes=[pltpu.VMEM((1, LANES), F32)],
        compiler_params=_params("arbitrary"),
        name="fox_gates",
    )(x, whl, bias)


def _fox_attn_kernel(q_ref, k_ref, v_ref, c_ref, o_ref, kaug_ref, vt_ref, qt_ref, st0_ref, st1_ref,
                     p0_ref, p1_ref, alpha_ref, acc_ref, m_ref):
    h = pl.program_id(0)
    qi = pl.program_id(1)
    tq, dh = q_ref.shape
    s_len = k_ref.shape[0]
    tk = ATT_TK
    log2e = 1.4426950408889634
    scale = dh ** -0.5 * log2e

    def head_column(rows, n):
        lane = lax.broadcasted_iota(jnp.int32, (n, LANES), 1)
        return log2e * jnp.sum(jnp.where(lane == h, c_ref[rows, :], 0.0), axis=1, keepdims=True)

    def bias_columns(col, first, n):
        lane = lax.broadcasted_iota(jnp.int32, (n, LANES), 1)
        hi, mid, lo = _split3(col)
        ones_first = 3 - first
        out = jnp.where((lane >= ones_first) & (lane < ones_first + 3), 1.0, 0.0)
        out = jnp.where(lane == first, hi.astype(F32), out)
        out = jnp.where(lane == first + 1, mid.astype(F32), out)
        return jnp.where(lane == first + 2, lo.astype(F32), out).astype(BF16)

    @pl.when(qi == 0)
    def _():
        ones_row = jnp.where(lax.broadcasted_iota(jnp.int32, (16, tk), 0) == 0, 1.0, 0.0).astype(BF16)

        def build(ci, carry):
            rows = pl.ds(pl.multiple_of(ci * tk, tk), tk)
            kaug_ref[rows, :dh] = k_ref[rows, :]
            kaug_ref[rows, dh:] = bias_columns(-head_column(rows, tk), 0, tk)
            vt_ref[:dh, rows] = v_ref[rows, :].astype(F32).T.astype(BF16)
            vt_ref[dh:, rows] = ones_row
            return carry
        lax.fori_loop(0, s_len // tk, build, 0)

    q_rows = pl.ds(pl.multiple_of(qi * tq, tq), tq)
    q_aug = jnp.concatenate([q_ref[...].astype(F32) * scale,
                             bias_columns(head_column(q_rows, tq), 3, tq).astype(F32)], axis=1)
    qt_ref[...] = q_aug.T.astype(BF16)

    acc_ref[...] = jnp.zeros_like(acc_ref)
    m_ref[...] = jnp.full_like(m_ref, -jnp.inf)

    st_refs, p_refs = (st0_ref, st1_ref), (p0_ref, p1_ref)

    def scores(k_start, slot):
        st_refs[slot][...] = jnp.dot(kaug_ref[pl.ds(k_start, tk), :], qt_ref[...],
                                     preferred_element_type=F32)

    def softmax(slot, diag_offset):
        st = st_refs[slot][...]
        if diag_offset is not None:
            kr = lax.broadcasted_iota(jnp.int32, (tk, tq), 0) + diag_offset
            qc = lax.broadcasted_iota(jnp.int32, (tk, tq), 1)
            st = jnp.where(kr <= qc, st, -jnp.inf)
        m_prev = m_ref[...]
        m_new = jnp.maximum(m_prev, jnp.max(st, axis=0, keepdims=True))
        m_ref[...] = m_new
        p_refs[slot][...] = jnp.exp2(st - m_new).astype(BF16)
        alpha_ref[slot] = jnp.exp2(m_prev - m_new)

    def values(k_start, slot):
        acc_ref[...] = alpha_ref[slot] * acc_ref[...] + jnp.dot(
            vt_ref[:, pl.ds(k_start, tk)], p_refs[slot][...], preferred_element_type=F32)

    n_diag = tq // tk
    assert n_diag == 2
    d0 = pl.multiple_of(qi * tq, tk)
    d1 = pl.multiple_of(qi * tq + tk, tk)
    n_full = qi * n_diag

    def full_start(i):
        return pl.multiple_of(jnp.minimum(i, jnp.maximum(n_full - 1, 0)) * tk, tk)

    scores(d0, 0)
    scores(d1, 1)
    softmax(0, 0)
    scores(full_start(0), 0)
    softmax(1, tk)
    values(d0, 0)

    def pair(j, carry):
        i0 = 2 * j
        prev = jnp.where(j == 0, d1, (i0 - 1) * tk)
        scores(full_start(i0 + 1), 1)
        softmax(0, None)
        values(pl.multiple_of(prev, tk), 1)
        scores(full_start(i0 + 2), 0)
        softmax(1, None)
        values(pl.multiple_of(i0 * tk, tk), 0)
        return carry

    lax.fori_loop(0, qi, pair, 0)
    last = jnp.where(qi == 0, d1, (n_full - 1) * tk)
    values(pl.multiple_of(last, tk), 1)

    o_ref[...] = (acc_ref[:dh, :] / acc_ref[dh:dh + 1, :]).T.astype(o_ref.dtype)


def _fox_attention(qkv, c, n_heads):
    s = qkv.shape[0]
    dh = FOX_HEAD_DIM
    tq = min(ATT_TQ, s)
    hh = n_heads
    return pl.pallas_call(
        _fox_attn_kernel,
        grid=(hh, s // tq),
        in_specs=[pl.BlockSpec((tq, dh), lambda h, i: (i, h)),
                  pl.BlockSpec((s, dh), lambda h, i: (0, hh + h)),
                  pl.BlockSpec((s, dh), lambda h, i: (0, 2 * hh + h)),
                  pl.BlockSpec((s, LANES), lambda h, i: (0, 0))],
        out_specs=pl.BlockSpec((tq, dh), lambda h, i: (i, h)),
        out_shape=jax.ShapeDtypeStruct((s, hh * dh), BF16),
        scratch_shapes=[pltpu.VMEM((s, 2 * dh), BF16),
                        pltpu.VMEM((dh + 16, s), BF16),
                        pltpu.VMEM((2 * dh, tq), BF16),
                        pltpu.VMEM((ATT_TK, tq), F32),
                        pltpu.VMEM((ATT_TK, tq), F32),
                        pltpu.VMEM((ATT_TK, tq), BF16),
                        pltpu.VMEM((ATT_TK, tq), BF16),
                        pltpu.VMEM((2, 1, tq), F32),
                        pltpu.VMEM((dh + 16, tq), F32),
                        pltpu.VMEM((1, tq), F32)],
        compiler_params=_params("arbitrary", "arbitrary"),
        name="fox_attention",
    )(qkv, qkv, qkv, c)


def _proj_ln_kernel(o_ref, w_ref, x_ref, g_ref, b_ref, y_ref):
    mix = jnp.dot(o_ref[...], w_ref[...], preferred_element_type=F32)
    z = DEEPNORM_ALPHA * x_ref[...] + mix
    y_ref[...] = _layer_norm_rows(z, g_ref[...], b_ref[...])


def _proj_ln(o, w, x, g, b):
    s, d = x.shape
    kd = o.shape[1]
    tm = min(ROW_TILE, s)
    return pl.pallas_call(
        _proj_ln_kernel,
        grid=(s // tm,),
        in_specs=[pl.BlockSpec((tm, kd), lambda i: (i, 0)),
                  pl.BlockSpec((kd, d), lambda i: (0, 0)),
                  pl.BlockSpec((tm, d), lambda i: (i, 0)),
                  pl.BlockSpec((1, d), lambda i: (0, 0)),
                  pl.BlockSpec((1, d), lambda i: (0, 0))],
        out_specs=pl.BlockSpec((tm, d), lambda i: (i, 0)),
        out_shape=jax.ShapeDtypeStruct((s, d), F32),
        compiler_params=_params("parallel"),
        name="proj_ln",
    )(o, w, x, g.reshape(1, d), b.reshape(1, d))


def _first_lane_eq(vals, target, lane):
    return jnp.min(jnp.where(vals == target, lane, LANES), axis=1, keepdims=True)


def _router_kernel(x_ref, w_ref, b_ref, route_ref, cnt_ref, carry_ref):
    @pl.when(pl.program_id(0) == 0)
    def _():
        carry_ref[...] = jnp.zeros_like(carry_ref)

    tm = x_ref.shape[0]
    neg = -jnp.inf
    logits = _dot_hi_lo(x_ref[...], w_ref) + b_ref[...]
    lane = lax.broadcasted_iota(jnp.int32, (tm, LANES), 1)

    gl = jnp.where(lane < N_GROUPS, logits, neg)
    gmax = jnp.max(gl, axis=1, keepdims=True)
    gsum = jnp.sum(jnp.exp(gl - gmax), axis=1, keepdims=True)
    grp_p = 1.0 / gsum
    grp = _first_lane_eq(gl, gmax, lane)

    lo = N_GROUPS + grp * EXPERTS_PER_GROUP
    el = jnp.where((lane >= lo) & (lane < lo + EXPERTS_PER_GROUP), logits, neg)
    emax = jnp.max(el, axis=1, keepdims=True)
    esum = jnp.sum(jnp.exp(el - emax), axis=1, keepdims=True)
    idx1 = _first_lane_eq(el, emax, lane)
    el2 = jnp.where(lane == idx1, neg, el)
    emax2 = jnp.max(el2, axis=1, keepdims=True)
    idx2 = _first_lane_eq(el2, emax2, lane)
    p1 = 1.0 / esum
    p2 = jnp.exp(emax2 - emax) / esum
    psum = p1 + p2
    g1 = grp_p * (p1 / psum)
    g2 = grp_p * (p2 / psum)

    oh1 = lane == idx1
    oh2 = lane == idx2
    both = jnp.where(oh1 | oh2, 1.0, 0.0)
    before = jnp.dot(_tri(tm, strict=True), both.astype(BF16), preferred_element_type=F32)
    before = before + carry_ref[...]
    r1 = jnp.sum(jnp.where(oh1, before, 0.0), axis=1, keepdims=True)
    r2 = jnp.sum(jnp.where(oh2, before, 0.0), axis=1, keepdims=True)
    carry_ref[...] = carry_ref[...] + jnp.sum(both, axis=0, keepdims=True)
    cnt_ref[...] = carry_ref[...]

    e1 = (idx1 - N_GROUPS).astype(F32)
    e2 = (idx2 - N_GROUPS).astype(F32)
    out = jnp.where(lane == 0, e1, 0.0)
    out = jnp.where(lane == 1, e2, out)
    out = jnp.where(lane == 2, g1, out)
    out = jnp.where(lane == 3, g2, out)
    out = jnp.where(lane == 4, r1, out)
    out = jnp.where(lane == 5, r2, out)
    route_ref[...] = out


def _router(x, w_group, b_group, w_expert, b_expert):
    s, d = x.shape
    tm = min(ROW_TILE, s)
    whl = _pack_hi_lo(jnp.concatenate([w_group, w_expert], axis=1))
    nb = N_GROUPS + N_EXPERTS
    bias = jnp.zeros((1, LANES), F32).at[0, :nb].set(jnp.concatenate([b_group, b_expert]))
    return pl.pallas_call(
        _router_kernel,
        grid=(s // tm,),
        in_specs=[pl.BlockSpec((tm, d), lambda i: (i, 0)),
                  pl.BlockSpec((d, 2 * LANES), lambda i: (0, 0)),
                  pl.BlockSpec((1, LANES), lambda i: (0, 0))],
        out_specs=[pl.BlockSpec((tm, LANES), lambda i: (i, 0)),
                   pl.BlockSpec((1, LANES), lambda i: (0, 0))],
        out_shape=[jax.ShapeDtypeStruct((s, LANES), F32),
                   jax.ShapeDtypeStruct((1, LANES), F32)],
        scratch_shapes=[pltpu.VMEM((1, LANES), F32)],
        compiler_params=_params("arbitrary"),
        name="router",
    )(x, whl, bias)


def _expert_kernel(be_ref, nv_ref, tok_ref, x_hbm, wg_ref, wu_ref, wd_ref, y_ref,
                   xg_ref, wgb_ref, wub_ref, wdb_ref, sem):
    b = pl.program_id(0)
    n_valid = nv_ref[0]
    bm = xg_ref.shape[1]

    def row_copy(blk, slot, r):
        tok = tok_ref[blk * bm + r]
        return pltpu.make_async_copy(x_hbm.at[pl.ds(tok, 1), :],
                                     xg_ref.at[slot, pl.ds(r, 1), :], sem.at[slot])

    def start_gather(blk, slot):
        def body(r, carry):
            row_copy(blk, slot, r).start()
            return carry
        lax.fori_loop(0, bm, body, 0)

    def wait_gather(blk, slot):
        def body(r, carry):
            row_copy(blk, slot, r).wait()
            return carry
        lax.fori_loop(0, bm, body, 0)

    @pl.when((b == 0) & (n_valid > 0))
    def _():
        start_gather(0, 0)

    @pl.when(b + 1 < n_valid)
    def _():
        start_gather(b + 1, (b + 1) % 2)

    @pl.when(b < n_valid)
    def _():
        prev = be_ref[jnp.maximum(b - 1, 0)]

        @pl.when((b == 0) | (be_ref[b] != prev))
        def _():
            wgb_ref[...] = wg_ref[0, 0].astype(BF16)
            wub_ref[...] = wu_ref[0, 0].astype(BF16)
            wdb_ref[...] = wd_ref[0, 0].astype(BF16)

        slot = b % 2
        wait_gather(b, slot)
        xb = xg_ref[slot].astype(BF16)
        g = jnp.dot(xb, wgb_ref[...], preferred_element_type=F32)
        u = jnp.dot(xb, wub_ref[...], preferred_element_type=F32)
        hid = (g * jax.nn.sigmoid(g) * u).astype(BF16)
        y_ref[...] = jnp.dot(hid, wdb_ref[...], preferred_element_type=F32)

    @pl.when(b >= n_valid)
    def _():
        y_ref[...] = jnp.zeros_like(y_ref)


def _experts(x, slot_tok, blk_expert, n_valid, w_gate, w_up, w_down, layer):
    t, d = x.shape
    _, e, _, f = w_gate.shape
    p = slot_tok.shape[0]
    bm = MOE_BLOCK
    nb = p // bm

    def w_idx(b, be, nv, tok):
        return (layer, be[jnp.minimum(b, jnp.maximum(nv[0] - 1, 0))], 0, 0)

    grid_spec = pltpu.PrefetchScalarGridSpec(
        num_scalar_prefetch=3,
        grid=(nb,),
        in_specs=[pl.BlockSpec(memory_space=pl.ANY),
                  pl.BlockSpec((1, 1, d, f), w_idx),
                  pl.BlockSpec((1, 1, d, f), w_idx),
                  pl.BlockSpec((1, 1, f, d), w_idx)],
        out_specs=pl.BlockSpec((bm, d), lambda b, be, nv, tok: (b, 0)),
        scratch_shapes=[pltpu.VMEM((2, bm, d), F32),
                        pltpu.VMEM((d, f), BF16),
                        pltpu.VMEM((d, f), BF16),
                        pltpu.VMEM((f, d), BF16),
                        pltpu.SemaphoreType.DMA((2,))],
    )
    return pl.pallas_call(
        _expert_kernel,
        grid_spec=grid_spec,
        out_shape=jax.ShapeDtypeStruct((p, d), F32),
        compiler_params=_params("arbitrary"),
        name="moe_experts",
    )(blk_expert, n_valid, slot_tok, x, w_gate, w_up, w_down)


def _combine_ln_kernel(dest_ref, y_hbm, x_ref, route_ref, g_ref, b_ref, o_ref, yg_ref, sem):
    i = pl.program_id(0)
    n = pl.num_programs(0)
    tm = x_ref.shape[0]

    def row_copy(step, slot, r):
        src = dest_ref[step * (2 * tm) + r]
        return pltpu.make_async_copy(y_hbm.at[pl.ds(src, 1), :],
                                     yg_ref.at[slot, pl.ds(r, 1), :], sem.at[slot])

    def start_gather(step, slot):
        def body(r, carry):
            row_copy(step, slot, r).start()
            return carry
        lax.fori_loop(0, 2 * tm, body, 0)

    def wait_gather(step, slot):
        def body(r, carry):
            row_copy(step, slot, r).wait()
            return carry
        lax.fori_loop(0, 2 * tm, body, 0)

    @pl.when(i == 0)
    def _():
        start_gather(0, 0)

    @pl.when(i + 1 < n)
    def _():
        start_gather(i + 1, (i + 1) % 2)

    slot = i % 2
    wait_gather(i, slot)
    route = route_ref[...]
    lane = lax.broadcasted_iota(jnp.int32, route.shape, 1)
    g1 = jnp.sum(jnp.where(lane == 2, route, 0.0), axis=1, keepdims=True)
    g2 = jnp.sum(jnp.where(lane == 3, route, 0.0), axis=1, keepdims=True)
    y1 = yg_ref[slot, pl.ds(0, tm), :]
    y2 = yg_ref[slot, pl.ds(tm, tm), :]
    z = DEEPNORM_ALPHA * x_ref[...] + (y1 * g1 + y2 * g2)
    o_ref[...] = _layer_norm_rows(z, g_ref[...], b_ref[...])


def _combine_ln(y, dest, x, route, g, b):
    t, d = x.shape
    tm = min(COMB_TM, t)
    grid_spec = pltpu.PrefetchScalarGridSpec(
        num_scalar_prefetch=1,
        grid=(t // tm,),
        in_specs=[pl.BlockSpec(memory_space=pl.ANY),
                  pl.BlockSpec((tm, d), lambda i, dest: (i, 0)),
                  pl.BlockSpec((tm, LANES), lambda i, dest: (i, 0)),
                  pl.BlockSpec((1, d), lambda i, dest: (0, 0)),
                  pl.BlockSpec((1, d), lambda i, dest: (0, 0))],
        out_specs=pl.BlockSpec((tm, d), lambda i, dest: (i, 0)),
        scratch_shapes=[pltpu.VMEM((2, 2 * tm, d), F32),
                        pltpu.SemaphoreType.DMA((2,))],
    )
    return pl.pallas_call(
        _combine_ln_kernel,
        grid_spec=grid_spec,
        out_shape=jax.ShapeDtypeStruct((t, d), F32),
        compiler_params=_params("arbitrary"),
        name="moe_combine_ln",
    )(dest, y, x, route, g.reshape(1, d), b.reshape(1, d))


def _hier_moe_ln(x, w_group, b_group, w_expert, b_expert, w_gate, w_up, w_down, layer, ln_g, ln_b):
    t, d = x.shape
    bm = MOE_BLOCK
    route, counts = _router(x, w_group, b_group, w_expert, b_expert)
    eid = route[:, 0:2].astype(jnp.int32)
    rank = route[:, 4:6].astype(jnp.int32)
    counts = counts[0, N_GROUPS:N_GROUPS + N_EXPERTS].astype(jnp.int32)
    padded = ((counts + bm - 1) // bm) * bm
    pad_end = jnp.cumsum(padded)
    pad_start = pad_end - padded
    dest = pad_start[eid] + rank
    p = ((t * TOP_K + bm - 1) // bm) * bm + N_EXPERTS * bm
    tok = jnp.broadcast_to(jnp.arange(t, dtype=jnp.int32)[:, None], (t, TOP_K))
    slot_tok = jnp.zeros((p,), jnp.int32).at[dest.reshape(-1)].set(tok.reshape(-1))
    nb = p // bm
    blk_start = jnp.arange(nb, dtype=jnp.int32) * bm
    blk_expert = jnp.minimum(jnp.searchsorted(pad_end, blk_start, side="right"),
                             N_EXPERTS - 1).astype(jnp.int32)
    n_valid = (pad_end[-1] // bm).astype(jnp.int32).reshape(1)
    y = _experts(x, slot_tok, blk_expert, n_valid, w_gate, w_up, w_down, layer)
    tm = min(COMB_TM, t)
    dest_tiles = dest.reshape(t // tm, tm, TOP_K).transpose(0, 2, 1).reshape(-1)
    return _combine_ln(y, dest_tiles, x, route, ln_g, ln_b)


def _gla_gate_kernel(x_ref, wl_ref, wu_ref, b_ref, la_ref):
    g_low = _dot_hi_lo(x_ref[...], wl_ref)
    n = wu_ref.shape[1] // 2
    gh, gl = _split2(g_low)
    a = jnp.dot(gh, wu_ref[...], preferred_element_type=F32)
    c = jnp.dot(gl, wu_ref[:, :n], preferred_element_type=F32)
    logit = a[:, :n] + a[:, n:] + c + b_ref[...]
    la_ref[...] = _log_sigmoid(logit) / GLA_GATE_TAU


def _gla_gates(x, w_low, w_gate_up, b_gate):
    s, d = x.shape
    rank, dk = w_gate_up.shape
    ts = min(ROW_TILE, s)
    wl = _pack_hi_lo(w_low)
    wu_pad = jnp.zeros((LANES, dk), F32).at[:rank].set(w_gate_up)
    wu = jnp.concatenate(_split2(wu_pad), axis=1)
    return pl.pallas_call(
        _gla_gate_kernel,
        grid=(s // ts,),
        in_specs=[pl.BlockSpec((ts, d), lambda i: (i, 0)),
                  pl.BlockSpec((d, 2 * LANES), lambda i: (0, 0)),
                  pl.BlockSpec((LANES, 2 * dk), lambda i: (0, 0)),
                  pl.BlockSpec((1, dk), lambda i: (0, 0))],
        out_specs=pl.BlockSpec((ts, dk), lambda i: (i, 0)),
        out_shape=jax.ShapeDtypeStruct((s, dk), F32),
        compiler_params=_params("parallel"),
        name="gla_gates",
    )(x, wl, wu, b_gate.reshape(1, dk))


def _gla_kernel(q_ref, k_ref, v_ref, la_ref, r_ref, g_ref, o_ref, state_ref):
    @pl.when(pl.program_id(1) == 0)
    def _():
        state_ref[...] = jnp.zeros_like(state_ref)

    rows, dk = q_ref.shape
    cs = GLA_CHUNK
    tri = _tri(cs)
    rr = lax.broadcasted_iota(jnp.int32, (cs, cs), 0)
    cc = lax.broadcasted_iota(jnp.int32, (cs, cs), 1)
    causal = rr >= cc
    for ci in range(rows // cs):
        sl = pl.ds(ci * cs, cs)
        b = _cumsum_rows(la_ref[sl, :], tri)
        b_mid = b[cs // 2 - 1:cs // 2, :]
        b_last = b[cs - 1:cs, :]
        q = q_ref[sl, :].astype(F32) * (dk ** -0.5)
        k = k_ref[sl, :].astype(F32)
        v = v_ref[sl, :]
        qa = (q * jnp.exp(b - b_mid)).astype(BF16)
        ka = (k * jnp.exp(b_mid - b)).astype(BF16)
        a = lax.dot_general(qa, ka, (((1,), (1,)), ((), ())), preferred_element_type=F32)
        a = jnp.where(causal, a, 0.0)
        o = jnp.dot(a.astype(BF16), v, preferred_element_type=F32)
        state = state_ref[...]
        o = o + jnp.dot((q * jnp.exp(b)).astype(BF16), state.astype(BF16), preferred_element_type=F32)
        k_end_t = (k * jnp.exp(b_last - b)).T.astype(BF16)
        decay = jnp.exp(b.T[:, cs - 1:cs])
        state_ref[...] = decay * state + jnp.dot(k_end_t, v, preferred_element_type=F32)
        o = o * lax.rsqrt(jnp.mean(o * o, axis=-1, keepdims=True) + RMS_EPS) * g_ref[...]
        r = r_ref[sl, :].astype(F32)
        o_ref[sl, :] = (o * (r * jax.nn.sigmoid(r))).astype(o_ref.dtype)


def _gla(proj, la, norm_g):
    s = proj.shape[0]
    dk_all = la.shape[1]
    nh = GLA_HEADS
    dk = dk_all // nh
    dv = norm_g.shape[0]
    rows = min(GLA_ROWS, s)
    kq, kv = dk_all // dk, 2 * dk_all // dv
    kr = kv + nh
    return pl.pallas_call(
        _gla_kernel,
        grid=(nh, s // rows),
        in_specs=[pl.BlockSpec((rows, dk), lambda h, i: (i, h)),
                  pl.BlockSpec((rows, dk), lambda h, i: (i, kq + h)),
                  pl.BlockSpec((rows, dv), lambda h, i: (i, kv + h)),
                  pl.BlockSpec((rows, dk), lambda h, i: (i, h)),
                  pl.BlockSpec((rows, dv), lambda h, i: (i, kr + h)),
                  pl.BlockSpec((1, dv), lambda h, i: (0, 0))],
        out_specs=pl.BlockSpec((rows, dv), lambda h, i: (i, h)),
        out_shape=jax.ShapeDtypeStruct((s, nh * dv), BF16),
        scratch_shapes=[pltpu.VMEM((dk, dv), F32)],
        compiler_params=_params("parallel", "arbitrary"),
        name="gla_chunks",
    )(proj, proj, proj, la, proj, norm_g.reshape(1, dv))


def kernel(x, fox_w_in, fox_b_f, fox_w_o, gla_w_in, gla_w_gate_up, gla_b_gate, gla_norm_g, gla_w_o,
           ln_mix_g, ln_mix_b, ln_ffn_g, ln_ffn_b, moe_w_group, moe_b_group, moe_w_expert,
           moe_b_expert, moe_w_gate, moe_w_up, moe_w_down):
    bsz, s, d = x.shape
    outs = []
    for bi in range(bsz):
        xt = x[bi]
        for i in range(DEPTH):
            j = i // 2
            if i % 2 == 0:
                w_in = fox_w_in[j]
                qkv = _matmul(xt, w_in[:, :3 * d].astype(BF16), BF16)
                c = _fox_gates(xt, w_in[:, 3 * d:], fox_b_f[j])
                o = _fox_attention(qkv, c, FOX_HEADS)
                w_o = fox_w_o[j]
            else:
                w_in = gla_w_in[j]
                n_main = w_in.shape[1] - gla_w_gate_up.shape[1]
                proj = _matmul(xt, w_in[:, :n_main].astype(BF16), BF16)
                la = _gla_gates(xt, w_in[:, n_main:], gla_w_gate_up[j], gla_b_gate[j])
                o = _gla(proj, la, gla_norm_g[j])
                w_o = gla_w_o[j]
            xt = _proj_ln(o, w_o.astype(BF16), xt, ln_mix_g[i], ln_mix_b[i])
            xt = _hier_moe_ln(xt, moe_w_group[i], moe_b_group[i], moe_w_expert[i], moe_b_expert[i],
                              moe_w_gate, moe_w_up, moe_w_down, i, ln_ffn_g[i], ln_ffn_b[i])
        outs.append(xt)
    return jnp.stack(outs, axis=0)
```

```python
import functools

import jax
import jax.numpy as jnp
from jax import lax
from jax.experimental import pallas as pl
from jax.experimental.pallas import tpu as pltpu

F32 = jnp.float32
BF16 = jnp.bfloat16

DEPTH = 2
FOX_HEADS = 16
FOX_HEAD_DIM = 128
GLA_HEADS = 4
GLA_CHUNK = 64
GLA_GATE_TAU = 16.0
N_GROUPS = 8
EXPERTS_PER_GROUP = 8
N_EXPERTS = N_GROUPS * EXPERTS_PER_GROUP
TOP_K = 2
DEEPNORM_ALPHA = (2 * DEPTH) ** 0.25
LN_EPS = 1e-5
RMS_EPS = 1e-6

LANES = 128
VMEM_LIMIT = 56 * 2**20

MM_TM, MM_TN = 1024, 1024
ROW_TILE = 512
ATT_TQ, ATT_TK = 1024, 512
GLA_ROWS = 512
MOE_BLOCK = 128
COMB_TM = 256


def _params(*sem):
    return pltpu.CompilerParams(dimension_semantics=sem, vmem_limit_bytes=VMEM_LIMIT)


def _split2(a):
    hi = a.astype(BF16)
    lo = (a - hi.astype(F32)).astype(BF16)
    return hi, lo


def _split3(a):
    hi = a.astype(BF16)
    r = a - hi.astype(F32)
    mid = r.astype(BF16)
    lo = (r - mid.astype(F32)).astype(BF16)
    return hi, mid, lo


def _pack_hi_lo(w, n_pad=LANES):
    k, n = w.shape
    wp = jnp.zeros((k, n_pad), F32).at[:, :n].set(w)
    hi, lo = _split2(wp)
    return jnp.concatenate([hi, lo], axis=1)


def _dot_hi_lo(x, whl_ref, n_pad=LANES):
    xh, xl = _split2(x)
    a = jnp.dot(xh, whl_ref[...], preferred_element_type=F32)
    b = jnp.dot(xl, whl_ref[:, :n_pad], preferred_element_type=F32)
    return a[:, :n_pad] + a[:, n_pad:] + b


def _log_sigmoid(x):
    return -(jnp.maximum(-x, 0.0) + jnp.log1p(jnp.exp(-jnp.abs(x))))


def _cumsum_rows(a, incl_tri):
    n = a.shape[1]
    parts = jnp.concatenate(_split3(a), axis=1)
    c = jnp.dot(incl_tri, parts, preferred_element_type=F32)
    return c[:, :n] + c[:, n:2 * n] + c[:, 2 * n:]


def _tri(n, strict=False):
    r = lax.broadcasted_iota(jnp.int32, (n, n), 0)
    c = lax.broadcasted_iota(jnp.int32, (n, n), 1)
    return jnp.where((r > c) if strict else (r >= c), 1.0, 0.0).astype(BF16)


def _layer_norm_rows(z, g, b):
    mu = jnp.mean(z, axis=-1, keepdims=True)
    d = z - mu
    var = jnp.mean(d * d, axis=-1, keepdims=True)
    return d * lax.rsqrt(var + LN_EPS) * g + b


def _mm_kernel(x_ref, w_ref, o_ref, xb_ref):
    @pl.when(pl.program_id(1) == 0)
    def _():
        xb_ref[...] = x_ref[...].astype(BF16)

    o_ref[...] = jnp.dot(xb_ref[...], w_ref[...], preferred_element_type=F32).astype(o_ref.dtype)


def _matmul(x, w, out_dtype):
    m, k = x.shape
    n = w.shape[1]
    tm, tn = min(MM_TM, m), min(MM_TN, n)
    return pl.pallas_call(
        _mm_kernel,
        grid=(m // tm, n // tn),
        in_specs=[pl.BlockSpec((tm, k), lambda i, j: (i, 0)),
                  pl.BlockSpec((k, tn), lambda i, j: (0, j))],
        out_specs=pl.BlockSpec((tm, tn), lambda i, j: (i, j)),
        out_shape=jax.ShapeDtypeStruct((m, n), out_dtype),
        scratch_shapes=[pltpu.VMEM((tm, k), BF16)],
        compiler_params=_params("parallel", "arbitrary"),
        name="dense_proj",
    )(x, w)


def _fox_gate_kernel(x_ref, w_ref, b_ref, c_ref, carry_ref):
    @pl.when(pl.program_id(0) == 0)
    def _():
        carry_ref[...] = jnp.zeros_like(carry_ref)

    ts = x_ref.shape[0]
    logits = _dot_hi_lo(x_ref[...], w_ref) + b_ref[...]
    log_f = _log_sigmoid(logits)
    c = _cumsum_rows(log_f, _tri(ts)) + carry_ref[...]
    carry_ref[...] = c[ts - 1:ts, :]
    c_ref[...] = c


def _fox_gates(x, w_f, b_f):
    s, d = x.shape
    h = w_f.shape[1]
    ts = min(ROW_TILE, s)
    whl = _pack_hi_lo(w_f)
    bias = jnp.zeros((1, LANES), F32).at[0, :h].set(b_f)
    return pl.pallas_call(
        _fox_gate_kernel,
        grid=(s // ts,),
        in_specs=[pl.BlockSpec((ts, d), lambda i: (i, 0)),
                  pl.BlockSpec((d, 2 * LANES), lambda i: (0, 0)),
                  pl.BlockSpec((1, LANES), lambda i: (0, 0))],
        out_specs=pl.BlockSpec((ts, LANES), lambda i: (i, 0)),
        out_shape=jax.ShapeDtypeStruct((s, LANES), F32),
        scratch_shapes=[pltpu.VMEM((1, LANES), F32)],
        compiler_params=_params("arbitrary"),
        name="fox_gates",
    )(x, whl, bias)


def _fox_attn_kernel(q_ref, k_ref, v_ref, c_ref, o_ref, kaug_ref, vt_ref, qt_ref, st0_ref, st1_ref,
                     p0_ref, p1_ref, alpha_ref, acc_ref, m_ref):
    h = pl.program_id(0)
    qi = pl.program_id(1)
    tq, dh = q_ref.shape
    s_len = k_ref.shape[0]
    tk = ATT_TK
    log2e = 1.4426950408889634
    scale = dh ** -0.5 * log2e

    def head_column(rows, n):
        lane = lax.broadcasted_iota(jnp.int32, (n, LANES), 1)
        return log2e * jnp.sum(jnp.where(lane == h, c_ref[rows, :], 0.0), axis=1, keepdims=True)

    def bias_columns(col, first, n):
        lane = lax.broadcasted_iota(jnp.int32, (n, LANES), 1)
        hi, mid, lo = _split3(col)
        ones_first = 3 - first
        out = jnp.where((lane >= ones_first) & (lane < ones_first + 3), 1.0, 0.0)
        out = jnp.where(lane == first, hi.astype(F32), out)
        out = jnp.where(lane == first + 1, mid.astype(F32), out)
        return jnp.where(lane == first + 2, lo.astype(F32), out).astype(BF16)

    @pl.when(qi == 0)
    def _():
        ones_row = jnp.where(lax.broadcasted_iota(jnp.int32, (16, tk), 0) == 0, 1.0, 0.0).astype(BF16)

        def build(ci, carry):
            rows = pl.ds(pl.multiple_of(ci * tk, tk), tk)
            kaug_ref[rows, :dh] = k_ref[rows, :]
            kaug_ref[rows, dh:] = bias_columns(-head_column(rows, tk), 0, tk)
            vt_ref[:dh, rows] = v_ref[rows, :].astype(F32).T.astype(BF16)
            vt_ref[dh:, rows] = ones_row
            return carry
        lax.fori_loop(0, s_len // tk, build, 0)

    q_rows = pl.ds(pl.multiple_of(qi * tq, tq), tq)
    q_aug = jnp.concatenate([q_ref[...].astype(F32) * scale,
                             bias_columns(head_column(q_rows, tq), 3, tq).astype(F32)], axis=1)
    qt_ref[...] = q_aug.T.astype(BF16)

    acc_ref[...] = jnp.zeros_like(acc_ref)
    m_ref[...] = jnp.full_like(m_ref, -jnp.inf)

    st_refs, p_refs = (st0_ref, st1_ref), (p0_ref, p1_ref)

    def scores(k_start, slot):
        st_refs[slot][...] = jnp.dot(kaug_ref[pl.ds(k_start, tk), :], qt_ref[...],
                                     preferred_element_type=F32)

    def softmax(slot, diag_offset):
        st = st_refs[slot][...]
        if diag_offset is not None:
            kr = lax.broadcasted_iota(jnp.int32, (tk, tq), 0) + diag_offset
            qc = lax.broadcasted_iota(jnp.int32, (tk, tq), 1)
            st = jnp.where(kr <= qc, st, -jnp.inf)
        m_prev = m_ref[...]
        m_new = jnp.maximum(m_prev, jnp.max(st, axis=0, keepdims=True))
        m_ref[...] = m_new
        p_refs[slot][...] = jnp.exp2(st - m_new).astype(BF16)
        alpha_ref[slot] = jnp.exp2(m_prev - m_new)

    def values(k_start, slot):
        acc_ref[...] = alpha_ref[slot] * acc_ref[...] + jnp.dot(
            vt_ref[:, pl.ds(k_start, tk)], p_refs[slot][...], preferred_element_type=F32)

    n_diag = tq // tk
    assert n_diag == 2
    d0 = pl.multiple_of(qi * tq, tk)
    d1 = pl.multiple_of(qi * tq + tk, tk)
    n_full = qi * n_diag

    def full_start(i):
        return pl.multiple_of(jnp.minimum(i, jnp.maximum(n_full - 1, 0)) * tk, tk)

    scores(d0, 0)
    scores(d1, 1)
    softmax(0, 0)
    scores(full_start(0), 0)
    softmax(1, tk)
    values(d0, 0)

    def pair(j, carry):
        i0 = 2 * j
        prev = jnp.where(j == 0, d1, (i0 - 1) * tk)
        scores(full_start(i0 + 1), 1)
        softmax(0, None)
        values(pl.multiple_of(prev, tk), 1)
        scores(full_start(i0 + 2), 0)
        softmax(1, None)
        values(pl.multiple_of(i0 * tk, tk), 0)
        return carry

    lax.fori_loop(0, qi, pair, 0)
    last = jnp.where(qi == 0, d1, (n_full - 1) * tk)
    values(pl.multiple_of(last, tk), 1)

    o_ref[...] = (acc_ref[:dh, :] / acc_ref[dh:dh + 1, :]).T.astype(o_ref.dtype)


def _fox_attention(qkv, c, n_heads):
    s = qkv.shape[0]
    dh = FOX_HEAD_DIM
    tq = min(ATT_TQ, s)
    hh = n_heads
    return pl.pallas_call(
        _fox_attn_kernel,
        grid=(hh, s // tq),
        in_specs=[pl.BlockSpec((tq, dh), lambda h, i: (i, h)),
                  pl.BlockSpec((s, dh), lambda h, i: (0, hh + h)),
                  pl.BlockSpec((s, dh), lambda h, i: (0, 2 * hh + h)),
                  pl.BlockSpec((s, LANES), lambda h, i: (0, 0))],
        out_specs=pl.BlockSpec((tq, dh), lambda h, i: (i, h)),
        out_shape=jax.ShapeDtypeStruct((s, hh * dh), BF16),
        scratch_shapes=[pltpu.VMEM((s, 2 * dh), BF16),
                        pltpu.VMEM((dh + 16, s), BF16),
                        pltpu.VMEM((2 * dh, tq), BF16),
                        pltpu.VMEM((ATT_TK, tq), F32),
                        pltpu.VMEM((ATT_TK, tq), F32),
                        pltpu.VMEM((ATT_TK, tq), BF16),
                        pltpu.VMEM((ATT_TK, tq), BF16),
                        pltpu.VMEM((2, 1, tq), F32),
                        pltpu.VMEM((dh + 16, tq), F32),
                        pltpu.VMEM((1, tq), F32)],
        compiler_params=_params("arbitrary", "arbitrary"),
        name="fox_attention",
    )(qkv, qkv, qkv, c)


def _first_lane_eq(vals, target, lane):
    return jnp.min(jnp.where(vals == target, lane, LANES), axis=1, keepdims=True)


def _route_rows(x, w_ref, b_ref, carry_ref):
    tm = x.shape[0]
    neg = -jnp.inf
    logits = _dot_hi_lo(x, w_ref) + b_ref[...]
    lane = lax.broadcasted_iota(jnp.int32, (tm, LANES), 1)

    gl = jnp.where(lane < N_GROUPS, logits, neg)
    gmax = jnp.max(gl, axis=1, keepdims=True)
    gsum = jnp.sum(jnp.exp(gl - gmax), axis=1, keepdims=True)
    grp_p = 1.0 / gsum
    grp = _first_lane_eq(gl, gmax, lane)

    lo = N_GROUPS + grp * EXPERTS_PER_GROUP
    el = jnp.where((lane >= lo) & (lane < lo + EXPERTS_PER_GROUP), logits, neg)
    emax = jnp.max(el, axis=1, keepdims=True)
    esum = jnp.sum(jnp.exp(el - emax), axis=1, keepdims=True)
    idx1 = _first_lane_eq(el, emax, lane)
    el2 = jnp.where(lane == idx1, neg, el)
    emax2 = jnp.max(el2, axis=1, keepdims=True)
    idx2 = _first_lane_eq(el2, emax2, lane)
    p1 = 1.0 / esum
    p2 = jnp.exp(emax2 - emax) / esum
    psum = p1 + p2
    g1 = grp_p * (p1 / psum)
    g2 = grp_p * (p2 / psum)

    oh1 = lane == idx1
    oh2 = lane == idx2
    both = jnp.where(oh1 | oh2, 1.0, 0.0)
    before = jnp.dot(_tri(tm, strict=True), both.astype(BF16), preferred_element_type=F32)
    before = before + carry_ref[...]
    r1 = jnp.sum(jnp.where(oh1, before, 0.0), axis=1, keepdims=True)
    r2 = jnp.sum(jnp.where(oh2, before, 0.0), axis=1, keepdims=True)
    carry_ref[...] = carry_ref[...] + jnp.sum(both, axis=0, keepdims=True)

    e1 = (idx1 - N_GROUPS).astype(F32)
    e2 = (idx2 - N_GROUPS).astype(F32)
    out = jnp.where(lane == 0, e1, 0.0)
    out = jnp.where(lane == 1, e2, out)
    out = jnp.where(lane == 2, g1, out)
    out = jnp.where(lane == 3, g2, out)
    out = jnp.where(lane == 4, r1, out)
    return jnp.where(lane == 5, r2, out)


def _proj_ln_route_kernel(o_ref, w_ref, x_ref, g_ref, b_ref, wr_ref, br_ref,
                          y_ref, route_ref, cnt_ref, carry_ref):
    @pl.when(pl.program_id(0) == 0)
    def _():
        carry_ref[...] = jnp.zeros_like(carry_ref)

    mix = jnp.dot(o_ref[...], w_ref[...], preferred_element_type=F32)
    y = _layer_norm_rows(DEEPNORM_ALPHA * x_ref[...] + mix, g_ref[...], b_ref[...])
    y_ref[...] = y
    route_ref[...] = _route_rows(y, wr_ref, br_ref, carry_ref)
    cnt_ref[...] = carry_ref[...]


def _proj_ln_route(o, w, x, g, b, w_group, b_group, w_expert, b_expert):
    s, d = x.shape
    kd = o.shape[1]
    tm = min(ROW_TILE, s)
    whl = _pack_hi_lo(jnp.concatenate([w_group, w_expert], axis=1))
    nb = N_GROUPS + N_EXPERTS
    bias = jnp.zeros((1, LANES), F32).at[0, :nb].set(jnp.concatenate([b_group, b_expert]))
    return pl.pallas_call(
        _proj_ln_route_kernel,
        grid=(s // tm,),
        in_specs=[pl.BlockSpec((tm, kd), lambda i: (i, 0)),
                  pl.BlockSpec((kd, d), lambda i: (0, 0)),
                  pl.BlockSpec((tm, d), lambda i: (i, 0)),
                  pl.BlockSpec((1, d), lambda i: (0, 0)),
                  pl.BlockSpec((1, d), lambda i: (0, 0)),
                  pl.BlockSpec((d, 2 * LANES), lambda i: (0, 0)),
                  pl.BlockSpec((1, LANES), lambda i: (0, 0))],
        out_specs=[pl.BlockSpec((tm, d), lambda i: (i, 0)),
                   pl.BlockSpec((tm, LANES), lambda i: (i, 0)),
                   pl.BlockSpec((1, LANES), lambda i: (0, 0))],
        out_shape=[jax.ShapeDtypeStruct((s, d), F32),
                   jax.ShapeDtypeStruct((s, LANES), F32),
                   jax.ShapeDtypeStruct((1, LANES), F32)],
        scratch_shapes=[pltpu.VMEM((1, LANES), F32)],
        compiler_params=_params("arbitrary"),
        name="proj_ln_route",
    )(o, w, x, g.reshape(1, d), b.reshape(1, d), whl, bias)


def _dispatch_kernel(dest_ref, cnt_ref, pstart_ref, nv_ref, x_ref, xs_hbm, zero_ref, sem, zsem):
    i = pl.program_id(0)
    tm = x_ref.shape[0]
    bm = MOE_BLOCK
    n_exp = cnt_ref.shape[0]
    n_blocks = xs_hbm.shape[0] // bm

    def pad_copies(e, fn):
        cnt = cnt_ref[e]
        pos = pstart_ref[e] + cnt

        def row(r, carry):
            fn(pltpu.make_async_copy(zero_ref.at[pl.ds(0, 1), :], xs_hbm.at[pl.ds(pos + r, 1), :], zsem))
            return carry
        lax.fori_loop(0, (bm - cnt % bm) % bm, row, 0)

    def tail_copy(g):
        return pltpu.make_async_copy(zero_ref, xs_hbm.at[pl.ds(pl.multiple_of(g * bm, bm), bm), :], zsem)

    def for_all_fills(fn):
        def per_expert(e, carry):
            pad_copies(e, fn)
            return carry
        lax.fori_loop(0, n_exp, per_expert, 0)

        def per_tail(g, carry):
            fn(tail_copy(g))
            return carry
        lax.fori_loop(nv_ref[0], n_blocks, per_tail, 0)

    @pl.when(i == 0)
    def _():
        zero_ref[...] = jnp.zeros_like(zero_ref)
        for_all_fills(lambda c: c.start())

    base = i * (TOP_K * tm)

    def body(r, carry):
        for k in range(TOP_K):
            dst = dest_ref[base + TOP_K * r + k]
            pltpu.make_async_copy(x_ref.at[pl.ds(r, 1), :], xs_hbm.at[pl.ds(dst, 1), :], sem).start()
        return carry
    lax.fori_loop(0, tm, body, 0, unroll=8)

    @pl.when(i == 0)
    def _():
        for_all_fills(lambda c: c.wait())

    for _ in range(TOP_K):
        pltpu.make_async_copy(x_ref, xs_hbm.at[pl.ds(0, tm), :], sem).wait()


def _dispatch(x, dest, cnt, pstart, n_valid, n_slots):
    t, d = x.shape
    tm = min(ROW_TILE, t)
    grid_spec = pltpu.PrefetchScalarGridSpec(
        num_scalar_prefetch=4,
        grid=(t // tm,),
        in_specs=[pl.BlockSpec((tm, d), lambda i, *_: (i, 0))],
        out_specs=pl.BlockSpec(memory_space=pl.ANY),
        scratch_shapes=[pltpu.VMEM((MOE_BLOCK, d), F32),
                        pltpu.SemaphoreType.DMA,
                        pltpu.SemaphoreType.DMA],
    )
    return pl.pallas_call(
        _dispatch_kernel,
        grid_spec=grid_spec,
        out_shape=jax.ShapeDtypeStruct((n_slots, d), F32),
        compiler_params=_params("arbitrary"),
        name="moe_dispatch",
    )(dest, cnt, pstart, n_valid, x)


def _expert_kernel(nblk_ref, bstart_ref, nv_ref, xs_hbm, wg_ref, wu_ref, wd_ref, y_hbm,
                   xin_ref, yout_ref, wgb_ref, wub_ref, wdb_ref, in_sem, out_sem):
    e = pl.program_id(0)
    n_valid = nv_ref[0]
    bm = MOE_BLOCK
    n_blocks = y_hbm.shape[0] // bm

    def in_copy(g, slot):
        return pltpu.make_async_copy(xs_hbm.at[pl.ds(pl.multiple_of(g * bm, bm), bm), :],
                                     xin_ref.at[slot], in_sem.at[slot])

    def out_copy(g, slot):
        return pltpu.make_async_copy(yout_ref.at[slot],
                                     y_hbm.at[pl.ds(pl.multiple_of(g * bm, bm), bm), :], out_sem.at[slot])

    @pl.when((e == 0) & (n_valid > 0))
    def _():
        in_copy(0, 0).start()

    nb = nblk_ref[e]
    g0 = bstart_ref[e]

    @pl.when(nb > 0)
    def _():
        wgb_ref[...] = wg_ref[0, 0].astype(BF16)
        wub_ref[...] = wu_ref[0, 0].astype(BF16)
        wdb_ref[...] = wd_ref[0, 0].astype(BF16)

        def block(j, carry):
            g = g0 + j
            slot = g % 2
            in_copy(g, slot).wait()

            @pl.when(g + 1 < n_valid)
            def _():
                in_copy(g + 1, 1 - slot).start()

            @pl.when(g >= 2)
            def _():
                out_copy(g - 2, slot).wait()

            xb = xin_ref[slot].astype(BF16)
            gate = jnp.dot(xb, wgb_ref[...], preferred_element_type=F32)
            up = jnp.dot(xb, wub_ref[...], preferred_element_type=F32)
            hid = (gate * jax.nn.sigmoid(gate) * up).astype(BF16)
            yout_ref[slot] = jnp.dot(hid, wdb_ref[...], preferred_element_type=F32)
            out_copy(g, slot).start()
            return carry
        lax.fori_loop(0, nb, block, 0)

    @pl.when(e == pl.num_programs(0) - 1)
    def _():
        for back in (2, 1):
            @pl.when(n_valid >= back)
            def _(back=back):
                out_copy(n_valid - back, (n_valid - back) % 2).wait()

        yout_ref[0] = jnp.zeros(yout_ref.shape[1:], yout_ref.dtype)

        def fill(g, carry):
            out_copy(g, 0).start()
            return carry
        lax.fori_loop(n_valid, n_blocks, fill, 0)

        def drain(g, carry):
            out_copy(g, 0).wait()
            return carry
        lax.fori_loop(n_valid, n_blocks, drain, 0)


def _experts(xs, nblk, bstart, n_valid, w_gate, w_up, w_down, layer):
    p, d = xs.shape
    _, n_exp, _, f = w_gate.shape
    bm = MOE_BLOCK

    def w_idx(e, *_):
        return (layer, e, 0, 0)

    grid_spec = pltpu.PrefetchScalarGridSpec(
        num_scalar_prefetch=3,
        grid=(n_exp,),
        in_specs=[pl.BlockSpec(memory_space=pl.ANY),
                  pl.BlockSpec((1, 1, d, f), w_idx),
                  pl.BlockSpec((1, 1, d, f), w_idx),
                  pl.BlockSpec((1, 1, f, d), w_idx)],
        out_specs=pl.BlockSpec(memory_space=pl.ANY),
        scratch_shapes=[pltpu.VMEM((2, bm, d), F32),
                        pltpu.VMEM((2, bm, d), F32),
                        pltpu.VMEM((d, f), BF16),
                        pltpu.VMEM((d, f), BF16),
                        pltpu.VMEM((f, d), BF16),
                        pltpu.SemaphoreType.DMA((2,)),
                        pltpu.SemaphoreType.DMA((2,))],
    )
    return pl.pallas_call(
        _expert_kernel,
        grid_spec=grid_spec,
        out_shape=jax.ShapeDtypeStruct((p, d), F32),
        compiler_params=_params("arbitrary"),
        name="moe_experts",
    )(nblk, bstart, n_valid, xs, w_gate, w_up, w_down)


def _combine_ln_kernel(dest_ref, y_hbm, x_ref, route_ref, g_ref, b_ref, o_ref, yg_ref, sem):
    i = pl.program_id(0)
    n = pl.num_programs(0)
    tm = x_ref.shape[0]

    def start_gather(step, slot):
        base = step * (TOP_K * tm)

        def body(r, carry):
            for k in range(TOP_K):
                src = dest_ref[base + TOP_K * r + k]
                pltpu.make_async_copy(y_hbm.at[pl.ds(src, 1), :],
                                      yg_ref.at[slot, pl.ds(k * tm + r, 1), :], sem.at[slot]).start()
            return carry
        lax.fori_loop(0, tm, body, 0, unroll=8)

    def wait_gather(slot):
        pltpu.make_async_copy(y_hbm.at[pl.ds(0, TOP_K * tm), :], yg_ref.at[slot], sem.at[slot]).wait()

    @pl.when(i == 0)
    def _():
        start_gather(0, 0)

    @pl.when(i + 1 < n)
    def _():
        start_gather(i + 1, (i + 1) % 2)

    slot = i % 2
    wait_gather(slot)
    route = route_ref[...]
    lane = lax.broadcasted_iota(jnp.int32, route.shape, 1)
    g1 = jnp.sum(jnp.where(lane == 2, route, 0.0), axis=1, keepdims=True)
    g2 = jnp.sum(jnp.where(lane == 3, route, 0.0), axis=1, keepdims=True)
    y1 = yg_ref[slot, pl.ds(0, tm), :]
    y2 = yg_ref[slot, pl.ds(tm, tm), :]
    z = DEEPNORM_ALPHA * x_ref[...] + (y1 * g1 + y2 * g2)
    o_ref[...] = _layer_norm_rows(z, g_ref[...], b_ref[...])


def _combine_ln(y, dest, x, route, g, b):
    t, d = x.shape
    tm = min(COMB_TM, t)
    grid_spec = pltpu.PrefetchScalarGridSpec(
        num_scalar_prefetch=1,
        grid=(t // tm,),
        in_specs=[pl.BlockSpec(memory_space=pl.ANY),
                  pl.BlockSpec((tm, d), lambda i, dest: (i, 0)),
                  pl.BlockSpec((tm, LANES), lambda i, dest: (i, 0)),
                  pl.BlockSpec((1, d), lambda i, dest: (0, 0)),
                  pl.BlockSpec((1, d), lambda i, dest: (0, 0))],
        out_specs=pl.BlockSpec((tm, d), lambda i, dest: (i, 0)),
        scratch_shapes=[pltpu.VMEM((2, 2 * tm, d), F32),
                        pltpu.SemaphoreType.DMA((2,))],
    )
    return pl.pallas_call(
        _combine_ln_kernel,
        grid_spec=grid_spec,
        out_shape=jax.ShapeDtypeStruct((t, d), F32),
        compiler_params=_params("arbitrary"),
        name="moe_combine_ln",
    )(dest, y, x, route, g.reshape(1, d), b.reshape(1, d))


def _moe_ffn_ln(x, route, counts, w_gate, w_up, w_down, layer, ln_g, ln_b):
    t, d = x.shape
    bm = MOE_BLOCK
    eid = route[:, 0:TOP_K].astype(jnp.int32)
    rank = route[:, 4:4 + TOP_K].astype(jnp.int32)
    cnt = counts[0, N_GROUPS:N_GROUPS + N_EXPERTS].astype(jnp.int32)
    nblk = (cnt + bm - 1) // bm
    bend = jnp.cumsum(nblk)
    bstart = bend - nblk
    pstart = bstart * bm
    onehot = eid[:, :, None] == jnp.arange(N_EXPERTS, dtype=jnp.int32)
    dest = (jnp.sum(jnp.where(onehot, pstart, 0), axis=-1) + rank).reshape(-1)
    n_valid = bend[-1:]
    n_slots = (-(-t * TOP_K // bm) + N_EXPERTS) * bm
    xs = _dispatch(x, dest, cnt, pstart, n_valid, n_slots)
    y = _experts(xs, nblk, bstart, n_valid, w_gate, w_up, w_down, layer)
    return _combine_ln(y, dest, x, route, ln_g, ln_b)


def _gla_gate_kernel(x_ref, wl_ref, wu_ref, b_ref, la_ref):
    g_low = _dot_hi_lo(x_ref[...], wl_ref)
    n = wu_ref.shape[1] // 2
    gh, gl = _split2(g_low)
    a = jnp.dot(gh, wu_ref[...], preferred_element_type=F32)
    c = jnp.dot(gl, wu_ref[:, :n], preferred_element_type=F32)
    logit = a[:, :n] + a[:, n:] + c + b_ref[...]
    la_ref[...] = _log_sigmoid(logit) / GLA_GATE_TAU


def _gla_gates(x, w_low, w_gate_up, b_gate):
    s, d = x.shape
    rank, dk = w_gate_up.shape
    ts = min(ROW_TILE, s)
    wl = _pack_hi_lo(w_low)
    wu_pad = jnp.zeros((LANES, dk), F32).at[:rank].set(w_gate_up)
    wu = jnp.concatenate(_split2(wu_pad), axis=1)
    return pl.pallas_call(
        _gla_gate_kernel,
        grid=(s // ts,),
        in_specs=[pl.BlockSpec((ts, d), lambda i: (i, 0)),
                  pl.BlockSpec((d, 2 * LANES), lambda i: (0, 0)),
                  pl.BlockSpec((LANES, 2 * dk), lambda i: (0, 0)),
                  pl.BlockSpec((1, dk), lambda i: (0, 0))],
        out_specs=pl.BlockSpec((ts, dk), lambda i: (i, 0)),
        out_shape=jax.ShapeDtypeStruct((s, dk), F32),
        compiler_params=_params("parallel"),
        name="gla_gates",
    )(x, wl, wu, b_gate.reshape(1, dk))


def _gla_kernel(q_ref, k_ref, v_ref, la_ref, r_ref, g_ref, o_ref, state_ref):
    @pl.when(pl.program_id(1) == 0)
    def _():
        state_ref[...] = jnp.zeros_like(state_ref)

    rows, dk = q_ref.shape
    cs = GLA_CHUNK
    tri = _tri(cs)
    rr = lax.broadcasted_iota(jnp.int32, (cs, cs), 0)
    cc = lax.broadcasted_iota(jnp.int32, (cs, cs), 1)
    causal = rr >= cc
    for ci in range(rows // cs):
        sl = pl.ds(ci * cs, cs)
        b = _cumsum_rows(la_ref[sl, :], tri)
        b_mid = b[cs // 2 - 1:cs // 2, :]
        b_last = b[cs - 1:cs, :]
        q = q_ref[sl, :].astype(F32) * (dk ** -0.5)
        k = k_ref[sl, :].astype(F32)
        v = v_ref[sl, :]
        qa = (q * jnp.exp(b - b_mid)).astype(BF16)
        ka = (k * jnp.exp(b_mid - b)).astype(BF16)
        a = lax.dot_general(qa, ka, (((1,), (1,)), ((), ())), preferred_element_type=F32)
        a = jnp.where(causal, a, 0.0)
        o = jnp.dot(a.astype(BF16), v, preferred_element_type=F32)
        state = state_ref[...]
        o = o + jnp.dot((q * jnp.exp(b)).astype(BF16), state.astype(BF16), preferred_element_type=F32)
        k_end_t = (k * jnp.exp(b_last - b)).T.astype(BF16)
        decay = jnp.exp(b.T[:, cs - 1:cs])
        state_ref[...] = decay * state + jnp.dot(k_end_t, v, preferred_element_type=F32)
        o = o * lax.rsqrt(jnp.mean(o * o, axis=-1, keepdims=True) + RMS_EPS) * g_ref[...]
        r = r_ref[sl, :].astype(F32)
        o_ref[sl, :] = (o * (r * jax.nn.sigmoid(r))).astype(o_ref.dtype)


def _gla(proj, la, norm_g):
    s = proj.shape[0]
    dk_all = la.shape[1]
    nh = GLA_HEADS
    dk = dk_all // nh
    dv = norm_g.shape[0]
    rows = min(GLA_ROWS, s)
    kq, kv = dk_all // dk, 2 * dk_all // dv
    kr = kv + nh
    return pl.pallas_call(
        _gla_kernel,
        grid=(nh, s // rows),
        in_specs=[pl.BlockSpec((rows, dk), lambda h, i: (i, h)),
                  pl.BlockSpec((rows, dk), lambda h, i: (i, kq + h)),
                  pl.BlockSpec((rows, dv), lambda h, i: (i, kv + h)),
                  pl.BlockSpec((rows, dk), lambda h, i: (i, h)),
                  pl.BlockSpec((rows, dv), lambda h, i: (i, kr + h)),
                  pl.BlockSpec((1, dv), lambda h, i: (0, 0))],
        out_specs=pl.BlockSpec((rows, dv), lambda h, i: (i, h)),
        out_shape=jax.ShapeDtypeStruct((s, nh * dv), BF16),
        scratch_shapes=[pltpu.VMEM((dk, dv), F32)],
        compiler_params=_params("parallel", "arbitrary"),
        name="gla_chunks",
    )(proj, proj, proj, la, proj, norm_g.reshape(1, dv))


def kernel(x, fox_w_in, fox_b_f, fox_w_o, gla_w_in, gla_w_gate_up, gla_b_gate, gla_norm_g, gla_w_o,
           ln_mix_g, ln_mix_b, ln_ffn_g, ln_ffn_b, moe_w_group, moe_b_group, moe_w_expert,
           moe_b_expert, moe_w_gate, moe_w_up, moe_w_down):
    bsz, s, d = x.shape
    outs = []
    for bi in range(bsz):
        xt = x[bi]
        for i in range(DEPTH):
            j = i // 2
            if i % 2 == 0:
                w_in = fox_w_in[j]
                qkv = _matmul(xt, w_in[:, :3 * d].astype(BF16), BF16)
                c = _fox_gates(xt, w_in[:, 3 * d:], fox_b_f[j])
                o = _fox_attention(qkv, c, FOX_HEADS)
                w_o = fox_w_o[j]
            else:
                w_in = gla_w_in[j]
                n_main = w_in.shape[1] - gla_w_gate_up.shape[1]
                proj = _matmul(xt, w_in[:, :n_main].astype(BF16), BF16)
                la = _gla_gates(xt, w_in[:, n_main:], gla_w_gate_up[j], gla_b_gate[j])
                o = _gla(proj, la, gla_norm_g[j])
                w_o = gla_w_o[j]
            xt, route, counts = _proj_ln_route(o, w_o.astype(BF16), xt, ln_mix_g[i], ln_mix_b[i],
                                               moe_w_group[i], moe_b_group[i], moe_w_expert[i], moe_b_expert[i])
            xt = _moe_ffn_ln(xt, route, counts, moe_w_gate, moe_w_up, moe_w_down, i, ln_ffn_g[i], ln_ffn_b[i])
        outs.append(xt)
    return jnp.stack(outs, axis=0)
```

```python
import functools

import jax
import jax.numpy as jnp
from jax import lax
from jax.experimental import pallas as pl
from jax.experimental.pallas import tpu as pltpu

F32 = jnp.float32
BF16 = jnp.bfloat16

DEPTH = 2
FOX_HEADS = 16
FOX_HEAD_DIM = 128
GLA_HEADS = 4
GLA_CHUNK = 64
GLA_GATE_TAU = 16.0
N_GROUPS = 8
EXPERTS_PER_GROUP = 8
N_EXPERTS = N_GROUPS * EXPERTS_PER_GROUP
TOP_K = 2
DEEPNORM_ALPHA = (2 * DEPTH) ** 0.25
LN_EPS = 1e-5
RMS_EPS = 1e-6

LANES = 128
VMEM_LIMIT = 56 * 2**20

MM_TM, MM_TN = 1024, 1024
ROW_TILE = 512
ATT_TQ, ATT_TK = 1024, 512
GLA_ROWS = 512
MOE_BLOCK = 128
COMB_TM = 256


def _params(*sem):
    return pltpu.CompilerParams(dimension_semantics=sem, vmem_limit_bytes=VMEM_LIMIT)


def _split2(a):
    hi = a.astype(BF16)
    lo = (a - hi.astype(F32)).astype(BF16)
    return hi, lo


def _split3(a):
    hi = a.astype(BF16)
    r = a - hi.astype(F32)
    mid = r.astype(BF16)
    lo = (r - mid.astype(F32)).astype(BF16)
    return hi, mid, lo


def _pack_hi_lo(w, n_pad=LANES):
    k, n = w.shape
    wp = jnp.zeros((k, n_pad), F32).at[:, :n].set(w)
    hi, lo = _split2(wp)
    return jnp.concatenate([hi, lo], axis=1)


def _dot_hi_lo(x, whl_ref, n_pad=LANES):
    xh, xl = _split2(x)
    a = jnp.dot(xh, whl_ref[...], preferred_element_type=F32)
    b = jnp.dot(xl, whl_ref[:, :n_pad], preferred_element_type=F32)
    return a[:, :n_pad] + a[:, n_pad:] + b


def _log_sigmoid(x):
    return -(jnp.maximum(-x, 0.0) + jnp.log1p(jnp.exp(-jnp.abs(x))))


def _cumsum_rows(a, incl_tri):
    n = a.shape[1]
    parts = jnp.concatenate(_split3(a), axis=1)
    c = jnp.dot(incl_tri, parts, preferred_element_type=F32)
    return c[:, :n] + c[:, n:2 * n] + c[:, 2 * n:]


def _tri(n, strict=False):
    r = lax.broadcasted_iota(jnp.int32, (n, n), 0)
    c = lax.broadcasted_iota(jnp.int32, (n, n), 1)
    return jnp.where((r > c) if strict else (r >= c), 1.0, 0.0).astype(BF16)


def _layer_norm_rows(z, g, b):
    mu = jnp.mean(z, axis=-1, keepdims=True)
    d = z - mu
    var = jnp.mean(d * d, axis=-1, keepdims=True)
    return d * lax.rsqrt(var + LN_EPS) * g + b


def _mm_kernel(x_ref, w_ref, o_ref, wb_ref):
    @pl.when(pl.program_id(1) == 0)
    def _():
        wb_ref[...] = w_ref[...].astype(BF16)

    o_ref[...] = jnp.dot(x_ref[...].astype(BF16), wb_ref[...],
                         preferred_element_type=F32).astype(o_ref.dtype)


def _matmul(x, w, n, out_dtype):
    m, k = x.shape
    tm, tn = min(MM_TM, m), min(MM_TN, n)
    return pl.pallas_call(
        _mm_kernel,
        grid=(n // tn, m // tm),
        in_specs=[pl.BlockSpec((tm, k), lambda j, i: (i, 0)),
                  pl.BlockSpec((k, tn), lambda j, i: (0, j))],
        out_specs=pl.BlockSpec((tm, tn), lambda j, i: (i, j)),
        out_shape=jax.ShapeDtypeStruct((m, n), out_dtype),
        scratch_shapes=[pltpu.VMEM((k, tn), BF16)],
        compiler_params=_params("arbitrary", "arbitrary"),
        name="dense_proj",
    )(x, w)


def _fox_gate_kernel(x_ref, w_ref, b_ref, c_ref, carry_ref):
    @pl.when(pl.program_id(0) == 0)
    def _():
        carry_ref[...] = jnp.zeros_like(carry_ref)

    ts = x_ref.shape[0]
    logits = _dot_hi_lo(x_ref[...], w_ref) + b_ref[...]
    log_f = _log_sigmoid(logits)
    c = _cumsum_rows(log_f, _tri(ts)) + carry_ref[...]
    carry_ref[...] = c[ts - 1:ts, :]
    c_ref[...] = c


def _fox_gates(x, w_f, b_f):
    s, d = x.shape
    h = w_f.shape[1]
    ts = min(ROW_TILE, s)
    whl = _pack_hi_lo(w_f)
    bias = jnp.zeros((1, LANES), F32).at[0, :h].set(b_f)
    return pl.pallas_call(
        _fox_gate_kernel,
        grid=(s // ts,),
        in_specs=[pl.BlockSpec((ts, d), lambda i: (i, 0)),
                  pl.BlockSpec((d, 2 * LANES), lambda i: (0, 0)),
                  pl.BlockSpec((1, LANES), lambda i: (0, 0))],
        out_specs=pl.BlockSpec((ts, LANES), lambda i: (i, 0)),
        out_shape=jax.ShapeDtypeStruct((s, LANES), F32),
        scratch_shapes=[pltpu.VMEM((1, LANES), F32)],
        compiler_params=_params("arbitrary"),
        name="fox_gates",
    )(x, whl, bias)


def _fox_attn_kernel(q_ref, k_ref, v_ref, c_ref, o_ref, kaug_ref, vt_ref, qt_ref, st0_ref, st1_ref,
                     p0_ref, p1_ref, alpha_ref, acc_ref, m_ref):
    h = pl.program_id(0)
    qi = pl.program_id(1)
    tq, dh = q_ref.shape
    s_len = k_ref.shape[0]
    tk = ATT_TK
    log2e = 1.4426950408889634
    scale = dh ** -0.5 * log2e

    def head_column(rows, n):
        lane = lax.broadcasted_iota(jnp.int32, (n, LANES), 1)
        return log2e * jnp.sum(jnp.where(lane == h, c_ref[rows, :], 0.0), axis=1, keepdims=True)

    def bias_columns(col, first, n):
        lane = lax.broadcasted_iota(jnp.int32, (n, LANES), 1)
        hi, mid, lo = _split3(col)
        ones_first = 3 - first
        out = jnp.where((lane >= ones_first) & (lane < ones_first + 3), 1.0, 0.0)
        out = jnp.where(lane == first, hi.astype(F32), out)
        out = jnp.where(lane == first + 1, mid.astype(F32), out)
        return jnp.where(lane == first + 2, lo.astype(F32), out).astype(BF16)

    @pl.when(qi == 0)
    def _():
        ones_row = jnp.where(lax.broadcasted_iota(jnp.int32, (16, tk), 0) == 0, 1.0, 0.0).astype(BF16)

        def build(ci, carry):
            rows = pl.ds(pl.multiple_of(ci * tk, tk), tk)
            kaug_ref[rows, :dh] = k_ref[rows, :]
            kaug_ref[rows, dh:] = bias_columns(-head_column(rows, tk), 0, tk)
            vt_ref[:dh, rows] = v_ref[rows, :].astype(F32).T.astype(BF16)
            vt_ref[dh:, rows] = ones_row
            return carry
        lax.fori_loop(0, s_len // tk, build, 0)

    q_rows = pl.ds(pl.multiple_of(qi * tq, tq), tq)
    q_aug = jnp.concatenate([q_ref[...].astype(F32) * scale,
                             bias_columns(head_column(q_rows, tq), 3, tq).astype(F32)], axis=1)
    qt_ref[...] = q_aug.T.astype(BF16)

    acc_ref[...] = jnp.zeros_like(acc_ref)
    m_ref[...] = jnp.full_like(m_ref, -jnp.inf)

    st_refs, p_refs = (st0_ref, st1_ref), (p0_ref, p1_ref)

    def scores(k_start, slot):
        st_refs[slot][...] = jnp.dot(kaug_ref[pl.ds(k_start, tk), :], qt_ref[...],
                                     preferred_element_type=F32)

    def softmax(slot, diag_offset):
        st = st_refs[slot][...]
        if diag_offset is not None:
            kr = lax.broadcasted_iota(jnp.int32, (tk, tq), 0) + diag_offset
            qc = lax.broadcasted_iota(jnp.int32, (tk, tq), 1)
            st = jnp.where(kr <= qc, st, -jnp.inf)
        m_prev = m_ref[...]
        m_new = jnp.maximum(m_prev, jnp.max(st, axis=0, keepdims=True))
        m_ref[...] = m_new
        p_refs[slot][...] = jnp.exp2(st - m_new).astype(BF16)
        alpha_ref[slot] = jnp.exp2(m_prev - m_new)

    def values(k_start, slot):
        acc_ref[...] = alpha_ref[slot] * acc_ref[...] + jnp.dot(
            vt_ref[:, pl.ds(k_start, tk)], p_refs[slot][...], preferred_element_type=F32)

    n_diag = tq // tk
    assert n_diag == 2
    d0 = pl.multiple_of(qi * tq, tk)
    d1 = pl.multiple_of(qi * tq + tk, tk)
    n_full = qi * n_diag

    def full_start(i):
        return pl.multiple_of(jnp.minimum(i, jnp.maximum(n_full - 1, 0)) * tk, tk)

    scores(d0, 0)
    scores(d1, 1)
    softmax(0, 0)
    scores(full_start(0), 0)
    softmax(1, tk)
    values(d0, 0)

    def pair(j, carry):
        i0 = 2 * j
        prev = jnp.where(j == 0, d1, (i0 - 1) * tk)
        scores(full_start(i0 + 1), 1)
        softmax(0, None)
        values(pl.multiple_of(prev, tk), 1)
        scores(full_start(i0 + 2), 0)
        softmax(1, None)
        values(pl.multiple_of(i0 * tk, tk), 0)
        return carry

    lax.fori_loop(0, qi, pair, 0)
    last = jnp.where(qi == 0, d1, (n_full - 1) * tk)
    values(pl.multiple_of(last, tk), 1)

    o_ref[...] = (acc_ref[:dh, :] / acc_ref[dh:dh + 1, :]).T.astype(o_ref.dtype)


def _fox_attention(qkv, c, n_heads):
    s = qkv.shape[0]
    dh = FOX_HEAD_DIM
    tq = min(ATT_TQ, s)
    hh = n_heads
    return pl.pallas_call(
        _fox_attn_kernel,
        grid=(hh, s // tq),
        in_specs=[pl.BlockSpec((tq, dh), lambda h, i: (i, h)),
                  pl.BlockSpec((s, dh), lambda h, i: (0, hh + h)),
                  pl.BlockSpec((s, dh), lambda h, i: (0, 2 * hh + h)),
                  pl.BlockSpec((s, LANES), lambda h, i: (0, 0))],
        out_specs=pl.BlockSpec((tq, dh), lambda h, i: (i, h)),
        out_shape=jax.ShapeDtypeStruct((s, hh * dh), BF16),
        scratch_shapes=[pltpu.VMEM((s, 2 * dh), BF16),
                        pltpu.VMEM((dh + 16, s), BF16),
                        pltpu.VMEM((2 * dh, tq), BF16),
                        pltpu.VMEM((ATT_TK, tq), F32),
                        pltpu.VMEM((ATT_TK, tq), F32),
                        pltpu.VMEM((ATT_TK, tq), BF16),
                        pltpu.VMEM((ATT_TK, tq), BF16),
                        pltpu.VMEM((2, 1, tq), F32),
                        pltpu.VMEM((dh + 16, tq), F32),
                        pltpu.VMEM((1, tq), F32)],
        compiler_params=_params("arbitrary", "arbitrary"),
        name="fox_attention",
    )(qkv, qkv, qkv, c)


def _first_lane_eq(vals, target, lane):
    return jnp.min(jnp.where(vals == target, lane, LANES), axis=1, keepdims=True)


def _route_rows(x, w_ref, b_ref, carry_ref):
    tm = x.shape[0]
    neg = -jnp.inf
    logits = _dot_hi_lo(x, w_ref) + b_ref[...]
    lane = lax.broadcasted_iota(jnp.int32, (tm, LANES), 1)

    gl = jnp.where(lane < N_GROUPS, logits, neg)
    gmax = jnp.max(gl, axis=1, keepdims=True)
    gsum = jnp.sum(jnp.exp(gl - gmax), axis=1, keepdims=True)
    grp_p = 1.0 / gsum
    grp = _first_lane_eq(gl, gmax, lane)

    lo = N_GROUPS + grp * EXPERTS_PER_GROUP
    el = jnp.where((lane >= lo) & (lane < lo + EXPERTS_PER_GROUP), logits, neg)
    emax = jnp.max(el, axis=1, keepdims=True)
    esum = jnp.sum(jnp.exp(el - emax), axis=1, keepdims=True)
    idx1 = _first_lane_eq(el, emax, lane)
    el2 = jnp.where(lane == idx1, neg, el)
    emax2 = jnp.max(el2, axis=1, keepdims=True)
    idx2 = _first_lane_eq(el2, emax2, lane)
    p1 = 1.0 / esum
    p2 = jnp.exp(emax2 - emax) / esum
    psum = p1 + p2
    g1 = grp_p * (p1 / psum)
    g2 = grp_p * (p2 / psum)

    oh1 = lane == idx1
    oh2 = lane == idx2
    both = jnp.where(oh1 | oh2, 1.0, 0.0)
    before = jnp.dot(_tri(tm, strict=True), both.astype(BF16), preferred_element_type=F32)
    before = before + carry_ref[...]
    r1 = jnp.sum(jnp.where(oh1, before, 0.0), axis=1, keepdims=True)
    r2 = jnp.sum(jnp.where(oh2, before, 0.0), axis=1, keepdims=True)
    carry_ref[...] = carry_ref[...] + jnp.sum(both, axis=0, keepdims=True)

    e1 = (idx1 - N_GROUPS).astype(F32)
    e2 = (idx2 - N_GROUPS).astype(F32)
    out = jnp.where(lane == 0, e1, 0.0)
    out = jnp.where(lane == 1, e2, out)
    out = jnp.where(lane == 2, g1, out)
    out = jnp.where(lane == 3, g2, out)
    out = jnp.where(lane == 4, r1, out)
    return jnp.where(lane == 5, r2, out)


def _proj_ln_route_kernel(o_ref, w_ref, x_ref, g_ref, b_ref, wr_ref, br_ref,
                          y_ref, route_ref, cnt_ref, carry_ref):
    @pl.when(pl.program_id(0) == 0)
    def _():
        carry_ref[...] = jnp.zeros_like(carry_ref)

    mix = jnp.dot(o_ref[...], w_ref[...], preferred_element_type=F32)
    y = _layer_norm_rows(DEEPNORM_ALPHA * x_ref[...] + mix, g_ref[...], b_ref[...])
    y_ref[...] = y
    route_ref[...] = _route_rows(y, wr_ref, br_ref, carry_ref)
    cnt_ref[...] = carry_ref[...]


def _proj_ln_route(o, w, x, g, b, w_group, b_group, w_expert, b_expert):
    s, d = x.shape
    kd = o.shape[1]
    tm = min(ROW_TILE, s)
    whl = _pack_hi_lo(jnp.concatenate([w_group, w_expert], axis=1))
    nb = N_GROUPS + N_EXPERTS
    bias = jnp.zeros((1, LANES), F32).at[0, :nb].set(jnp.concatenate([b_group, b_expert]))
    return pl.pallas_call(
        _proj_ln_route_kernel,
        grid=(s // tm,),
        in_specs=[pl.BlockSpec((tm, kd), lambda i: (i, 0)),
                  pl.BlockSpec((kd, d), lambda i: (0, 0)),
                  pl.BlockSpec((tm, d), lambda i: (i, 0)),
                  pl.BlockSpec((1, d), lambda i: (0, 0)),
                  pl.BlockSpec((1, d), lambda i: (0, 0)),
                  pl.BlockSpec((d, 2 * LANES), lambda i: (0, 0)),
                  pl.BlockSpec((1, LANES), lambda i: (0, 0))],
        out_specs=[pl.BlockSpec((tm, d), lambda i: (i, 0)),
                   pl.BlockSpec((tm, LANES), lambda i: (i, 0)),
                   pl.BlockSpec((1, LANES), lambda i: (0, 0))],
        out_shape=[jax.ShapeDtypeStruct((s, d), F32),
                   jax.ShapeDtypeStruct((s, LANES), F32),
                   jax.ShapeDtypeStruct((1, LANES), F32)],
        scratch_shapes=[pltpu.VMEM((1, LANES), F32)],
        compiler_params=_params("arbitrary"),
        name="proj_ln_route",
    )(o, w, x, g.reshape(1, d), b.reshape(1, d), whl, bias)


def _dispatch_kernel(dest_ref, cnt_ref, pstart_ref, nv_ref, x_ref, xs_hbm, zero_ref, sem, zsem):
    i = pl.program_id(0)
    tm = x_ref.shape[0]
    bm = MOE_BLOCK
    n_exp = cnt_ref.shape[0]
    n_blocks = xs_hbm.shape[0] // bm

    def pad_copies(e, fn):
        cnt = cnt_ref[e]
        pos = pstart_ref[e] + cnt

        def row(r, carry):
            fn(pltpu.make_async_copy(zero_ref.at[pl.ds(0, 1), :], xs_hbm.at[pl.ds(pos + r, 1), :], zsem))
            return carry
        lax.fori_loop(0, (bm - cnt % bm) % bm, row, 0)

    def tail_copy(g):
        return pltpu.make_async_copy(zero_ref, xs_hbm.at[pl.ds(pl.multiple_of(g * bm, bm), bm), :], zsem)

    def for_all_fills(fn):
        def per_expert(e, carry):
            pad_copies(e, fn)
            return carry
        lax.fori_loop(0, n_exp, per_expert, 0)

        def per_tail(g, carry):
            fn(tail_copy(g))
            return carry
        lax.fori_loop(nv_ref[0], n_blocks, per_tail, 0)

    @pl.when(i == 0)
    def _():
        zero_ref[...] = jnp.zeros_like(zero_ref)
        for_all_fills(lambda c: c.start())

    base = i * (TOP_K * tm)

    def body(r, carry):
        for k in range(TOP_K):
            dst = dest_ref[base + TOP_K * r + k]
            pltpu.make_async_copy(x_ref.at[pl.ds(r, 1), :], xs_hbm.at[pl.ds(dst, 1), :], sem).start(priority=k)
        return carry
    lax.fori_loop(0, tm, body, 0, unroll=8)

    @pl.when(i == 0)
    def _():
        for_all_fills(lambda c: c.wait())

    for _ in range(TOP_K):
        pltpu.make_async_copy(x_ref, xs_hbm.at[pl.ds(0, tm), :], sem).wait()


def _dispatch(x, dest, cnt, pstart, n_valid, n_slots):
    t, d = x.shape
    tm = min(ROW_TILE, t)
    grid_spec = pltpu.PrefetchScalarGridSpec(
        num_scalar_prefetch=4,
        grid=(t // tm,),
        in_specs=[pl.BlockSpec((tm, d), lambda i, *_: (i, 0))],
        out_specs=pl.BlockSpec(memory_space=pl.ANY),
        scratch_shapes=[pltpu.VMEM((MOE_BLOCK, d), F32),
                        pltpu.SemaphoreType.DMA,
                        pltpu.SemaphoreType.DMA],
    )
    return pl.pallas_call(
        _dispatch_kernel,
        grid_spec=grid_spec,
        out_shape=jax.ShapeDtypeStruct((n_slots, d), F32),
        compiler_params=_params("arbitrary"),
        name="moe_dispatch",
    )(dest, cnt, pstart, n_valid, x)


def _expert_kernel(nblk_ref, bstart_ref, nv_ref, xs_hbm, wg_ref, wu_ref, wd_ref, y_hbm,
                   xin_ref, yout_ref, wgb_ref, wub_ref, wdb_ref, in_sem, out_sem):
    e = pl.program_id(0)
    n_valid = nv_ref[0]
    bm = MOE_BLOCK
    n_blocks = y_hbm.shape[0] // bm

    def in_copy(g, slot):
        return pltpu.make_async_copy(xs_hbm.at[pl.ds(pl.multiple_of(g * bm, bm), bm), :],
                                     xin_ref.at[slot], in_sem.at[slot])

    def out_copy(g, slot):
        return pltpu.make_async_copy(yout_ref.at[slot],
                                     y_hbm.at[pl.ds(pl.multiple_of(g * bm, bm), bm), :], out_sem.at[slot])

    @pl.when((e == 0) & (n_valid > 0))
    def _():
        in_copy(0, 0).start(priority=1)

    nb = nblk_ref[e]
    g0 = bstart_ref[e]

    @pl.when(nb > 0)
    def _():
        wgb_ref[...] = wg_ref[0, 0].astype(BF16)
        wub_ref[...] = wu_ref[0, 0].astype(BF16)
        wdb_ref[...] = wd_ref[0, 0].astype(BF16)

        def block(j, carry):
            g = g0 + j
            slot = g % 2
            in_copy(g, slot).wait()

            @pl.when(g + 1 < n_valid)
            def _():
                in_copy(g + 1, 1 - slot).start(priority=1)

            @pl.when(g >= 2)
            def _():
                out_copy(g - 2, slot).wait()

            xb = xin_ref[slot].astype(BF16)
            gate = jnp.dot(xb, wgb_ref[...], preferred_element_type=F32)
            up = jnp.dot(xb, wub_ref[...], preferred_element_type=F32)
            hid = (gate * jax.nn.sigmoid(gate) * up).astype(BF16)
            yout_ref[slot] = jnp.dot(hid, wdb_ref[...], preferred_element_type=F32)
            out_copy(g, slot).start(priority=1)
            return carry
        lax.fori_loop(0, nb, block, 0)

    @pl.when(e == pl.num_programs(0) - 1)
    def _():
        for back in (2, 1):
            @pl.when(n_valid >= back)
            def _(back=back):
                out_copy(n_valid - back, (n_valid - back) % 2).wait()

        yout_ref[0] = jnp.zeros(yout_ref.shape[1:], yout_ref.dtype)

        def fill(g, carry):
            out_copy(g, 0).start()
            return carry
        lax.fori_loop(n_valid, n_blocks, fill, 0)

        def drain(g, carry):
            out_copy(g, 0).wait()
            return carry
        lax.fori_loop(n_valid, n_blocks, drain, 0)


def _experts(xs, nblk, bstart, n_valid, w_gate, w_up, w_down, layer):
    p, d = xs.shape
    _, n_exp, _, f = w_gate.shape
    bm = MOE_BLOCK

    def w_idx(e, *_):
        return (layer, e, 0, 0)

    grid_spec = pltpu.PrefetchScalarGridSpec(
        num_scalar_prefetch=3,
        grid=(n_exp,),
        in_specs=[pl.BlockSpec(memory_space=pl.ANY),
                  pl.BlockSpec((1, 1, d, f), w_idx),
                  pl.BlockSpec((1, 1, d, f), w_idx),
                  pl.BlockSpec((1, 1, f, d), w_idx)],
        out_specs=pl.BlockSpec(memory_space=pl.ANY),
        scratch_shapes=[pltpu.VMEM((2, bm, d), F32),
                        pltpu.VMEM((2, bm, d), F32),
                        pltpu.VMEM((d, f), BF16),
                        pltpu.VMEM((d, f), BF16),
                        pltpu.VMEM((f, d), BF16),
                        pltpu.SemaphoreType.DMA((2,)),
                        pltpu.SemaphoreType.DMA((2,))],
    )
    return pl.pallas_call(
        _expert_kernel,
        grid_spec=grid_spec,
        out_shape=jax.ShapeDtypeStruct((p, d), F32),
        compiler_params=_params("arbitrary"),
        name="moe_experts",
    )(nblk, bstart, n_valid, xs, w_gate, w_up, w_down)


def _combine_ln_kernel(dest_ref, y_hbm, x_ref, route_ref, g_ref, b_ref, o_ref, yg_ref, sem):
    i = pl.program_id(0)
    n = pl.num_programs(0)
    tm = x_ref.shape[0]

    def start_gather(step, slot):
        base = step * (TOP_K * tm)

        def body(r, carry):
            for k in range(TOP_K):
                src = dest_ref[base + TOP_K * r + k]
                pltpu.make_async_copy(y_hbm.at[pl.ds(src, 1), :],
                                      yg_ref.at[slot, pl.ds(k * tm + r, 1), :], sem.at[slot]).start(priority=k)
            return carry
        lax.fori_loop(0, tm, body, 0, unroll=8)

    def wait_gather(slot):
        pltpu.make_async_copy(y_hbm.at[pl.ds(0, TOP_K * tm), :], yg_ref.at[slot], sem.at[slot]).wait()

    @pl.when(i == 0)
    def _():
        start_gather(0, 0)

    @pl.when(i + 1 < n)
    def _():
        start_gather(i + 1, (i + 1) % 2)

    slot = i % 2
    wait_gather(slot)
    route = route_ref[...]
    lane = lax.broadcasted_iota(jnp.int32, route.shape, 1)
    g1 = jnp.sum(jnp.where(lane == 2, route, 0.0), axis=1, keepdims=True)
    g2 = jnp.sum(jnp.where(lane == 3, route, 0.0), axis=1, keepdims=True)
    y1 = yg_ref[slot, pl.ds(0, tm), :]
    y2 = yg_ref[slot, pl.ds(tm, tm), :]
    z = DEEPNORM_ALPHA * x_ref[...] + (y1 * g1 + y2 * g2)
    o_ref[...] = _layer_norm_rows(z, g_ref[...], b_ref[...])


def _combine_ln(y, dest, x, route, g, b):
    t, d = x.shape
    tm = min(COMB_TM, t)
    grid_spec = pltpu.PrefetchScalarGridSpec(
        num_scalar_prefetch=1,
        grid=(t // tm,),
        in_specs=[pl.BlockSpec(memory_space=pl.ANY),
                  pl.BlockSpec((tm, d), lambda i, dest: (i, 0)),
                  pl.BlockSpec((tm, LANES), lambda i, dest: (i, 0)),
                  pl.BlockSpec((1, d), lambda i, dest: (0, 0)),
                  pl.BlockSpec((1, d), lambda i, dest: (0, 0))],
        out_specs=pl.BlockSpec((tm, d), lambda i, dest: (i, 0)),
        scratch_shapes=[pltpu.VMEM((2, 2 * tm, d), F32),
                        pltpu.SemaphoreType.DMA((2,))],
    )
    return pl.pallas_call(
        _combine_ln_kernel,
        grid_spec=grid_spec,
        out_shape=jax.ShapeDtypeStruct((t, d), F32),
        compiler_params=_params("arbitrary"),
        name="moe_combine_ln",
    )(dest, y, x, route, g.reshape(1, d), b.reshape(1, d))


def _moe_ffn_ln(x, route, counts, w_gate, w_up, w_down, layer, ln_g, ln_b):
    t, d = x.shape
    bm = MOE_BLOCK
    eid = route[:, 0:TOP_K].astype(jnp.int32)
    rank = route[:, 4:4 + TOP_K].astype(jnp.int32)
    cnt = counts[0, N_GROUPS:N_GROUPS + N_EXPERTS].astype(jnp.int32)
    nblk = (cnt + bm - 1) // bm
    bend = jnp.cumsum(nblk)
    bstart = bend - nblk
    pstart = bstart * bm
    onehot = eid[:, :, None] == jnp.arange(N_EXPERTS, dtype=jnp.int32)
    dest = (jnp.sum(jnp.where(onehot, pstart, 0), axis=-1) + rank).reshape(-1)
    n_valid = bend[-1:]
    n_slots = (-(-t * TOP_K // bm) + N_EXPERTS) * bm
    xs = _dispatch(x, dest, cnt, pstart, n_valid, n_slots)
    y = _experts(xs, nblk, bstart, n_valid, w_gate, w_up, w_down, layer)
    return _combine_ln(y, dest, x, route, ln_g, ln_b)


def _gla_gate_kernel(x_ref, wl_ref, wu_ref, b_ref, la_ref):
    g_low = _dot_hi_lo(x_ref[...], wl_ref)
    n = wu_ref.shape[1] // 2
    gh, gl = _split2(g_low)
    a = jnp.dot(gh, wu_ref[...], preferred_element_type=F32)
    c = jnp.dot(gl, wu_ref[:, :n], preferred_element_type=F32)
    logit = a[:, :n] + a[:, n:] + c + b_ref[...]
    la_ref[...] = _log_sigmoid(logit) / GLA_GATE_TAU


def _gla_gates(x, w_low, w_gate_up, b_gate):
    s, d = x.shape
    rank, dk = w_gate_up.shape
    ts = min(ROW_TILE, s)
    wl = _pack_hi_lo(w_low)
    wu_pad = jnp.zeros((LANES, dk), F32).at[:rank].set(w_gate_up)
    wu = jnp.concatenate(_split2(wu_pad), axis=1)
    return pl.pallas_call(
        _gla_gate_kernel,
        grid=(s // ts,),
        in_specs=[pl.BlockSpec((ts, d), lambda i: (i, 0)),
                  pl.BlockSpec((d, 2 * LANES), lambda i: (0, 0)),
                  pl.BlockSpec((LANES, 2 * dk), lambda i: (0, 0)),
                  pl.BlockSpec((1, dk), lambda i: (0, 0))],
        out_specs=pl.BlockSpec((ts, dk), lambda i: (i, 0)),
        out_shape=jax.ShapeDtypeStruct((s, dk), F32),
        compiler_params=_params("parallel"),
        name="gla_gates",
    )(x, wl, wu, b_gate.reshape(1, dk))


def _gla_kernel(q_ref, k_ref, v_ref, la_ref, r_ref, g_ref, o_ref, state_ref):
    @pl.when(pl.program_id(1) == 0)
    def _():
        state_ref[...] = jnp.zeros_like(state_ref)

    rows, dk = q_ref.shape
    cs = GLA_CHUNK
    tri = _tri(cs)
    rr = lax.broadcasted_iota(jnp.int32, (cs, cs), 0)
    cc = lax.broadcasted_iota(jnp.int32, (cs, cs), 1)
    causal = rr >= cc
    for ci in range(rows // cs):
        sl = pl.ds(ci * cs, cs)
        b = _cumsum_rows(la_ref[sl, :], tri)
        b_mid = b[cs // 2 - 1:cs // 2, :]
        b_last = b[cs - 1:cs, :]
        q = q_ref[sl, :].astype(F32) * (dk ** -0.5)
        k = k_ref[sl, :].astype(F32)
        v = v_ref[sl, :]
        qa = (q * jnp.exp(b - b_mid)).astype(BF16)
        ka = (k * jnp.exp(b_mid - b)).astype(BF16)
        a = lax.dot_general(qa, ka, (((1,), (1,)), ((), ())), preferred_element_type=F32)
        a = jnp.where(causal, a, 0.0)
        o = jnp.dot(a.astype(BF16), v, preferred_element_type=F32)
        state = state_ref[...]
        o = o + jnp.dot((q * jnp.exp(b)).astype(BF16), state.astype(BF16), preferred_element_type=F32)
        k_end_t = (k * jnp.exp(b_last - b)).T.astype(BF16)
        decay = jnp.exp(b.T[:, cs - 1:cs])
        state_ref[...] = decay * state + jnp.dot(k_end_t, v, preferred_element_type=F32)
        o = o * lax.rsqrt(jnp.mean(o * o, axis=-1, keepdims=True) + RMS_EPS) * g_ref[...]
        r = r_ref[sl, :].astype(F32)
        o_ref[sl, :] = (o * (r * jax.nn.sigmoid(r))).astype(o_ref.dtype)


def _gla(proj, la, norm_g):
    s = proj.shape[0]
    dk_all = la.shape[1]
    nh = GLA_HEADS
    dk = dk_all // nh
    dv = norm_g.shape[0]
    rows = min(GLA_ROWS, s)
    kq, kv = dk_all // dk, 2 * dk_all // dv
    kr = kv + nh
    return pl.pallas_call(
        _gla_kernel,
        grid=(nh, s // rows),
        in_specs=[pl.BlockSpec((rows, dk), lambda h, i: (i, h)),
                  pl.BlockSpec((rows, dk), lambda h, i: (i, kq + h)),
                  pl.BlockSpec((rows, dv), lambda h, i: (i, kv + h)),
                  pl.BlockSpec((rows, dk), lambda h, i: (i, h)),
                  pl.BlockSpec((rows, dv), lambda h, i: (i, kr + h)),
                  pl.BlockSpec((1, dv), lambda h, i: (0, 0))],
        out_specs=pl.BlockSpec((rows, dv), lambda h, i: (i, h)),
        out_shape=jax.ShapeDtypeStruct((s, nh * dv), BF16),
        scratch_shapes=[pltpu.VMEM((dk, dv), F32)],
        compiler_params=_params("parallel", "arbitrary"),
        name="gla_chunks",
    )(proj, proj, proj, la, proj, norm_g.reshape(1, dv))


def kernel(x, fox_w_in, fox_b_f, fox_w_o, gla_w_in, gla_w_gate_up, gla_b_gate, gla_norm_g, gla_w_o,
           ln_mix_g, ln_mix_b, ln_ffn_g, ln_ffn_b, moe_w_group, moe_b_group, moe_w_expert,
           moe_b_expert, moe_w_gate, moe_w_up, moe_w_down):
    bsz, s, d = x.shape
    outs = []
    for bi in range(bsz):
        xt = x[bi]
        for i in range(DEPTH):
            j = i // 2
            if i % 2 == 0:
                w_in = fox_w_in[j]
                qkv = _matmul(xt, w_in, 3 * d, BF16)
                c = _fox_gates(xt, w_in[:, 3 * d:], fox_b_f[j])
                o = _fox_attention(qkv, c, FOX_HEADS)
                w_o = fox_w_o[j]
            else:
                w_in = gla_w_in[j]
                n_main = w_in.shape[1] - gla_w_gate_up.shape[1]
                proj = _matmul(xt, w_in, n_main, BF16)
                la = _gla_gates(xt, w_in[:, n_main:], gla_w_gate_up[j], gla_b_gate[j])
                o = _gla(proj, la, gla_norm_g[j])
                w_o = gla_w_o[j]
            xt, route, counts = _proj_ln_route(o, w_o.astype(BF16), xt, ln_mix_g[i], ln_mix_b[i],
                                               moe_w_group[i], moe_b_group[i], moe_w_expert[i], moe_b_expert[i])
            xt = _moe_ffn_ln(xt, route, counts, moe_w_gate, moe_w_up, moe_w_down, i, ln_ffn_g[i], ln_ffn_b[i])
        outs.append(xt)
    return outs[0].reshape(1, s, d) if bsz == 1 else jnp.stack(outs, axis=0)
```

```python
import functools

import jax
import jax.numpy as jnp
from jax import lax
from jax.experimental import pallas as pl
from jax.experimental.pallas import tpu as pltpu

F32 = jnp.float32
BF16 = jnp.bfloat16

DEPTH = 2
FOX_HEADS = 16
FOX_HEAD_DIM = 128
GLA_HEADS = 4
GLA_CHUNK = 64
GLA_GATE_TAU = 16.0
N_GROUPS = 8
EXPERTS_PER_GROUP = 8
N_EXPERTS = N_GROUPS * EXPERTS_PER_GROUP
TOP_K = 2
DEEPNORM_ALPHA = (2 * DEPTH) ** 0.25
LN_EPS = 1e-5
RMS_EPS = 1e-6

LANES = 128
VMEM_LIMIT = 56 * 2**20

MM_TM, MM_TN = 1024, 1024
ROW_TILE = 512
ATT_TQ, ATT_TK = 1024, 512
GLA_ROWS = 512
MOE_BLOCK = 128
MOE_W_SLOTS = 3
COMB_TM = 256


def _params(*sem):
    return pltpu.CompilerParams(dimension_semantics=sem, vmem_limit_bytes=VMEM_LIMIT)


def _split2(a):
    hi = a.astype(BF16)
    lo = (a - hi.astype(F32)).astype(BF16)
    return hi, lo


def _split3(a):
    hi = a.astype(BF16)
    r = a - hi.astype(F32)
    mid = r.astype(BF16)
    lo = (r - mid.astype(F32)).astype(BF16)
    return hi, mid, lo


def _pack_hi_lo(w_t, n_pad=LANES):
    n, k = w_t.shape
    wp = jnp.zeros((n_pad, k), F32).at[:n].set(w_t)
    hi, lo = _split2(wp)
    return jnp.concatenate([hi, lo], axis=0)


def _dot_nt(a, b):
    return lax.dot_general(a, b, (((1,), (1,)), ((), ())), preferred_element_type=F32)


def _dot_hi_lo(x, whl_ref, n_pad=LANES):
    xh, xl = _split2(x)
    a = _dot_nt(xh, whl_ref[...])
    b = _dot_nt(xl, whl_ref[:n_pad, :])
    return a[:, :n_pad] + a[:, n_pad:] + b


def _log_sigmoid(x):
    return -(jnp.maximum(-x, 0.0) + jnp.log1p(jnp.exp(-jnp.abs(x))))


def _cumsum_rows(a, incl_tri):
    n = a.shape[1]
    parts = jnp.concatenate(_split3(a), axis=1)
    c = jnp.dot(incl_tri, parts, preferred_element_type=F32)
    return c[:, :n] + c[:, n:2 * n] + c[:, 2 * n:]


def _tri(n, strict=False):
    r = lax.broadcasted_iota(jnp.int32, (n, n), 0)
    c = lax.broadcasted_iota(jnp.int32, (n, n), 1)
    return jnp.where((r > c) if strict else (r >= c), 1.0, 0.0).astype(BF16)


def _layer_norm_rows(z, g, b):
    mu = jnp.mean(z, axis=-1, keepdims=True)
    d = z - mu
    var = jnp.mean(d * d, axis=-1, keepdims=True)
    return d * lax.rsqrt(var + LN_EPS) * g + b


def _mm_kernel(x_ref, w_ref, o_ref, wb_ref):
    @pl.when(pl.program_id(1) == 0)
    def _():
        wb_ref[...] = w_ref[...].astype(BF16)

    o_ref[...] = lax.dot_general(x_ref[...].astype(BF16), wb_ref[...], (((1,), (1,)), ((), ())),
                                 preferred_element_type=F32).astype(o_ref.dtype)


def _matmul_nt(x, w_t, n, out_dtype):
    m, k = x.shape
    tm, tn = min(MM_TM, m), min(MM_TN, n)
    return pl.pallas_call(
        _mm_kernel,
        grid=(n // tn, m // tm),
        in_specs=[pl.BlockSpec((tm, k), lambda j, i: (i, 0)),
                  pl.BlockSpec((tn, k), lambda j, i: (j, 0))],
        out_specs=pl.BlockSpec((tm, tn), lambda j, i: (i, j)),
        out_shape=jax.ShapeDtypeStruct((m, n), out_dtype),
        scratch_shapes=[pltpu.VMEM((tn, k), BF16)],
        compiler_params=_params("arbitrary", "arbitrary"),
        name="dense_proj",
    )(x, w_t)


def _fox_gate_kernel(x_ref, w_ref, b_ref, c_ref, carry_ref):
    @pl.when(pl.program_id(0) == 0)
    def _():
        carry_ref[...] = jnp.zeros_like(carry_ref)

    ts = x_ref.shape[0]
    logits = _dot_hi_lo(x_ref[...], w_ref) + b_ref[...]
    log_f = _log_sigmoid(logits)
    c = _cumsum_rows(log_f, _tri(ts)) + carry_ref[...]
    carry_ref[...] = c[ts - 1:ts, :]
    c_ref[...] = c


def _fox_gates(x, w_f_t, b_f):
    s, d = x.shape
    h = w_f_t.shape[0]
    ts = min(ROW_TILE, s)
    whl = _pack_hi_lo(w_f_t)
    bias = jnp.zeros((1, LANES), F32).at[0, :h].set(b_f)
    return pl.pallas_call(
        _fox_gate_kernel,
        grid=(s // ts,),
        in_specs=[pl.BlockSpec((ts, d), lambda i: (i, 0)),
                  pl.BlockSpec((2 * LANES, d), lambda i: (0, 0)),
                  pl.BlockSpec((1, LANES), lambda i: (0, 0))],
        out_specs=pl.BlockSpec((ts, LANES), lambda i: (i, 0)),
        out_shape=jax.ShapeDtypeStruct((s, LANES), F32),
        scratch_shapes=[pltpu.VMEM((1, LANES), F32)],
        compiler_params=_params("arbitrary"),
        name="fox_gates",
    )(x, whl, bias)


def _fox_attn_kernel(q_ref, k_ref, v_ref, c_ref, o_ref, kaug_ref, vt_ref, qt_ref, st0_ref, st1_ref,
                     p0_ref, p1_ref, alpha_ref, acc_ref, m_ref):
    h = pl.program_id(0)
    qi = pl.program_id(1)
    tq, dh = q_ref.shape
    s_len = k_ref.shape[0]
    tk = ATT_TK
    log2e = 1.4426950408889634
    scale = dh ** -0.5 * log2e

    def head_column(rows, n):
        lane = lax.broadcasted_iota(jnp.int32, (n, LANES), 1)
        return log2e * jnp.sum(jnp.where(lane == h, c_ref[rows, :], 0.0), axis=1, keepdims=True)

    def bias_columns(col, first, n):
        lane = lax.broadcasted_iota(jnp.int32, (n, LANES), 1)
        hi, mid, lo = _split3(col)
        ones_first = 3 - first
        out = jnp.where((lane >= ones_first) & (lane < ones_first + 3), 1.0, 0.0)
        out = jnp.where(lane == first, hi.astype(F32), out)
        out = jnp.where(lane == first + 1, mid.astype(F32), out)
        return jnp.where(lane == first + 2, lo.astype(F32), out).astype(BF16)

    @pl.when(qi == 0)
    def _():
        ones_row = jnp.where(lax.broadcasted_iota(jnp.int32, (16, tk), 0) == 0, 1.0, 0.0).astype(BF16)

        def build(ci, carry):
            rows = pl.ds(pl.multiple_of(ci * tk, tk), tk)
            kaug_ref[rows, :dh] = k_ref[rows, :]
            kaug_ref[rows, dh:] = bias_columns(-head_column(rows, tk), 0, tk)
            vt_ref[:dh, rows] = v_ref[rows, :].astype(F32).T.astype(BF16)
            vt_ref[dh:, rows] = ones_row
            return carry
        lax.fori_loop(0, s_len // tk, build, 0)

    q_rows = pl.ds(pl.multiple_of(qi * tq, tq), tq)
    q_aug = jnp.concatenate([q_ref[...].astype(F32) * scale,
                             bias_columns(head_column(q_rows, tq), 3, tq).astype(F32)], axis=1)
    qt_ref[...] = q_aug.T.astype(BF16)

    acc_ref[...] = jnp.zeros_like(acc_ref)
    m_ref[...] = jnp.full_like(m_ref, -jnp.inf)

    st_refs, p_refs = (st0_ref, st1_ref), (p0_ref, p1_ref)

    def scores(k_start, slot):
        st_refs[slot][...] = jnp.dot(kaug_ref[pl.ds(k_start, tk), :], qt_ref[...],
                                     preferred_element_type=F32)

    def softmax(slot, diag_offset):
        st = st_refs[slot][...]
        if diag_offset is not None:
            kr = lax.broadcasted_iota(jnp.int32, (tk, tq), 0) + diag_offset
            qc = lax.broadcasted_iota(jnp.int32, (tk, tq), 1)
            st = jnp.where(kr <= qc, st, -jnp.inf)
        m_prev = m_ref[...]
        m_new = jnp.maximum(m_prev, jnp.max(st, axis=0, keepdims=True))
        m_ref[...] = m_new
        p_refs[slot][...] = jnp.exp2(st - m_new).astype(BF16)
        alpha_ref[slot] = jnp.exp2(m_prev - m_new)

    def values(k_start, slot):
        acc_ref[...] = alpha_ref[slot] * acc_ref[...] + jnp.dot(
            vt_ref[:, pl.ds(k_start, tk)], p_refs[slot][...], preferred_element_type=F32)

    n_diag = tq // tk
    assert n_diag == 2
    d0 = pl.multiple_of(qi * tq, tk)
    d1 = pl.multiple_of(qi * tq + tk, tk)
    n_full = qi * n_diag

    def full_start(i):
        return pl.multiple_of(jnp.minimum(i, jnp.maximum(n_full - 1, 0)) * tk, tk)

    scores(d0, 0)
    scores(d1, 1)
    softmax(0, 0)
    scores(full_start(0), 0)
    softmax(1, tk)
    values(d0, 0)

    def pair(j, carry):
        i0 = 2 * j
        prev = jnp.where(j == 0, d1, (i0 - 1) * tk)
        scores(full_start(i0 + 1), 1)
        softmax(0, None)
        values(pl.multiple_of(prev, tk), 1)
        scores(full_start(i0 + 2), 0)
        softmax(1, None)
        values(pl.multiple_of(i0 * tk, tk), 0)
        return carry

    lax.fori_loop(0, qi, pair, 0)
    last = jnp.where(qi == 0, d1, (n_full - 1) * tk)
    values(pl.multiple_of(last, tk), 1)

    o_ref[...] = (acc_ref[:dh, :] / acc_ref[dh:dh + 1, :]).T.astype(o_ref.dtype)


def _fox_attention(qkv, c, n_heads):
    s = qkv.shape[0]
    dh = FOX_HEAD_DIM
    tq = min(ATT_TQ, s)
    hh = n_heads
    return pl.pallas_call(
        _fox_attn_kernel,
        grid=(hh, s // tq),
        in_specs=[pl.BlockSpec((tq, dh), lambda h, i: (i, h)),
                  pl.BlockSpec((s, dh), lambda h, i: (0, hh + h)),
                  pl.BlockSpec((s, dh), lambda h, i: (0, 2 * hh + h)),
                  pl.BlockSpec((s, LANES), lambda h, i: (0, 0))],
        out_specs=pl.BlockSpec((tq, dh), lambda h, i: (i, h)),
        out_shape=jax.ShapeDtypeStruct((s, hh * dh), BF16),
        scratch_shapes=[pltpu.VMEM((s, 2 * dh), BF16),
                        pltpu.VMEM((dh + 16, s), BF16),
                        pltpu.VMEM((2 * dh, tq), BF16),
                        pltpu.VMEM((ATT_TK, tq), F32),
                        pltpu.VMEM((ATT_TK, tq), F32),
                        pltpu.VMEM((ATT_TK, tq), BF16),
                        pltpu.VMEM((ATT_TK, tq), BF16),
                        pltpu.VMEM((2, 1, tq), F32),
                        pltpu.VMEM((dh + 16, tq), F32),
                        pltpu.VMEM((1, tq), F32)],
        compiler_params=_params("arbitrary", "arbitrary"),
        name="fox_attention",
    )(qkv, qkv, qkv, c)


def _first_lane_eq(vals, target, lane):
    return jnp.min(jnp.where(vals == target, lane, LANES), axis=1, keepdims=True)


def _route_rows(x, w_ref, b_ref, carry_ref):
    tm = x.shape[0]
    neg = -jnp.inf
    logits = _dot_hi_lo(x, w_ref) + b_ref[...]
    lane = lax.broadcasted_iota(jnp.int32, (tm, LANES), 1)

    gl = jnp.where(lane < N_GROUPS, logits, neg)
    gmax = jnp.max(gl, axis=1, keepdims=True)
    gsum = jnp.sum(jnp.exp(gl - gmax), axis=1, keepdims=True)
    grp_p = 1.0 / gsum
    grp = _first_lane_eq(gl, gmax, lane)

    lo = N_GROUPS + grp * EXPERTS_PER_GROUP
    el = jnp.where((lane >= lo) & (lane < lo + EXPERTS_PER_GROUP), logits, neg)
    emax = jnp.max(el, axis=1, keepdims=True)
    esum = jnp.sum(jnp.exp(el - emax), axis=1, keepdims=True)
    idx1 = _first_lane_eq(el, emax, lane)
    el2 = jnp.where(lane == idx1, neg, el)
    emax2 = jnp.max(el2, axis=1, keepdims=True)
    idx2 = _first_lane_eq(el2, emax2, lane)
    p1 = 1.0 / esum
    p2 = jnp.exp(emax2 - emax) / esum
    psum = p1 + p2
    g1 = grp_p * (p1 / psum)
    g2 = grp_p * (p2 / psum)

    oh1 = lane == idx1
    oh2 = lane == idx2
    both = jnp.where(oh1 | oh2, 1.0, 0.0)
    before = jnp.dot(_tri(tm, strict=True), both.astype(BF16), preferred_element_type=F32)
    before = before + carry_ref[...]
    r1 = jnp.sum(jnp.where(oh1, before, 0.0), axis=1, keepdims=True)
    r2 = jnp.sum(jnp.where(oh2, before, 0.0), axis=1, keepdims=True)
    carry_ref[...] = carry_ref[...] + jnp.sum(both, axis=0, keepdims=True)

    e1 = (idx1 - N_GROUPS).astype(F32)
    e2 = (idx2 - N_GROUPS).astype(F32)
    out = jnp.where(lane == 0, e1, 0.0)
    out = jnp.where(lane == 1, e2, out)
    out = jnp.where(lane == 2, g1, out)
    out = jnp.where(lane == 3, g2, out)
    out = jnp.where(lane == 4, r1, out)
    return jnp.where(lane == 5, r2, out)


def _proj_ln_route_kernel(o_ref, w_ref, x_ref, g_ref, b_ref, wr_ref, br_ref,
                          y_ref, route_ref, cnt_ref, carry_ref):
    @pl.when(pl.program_id(0) == 0)
    def _():
        carry_ref[...] = jnp.zeros_like(carry_ref)

    mix = jnp.dot(o_ref[...], w_ref[...], preferred_element_type=F32)
    y = _layer_norm_rows(DEEPNORM_ALPHA * x_ref[...] + mix, g_ref[...], b_ref[...])
    y_ref[...] = y
    route_ref[...] = _route_rows(y, wr_ref, br_ref, carry_ref)
    cnt_ref[...] = carry_ref[...]


def _proj_ln_route(o, w, x, g, b, w_group, b_group, w_expert, b_expert):
    s, d = x.shape
    kd = o.shape[1]
    tm = min(ROW_TILE, s)
    whl = _pack_hi_lo(jnp.concatenate([w_group.T, w_expert.T], axis=0))
    nb = N_GROUPS + N_EXPERTS
    bias = jnp.zeros((1, LANES), F32).at[0, :nb].set(jnp.concatenate([b_group, b_expert]))
    return pl.pallas_call(
        _proj_ln_route_kernel,
        grid=(s // tm,),
        in_specs=[pl.BlockSpec((tm, kd), lambda i: (i, 0)),
                  pl.BlockSpec((kd, d), lambda i: (0, 0)),
                  pl.BlockSpec((tm, d), lambda i: (i, 0)),
                  pl.BlockSpec((1, d), lambda i: (0, 0)),
                  pl.BlockSpec((1, d), lambda i: (0, 0)),
                  pl.BlockSpec((2 * LANES, d), lambda i: (0, 0)),
                  pl.BlockSpec((1, LANES), lambda i: (0, 0))],
        out_specs=[pl.BlockSpec((tm, d), lambda i: (i, 0)),
                   pl.BlockSpec((tm, LANES), lambda i: (i, 0)),
                   pl.BlockSpec((1, LANES), lambda i: (0, 0))],
        out_shape=[jax.ShapeDtypeStruct((s, d), F32),
                   jax.ShapeDtypeStruct((s, LANES), F32),
                   jax.ShapeDtypeStruct((1, LANES), F32)],
        scratch_shapes=[pltpu.VMEM((1, LANES), F32)],
        compiler_params=_params("arbitrary"),
        name="proj_ln_route",
    )(o, w, x, g.reshape(1, d), b.reshape(1, d), whl, bias)


def _dispatch_kernel(dest_ref, cnt_ref, pstart_ref, nv_ref, x_ref, xs_hbm, zero_ref, sem, zsem):
    i = pl.program_id(0)
    tm = x_ref.shape[0]
    bm = MOE_BLOCK
    n_exp = cnt_ref.shape[0]
    n_blocks = xs_hbm.shape[0] // bm

    def pad_copies(e, fn):
        cnt = cnt_ref[e]
        pos = pstart_ref[e] + cnt

        def row(r, carry):
            fn(pltpu.make_async_copy(zero_ref.at[pl.ds(0, 1), :], xs_hbm.at[pl.ds(pos + r, 1), :], zsem))
            return carry
        lax.fori_loop(0, (bm - cnt % bm) % bm, row, 0)

    def tail_copy(g):
        return pltpu.make_async_copy(zero_ref, xs_hbm.at[pl.ds(pl.multiple_of(g * bm, bm), bm), :], zsem)

    def for_all_fills(fn):
        def per_expert(e, carry):
            pad_copies(e, fn)
            return carry
        lax.fori_loop(0, n_exp, per_expert, 0)

        def per_tail(g, carry):
            fn(tail_copy(g))
            return carry
        lax.fori_loop(nv_ref[0], n_blocks, per_tail, 0)

    @pl.when(i == 0)
    def _():
        zero_ref[...] = jnp.zeros_like(zero_ref)
        for_all_fills(lambda c: c.start())

    base = i * (TOP_K * tm)

    def body(r, carry):
        for k in range(TOP_K):
            dst = dest_ref[base + TOP_K * r + k]
            pltpu.make_async_copy(x_ref.at[pl.ds(r, 1), :], xs_hbm.at[pl.ds(dst, 1), :], sem).start(priority=k)
        return carry
    lax.fori_loop(0, tm, body, 0, unroll=8)

    @pl.when(i == 0)
    def _():
        for_all_fills(lambda c: c.wait())

    for _ in range(TOP_K):
        pltpu.make_async_copy(x_ref, xs_hbm.at[pl.ds(0, tm), :], sem).wait()


def _dispatch(x, dest, cnt, pstart, n_valid, n_slots):
    t, d = x.shape
    tm = min(ROW_TILE, t)
    grid_spec = pltpu.PrefetchScalarGridSpec(
        num_scalar_prefetch=4,
        grid=(t // tm,),
        in_specs=[pl.BlockSpec((tm, d), lambda i, *_: (i, 0))],
        out_specs=pl.BlockSpec(memory_space=pl.ANY),
        scratch_shapes=[pltpu.VMEM((MOE_BLOCK, d), F32),
                        pltpu.SemaphoreType.DMA,
                        pltpu.SemaphoreType.DMA],
    )
    return pl.pallas_call(
        _dispatch_kernel,
        grid_spec=grid_spec,
        out_shape=jax.ShapeDtypeStruct((n_slots, d), F32),
        compiler_params=_params("arbitrary"),
        name="moe_dispatch",
    )(dest, cnt, pstart, n_valid, x)


def _expert_kernel(nblk_ref, bstart_ref, nv_ref, xs_hbm, wg_hbm, wu_hbm, wd_hbm, y_hbm,
                   xin_ref, yout_ref, wgf_ref, wuf_ref, wdf_ref, wgb_ref, wub_ref, wdb_ref,
                   in_sem, out_sem, w_sem, *, layer):
    e = pl.program_id(0)
    n_exp = pl.num_programs(0)
    n_valid = nv_ref[0]
    bm = MOE_BLOCK
    n_blocks = y_hbm.shape[0] // bm
    ns = MOE_W_SLOTS

    def w_copies(ex, slot):
        return (pltpu.make_async_copy(wg_hbm.at[layer, ex], wgf_ref.at[slot], w_sem.at[slot]),
                pltpu.make_async_copy(wu_hbm.at[layer, ex], wuf_ref.at[slot], w_sem.at[slot]),
                pltpu.make_async_copy(wd_hbm.at[layer, ex], wdf_ref.at[slot], w_sem.at[slot]))

    def start_weights(ex):
        exc = jnp.minimum(ex, n_exp - 1)

        @pl.when((ex < n_exp) & (nblk_ref[exc] > 0))
        def _():
            for c in w_copies(exc, exc % ns):
                c.start()

    @pl.when(e == 0)
    def _():
        for ahead in range(ns - 1):
            start_weights(ahead)

    start_weights(e + ns - 1)

    def in_copy(g, slot):
        return pltpu.make_async_copy(xs_hbm.at[pl.ds(pl.multiple_of(g * bm, bm), bm), :],
                                     xin_ref.at[slot], in_sem.at[slot])

    def out_copy(g, slot):
        return pltpu.make_async_copy(yout_ref.at[slot],
                                     y_hbm.at[pl.ds(pl.multiple_of(g * bm, bm), bm), :], out_sem.at[slot])

    @pl.when((e == 0) & (n_valid > 0))
    def _():
        in_copy(0, 0).start(priority=1)

    nb = nblk_ref[e]
    g0 = bstart_ref[e]

    @pl.when(nb > 0)
    def _():
        w_slot = e % ns
        for c in w_copies(e, w_slot):
            c.wait()
        wgb_ref[...] = wgf_ref[w_slot].astype(BF16)
        wub_ref[...] = wuf_ref[w_slot].astype(BF16)
        wdb_ref[...] = wdf_ref[w_slot].astype(BF16)

        def block(j, carry):
            g = g0 + j
            slot = g % 2
            in_copy(g, slot).wait()

            @pl.when(g + 1 < n_valid)
            def _():
                in_copy(g + 1, 1 - slot).start(priority=1)

            @pl.when(g >= 2)
            def _():
                out_copy(g - 2, slot).wait()

            xb = xin_ref[slot].astype(BF16)
            gate = jnp.dot(xb, wgb_ref[...], preferred_element_type=F32)
            up = jnp.dot(xb, wub_ref[...], preferred_element_type=F32)
            hid = (gate * jax.nn.sigmoid(gate) * up).astype(BF16)
            yout_ref[slot] = jnp.dot(hid, wdb_ref[...], preferred_element_type=F32)
            out_copy(g, slot).start(priority=1)
            return carry
        lax.fori_loop(0, nb, block, 0)

    @pl.when(e == pl.num_programs(0) - 1)
    def _():
        for back in (2, 1):
            @pl.when(n_valid >= back)
            def _(back=back):
                out_copy(n_valid - back, (n_valid - back) % 2).wait()

        yout_ref[0] = jnp.zeros(yout_ref.shape[1:], yout_ref.dtype)

        def fill(g, carry):
            out_copy(g, 0).start()
            return carry
        lax.fori_loop(n_valid, n_blocks, fill, 0)

        def drain(g, carry):
            out_copy(g, 0).wait()
            return carry
        lax.fori_loop(n_valid, n_blocks, drain, 0)


def _experts(xs, nblk, bstart, n_valid, w_gate, w_up, w_down, layer):
    p, d = xs.shape
    _, n_exp, _, f = w_gate.shape
    bm = MOE_BLOCK
    ns = MOE_W_SLOTS
    grid_spec = pltpu.PrefetchScalarGridSpec(
        num_scalar_prefetch=3,
        grid=(n_exp,),
        in_specs=[pl.BlockSpec(memory_space=pl.ANY)] * 4,
        out_specs=pl.BlockSpec(memory_space=pl.ANY),
        scratch_shapes=[pltpu.VMEM((2, bm, d), F32),
                        pltpu.VMEM((2, bm, d), F32),
                        pltpu.VMEM((ns, d, f), F32),
                        pltpu.VMEM((ns, d, f), F32),
                        pltpu.VMEM((ns, f, d), F32),
                        pltpu.VMEM((d, f), BF16),
                        pltpu.VMEM((d, f), BF16),
                        pltpu.VMEM((f, d), BF16),
                        pltpu.SemaphoreType.DMA((2,)),
                        pltpu.SemaphoreType.DMA((2,)),
                        pltpu.SemaphoreType.DMA((ns,))],
    )
    return pl.pallas_call(
        functools.partial(_expert_kernel, layer=layer),
        grid_spec=grid_spec,
        out_shape=jax.ShapeDtypeStruct((p, d), F32),
        compiler_params=_params("arbitrary"),
        name="moe_experts",
    )(nblk, bstart, n_valid, xs, w_gate, w_up, w_down)


def _combine_ln_kernel(dest_ref, y_hbm, x_ref, route_ref, g_ref, b_ref, o_ref, yg_ref, sem):
    i = pl.program_id(0)
    n = pl.num_programs(0)
    tm = x_ref.shape[0]

    def start_gather(step, slot):
        base = step * (TOP_K * tm)

        def body(r, carry):
            for k in range(TOP_K):
                src = dest_ref[base + TOP_K * r + k]
                pltpu.make_async_copy(y_hbm.at[pl.ds(src, 1), :],
                                      yg_ref.at[slot, pl.ds(k * tm + r, 1), :], sem.at[slot]).start(priority=k)
            return carry
        lax.fori_loop(0, tm, body, 0, unroll=8)

    def wait_gather(slot):
        pltpu.make_async_copy(y_hbm.at[pl.ds(0, TOP_K * tm), :], yg_ref.at[slot], sem.at[slot]).wait()

    @pl.when(i == 0)
    def _():
        start_gather(0, 0)

    @pl.when(i + 1 < n)
    def _():
        start_gather(i + 1, (i + 1) % 2)

    slot = i % 2
    wait_gather(slot)
    route = route_ref[...]
    lane = lax.broadcasted_iota(jnp.int32, route.shape, 1)
    g1 = jnp.sum(jnp.where(lane == 2, route, 0.0), axis=1, keepdims=True)
    g2 = jnp.sum(jnp.where(lane == 3, route, 0.0), axis=1, keepdims=True)
    y1 = yg_ref[slot, pl.ds(0, tm), :]
    y2 = yg_ref[slot, pl.ds(tm, tm), :]
    z = DEEPNORM_ALPHA * x_ref[...] + (y1 * g1 + y2 * g2)
    o_ref[...] = _layer_norm_rows(z, g_ref[...], b_ref[...])


def _combine_ln(y, dest, x, route, g, b):
    t, d = x.shape
    tm = min(COMB_TM, t)
    grid_spec = pltpu.PrefetchScalarGridSpec(
        num_scalar_prefetch=1,
        grid=(t // tm,),
        in_specs=[pl.BlockSpec(memory_space=pl.ANY),
                  pl.BlockSpec((tm, d), lambda i, dest: (i, 0)),
                  pl.BlockSpec((tm, LANES), lambda i, dest: (i, 0)),
                  pl.BlockSpec((1, d), lambda i, dest: (0, 0)),
                  pl.BlockSpec((1, d), lambda i, dest: (0, 0))],
        out_specs=pl.BlockSpec((tm, d), lambda i, dest: (i, 0)),
        scratch_shapes=[pltpu.VMEM((2, 2 * tm, d), F32),
                        pltpu.SemaphoreType.DMA((2,))],
    )
    return pl.pallas_call(
        _combine_ln_kernel,
        grid_spec=grid_spec,
        out_shape=jax.ShapeDtypeStruct((t, d), F32),
        compiler_params=_params("arbitrary"),
        name="moe_combine_ln",
    )(dest, y, x, route, g.reshape(1, d), b.reshape(1, d))


def _moe_ffn_ln(x, route, counts, w_gate, w_up, w_down, layer, ln_g, ln_b):
    t, d = x.shape
    bm = MOE_BLOCK
    eid = route[:, 0:TOP_K].astype(jnp.int32)
    rank = route[:, 4:4 + TOP_K].astype(jnp.int32)
    cnt = counts[0, N_GROUPS:N_GROUPS + N_EXPERTS].astype(jnp.int32)
    nblk = (cnt + bm - 1) // bm
    bend = jnp.cumsum(nblk)
    bstart = bend - nblk
    pstart = bstart * bm
    onehot = eid[:, :, None] == jnp.arange(N_EXPERTS, dtype=jnp.int32)
    dest = (jnp.sum(jnp.where(onehot, pstart, 0), axis=-1) + rank).reshape(-1)
    n_valid = bend[-1:]
    n_slots = (-(-t * TOP_K // bm) + N_EXPERTS) * bm
    xs = _dispatch(x, dest, cnt, pstart, n_valid, n_slots)
    y = _experts(xs, nblk, bstart, n_valid, w_gate, w_up, w_down, layer)
    return _combine_ln(y, dest, x, route, ln_g, ln_b)


def _gla_gate_kernel(x_ref, wl_ref, wu_ref, b_ref, la_ref):
    g_low = _dot_hi_lo(x_ref[...], wl_ref)
    n = wu_ref.shape[1] // 2
    gh, gl = _split2(g_low)
    a = jnp.dot(gh, wu_ref[...], preferred_element_type=F32)
    c = jnp.dot(gl, wu_ref[:, :n], preferred_element_type=F32)
    logit = a[:, :n] + a[:, n:] + c + b_ref[...]
    la_ref[...] = _log_sigmoid(logit) / GLA_GATE_TAU


def _gla_gates(x, w_low_t, w_gate_up, b_gate):
    s, d = x.shape
    rank, dk = w_gate_up.shape
    ts = min(ROW_TILE, s)
    wl = _pack_hi_lo(w_low_t)
    wu_pad = jnp.zeros((LANES, dk), F32).at[:rank].set(w_gate_up)
    wu = jnp.concatenate(_split2(wu_pad), axis=1)
    return pl.pallas_call(
        _gla_gate_kernel,
        grid=(s // ts,),
        in_specs=[pl.BlockSpec((ts, d), lambda i: (i, 0)),
                  pl.BlockSpec((2 * LANES, d), lambda i: (0, 0)),
                  pl.BlockSpec((LANES, 2 * dk), lambda i: (0, 0)),
                  pl.BlockSpec((1, dk), lambda i: (0, 0))],
        out_specs=pl.BlockSpec((ts, dk), lambda i: (i, 0)),
        out_shape=jax.ShapeDtypeStruct((s, dk), F32),
        compiler_params=_params("parallel"),
        name="gla_gates",
    )(x, wl, wu, b_gate.reshape(1, dk))


def _gla_kernel(q_ref, k_ref, v_ref, la_ref, r_ref, g_ref, o_ref, state_ref):
    @pl.when(pl.program_id(1) == 0)
    def _():
        state_ref[...] = jnp.zeros_like(state_ref)

    rows, dk = q_ref.shape
    cs = GLA_CHUNK
    tri = _tri(cs)
    rr = lax.broadcasted_iota(jnp.int32, (cs, cs), 0)
    cc = lax.broadcasted_iota(jnp.int32, (cs, cs), 1)
    causal = rr >= cc
    for ci in range(rows // cs):
        sl = pl.ds(ci * cs, cs)
        b = _cumsum_rows(la_ref[sl, :], tri)
        b_mid = b[cs // 2 - 1:cs // 2, :]
        b_last = b[cs - 1:cs, :]
        q = q_ref[sl, :].astype(F32) * (dk ** -0.5)
        k = k_ref[sl, :].astype(F32)
        v = v_ref[sl, :]
        qa = (q * jnp.exp(b - b_mid)).astype(BF16)
        ka = (k * jnp.exp(b_mid - b)).astype(BF16)
        a = lax.dot_general(qa, ka, (((1,), (1,)), ((), ())), preferred_element_type=F32)
        a = jnp.where(causal, a, 0.0)
        o = jnp.dot(a.astype(BF16), v, preferred_element_type=F32)
        state = state_ref[...]
        o = o + jnp.dot((q * jnp.exp(b)).astype(BF16), state.astype(BF16), preferred_element_type=F32)
        k_end_t = (k * jnp.exp(b_last - b)).T.astype(BF16)
        decay = jnp.exp(b.T[:, cs - 1:cs])
        state_ref[...] = decay * state + jnp.dot(k_end_t, v, preferred_element_type=F32)
        o = o * lax.rsqrt(jnp.mean(o * o, axis=-1, keepdims=True) + RMS_EPS) * g_ref[...]
        r = r_ref[sl, :].astype(F32)
        o_ref[sl, :] = (o * (r * jax.nn.sigmoid(r))).astype(o_ref.dtype)


def _gla(proj, la, norm_g):
    s = proj.shape[0]
    dk_all = la.shape[1]
    nh = GLA_HEADS
    dk = dk_all // nh
    dv = norm_g.shape[0]
    rows = min(GLA_ROWS, s)
    kq, kv = dk_all // dk, 2 * dk_all // dv
    kr = kv + nh
    return pl.pallas_call(
        _gla_kernel,
        grid=(nh, s // rows),
        in_specs=[pl.BlockSpec((rows, dk), lambda h, i: (i, h)),
                  pl.BlockSpec((rows, dk), lambda h, i: (i, kq + h)),
                  pl.BlockSpec((rows, dv), lambda h, i: (i, kv + h)),
                  pl.BlockSpec((rows, dk), lambda h, i: (i, h)),
                  pl.BlockSpec((rows, dv), lambda h, i: (i, kr + h)),
                  pl.BlockSpec((1, dv), lambda h, i: (0, 0))],
        out_specs=pl.BlockSpec((rows, dv), lambda h, i: (i, h)),
        out_shape=jax.ShapeDtypeStruct((s, nh * dv), BF16),
        scratch_shapes=[pltpu.VMEM((dk, dv), F32)],
        compiler_params=_params("parallel", "arbitrary"),
        name="gla_chunks",
    )(proj, proj, proj, la, proj, norm_g.reshape(1, dv))


def kernel(x, fox_w_in, fox_b_f, fox_w_o, gla_w_in, gla_w_gate_up, gla_b_gate, gla_norm_g, gla_w_o,
           ln_mix_g, ln_mix_b, ln_ffn_g, ln_ffn_b, moe_w_group, moe_b_group, moe_w_expert,
           moe_b_expert, moe_w_gate, moe_w_up, moe_w_down):
    bsz, s, d = x.shape
    outs = []
    for bi in range(bsz):
        xt = x[bi]
        for i in range(DEPTH):
            j = i // 2
            if i % 2 == 0:
                w_in_t = fox_w_in[j].T
                qkv = _matmul_nt(xt, w_in_t, 3 * d, BF16)
                c = _fox_gates(xt, w_in_t[3 * d:], fox_b_f[j])
                o = _fox_attention(qkv, c, FOX_HEADS)
                w_o = fox_w_o[j]
            else:
                w_in_t = gla_w_in[j].T
                n_main = w_in_t.shape[0] - gla_w_gate_up.shape[1]
                proj = _matmul_nt(xt, w_in_t, n_main, BF16)
                la = _gla_gates(xt, w_in_t[n_main:], gla_w_gate_up[j], gla_b_gate[j])
                o = _gla(proj, la, gla_norm_g[j])
                w_o = gla_w_o[j]
            xt, route, counts = _proj_ln_route(o, w_o.astype(BF16), xt, ln_mix_g[i], ln_mix_b[i],
                                               moe_w_group[i], moe_b_group[i], moe_w_expert[i], moe_b_expert[i])
            xt = _moe_ffn_ln(xt, route, counts, moe_w_gate, moe_w_up, moe_w_down, i, ln_ffn_g[i], ln_ffn_b[i])
        outs.append(xt)
    return outs[0].reshape(1, s, d) if bsz == 1 else jnp.stack(outs, axis=0)
```

```python
import functools

import jax
import jax.numpy as jnp
from jax import lax
from jax.experimental import pallas as pl
from jax.experimental.pallas import tpu as pltpu

F32 = jnp.float32
BF16 = jnp.bfloat16

DEPTH = 2
FOX_HEADS = 16
FOX_HEAD_DIM = 128
GLA_HEADS = 4
GLA_CHUNK = 64
GLA_GATE_TAU = 16.0
N_GROUPS = 8
EXPERTS_PER_GROUP = 8
N_EXPERTS = N_GROUPS * EXPERTS_PER_GROUP
TOP_K = 2
DEEPNORM_ALPHA = (2 * DEPTH) ** 0.25
LN_EPS = 1e-5
RMS_EPS = 1e-6

LANES = 128
VMEM_LIMIT = 56 * 2**20

MM_TM, MM_TN = 1024, 1024
ROW_TILE = 512
ATT_TQ, ATT_TK = 1024, 512
GLA_ROWS = 256
MOE_BLOCK = 128
MOE_W_SLOTS = 3
COMB_TM = 256


def _params(*sem):
    return pltpu.CompilerParams(dimension_semantics=sem, vmem_limit_bytes=VMEM_LIMIT)


def _split2(a):
    hi = a.astype(BF16)
    lo = (a - hi.astype(F32)).astype(BF16)
    return hi, lo


def _split3(a):
    hi = a.astype(BF16)
    r = a - hi.astype(F32)
    mid = r.astype(BF16)
    lo = (r - mid.astype(F32)).astype(BF16)
    return hi, mid, lo


def _pack_hi_lo(w_t, n_pad=LANES):
    n, k = w_t.shape
    wp = jnp.zeros((n_pad, k), F32).at[:n].set(w_t)
    hi, lo = _split2(wp)
    return jnp.concatenate([hi, lo], axis=0)


def _dot_nt(a, b):
    return lax.dot_general(a, b, (((1,), (1,)), ((), ())), preferred_element_type=F32)


def _dot_hi_lo(x, whl_ref, n_pad=LANES):
    xh, xl = _split2(x)
    a = _dot_nt(xh, whl_ref[...])
    b = _dot_nt(xl, whl_ref[:n_pad, :])
    return a[:, :n_pad] + a[:, n_pad:] + b


def _log_sigmoid(x):
    return -(jnp.maximum(-x, 0.0) + jnp.log1p(jnp.exp(-jnp.abs(x))))


def _cumsum_rows(a, incl_tri):
    n = a.shape[1]
    parts = jnp.concatenate(_split3(a), axis=1)
    c = jnp.dot(incl_tri, parts, preferred_element_type=F32)
    return c[:, :n] + c[:, n:2 * n] + c[:, 2 * n:]


def _tri(n, strict=False):
    r = lax.broadcasted_iota(jnp.int32, (n, n), 0)
    c = lax.broadcasted_iota(jnp.int32, (n, n), 1)
    return jnp.where((r > c) if strict else (r >= c), 1.0, 0.0).astype(BF16)


def _layer_norm_rows(z, g, b):
    mu = jnp.mean(z, axis=-1, keepdims=True)
    d = z - mu
    var = jnp.mean(d * d, axis=-1, keepdims=True)
    return d * lax.rsqrt(var + LN_EPS) * g + b


def _mm_kernel(x_ref, w_ref, o_ref, wb_ref):
    @pl.when(pl.program_id(1) == 0)
    def _():
        wb_ref[...] = w_ref[...].astype(BF16)

    o_ref[...] = lax.dot_general(x_ref[...].astype(BF16), wb_ref[...], (((1,), (1,)), ((), ())),
                                 preferred_element_type=F32).astype(o_ref.dtype)


def _matmul_nt(x, w_t, n, out_dtype):
    m, k = x.shape
    tm, tn = min(MM_TM, m), min(MM_TN, n)
    return pl.pallas_call(
        _mm_kernel,
        grid=(n // tn, m // tm),
        in_specs=[pl.BlockSpec((tm, k), lambda j, i: (i, 0)),
                  pl.BlockSpec((tn, k), lambda j, i: (j, 0))],
        out_specs=pl.BlockSpec((tm, tn), lambda j, i: (i, j)),
        out_shape=jax.ShapeDtypeStruct((m, n), out_dtype),
        scratch_shapes=[pltpu.VMEM((tn, k), BF16)],
        compiler_params=_params("arbitrary", "arbitrary"),
        name="dense_proj",
    )(x, w_t)


def _fox_gate_kernel(x_ref, w_ref, b_ref, c_ref, carry_ref):
    @pl.when(pl.program_id(0) == 0)
    def _():
        carry_ref[...] = jnp.zeros_like(carry_ref)

    ts = x_ref.shape[0]
    logits = _dot_hi_lo(x_ref[...], w_ref) + b_ref[...]
    log_f = _log_sigmoid(logits)
    c = _cumsum_rows(log_f, _tri(ts)) + carry_ref[...]
    carry_ref[...] = c[ts - 1:ts, :]
    c_ref[...] = c


def _fox_gates(x, w_f_t, b_f):
    s, d = x.shape
    h = w_f_t.shape[0]
    ts = min(ROW_TILE, s)
    whl = _pack_hi_lo(w_f_t)
    bias = jnp.zeros((1, LANES), F32).at[0, :h].set(b_f)
    return pl.pallas_call(
        _fox_gate_kernel,
        grid=(s // ts,),
        in_specs=[pl.BlockSpec((ts, d), lambda i: (i, 0)),
                  pl.BlockSpec((2 * LANES, d), lambda i: (0, 0)),
                  pl.BlockSpec((1, LANES), lambda i: (0, 0))],
        out_specs=pl.BlockSpec((ts, LANES), lambda i: (i, 0)),
        out_shape=jax.ShapeDtypeStruct((s, LANES), F32),
        scratch_shapes=[pltpu.VMEM((1, LANES), F32)],
        compiler_params=_params("arbitrary"),
        name="fox_gates",
    )(x, whl, bias)


def _fox_attn_kernel(q_ref, k_ref, v_ref, c_ref, o_ref, kaug_ref, vt_ref, qt_ref, st0_ref, st1_ref,
                     p0_ref, p1_ref, alpha_ref, acc_ref, m_ref):
    h = pl.program_id(0)
    qi = pl.program_id(1)
    tq, dh = q_ref.shape
    s_len = k_ref.shape[0]
    tk = ATT_TK
    log2e = 1.4426950408889634
    scale = dh ** -0.5 * log2e

    def head_column(rows, n):
        lane = lax.broadcasted_iota(jnp.int32, (n, LANES), 1)
        return log2e * jnp.sum(jnp.where(lane == h, c_ref[rows, :], 0.0), axis=1, keepdims=True)

    def bias_columns(col, first, n):
        lane = lax.broadcasted_iota(jnp.int32, (n, LANES), 1)
        hi, mid, lo = _split3(col)
        ones_first = 3 - first
        out = jnp.where((lane >= ones_first) & (lane < ones_first + 3), 1.0, 0.0)
        out = jnp.where(lane == first, hi.astype(F32), out)
        out = jnp.where(lane == first + 1, mid.astype(F32), out)
        return jnp.where(lane == first + 2, lo.astype(F32), out).astype(BF16)

    @pl.when(qi == 0)
    def _():
        ones_row = jnp.where(lax.broadcasted_iota(jnp.int32, (16, tk), 0) == 0, 1.0, 0.0).astype(BF16)

        def build(ci, carry):
            rows = pl.ds(pl.multiple_of(ci * tk, tk), tk)
            kaug_ref[rows, :dh] = k_ref[rows, :]
            kaug_ref[rows, dh:] = bias_columns(-head_column(rows, tk), 0, tk)
            vt_ref[:dh, rows] = v_ref[rows, :].astype(F32).T.astype(BF16)
            vt_ref[dh:, rows] = ones_row
            return carry
        lax.fori_loop(0, s_len // tk, build, 0)

    q_rows = pl.ds(pl.multiple_of(qi * tq, tq), tq)
    q_aug = jnp.concatenate([q_ref[...].astype(F32) * scale,
                             bias_columns(head_column(q_rows, tq), 3, tq).astype(F32)], axis=1)
    qt_ref[...] = q_aug.T.astype(BF16)

    acc_ref[...] = jnp.zeros_like(acc_ref)
    m_ref[...] = jnp.full_like(m_ref, -jnp.inf)

    st_refs, p_refs = (st0_ref, st1_ref), (p0_ref, p1_ref)

    def scores(k_start, slot):
        st_refs[slot][...] = jnp.dot(kaug_ref[pl.ds(k_start, tk), :], qt_ref[...],
                                     preferred_element_type=F32)

    def softmax(slot, diag_offset):
        st = st_refs[slot][...]
        if diag_offset is not None:
            kr = lax.broadcasted_iota(jnp.int32, (tk, tq), 0) + diag_offset
            qc = lax.broadcasted_iota(jnp.int32, (tk, tq), 1)
            st = jnp.where(kr <= qc, st, -jnp.inf)
        m_prev = m_ref[...]
        m_new = jnp.maximum(m_prev, jnp.max(st, axis=0, keepdims=True))
        m_ref[...] = m_new
        p_refs[slot][...] = jnp.exp2(st - m_new).astype(BF16)
        alpha_ref[slot] = jnp.exp2(m_prev - m_new)

    def values(k_start, slot):
        acc_ref[...] = alpha_ref[slot] * acc_ref[...] + jnp.dot(
            vt_ref[:, pl.ds(k_start, tk)], p_refs[slot][...], preferred_element_type=F32)

    n_diag = tq // tk
    assert n_diag == 2
    d0 = pl.multiple_of(qi * tq, tk)
    d1 = pl.multiple_of(qi * tq + tk, tk)
    n_full = qi * n_diag

    def full_start(i):
        return pl.multiple_of(jnp.minimum(i, jnp.maximum(n_full - 1, 0)) * tk, tk)

    scores(d0, 0)
    scores(d1, 1)
    softmax(0, 0)
    scores(full_start(0), 0)
    softmax(1, tk)
    values(d0, 0)

    def pair(j, carry):
        i0 = 2 * j
        prev = jnp.where(j == 0, d1, (i0 - 1) * tk)
        scores(full_start(i0 + 1), 1)
        softmax(0, None)
        values(pl.multiple_of(prev, tk), 1)
        scores(full_start(i0 + 2), 0)
        softmax(1, None)
        values(pl.multiple_of(i0 * tk, tk), 0)
        return carry

    lax.fori_loop(0, qi, pair, 0)
    last = jnp.where(qi == 0, d1, (n_full - 1) * tk)
    values(pl.multiple_of(last, tk), 1)

    o_ref[...] = (acc_ref[:dh, :] / acc_ref[dh:dh + 1, :]).T.astype(o_ref.dtype)


def _fox_attention(qkv, c, n_heads):
    s = qkv.shape[0]
    dh = FOX_HEAD_DIM
    tq = min(ATT_TQ, s)
    hh = n_heads
    return pl.pallas_call(
        _fox_attn_kernel,
        grid=(hh, s // tq),
        in_specs=[pl.BlockSpec((tq, dh), lambda h, i: (i, h)),
                  pl.BlockSpec((s, dh), lambda h, i: (0, hh + h)),
                  pl.BlockSpec((s, dh), lambda h, i: (0, 2 * hh + h)),
                  pl.BlockSpec((s, LANES), lambda h, i: (0, 0))],
        out_specs=pl.BlockSpec((tq, dh), lambda h, i: (i, h)),
        out_shape=jax.ShapeDtypeStruct((s, hh * dh), BF16),
        scratch_shapes=[pltpu.VMEM((s, 2 * dh), BF16),
                        pltpu.VMEM((dh + 16, s), BF16),
                        pltpu.VMEM((2 * dh, tq), BF16),
                        pltpu.VMEM((ATT_TK, tq), F32),
                        pltpu.VMEM((ATT_TK, tq), F32),
                        pltpu.VMEM((ATT_TK, tq), BF16),
                        pltpu.VMEM((ATT_TK, tq), BF16),
                        pltpu.VMEM((2, 1, tq), F32),
                        pltpu.VMEM((dh + 16, tq), F32),
                        pltpu.VMEM((1, tq), F32)],
        compiler_params=_params("arbitrary", "arbitrary"),
        name="fox_attention",
    )(qkv, qkv, qkv, c)


def _first_lane_eq(vals, target, lane):
    return jnp.min(jnp.where(vals == target, lane, LANES), axis=1, keepdims=True)


def _route_rows(x, w_ref, b_ref, carry_ref):
    tm = x.shape[0]
    neg = -jnp.inf
    logits = _dot_hi_lo(x, w_ref) + b_ref[...]
    lane = lax.broadcasted_iota(jnp.int32, (tm, LANES), 1)

    gl = jnp.where(lane < N_GROUPS, logits, neg)
    gmax = jnp.max(gl, axis=1, keepdims=True)
    gsum = jnp.sum(jnp.exp(gl - gmax), axis=1, keepdims=True)
    grp_p = 1.0 / gsum
    grp = _first_lane_eq(gl, gmax, lane)

    lo = N_GROUPS + grp * EXPERTS_PER_GROUP
    el = jnp.where((lane >= lo) & (lane < lo + EXPERTS_PER_GROUP), logits, neg)
    emax = jnp.max(el, axis=1, keepdims=True)
    esum = jnp.sum(jnp.exp(el - emax), axis=1, keepdims=True)
    idx1 = _first_lane_eq(el, emax, lane)
    el2 = jnp.where(lane == idx1, neg, el)
    emax2 = jnp.max(el2, axis=1, keepdims=True)
    idx2 = _first_lane_eq(el2, emax2, lane)
    p1 = 1.0 / esum
    p2 = jnp.exp(emax2 - emax) / esum
    psum = p1 + p2
    g1 = grp_p * (p1 / psum)
    g2 = grp_p * (p2 / psum)

    oh1 = lane == idx1
    oh2 = lane == idx2
    both = jnp.where(oh1 | oh2, 1.0, 0.0)
    before = jnp.dot(_tri(tm, strict=True), both.astype(BF16), preferred_element_type=F32)
    before = before + carry_ref[...]
    r1 = jnp.sum(jnp.where(oh1, before, 0.0), axis=1, keepdims=True)
    r2 = jnp.sum(jnp.where(oh2, before, 0.0), axis=1, keepdims=True)
    carry_ref[...] = carry_ref[...] + jnp.sum(both, axis=0, keepdims=True)

    e1 = (idx1 - N_GROUPS).astype(F32)
    e2 = (idx2 - N_GROUPS).astype(F32)
    out = jnp.where(lane == 0, e1, 0.0)
    out = jnp.where(lane == 1, e2, out)
    out = jnp.where(lane == 2, g1, out)
    out = jnp.where(lane == 3, g2, out)
    out = jnp.where(lane == 4, r1, out)
    return jnp.where(lane == 5, r2, out)


def _proj_ln_route_kernel(o_ref, w_ref, x_ref, g_ref, b_ref, wr_ref, br_ref,
                          y_ref, route_ref, cnt_ref, carry_ref):
    @pl.when(pl.program_id(0) == 0)
    def _():
        carry_ref[...] = jnp.zeros_like(carry_ref)

    mix = jnp.dot(o_ref[...], w_ref[...], preferred_element_type=F32)
    y = _layer_norm_rows(DEEPNORM_ALPHA * x_ref[...] + mix, g_ref[...], b_ref[...])
    y_ref[...] = y
    route_ref[...] = _route_rows(y, wr_ref, br_ref, carry_ref)
    cnt_ref[...] = carry_ref[...]


def _proj_ln_route(o, w, x, g, b, w_group, b_group, w_expert, b_expert):
    s, d = x.shape
    kd = o.shape[1]
    tm = min(ROW_TILE, s)
    whl = _pack_hi_lo(jnp.concatenate([w_group.T, w_expert.T], axis=0))
    nb = N_GROUPS + N_EXPERTS
    bias = jnp.zeros((1, LANES), F32).at[0, :nb].set(jnp.concatenate([b_group, b_expert]))
    return pl.pallas_call(
        _proj_ln_route_kernel,
        grid=(s // tm,),
        in_specs=[pl.BlockSpec((tm, kd), lambda i: (i, 0)),
                  pl.BlockSpec((kd, d), lambda i: (0, 0)),
                  pl.BlockSpec((tm, d), lambda i: (i, 0)),
                  pl.BlockSpec((1, d), lambda i: (0, 0)),
                  pl.BlockSpec((1, d), lambda i: (0, 0)),
                  pl.BlockSpec((2 * LANES, d), lambda i: (0, 0)),
                  pl.BlockSpec((1, LANES), lambda i: (0, 0))],
        out_specs=[pl.BlockSpec((tm, d), lambda i: (i, 0)),
                   pl.BlockSpec((tm, LANES), lambda i: (i, 0)),
                   pl.BlockSpec((1, LANES), lambda i: (0, 0))],
        out_shape=[jax.ShapeDtypeStruct((s, d), F32),
                   jax.ShapeDtypeStruct((s, LANES), F32),
                   jax.ShapeDtypeStruct((1, LANES), F32)],
        scratch_shapes=[pltpu.VMEM((1, LANES), F32)],
        compiler_params=_params("arbitrary"),
        name="proj_ln_route",
    )(o, w, x, g.reshape(1, d), b.reshape(1, d), whl, bias)


def _dispatch_kernel(dest_ref, cnt_ref, pstart_ref, nv_ref, x_ref, xs_hbm, zero_ref, sem, zsem):
    i = pl.program_id(0)
    tm = x_ref.shape[0]
    bm = MOE_BLOCK
    n_exp = cnt_ref.shape[0]
    n_blocks = xs_hbm.shape[0] // bm

    def pad_copies(e, fn):
        cnt = cnt_ref[e]
        pos = pstart_ref[e] + cnt

        def row(r, carry):
            fn(pltpu.make_async_copy(zero_ref.at[pl.ds(0, 1), :], xs_hbm.at[pl.ds(pos + r, 1), :], zsem))
            return carry
        lax.fori_loop(0, (bm - cnt % bm) % bm, row, 0)

    def tail_copy(g):
        return pltpu.make_async_copy(zero_ref, xs_hbm.at[pl.ds(pl.multiple_of(g * bm, bm), bm), :], zsem)

    def for_all_fills(fn):
        def per_expert(e, carry):
            pad_copies(e, fn)
            return carry
        lax.fori_loop(0, n_exp, per_expert, 0)

        def per_tail(g, carry):
            fn(tail_copy(g))
            return carry
        lax.fori_loop(nv_ref[0], n_blocks, per_tail, 0)

    @pl.when(i == 0)
    def _():
        zero_ref[...] = jnp.zeros_like(zero_ref)
        for_all_fills(lambda c: c.start())

    base = i * (TOP_K * tm)

    def body(r, carry):
        for k in range(TOP_K):
            dst = dest_ref[base + TOP_K * r + k]
            pltpu.make_async_copy(x_ref.at[pl.ds(r, 1), :], xs_hbm.at[pl.ds(dst, 1), :], sem).start(priority=k)
        return carry
    lax.fori_loop(0, tm, body, 0, unroll=8)

    @pl.when(i == 0)
    def _():
        for_all_fills(lambda c: c.wait())

    for _ in range(TOP_K):
        pltpu.make_async_copy(x_ref, xs_hbm.at[pl.ds(0, tm), :], sem).wait()


def _dispatch(x, dest, cnt, pstart, n_valid, n_slots):
    t, d = x.shape
    tm = min(ROW_TILE, t)
    grid_spec = pltpu.PrefetchScalarGridSpec(
        num_scalar_prefetch=4,
        grid=(t // tm,),
        in_specs=[pl.BlockSpec((tm, d), lambda i, *_: (i, 0))],
        out_specs=pl.BlockSpec(memory_space=pl.ANY),
        scratch_shapes=[pltpu.VMEM((MOE_BLOCK, d), F32),
                        pltpu.SemaphoreType.DMA,
                        pltpu.SemaphoreType.DMA],
    )
    return pl.pallas_call(
        _dispatch_kernel,
        grid_spec=grid_spec,
        out_shape=jax.ShapeDtypeStruct((n_slots, d), F32),
        compiler_params=_params("arbitrary"),
        name="moe_dispatch",
    )(dest, cnt, pstart, n_valid, x)


def _expert_kernel(nblk_ref, bstart_ref, nv_ref, xs_hbm, wg_hbm, wu_hbm, wd_hbm, y_hbm,
                   xin_ref, yout_ref, wgf_ref, wuf_ref, wdf_ref, wgb_ref, wub_ref, wdb_ref,
                   in_sem, out_sem, w_sem, *, layer):
    e = pl.program_id(0)
    n_exp = pl.num_programs(0)
    n_valid = nv_ref[0]
    bm = MOE_BLOCK
    n_blocks = y_hbm.shape[0] // bm
    ns = MOE_W_SLOTS

    half = wd_hbm.shape[2] // 2

    def w_copies(ex, slot):
        return (pltpu.make_async_copy(wg_hbm.at[layer, ex], wgf_ref.at[slot], w_sem.at[slot]),
                pltpu.make_async_copy(wu_hbm.at[layer, ex], wuf_ref.at[slot], w_sem.at[slot]),
                pltpu.make_async_copy(wd_hbm.at[layer, ex, pl.ds(0, half)],
                                      wdf_ref.at[slot, pl.ds(0, half)], w_sem.at[slot]),
                pltpu.make_async_copy(wd_hbm.at[layer, ex, pl.ds(half, half)],
                                      wdf_ref.at[slot, pl.ds(half, half)], w_sem.at[slot]))

    def start_weights(ex):
        exc = jnp.minimum(ex, n_exp - 1)

        @pl.when((ex < n_exp) & (nblk_ref[exc] > 0))
        def _():
            for piece, c in enumerate(w_copies(exc, exc % ns)):
                c.start(priority=piece % 2)

    @pl.when(e == 0)
    def _():
        for ahead in range(ns - 1):
            start_weights(ahead)

    start_weights(e + ns - 1)

    def in_copy(g, slot):
        return pltpu.make_async_copy(xs_hbm.at[pl.ds(pl.multiple_of(g * bm, bm), bm), :],
                                     xin_ref.at[slot], in_sem.at[slot])

    def out_copy(g, slot):
        return pltpu.make_async_copy(yout_ref.at[slot],
                                     y_hbm.at[pl.ds(pl.multiple_of(g * bm, bm), bm), :], out_sem.at[slot])

    @pl.when((e == 0) & (n_valid > 0))
    def _():
        in_copy(0, 0).start(priority=0)

    nb = nblk_ref[e]
    g0 = bstart_ref[e]

    @pl.when(nb > 0)
    def _():
        w_slot = e % ns
        for c in w_copies(e, w_slot):
            c.wait()
        wgb_ref[...] = wgf_ref[w_slot].astype(BF16)
        wub_ref[...] = wuf_ref[w_slot].astype(BF16)
        wdb_ref[...] = wdf_ref[w_slot].astype(BF16)

        def block(j, carry):
            g = g0 + j
            slot = g % 2
            in_copy(g, slot).wait()

            @pl.when(g + 1 < n_valid)
            def _():
                in_copy(g + 1, 1 - slot).start(priority=0)

            @pl.when(g >= 2)
            def _():
                out_copy(g - 2, slot).wait()

            xb = xin_ref[slot].astype(BF16)
            gate = jnp.dot(xb, wgb_ref[...], preferred_element_type=F32)
            up = jnp.dot(xb, wub_ref[...], preferred_element_type=F32)
            hid = (gate * jax.nn.sigmoid(gate) * up).astype(BF16)
            yout_ref[slot] = jnp.dot(hid, wdb_ref[...], preferred_element_type=F32)
            out_copy(g, slot).start(priority=1)
            return carry
        lax.fori_loop(0, nb, block, 0)

    @pl.when(e == pl.num_programs(0) - 1)
    def _():
        for back in (2, 1):
            @pl.when(n_valid >= back)
            def _(back=back):
                out_copy(n_valid - back, (n_valid - back) % 2).wait()

        yout_ref[0] = jnp.zeros(yout_ref.shape[1:], yout_ref.dtype)

        def fill(g, carry):
            out_copy(g, 0).start()
            return carry
        lax.fori_loop(n_valid, n_blocks, fill, 0)

        def drain(g, carry):
            out_copy(g, 0).wait()
            return carry
        lax.fori_loop(n_valid, n_blocks, drain, 0)


def _experts(xs, nblk, bstart, n_valid, w_gate, w_up, w_down, layer):
    p, d = xs.shape
    _, n_exp, _, f = w_gate.shape
    bm = MOE_BLOCK
    ns = MOE_W_SLOTS
    grid_spec = pltpu.PrefetchScalarGridSpec(
        num_scalar_prefetch=3,
        grid=(n_exp,),
        in_specs=[pl.BlockSpec(memory_space=pl.ANY)] * 4,
        out_specs=pl.BlockSpec(memory_space=pl.ANY),
        scratch_shapes=[pltpu.VMEM((2, bm, d), F32),
                        pltpu.VMEM((2, bm, d), F32),
                        pltpu.VMEM((ns, d, f), F32),
                        pltpu.VMEM((ns, d, f), F32),
                        pltpu.VMEM((ns, f, d), F32),
                        pltpu.VMEM((d, f), BF16),
                        pltpu.VMEM((d, f), BF16),
                        pltpu.VMEM((f, d), BF16),
                        pltpu.SemaphoreType.DMA((2,)),
                        pltpu.SemaphoreType.DMA((2,)),
                        pltpu.SemaphoreType.DMA((ns,))],
    )
    return pl.pallas_call(
        functools.partial(_expert_kernel, layer=layer),
        grid_spec=grid_spec,
        out_shape=jax.ShapeDtypeStruct((p, d), F32),
        compiler_params=_params("arbitrary"),
        name="moe_experts",
    )(nblk, bstart, n_valid, xs, w_gate, w_up, w_down)


def _combine_ln_kernel(dest_ref, y_hbm, x_ref, route_ref, g_ref, b_ref, o_ref, yg_ref, sem):
    i = pl.program_id(0)
    n = pl.num_programs(0)
    tm = x_ref.shape[0]

    def start_gather(step, slot):
        base = step * (TOP_K * tm)

        def body(r, carry):
            for k in range(TOP_K):
                src = dest_ref[base + TOP_K * r + k]
                pltpu.make_async_copy(y_hbm.at[pl.ds(src, 1), :],
                                      yg_ref.at[slot, pl.ds(k * tm + r, 1), :], sem.at[slot]).start(priority=k)
            return carry
        lax.fori_loop(0, tm, body, 0, unroll=8)

    def wait_gather(slot):
        pltpu.make_async_copy(y_hbm.at[pl.ds(0, TOP_K * tm), :], yg_ref.at[slot], sem.at[slot]).wait()

    @pl.when(i == 0)
    def _():
        start_gather(0, 0)

    @pl.when(i + 1 < n)
    def _():
        start_gather(i + 1, (i + 1) % 2)

    slot = i % 2
    wait_gather(slot)
    route = route_ref[...]
    lane = lax.broadcasted_iota(jnp.int32, route.shape, 1)
    g1 = jnp.sum(jnp.where(lane == 2, route, 0.0), axis=1, keepdims=True)
    g2 = jnp.sum(jnp.where(lane == 3, route, 0.0), axis=1, keepdims=True)
    y1 = yg_ref[slot, pl.ds(0, tm), :]
    y2 = yg_ref[slot, pl.ds(tm, tm), :]
    z = DEEPNORM_ALPHA * x_ref[...] + (y1 * g1 + y2 * g2)
    o_ref[...] = _layer_norm_rows(z, g_ref[...], b_ref[...])


def _combine_ln(y, dest, x, route, g, b):
    t, d = x.shape
    tm = min(COMB_TM, t)
    grid_spec = pltpu.PrefetchScalarGridSpec(
        num_scalar_prefetch=1,
        grid=(t // tm,),
        in_specs=[pl.BlockSpec(memory_space=pl.ANY),
                  pl.BlockSpec((tm, d), lambda i, dest: (i, 0)),
                  pl.BlockSpec((tm, LANES), lambda i, dest: (i, 0)),
                  pl.BlockSpec((1, d), lambda i, dest: (0, 0)),
                  pl.BlockSpec((1, d), lambda i, dest: (0, 0))],
        out_specs=pl.BlockSpec((tm, d), lambda i, dest: (i, 0)),
        scratch_shapes=[pltpu.VMEM((2, 2 * tm, d), F32),
                        pltpu.SemaphoreType.DMA((2,))],
    )
    return pl.pallas_call(
        _combine_ln_kernel,
        grid_spec=grid_spec,
        out_shape=jax.ShapeDtypeStruct((t, d), F32),
        compiler_params=_params("arbitrary"),
        name="moe_combine_ln",
    )(dest, y, x, route, g.reshape(1, d), b.reshape(1, d))


def _moe_ffn_ln(x, route, counts, w_gate, w_up, w_down, layer, ln_g, ln_b):
    t, d = x.shape
    bm = MOE_BLOCK
    eid = route[:, 0:TOP_K].astype(jnp.int32)
    rank = route[:, 4:4 + TOP_K].astype(jnp.int32)
    cnt = counts[0, N_GROUPS:N_GROUPS + N_EXPERTS].astype(jnp.int32)
    nblk = (cnt + bm - 1) // bm
    bend = jnp.cumsum(nblk)
    bstart = bend - nblk
    pstart = bstart * bm
    onehot = eid[:, :, None] == jnp.arange(N_EXPERTS, dtype=jnp.int32)
    dest = (jnp.sum(jnp.where(onehot, pstart, 0), axis=-1) + rank).reshape(-1)
    n_valid = bend[-1:]
    n_slots = (-(-t * TOP_K // bm) + N_EXPERTS) * bm
    xs = _dispatch(x, dest, cnt, pstart, n_valid, n_slots)
    y = _experts(xs, nblk, bstart, n_valid, w_gate, w_up, w_down, layer)
    return _combine_ln(y, dest, x, route, ln_g, ln_b)


def _gla_gate_kernel(x_ref, wl_ref, wu_ref, b_ref, la_ref):
    g_low = _dot_hi_lo(x_ref[...], wl_ref)
    n = wu_ref.shape[1] // 2
    gh, gl = _split2(g_low)
    a = jnp.dot(gh, wu_ref[...], preferred_element_type=F32)
    c = jnp.dot(gl, wu_ref[:, :n], preferred_element_type=F32)
    logit = a[:, :n] + a[:, n:] + c + b_ref[...]
    la_ref[...] = _log_sigmoid(logit) / GLA_GATE_TAU


def _gla_gates(x, w_low_t, w_gate_up, b_gate):
    s, d = x.shape
    rank, dk = w_gate_up.shape
    ts = min(ROW_TILE, s)
    wl = _pack_hi_lo(w_low_t)
    wu_pad = jnp.zeros((LANES, dk), F32).at[:rank].set(w_gate_up)
    wu = jnp.concatenate(_split2(wu_pad), axis=1)
    return pl.pallas_call(
        _gla_gate_kernel,
        grid=(s // ts,),
        in_specs=[pl.BlockSpec((ts, d), lambda i: (i, 0)),
                  pl.BlockSpec((2 * LANES, d), lambda i: (0, 0)),
                  pl.BlockSpec((LANES, 2 * dk), lambda i: (0, 0)),
                  pl.BlockSpec((1, dk), lambda i: (0, 0))],
        out_specs=pl.BlockSpec((ts, dk), lambda i: (i, 0)),
        out_shape=jax.ShapeDtypeStruct((s, dk), F32),
        compiler_params=_params("parallel"),
        name="gla_gates",
    )(x, wl, wu, b_gate.reshape(1, dk))


def _gla_kernel(q_ref, k_ref, v_ref, la_ref, r_ref, g_ref, o_ref, state_ref):
    @pl.when(pl.program_id(0) == 0)
    def _():
        state_ref[...] = jnp.zeros_like(state_ref)

    rows = q_ref.shape[0]
    nh, dk, dv = state_ref.shape
    cs = GLA_CHUNK
    tri = _tri(cs)
    rr = lax.broadcasted_iota(jnp.int32, (cs, cs), 0)
    cc = lax.broadcasted_iota(jnp.int32, (cs, cs), 1)
    causal = rr >= cc
    for ci in range(rows // cs):
        sl = pl.ds(ci * cs, cs)
        for h in range(nh):
            kc = pl.ds(h * dk, dk)
            vc = pl.ds(h * dv, dv)
            b = _cumsum_rows(la_ref[sl, kc], tri)
            b_mid = b[cs // 2 - 1:cs // 2, :]
            b_last = b[cs - 1:cs, :]
            q = q_ref[sl, kc].astype(F32) * (dk ** -0.5)
            k = k_ref[sl, kc].astype(F32)
            v = v_ref[sl, vc]
            qa = (q * jnp.exp(b - b_mid)).astype(BF16)
            ka = (k * jnp.exp(b_mid - b)).astype(BF16)
            a = jnp.where(causal, _dot_nt(qa, ka), 0.0)
            o = jnp.dot(a.astype(BF16), v, preferred_element_type=F32)
            state = state_ref[h]
            o = o + jnp.dot((q * jnp.exp(b)).astype(BF16), state.astype(BF16), preferred_element_type=F32)
            k_end_t = (k * jnp.exp(b_last - b)).T.astype(BF16)
            decay = jnp.exp(b.T[:, cs - 1:cs])
            state_ref[h] = decay * state + jnp.dot(k_end_t, v, preferred_element_type=F32)
            o = o * lax.rsqrt(jnp.mean(o * o, axis=-1, keepdims=True) + RMS_EPS) * g_ref[...]
            r = r_ref[sl, vc].astype(F32)
            o_ref[sl, vc] = (o * (r * jax.nn.sigmoid(r))).astype(o_ref.dtype)


def _gla(proj, la, norm_g):
    s = proj.shape[0]
    dk_all = la.shape[1]
    nh = GLA_HEADS
    dk = dk_all // nh
    dv = norm_g.shape[0]
    dv_all = nh * dv
    assert 2 * dk_all == dv_all
    rows = min(GLA_ROWS, s)
    return pl.pallas_call(
        _gla_kernel,
        grid=(s // rows,),
        in_specs=[pl.BlockSpec((rows, dk_all), lambda i: (i, 0)),
                  pl.BlockSpec((rows, dk_all), lambda i: (i, 1)),
                  pl.BlockSpec((rows, dv_all), lambda i: (i, 1)),
                  pl.BlockSpec((rows, dk_all), lambda i: (i, 0)),
                  pl.BlockSpec((rows, dv_all), lambda i: (i, 2)),
                  pl.BlockSpec((1, dv), lambda i: (0, 0))],
        out_specs=pl.BlockSpec((rows, dv_all), lambda i: (i, 0)),
        out_shape=jax.ShapeDtypeStruct((s, dv_all), BF16),
        scratch_shapes=[pltpu.VMEM((nh, dk, dv), F32)],
        compiler_params=_params("arbitrary"),
        name="gla_chunks",
    )(proj, proj, proj, la, proj, norm_g.reshape(1, dv))


def kernel(x, fox_w_in, fox_b_f, fox_w_o, gla_w_in, gla_w_gate_up, gla_b_gate, gla_norm_g, gla_w_o,
           ln_mix_g, ln_mix_b, ln_ffn_g, ln_ffn_b, moe_w_group, moe_b_group, moe_w_expert,
           moe_b_expert, moe_w_gate, moe_w_up, moe_w_down):
    bsz, s, d = x.shape
    outs = []
    for bi in range(bsz):
        xt = x[bi]
        for i in range(DEPTH):
            j = i // 2
            if i % 2 == 0:
                w_in_t = fox_w_in[j].T
                qkv = _matmul_nt(xt, w_in_t, 3 * d, BF16)
                c = _fox_gates(xt, w_in_t[3 * d:], fox_b_f[j])
                o = _fox_attention(qkv, c, FOX_HEADS)
                w_o = fox_w_o[j]
            else:
                w_in_t = gla_w_in[j].T
                n_main = w_in_t.shape[0] - gla_w_gate_up.shape[1]
                proj = _matmul_nt(xt, w_in_t, n_main, BF16)
                la = _gla_gates(xt, w_in_t[n_main:], gla_w_gate_up[j], gla_b_gate[j])
                o = _gla(proj, la, gla_norm_g[j])
                w_o = gla_w_o[j]
            xt, route, counts = _proj_ln_route(o, w_o.astype(BF16), xt, ln_mix_g[i], ln_mix_b[i],
                                               moe_w_group[i], moe_b_group[i], moe_w_expert[i], moe_b_expert[i])
            xt = _moe_ffn_ln(xt, route, counts, moe_w_gate, moe_w_up, moe_w_down, i, ln_ffn_g[i], ln_ffn_b[i])
        outs.append(xt)
    return outs[0].reshape(1, s, d) if bsz == 1 else jnp.stack(outs, axis=0)
```

```python
import functools

import jax
import jax.numpy as jnp
from jax import lax
from jax.experimental import pallas as pl
from jax.experimental.pallas import tpu as pltpu

F32 = jnp.float32
BF16 = jnp.bfloat16

DEPTH = 2
FOX_HEADS = 16
FOX_HEAD_DIM = 128
GLA_HEADS = 4
GLA_CHUNK = 64
GLA_GATE_TAU = 16.0
N_GROUPS = 8
EXPERTS_PER_GROUP = 8
N_EXPERTS = N_GROUPS * EXPERTS_PER_GROUP
TOP_K = 2
DEEPNORM_ALPHA = (2 * DEPTH) ** 0.25
LN_EPS = 1e-5
RMS_EPS = 1e-6

LANES = 128
VMEM_LIMIT = 56 * 2**20

MM_TM, MM_TN = 1024, 1024
ROW_TILE = 512
ATT_TQ, ATT_TK = 1024, 512
GLA_ROWS = 256
MOE_BLOCK = 128
MOE_W_SLOTS = 3
COMB_TM = 256


def _params(*sem):
    return pltpu.CompilerParams(dimension_semantics=sem, vmem_limit_bytes=VMEM_LIMIT)


def _split2(a):
    hi = a.astype(BF16)
    lo = (a - hi.astype(F32)).astype(BF16)
    return hi, lo


def _split3(a):
    hi = a.astype(BF16)
    r = a - hi.astype(F32)
    mid = r.astype(BF16)
    lo = (r - mid.astype(F32)).astype(BF16)
    return hi, mid, lo


def _pack_hi_lo(w_t, n_pad=LANES):
    n, k = w_t.shape
    wp = jnp.zeros((n_pad, k), F32).at[:n].set(w_t)
    hi, lo = _split2(wp)
    return jnp.concatenate([hi, lo], axis=0)


def _dot_nt(a, b):
    return lax.dot_general(a, b, (((1,), (1,)), ((), ())), preferred_element_type=F32)


def _dot_hi_lo(x, whl_ref, n_pad=LANES):
    xh, xl = _split2(x)
    a = _dot_nt(xh, whl_ref[...])
    b = _dot_nt(xl, whl_ref[:n_pad, :])
    return a[:, :n_pad] + a[:, n_pad:] + b


def _pack_rows(a):
    half = a.shape[1] // 2
    hi = lax.bitcast_convert_type(a[:, :half].astype(BF16).astype(F32), jnp.uint32)
    lo = lax.bitcast_convert_type(a[:, half:].astype(BF16).astype(F32), jnp.uint32)
    return hi | (lo >> 16)


def _unpack_rows(w):
    hi = lax.bitcast_convert_type(w & jnp.uint32(0xFFFF0000), F32)
    lo = lax.bitcast_convert_type(w << 16, F32)
    return jnp.concatenate([hi, lo], axis=1)


def _log_sigmoid(x):
    return -(jnp.maximum(-x, 0.0) + jnp.log1p(jnp.exp(-jnp.abs(x))))


def _cumsum_rows(a, incl_tri):
    n = a.shape[1]
    parts = jnp.concatenate(_split3(a), axis=1)
    c = jnp.dot(incl_tri, parts, preferred_element_type=F32)
    return c[:, :n] + c[:, n:2 * n] + c[:, 2 * n:]


def _tri(n, strict=False):
    r = lax.broadcasted_iota(jnp.int32, (n, n), 0)
    c = lax.broadcasted_iota(jnp.int32, (n, n), 1)
    return jnp.where((r > c) if strict else (r >= c), 1.0, 0.0).astype(BF16)


def _layer_norm_rows(z, g, b):
    mu = jnp.mean(z, axis=-1, keepdims=True)
    d = z - mu
    var = jnp.mean(d * d, axis=-1, keepdims=True)
    return d * lax.rsqrt(var + LN_EPS) * g + b


def _mm_kernel(x_ref, w_ref, o_ref, wb_ref):
    @pl.when(pl.program_id(1) == 0)
    def _():
        wb_ref[...] = w_ref[...].astype(BF16)

    o_ref[...] = lax.dot_general(x_ref[...].astype(BF16), wb_ref[...], (((1,), (1,)), ((), ())),
                                 preferred_element_type=F32).astype(o_ref.dtype)


def _matmul_nt(x, w_t, n, out_dtype):
    m, k = x.shape
    tm, tn = min(MM_TM, m), min(MM_TN, n)
    return pl.pallas_call(
        _mm_kernel,
        grid=(n // tn, m // tm),
        in_specs=[pl.BlockSpec((tm, k), lambda j, i: (i, 0)),
                  pl.BlockSpec((tn, k), lambda j, i: (j, 0))],
        out_specs=pl.BlockSpec((tm, tn), lambda j, i: (i, j)),
        out_shape=jax.ShapeDtypeStruct((m, n), out_dtype),
        scratch_shapes=[pltpu.VMEM((tn, k), BF16)],
        compiler_params=_params("arbitrary", "arbitrary"),
        name="dense_proj",
    )(x, w_t)


def _fox_gate_kernel(x_ref, w_ref, b_ref, c_ref, carry_ref):
    @pl.when(pl.program_id(0) == 0)
    def _():
        carry_ref[...] = jnp.zeros_like(carry_ref)

    ts = x_ref.shape[0]
    logits = _dot_hi_lo(x_ref[...], w_ref) + b_ref[...]
    log_f = _log_sigmoid(logits)
    c = _cumsum_rows(log_f, _tri(ts)) + carry_ref[...]
    carry_ref[...] = c[ts - 1:ts, :]
    c_ref[...] = c


def _fox_gates(x, w_f_t, b_f):
    s, d = x.shape
    h = w_f_t.shape[0]
    ts = min(ROW_TILE, s)
    whl = _pack_hi_lo(w_f_t)
    bias = jnp.zeros((1, LANES), F32).at[0, :h].set(b_f)
    return pl.pallas_call(
        _fox_gate_kernel,
        grid=(s // ts,),
        in_specs=[pl.BlockSpec((ts, d), lambda i: (i, 0)),
                  pl.BlockSpec((2 * LANES, d), lambda i: (0, 0)),
                  pl.BlockSpec((1, LANES), lambda i: (0, 0))],
        out_specs=pl.BlockSpec((ts, LANES), lambda i: (i, 0)),
        out_shape=jax.ShapeDtypeStruct((s, LANES), F32),
        scratch_shapes=[pltpu.VMEM((1, LANES), F32)],
        compiler_params=_params("arbitrary"),
        name="fox_gates",
    )(x, whl, bias)


def _fox_attn_kernel(q_ref, k_ref, v_ref, c_ref, o_ref, kaug_ref, vt_ref, qt_ref, st0_ref, st1_ref,
                     p0_ref, p1_ref, alpha_ref, bmax_ref, acc_ref, m_ref):
    h = pl.program_id(0)
    qi = pl.program_id(1)
    tq, dh = q_ref.shape
    s_len = k_ref.shape[0]
    tk = ATT_TK
    log2e = 1.4426950408889634
    scale = dh ** -0.5 * log2e

    def head_column(rows, n):
        lane = lax.broadcasted_iota(jnp.int32, (n, LANES), 1)
        return log2e * jnp.sum(jnp.where(lane == h, c_ref[rows, :], 0.0), axis=1, keepdims=True)

    def bias_columns(col, first, n):
        lane = lax.broadcasted_iota(jnp.int32, (n, LANES), 1)
        hi, mid, lo = _split3(col)
        ones_first = 3 - first
        out = jnp.where((lane >= ones_first) & (lane < ones_first + 3), 1.0, 0.0)
        out = jnp.where(lane == first, hi.astype(F32), out)
        out = jnp.where(lane == first + 1, mid.astype(F32), out)
        return jnp.where(lane == first + 2, lo.astype(F32), out).astype(BF16)

    @pl.when(qi == 0)
    def _():
        ones_row = jnp.where(lax.broadcasted_iota(jnp.int32, (16, tk), 0) == 0, 1.0, 0.0).astype(BF16)

        def build(ci, carry):
            rows = pl.ds(pl.multiple_of(ci * tk, tk), tk)
            kaug_ref[rows, :dh] = k_ref[rows, :]
            kaug_ref[rows, dh:] = bias_columns(-head_column(rows, tk), 0, tk)
            vt_ref[:dh, rows] = v_ref[rows, :].astype(F32).T.astype(BF16)
            vt_ref[dh:, rows] = ones_row
            return carry
        lax.fori_loop(0, s_len // tk, build, 0)

    q_rows = pl.ds(pl.multiple_of(qi * tq, tq), tq)
    q_aug = jnp.concatenate([q_ref[...].astype(F32) * scale,
                             bias_columns(head_column(q_rows, tq), 3, tq).astype(F32)], axis=1)
    qt_ref[...] = q_aug.T.astype(BF16)

    acc_ref[...] = jnp.zeros_like(acc_ref)
    m_ref[...] = jnp.full_like(m_ref, -jnp.inf)

    st_refs, p_refs = (st0_ref, st1_ref), (p0_ref, p1_ref)

    def scores(k_start, slot, diag_offset=None):
        st = jnp.dot(kaug_ref[pl.ds(k_start, tk), :], qt_ref[...], preferred_element_type=F32)
        if diag_offset is not None:
            kr = lax.broadcasted_iota(jnp.int32, (tk, tq), 0) + diag_offset
            qc = lax.broadcasted_iota(jnp.int32, (tk, tq), 1)
            st = jnp.where(kr <= qc, st, -jnp.inf)
        st_refs[slot][...] = st
        bmax_ref[slot] = jnp.max(st, axis=0, keepdims=True)

    def softmax(slot):
        m_prev = m_ref[...]
        m_new = jnp.maximum(m_prev, bmax_ref[slot])
        m_ref[...] = m_new
        p_refs[slot][...] = jnp.exp2(st_refs[slot][...] - m_new).astype(BF16)
        alpha_ref[slot] = jnp.exp2(m_prev - m_new)

    def values(k_start, slot):
        acc_ref[...] = alpha_ref[slot] * acc_ref[...] + jnp.dot(
            vt_ref[:, pl.ds(k_start, tk)], p_refs[slot][...], preferred_element_type=F32)

    n_diag = tq // tk
    assert n_diag == 2
    d0 = pl.multiple_of(qi * tq, tk)
    d1 = pl.multiple_of(qi * tq + tk, tk)
    n_full = qi * n_diag

    def full_start(i):
        return pl.multiple_of(jnp.minimum(i, jnp.maximum(n_full - 1, 0)) * tk, tk)

    scores(d0, 0, 0)
    scores(d1, 1, tk)
    softmax(0)
    scores(full_start(0), 0)
    softmax(1)
    values(d0, 0)

    def pair(j, carry):
        i0 = 2 * j
        prev = jnp.where(j == 0, d1, (i0 - 1) * tk)
        scores(full_start(i0 + 1), 1)
        softmax(0)
        values(pl.multiple_of(prev, tk), 1)
        scores(full_start(i0 + 2), 0)
        softmax(1)
        values(pl.multiple_of(i0 * tk, tk), 0)
        return carry

    lax.fori_loop(0, qi, pair, 0)
    last = jnp.where(qi == 0, d1, (n_full - 1) * tk)
    values(pl.multiple_of(last, tk), 1)

    o_ref[...] = (acc_ref[:dh, :] / acc_ref[dh:dh + 1, :]).T.astype(o_ref.dtype)


def _fox_attention(qkv, c, n_heads):
    s = qkv.shape[0]
    dh = FOX_HEAD_DIM
    tq = min(ATT_TQ, s)
    hh = n_heads
    return pl.pallas_call(
        _fox_attn_kernel,
        grid=(hh, s // tq),
        in_specs=[pl.BlockSpec((tq, dh), lambda h, i: (i, h)),
                  pl.BlockSpec((s, dh), lambda h, i: (0, hh + h)),
                  pl.BlockSpec((s, dh), lambda h, i: (0, 2 * hh + h)),
                  pl.BlockSpec((s, LANES), lambda h, i: (0, 0))],
        out_specs=pl.BlockSpec((tq, dh), lambda h, i: (i, h)),
        out_shape=jax.ShapeDtypeStruct((s, hh * dh), BF16),
        scratch_shapes=[pltpu.VMEM((s, 2 * dh), BF16),
                        pltpu.VMEM((dh + 16, s), BF16),
                        pltpu.VMEM((2 * dh, tq), BF16),
                        pltpu.VMEM((ATT_TK, tq), F32),
                        pltpu.VMEM((ATT_TK, tq), F32),
                        pltpu.VMEM((ATT_TK, tq), BF16),
                        pltpu.VMEM((ATT_TK, tq), BF16),
                        pltpu.VMEM((2, 1, tq), F32),
                        pltpu.VMEM((2, 1, tq), F32),
                        pltpu.VMEM((dh + 16, tq), F32),
                        pltpu.VMEM((1, tq), F32)],
        compiler_params=_params("arbitrary", "arbitrary"),
        name="fox_attention",
    )(qkv, qkv, qkv, c)


def _first_lane_eq(vals, target, lane):
    return jnp.min(jnp.where(vals == target, lane, LANES), axis=1, keepdims=True)


def _route_rows(x, w_ref, b_ref, carry_ref):
    tm = x.shape[0]
    neg = -jnp.inf
    logits = _dot_hi_lo(x, w_ref) + b_ref[...]
    lane = lax.broadcasted_iota(jnp.int32, (tm, LANES), 1)

    gl = jnp.where(lane < N_GROUPS, logits, neg)
    gmax = jnp.max(gl, axis=1, keepdims=True)
    gsum = jnp.sum(jnp.exp(gl - gmax), axis=1, keepdims=True)
    grp_p = 1.0 / gsum
    grp = _first_lane_eq(gl, gmax, lane)

    lo = N_GROUPS + grp * EXPERTS_PER_GROUP
    el = jnp.where((lane >= lo) & (lane < lo + EXPERTS_PER_GROUP), logits, neg)
    emax = jnp.max(el, axis=1, keepdims=True)
    esum = jnp.sum(jnp.exp(el - emax), axis=1, keepdims=True)
    idx1 = _first_lane_eq(el, emax, lane)
    el2 = jnp.where(lane == idx1, neg, el)
    emax2 = jnp.max(el2, axis=1, keepdims=True)
    idx2 = _first_lane_eq(el2, emax2, lane)
    p1 = 1.0 / esum
    p2 = jnp.exp(emax2 - emax) / esum
    psum = p1 + p2
    g1 = grp_p * (p1 / psum)
    g2 = grp_p * (p2 / psum)

    oh1 = lane == idx1
    oh2 = lane == idx2
    both = jnp.where(oh1 | oh2, 1.0, 0.0)
    before = jnp.dot(_tri(tm, strict=True), both.astype(BF16), preferred_element_type=F32)
    before = before + carry_ref[...]
    r1 = jnp.sum(jnp.where(oh1, before, 0.0), axis=1, keepdims=True)
    r2 = jnp.sum(jnp.where(oh2, before, 0.0), axis=1, keepdims=True)
    carry_ref[...] = carry_ref[...] + jnp.sum(both, axis=0, keepdims=True)

    e1 = (idx1 - N_GROUPS).astype(F32)
    e2 = (idx2 - N_GROUPS).astype(F32)
    out = jnp.where(lane == 0, e1, 0.0)
    out = jnp.where(lane == 1, e2, out)
    out = jnp.where(lane == 2, g1, out)
    out = jnp.where(lane == 3, g2, out)
    out = jnp.where(lane == 4, r1, out)
    return jnp.where(lane == 5, r2, out)


def _proj_ln_route_kernel(o_ref, w_ref, x_ref, g_ref, b_ref, wr_ref, br_ref,
                          y_ref, yp_ref, route_ref, cnt_ref, carry_ref):
    @pl.when(pl.program_id(0) == 0)
    def _():
        carry_ref[...] = jnp.zeros_like(carry_ref)

    mix = jnp.dot(o_ref[...], w_ref[...], preferred_element_type=F32)
    y = _layer_norm_rows(DEEPNORM_ALPHA * x_ref[...] + mix, g_ref[...], b_ref[...])
    y_ref[...] = y
    yp_ref[...] = _pack_rows(y)
    route_ref[...] = _route_rows(y, wr_ref, br_ref, carry_ref)
    cnt_ref[...] = carry_ref[...]


def _proj_ln_route(o, w, x, g, b, w_group, b_group, w_expert, b_expert):
    s, d = x.shape
    kd = o.shape[1]
    tm = min(ROW_TILE, s)
    whl = _pack_hi_lo(jnp.concatenate([w_group.T, w_expert.T], axis=0))
    nb = N_GROUPS + N_EXPERTS
    bias = jnp.zeros((1, LANES), F32).at[0, :nb].set(jnp.concatenate([b_group, b_expert]))
    return pl.pallas_call(
        _proj_ln_route_kernel,
        grid=(s // tm,),
        in_specs=[pl.BlockSpec((tm, kd), lambda i: (i, 0)),
                  pl.BlockSpec((kd, d), lambda i: (0, 0)),
                  pl.BlockSpec((tm, d), lambda i: (i, 0)),
                  pl.BlockSpec((1, d), lambda i: (0, 0)),
                  pl.BlockSpec((1, d), lambda i: (0, 0)),
                  pl.BlockSpec((2 * LANES, d), lambda i: (0, 0)),
                  pl.BlockSpec((1, LANES), lambda i: (0, 0))],
        out_specs=[pl.BlockSpec((tm, d), lambda i: (i, 0)),
                   pl.BlockSpec((tm, d // 2), lambda i: (i, 0)),
                   pl.BlockSpec((tm, LANES), lambda i: (i, 0)),
                   pl.BlockSpec((1, LANES), lambda i: (0, 0))],
        out_shape=[jax.ShapeDtypeStruct((s, d), F32),
                   jax.ShapeDtypeStruct((s, d // 2), jnp.uint32),
                   jax.ShapeDtypeStruct((s, LANES), F32),
                   jax.ShapeDtypeStruct((1, LANES), F32)],
        scratch_shapes=[pltpu.VMEM((1, LANES), F32)],
        compiler_params=_params("arbitrary"),
        name="proj_ln_route",
    )(o, w, x, g.reshape(1, d), b.reshape(1, d), whl, bias)


def _dispatch_kernel(dest_ref, cnt_ref, pstart_ref, nv_ref, x_ref, xs_hbm, zero_ref, sem, zsem):
    i = pl.program_id(0)
    tm = x_ref.shape[0]
    bm = MOE_BLOCK
    n_exp = cnt_ref.shape[0]
    n_blocks = xs_hbm.shape[0] // bm

    def pad_copies(e, fn):
        cnt = cnt_ref[e]
        pos = pstart_ref[e] + cnt

        def row(r, carry):
            fn(pltpu.make_async_copy(zero_ref.at[pl.ds(0, 1), :], xs_hbm.at[pl.ds(pos + r, 1), :], zsem))
            return carry
        lax.fori_loop(0, (bm - cnt % bm) % bm, row, 0)

    def tail_copy(g):
        return pltpu.make_async_copy(zero_ref, xs_hbm.at[pl.ds(pl.multiple_of(g * bm, bm), bm), :], zsem)

    def for_all_fills(fn):
        def per_expert(e, carry):
            pad_copies(e, fn)
            return carry
        lax.fori_loop(0, n_exp, per_expert, 0)

        def per_tail(g, carry):
            fn(tail_copy(g))
            return carry
        lax.fori_loop(nv_ref[0], n_blocks, per_tail, 0)

    @pl.when(i == 0)
    def _():
        zero_ref[...] = jnp.zeros_like(zero_ref)
        for_all_fills(lambda c: c.start())

    base = i * (TOP_K * tm)

    def body(r, carry):
        for k in range(TOP_K):
            dst = dest_ref[base + TOP_K * r + k]
            pltpu.make_async_copy(x_ref.at[pl.ds(r, 1), :], xs_hbm.at[pl.ds(dst, 1), :], sem).start(priority=k)
        return carry
    lax.fori_loop(0, tm, body, 0, unroll=8)

    @pl.when(i == 0)
    def _():
        for_all_fills(lambda c: c.wait())

    for _ in range(TOP_K):
        pltpu.make_async_copy(x_ref, xs_hbm.at[pl.ds(0, tm), :], sem).wait()


def _dispatch(x, dest, cnt, pstart, n_valid, n_slots):
    t, d = x.shape
    tm = min(ROW_TILE, t)
    grid_spec = pltpu.PrefetchScalarGridSpec(
        num_scalar_prefetch=4,
        grid=(t // tm,),
        in_specs=[pl.BlockSpec((tm, d), lambda i, *_: (i, 0))],
        out_specs=pl.BlockSpec(memory_space=pl.ANY),
        scratch_shapes=[pltpu.VMEM((MOE_BLOCK, d), x.dtype),
                        pltpu.SemaphoreType.DMA,
                        pltpu.SemaphoreType.DMA],
    )
    return pl.pallas_call(
        _dispatch_kernel,
        grid_spec=grid_spec,
        out_shape=jax.ShapeDtypeStruct((n_slots, d), x.dtype),
        compiler_params=_params("arbitrary"),
        name="moe_dispatch",
    )(dest, cnt, pstart, n_valid, x)


def _expert_kernel(nblk_ref, bstart_ref, nv_ref, xs_hbm, wg_hbm, wu_hbm, wd_hbm, y_hbm,
                   xin_ref, yout_ref, wgf_ref, wuf_ref, wdf_ref, wgb_ref, wub_ref, wdb_ref,
                   in_sem, out_sem, w_sem, *, layer):
    e = pl.program_id(0)
    n_exp = pl.num_programs(0)
    n_valid = nv_ref[0]
    bm = MOE_BLOCK
    n_blocks = y_hbm.shape[0] // bm
    ns = MOE_W_SLOTS

    half = wd_hbm.shape[2] // 2

    def w_copies(ex, slot):
        return (pltpu.make_async_copy(wg_hbm.at[layer, ex], wgf_ref.at[slot], w_sem.at[slot]),
                pltpu.make_async_copy(wu_hbm.at[layer, ex], wuf_ref.at[slot], w_sem.at[slot]),
                pltpu.make_async_copy(wd_hbm.at[layer, ex, pl.ds(0, half)],
                                      wdf_ref.at[slot, pl.ds(0, half)], w_sem.at[slot]),
                pltpu.make_async_copy(wd_hbm.at[layer, ex, pl.ds(half, half)],
                                      wdf_ref.at[slot, pl.ds(half, half)], w_sem.at[slot]))

    def start_weights(ex):
        exc = jnp.minimum(ex, n_exp - 1)

        @pl.when((ex < n_exp) & (nblk_ref[exc] > 0))
        def _():
            for piece, c in enumerate(w_copies(exc, exc % ns)):
                c.start(priority=piece % 2)

    @pl.when(e == 0)
    def _():
        for ahead in range(ns - 1):
            start_weights(ahead)

    start_weights(e + ns - 1)

    def in_copy(g, slot):
        return pltpu.make_async_copy(xs_hbm.at[pl.ds(pl.multiple_of(g * bm, bm), bm), :],
                                     xin_ref.at[slot], in_sem.at[slot])

    def out_copy(g, slot):
        return pltpu.make_async_copy(yout_ref.at[slot],
                                     y_hbm.at[pl.ds(pl.multiple_of(g * bm, bm), bm), :], out_sem.at[slot])

    @pl.when((e == 0) & (n_valid > 0))
    def _():
        in_copy(0, 0).start(priority=0)

    nb = nblk_ref[e]
    g0 = bstart_ref[e]

    @pl.when(nb > 0)
    def _():
        w_slot = e % ns
        for c in w_copies(e, w_slot):
            c.wait()
        wgb_ref[...] = wgf_ref[w_slot].astype(BF16)
        wub_ref[...] = wuf_ref[w_slot].astype(BF16)
        wdb_ref[...] = wdf_ref[w_slot].astype(BF16)

        def block(j, carry):
            g = g0 + j
            slot = g % 2
            in_copy(g, slot).wait()

            @pl.when(g + 1 < n_valid)
            def _():
                in_copy(g + 1, 1 - slot).start(priority=0)

            @pl.when(g >= 2)
            def _():
                out_copy(g - 2, slot).wait()

            xb = _unpack_rows(xin_ref[slot]).astype(BF16)
            gate = jnp.dot(xb, wgb_ref[...], preferred_element_type=F32)
            up = jnp.dot(xb, wub_ref[...], preferred_element_type=F32)
            hid = (gate * jax.nn.sigmoid(gate) * up).astype(BF16)
            yout_ref[slot] = _pack_rows(jnp.dot(hid, wdb_ref[...], preferred_element_type=F32))
            out_copy(g, slot).start(priority=1)
            return carry
        lax.fori_loop(0, nb, block, 0)

    @pl.when(e == pl.num_programs(0) - 1)
    def _():
        for back in (2, 1):
            @pl.when(n_valid >= back)
            def _(back=back):
                out_copy(n_valid - back, (n_valid - back) % 2).wait()

        yout_ref[0] = jnp.zeros(yout_ref.shape[1:], yout_ref.dtype)

        def fill(g, carry):
            out_copy(g, 0).start()
            return carry
        lax.fori_loop(n_valid, n_blocks, fill, 0)

        def drain(g, carry):
            out_copy(g, 0).wait()
            return carry
        lax.fori_loop(n_valid, n_blocks, drain, 0)


def _experts(xs, nblk, bstart, n_valid, w_gate, w_up, w_down, layer):
    p, dp = xs.shape
    _, n_exp, d, f = w_gate.shape
    bm = MOE_BLOCK
    ns = MOE_W_SLOTS
    grid_spec = pltpu.PrefetchScalarGridSpec(
        num_scalar_prefetch=3,
        grid=(n_exp,),
        in_specs=[pl.BlockSpec(memory_space=pl.ANY)] * 4,
        out_specs=pl.BlockSpec(memory_space=pl.ANY),
        scratch_shapes=[pltpu.VMEM((2, bm, dp), jnp.uint32),
                        pltpu.VMEM((2, bm, dp), jnp.uint32),
                        pltpu.VMEM((ns, d, f), F32),
                        pltpu.VMEM((ns, d, f), F32),
                        pltpu.VMEM((ns, f, d), F32),
                        pltpu.VMEM((d, f), BF16),
                        pltpu.VMEM((d, f), BF16),
                        pltpu.VMEM((f, d), BF16),
                        pltpu.SemaphoreType.DMA((2,)),
                        pltpu.SemaphoreType.DMA((2,)),
                        pltpu.SemaphoreType.DMA((ns,))],
    )
    return pl.pallas_call(
        functools.partial(_expert_kernel, layer=layer),
        grid_spec=grid_spec,
        out_shape=jax.ShapeDtypeStruct((p, dp), jnp.uint32),
        compiler_params=_params("arbitrary"),
        name="moe_experts",
    )(nblk, bstart, n_valid, xs, w_gate, w_up, w_down)


def _combine_ln_kernel(dest_ref, y_hbm, x_ref, route_ref, g_ref, b_ref, o_ref, yg_ref, sem):
    i = pl.program_id(0)
    n = pl.num_programs(0)
    tm = x_ref.shape[0]

    def start_gather(step, slot):
        base = step * (TOP_K * tm)

        def body(r, carry):
            for k in range(TOP_K):
                src = dest_ref[base + TOP_K * r + k]
                pltpu.make_async_copy(y_hbm.at[pl.ds(src, 1), :],
                                      yg_ref.at[slot, pl.ds(k * tm + r, 1), :], sem.at[slot]).start(priority=k)
            return carry
        lax.fori_loop(0, tm, body, 0, unroll=8)

    def wait_gather(slot):
        pltpu.make_async_copy(y_hbm.at[pl.ds(0, TOP_K * tm), :], yg_ref.at[slot], sem.at[slot]).wait()

    @pl.when(i == 0)
    def _():
        start_gather(0, 0)

    @pl.when(i + 1 < n)
    def _():
        start_gather(i + 1, (i + 1) % 2)

    slot = i % 2
    wait_gather(slot)
    route = route_ref[...]
    lane = lax.broadcasted_iota(jnp.int32, route.shape, 1)
    g1 = jnp.sum(jnp.where(lane == 2, route, 0.0), axis=1, keepdims=True)
    g2 = jnp.sum(jnp.where(lane == 3, route, 0.0), axis=1, keepdims=True)
    y1 = _unpack_rows(yg_ref[slot, pl.ds(0, tm), :])
    y2 = _unpack_rows(yg_ref[slot, pl.ds(tm, tm), :])
    z = DEEPNORM_ALPHA * x_ref[...] + (y1 * g1 + y2 * g2)
    o_ref[...] = _layer_norm_rows(z, g_ref[...], b_ref[...])


def _combine_ln(y, dest, x, route, g, b):
    t, d = x.shape
    tm = min(COMB_TM, t)
    grid_spec = pltpu.PrefetchScalarGridSpec(
        num_scalar_prefetch=1,
        grid=(t // tm,),
        in_specs=[pl.BlockSpec(memory_space=pl.ANY),
                  pl.BlockSpec((tm, d), lambda i, dest: (i, 0)),
                  pl.BlockSpec((tm, LANES), lambda i, dest: (i, 0)),
                  pl.BlockSpec((1, d), lambda i, dest: (0, 0)),
                  pl.BlockSpec((1, d), lambda i, dest: (0, 0))],
        out_specs=pl.BlockSpec((tm, d), lambda i, dest: (i, 0)),
        scratch_shapes=[pltpu.VMEM((2, TOP_K * tm, y.shape[1]), y.dtype),
                        pltpu.SemaphoreType.DMA((2,))],
    )
    return pl.pallas_call(
        _combine_ln_kernel,
        grid_spec=grid_spec,
        out_shape=jax.ShapeDtypeStruct((t, d), F32),
        compiler_params=_params("arbitrary"),
        name="moe_combine_ln",
    )(dest, y, x, route, g.reshape(1, d), b.reshape(1, d))


def _moe_ffn_ln(x, xp, route, counts, w_gate, w_up, w_down, layer, ln_g, ln_b):
    t, d = x.shape
    bm = MOE_BLOCK
    eid = route[:, 0:TOP_K].astype(jnp.int32)
    rank = route[:, 4:4 + TOP_K].astype(jnp.int32)
    cnt = counts[0, N_GROUPS:N_GROUPS + N_EXPERTS].astype(jnp.int32)
    nblk = (cnt + bm - 1) // bm
    bend = jnp.cumsum(nblk)
    bstart = bend - nblk
    pstart = bstart * bm
    onehot = eid[:, :, None] == jnp.arange(N_EXPERTS, dtype=jnp.int32)
    dest = (jnp.sum(jnp.where(onehot, pstart, 0), axis=-1) + rank).reshape(-1)
    n_valid = bend[-1:]
    n_slots = (-(-t * TOP_K // bm) + N_EXPERTS) * bm
    xs = _dispatch(xp, dest, cnt, pstart, n_valid, n_slots)
    y = _experts(xs, nblk, bstart, n_valid, w_gate, w_up, w_down, layer)
    return _combine_ln(y, dest, x, route, ln_g, ln_b)


def _gla_gate_kernel(x_ref, wl_ref, wu_ref, b_ref, la_ref):
    g_low = _dot_hi_lo(x_ref[...], wl_ref)
    n = wu_ref.shape[1] // 2
    gh, gl = _split2(g_low)
    a = jnp.dot(gh, wu_ref[...], preferred_element_type=F32)
    c = jnp.dot(gl, wu_ref[:, :n], preferred_element_type=F32)
    logit = a[:, :n] + a[:, n:] + c + b_ref[...]
    la_ref[...] = _log_sigmoid(logit) / GLA_GATE_TAU


def _gla_gates(x, w_low_t, w_gate_up, b_gate):
    s, d = x.shape
    rank, dk = w_gate_up.shape
    ts = min(ROW_TILE, s)
    wl = _pack_hi_lo(w_low_t)
    wu_pad = jnp.zeros((LANES, dk), F32).at[:rank].set(w_gate_up)
    wu = jnp.concatenate(_split2(wu_pad), axis=1)
    return pl.pallas_call(
        _gla_gate_kernel,
        grid=(s // ts,),
        in_specs=[pl.BlockSpec((ts, d), lambda i: (i, 0)),
                  pl.BlockSpec((2 * LANES, d), lambda i: (0, 0)),
                  pl.BlockSpec((LANES, 2 * dk), lambda i: (0, 0)),
                  pl.BlockSpec((1, dk), lambda i: (0, 0))],
        out_specs=pl.BlockSpec((ts, dk), lambda i: (i, 0)),
        out_shape=jax.ShapeDtypeStruct((s, dk), F32),
        compiler_params=_params("parallel"),
        name="gla_gates",
    )(x, wl, wu, b_gate.reshape(1, dk))


def _gla_kernel(q_ref, k_ref, v_ref, la_ref, r_ref, g_ref, o_ref, state_ref):
    @pl.when(pl.program_id(0) == 0)
    def _():
        state_ref[...] = jnp.zeros_like(state_ref)

    rows = q_ref.shape[0]
    nh, dk, dv = state_ref.shape
    cs = GLA_CHUNK
    tri = _tri(cs)
    rr = lax.broadcasted_iota(jnp.int32, (cs, cs), 0)
    cc = lax.broadcasted_iota(jnp.int32, (cs, cs), 1)
    causal = rr >= cc
    for ci in range(rows // cs):
        sl = pl.ds(ci * cs, cs)
        for h in range(nh):
            kc = pl.ds(h * dk, dk)
            vc = pl.ds(h * dv, dv)
            b = _cumsum_rows(la_ref[sl, kc], tri)
            b_mid = b[cs // 2 - 1:cs // 2, :]
            b_last = b[cs - 1:cs, :]
            q = q_ref[sl, kc].astype(F32) * (dk ** -0.5)
            k = k_ref[sl, kc].astype(F32)
            v = v_ref[sl, vc]
            qa = (q * jnp.exp(b - b_mid)).astype(BF16)
            ka = (k * jnp.exp(b_mid - b)).astype(BF16)
            a = jnp.where(causal, _dot_nt(qa, ka), 0.0)
            o = jnp.dot(a.astype(BF16), v, preferred_element_type=F32)
            state = state_ref[h]
            o = o + jnp.dot((q * jnp.exp(b)).astype(BF16), state.astype(BF16), preferred_element_type=F32)
            k_end_t = (k * jnp.exp(b_last - b)).T.astype(BF16)
            decay = jnp.exp(b.T[:, cs - 1:cs])
            state_ref[h] = decay * state + jnp.dot(k_end_t, v, preferred_element_type=F32)
            o = o * lax.rsqrt(jnp.mean(o * o, axis=-1, keepdims=True) + RMS_EPS) * g_ref[...]
            r = r_ref[sl, vc].astype(F32)
            o_ref[sl, vc] = (o * (r * jax.nn.sigmoid(r))).astype(o_ref.dtype)


def _gla(proj, la, norm_g):
    s = proj.shape[0]
    dk_all = la.shape[1]
    nh = GLA_HEADS
    dk = dk_all // nh
    dv = norm_g.shape[0]
    dv_all = nh * dv
    assert 2 * dk_all == dv_all
    rows = min(GLA_ROWS, s)
    return pl.pallas_call(
        _gla_kernel,
        grid=(s // rows,),
        in_specs=[pl.BlockSpec((rows, dk_all), lambda i: (i, 0)),
                  pl.BlockSpec((rows, dk_all), lambda i: (i, 1)),
                  pl.BlockSpec((rows, dv_all), lambda i: (i, 1)),
                  pl.BlockSpec((rows, dk_all), lambda i: (i, 0)),
                  pl.BlockSpec((rows, dv_all), lambda i: (i, 2)),
                  pl.BlockSpec((1, dv), lambda i: (0, 0))],
        out_specs=pl.BlockSpec((rows, dv_all), lambda i: (i, 0)),
        out_shape=jax.ShapeDtypeStruct((s, dv_all), BF16),
        scratch_shapes=[pltpu.VMEM((nh, dk, dv), F32)],
        compiler_params=_params("arbitrary"),
        name="gla_chunks",
    )(proj, proj, proj, la, proj, norm_g.reshape(1, dv))


def kernel(x, fox_w_in, fox_b_f, fox_w_o, gla_w_in, gla_w_gate_up, gla_b_gate, gla_norm_g, gla_w_o,
           ln_mix_g, ln_mix_b, ln_ffn_g, ln_ffn_b, moe_w_group, moe_b_group, moe_w_expert,
           moe_b_expert, moe_w_gate, moe_w_up, moe_w_down):
    bsz, s, d = x.shape
    outs = []
    for bi in range(bsz):
        xt = x[bi]
        for i in range(DEPTH):
            j = i // 2
            if i % 2 == 0:
                w_in_t = fox_w_in[j].T
                qkv = _matmul_nt(xt, w_in_t, 3 * d, BF16)
                c = _fox_gates(xt, w_in_t[3 * d:], fox_b_f[j])
                o = _fox_attention(qkv, c, FOX_HEADS)
                w_o = fox_w_o[j]
            else:
                w_in_t = gla_w_in[j].T
                n_main = w_in_t.shape[0] - gla_w_gate_up.shape[1]
                proj = _matmul_nt(xt, w_in_t, n_main, BF16)
                la = _gla_gates(xt, w_in_t[n_main:], gla_w_gate_up[j], gla_b_gate[j])
                o = _gla(proj, la, gla_norm_g[j])
                w_o = gla_w_o[j]
            xt, xp, route, counts = _proj_ln_route(o, w_o.astype(BF16), xt, ln_mix_g[i], ln_mix_b[i], moe_w_group[i],
                                                   moe_b_group[i], moe_w_expert[i], moe_b_expert[i])
            xt = _moe_ffn_ln(xt, xp, route, counts, moe_w_gate, moe_w_up, moe_w_down, i, ln_ffn_g[i], ln_ffn_b[i])
        outs.append(xt)
    return outs[0].reshape(1, s, d) if bsz == 1 else jnp.stack(outs, axis=0)
```

```python
import functools

import jax
import jax.numpy as jnp
from jax import lax
from jax.experimental import pallas as pl
from jax.experimental.pallas import tpu as pltpu

F32 = jnp.float32
BF16 = jnp.bfloat16

DEPTH = 2
FOX_HEADS = 16
FOX_HEAD_DIM = 128
GLA_HEADS = 4
GLA_CHUNK = 64
GLA_GATE_TAU = 16.0
N_GROUPS = 8
EXPERTS_PER_GROUP = 8
N_EXPERTS = N_GROUPS * EXPERTS_PER_GROUP
TOP_K = 2
DEEPNORM_ALPHA = (2 * DEPTH) ** 0.25
LN_EPS = 1e-5
RMS_EPS = 1e-6

LANES = 128
VMEM_LIMIT = 56 * 2**20

MM_TM, MM_TN = 1024, 1024
ROW_TILE = 512
ATT_TQ, ATT_TK = 1024, 512
GLA_ROWS = 256
MOE_BLOCK = 128
MOE_W_SLOTS = 3
COMB_TM = 256


def _params(*sem):
    return pltpu.CompilerParams(dimension_semantics=sem, vmem_limit_bytes=VMEM_LIMIT)


def _split2(a):
    hi = a.astype(BF16)
    lo = (a - hi.astype(F32)).astype(BF16)
    return hi, lo


def _split3(a):
    hi = a.astype(BF16)
    r = a - hi.astype(F32)
    mid = r.astype(BF16)
    lo = (r - mid.astype(F32)).astype(BF16)
    return hi, mid, lo


def _pack_hi_lo(w_t, n_pad=LANES):
    n, k = w_t.shape
    wp = jnp.zeros((n_pad, k), F32).at[:n].set(w_t)
    hi, lo = _split2(wp)
    return jnp.concatenate([hi, lo], axis=0)


def _dot_nt(a, b):
    return lax.dot_general(a, b, (((1,), (1,)), ((), ())), preferred_element_type=F32)


def _dot_hi_lo(x, whl_ref, n_pad=LANES):
    xh, xl = _split2(x)
    a = _dot_nt(xh, whl_ref[...])
    b = _dot_nt(xl, whl_ref[:n_pad, :])
    return a[:, :n_pad] + a[:, n_pad:] + b


def _pack_rows(a):
    half = a.shape[1] // 2
    hi = lax.bitcast_convert_type(a[:, :half].astype(BF16).astype(F32), jnp.uint32)
    lo = lax.bitcast_convert_type(a[:, half:].astype(BF16).astype(F32), jnp.uint32)
    return hi | (lo >> 16)


def _unpack_rows(w):
    hi = lax.bitcast_convert_type(w & jnp.uint32(0xFFFF0000), F32)
    lo = lax.bitcast_convert_type(w << 16, F32)
    return jnp.concatenate([hi, lo], axis=1)


def _store_rows_as_lines(ref, first_line, packed):
    n_rows, width = packed.shape
    nt = width // LANES
    for j in range(nt):
        ref[pl.ds(first_line + j, n_rows, stride=nt), :] = packed[:, j * LANES:(j + 1) * LANES]


def _load_rows_from_lines(ref, first_line, n_rows, nt):
    return jnp.concatenate([ref[pl.ds(first_line + j, n_rows, stride=nt), :] for j in range(nt)], axis=1)


def _log_sigmoid(x):
    return -(jnp.maximum(-x, 0.0) + jnp.log1p(jnp.exp(-jnp.abs(x))))


def _cumsum_rows(a, incl_tri):
    n = a.shape[1]
    parts = jnp.concatenate(_split3(a), axis=1)
    c = jnp.dot(incl_tri, parts, preferred_element_type=F32)
    return c[:, :n] + c[:, n:2 * n] + c[:, 2 * n:]


def _tri(n, strict=False):
    r = lax.broadcasted_iota(jnp.int32, (n, n), 0)
    c = lax.broadcasted_iota(jnp.int32, (n, n), 1)
    return jnp.where((r > c) if strict else (r >= c), 1.0, 0.0).astype(BF16)


def _layer_norm_rows(z, g, b):
    mu = jnp.mean(z, axis=-1, keepdims=True)
    d = z - mu
    var = jnp.mean(d * d, axis=-1, keepdims=True)
    return d * lax.rsqrt(var + LN_EPS) * g + b


def _mm_kernel(x_ref, w_ref, o_ref, wb_ref):
    @pl.when(pl.program_id(1) == 0)
    def _():
        wb_ref[...] = w_ref[...].astype(BF16)

    o_ref[...] = lax.dot_general(x_ref[...].astype(BF16), wb_ref[...], (((1,), (1,)), ((), ())),
                                 preferred_element_type=F32).astype(o_ref.dtype)


def _matmul_nt(x, w_t, n, out_dtype):
    m, k = x.shape
    tm, tn = min(MM_TM, m), min(MM_TN, n)
    return pl.pallas_call(
        _mm_kernel,
        grid=(n // tn, m // tm),
        in_specs=[pl.BlockSpec((tm, k), lambda j, i: (i, 0)),
                  pl.BlockSpec((tn, k), lambda j, i: (j, 0))],
        out_specs=pl.BlockSpec((tm, tn), lambda j, i: (i, j)),
        out_shape=jax.ShapeDtypeStruct((m, n), out_dtype),
        scratch_shapes=[pltpu.VMEM((tn, k), BF16)],
        compiler_params=_params("arbitrary", "arbitrary"),
        name="dense_proj",
    )(x, w_t)


def _fox_gate_kernel(x_ref, w_ref, b_ref, c_ref, carry_ref):
    @pl.when(pl.program_id(0) == 0)
    def _():
        carry_ref[...] = jnp.zeros_like(carry_ref)

    ts = x_ref.shape[0]
    logits = _dot_hi_lo(x_ref[...], w_ref) + b_ref[...]
    log_f = _log_sigmoid(logits)
    c = _cumsum_rows(log_f, _tri(ts)) + carry_ref[...]
    carry_ref[...] = c[ts - 1:ts, :]
    c_ref[...] = c


def _fox_gates(x, w_f_t, b_f):
    s, d = x.shape
    h = w_f_t.shape[0]
    ts = min(ROW_TILE, s)
    whl = _pack_hi_lo(w_f_t)
    bias = jnp.zeros((1, LANES), F32).at[0, :h].set(b_f)
    return pl.pallas_call(
        _fox_gate_kernel,
        grid=(s // ts,),
        in_specs=[pl.BlockSpec((ts, d), lambda i: (i, 0)),
                  pl.BlockSpec((2 * LANES, d), lambda i: (0, 0)),
                  pl.BlockSpec((1, LANES), lambda i: (0, 0))],
        out_specs=pl.BlockSpec((ts, LANES), lambda i: (i, 0)),
        out_shape=jax.ShapeDtypeStruct((s, LANES), F32),
        scratch_shapes=[pltpu.VMEM((1, LANES), F32)],
        compiler_params=_params("arbitrary"),
        name="fox_gates",
    )(x, whl, bias)


def _fox_attn_kernel(q_ref, k_ref, v_ref, c_ref, o_ref, kaug_ref, vt_ref, qt_ref, st0_ref, st1_ref,
                     p0_ref, p1_ref, alpha_ref, bmax_ref, acc_ref, m_ref):
    h = pl.program_id(0)
    qi = pl.program_id(1)
    tq, dh = q_ref.shape
    s_len = k_ref.shape[0]
    tk = ATT_TK
    log2e = 1.4426950408889634
    scale = dh ** -0.5 * log2e

    def head_column(rows, n):
        lane = lax.broadcasted_iota(jnp.int32, (n, LANES), 1)
        return log2e * jnp.sum(jnp.where(lane == h, c_ref[rows, :], 0.0), axis=1, keepdims=True)

    def bias_columns(col, first, n):
        lane = lax.broadcasted_iota(jnp.int32, (n, LANES), 1)
        hi, mid, lo = _split3(col)
        ones_first = 3 - first
        out = jnp.where((lane >= ones_first) & (lane < ones_first + 3), 1.0, 0.0)
        out = jnp.where(lane == first, hi.astype(F32), out)
        out = jnp.where(lane == first + 1, mid.astype(F32), out)
        return jnp.where(lane == first + 2, lo.astype(F32), out).astype(BF16)

    @pl.when(qi == 0)
    def _():
        ones_row = jnp.where(lax.broadcasted_iota(jnp.int32, (16, tk), 0) == 0, 1.0, 0.0).astype(BF16)

        def build(ci, carry):
            rows = pl.ds(pl.multiple_of(ci * tk, tk), tk)
            kaug_ref[rows, :dh] = k_ref[rows, :]
            kaug_ref[rows, dh:] = bias_columns(-head_column(rows, tk), 0, tk)
            vt_ref[:dh, rows] = v_ref[rows, :].astype(F32).T.astype(BF16)
            vt_ref[dh:, rows] = ones_row
            return carry
        lax.fori_loop(0, s_len // tk, build, 0)

    q_rows = pl.ds(pl.multiple_of(qi * tq, tq), tq)
    q_aug = jnp.concatenate([q_ref[...].astype(F32) * scale,
                             bias_columns(head_column(q_rows, tq), 3, tq).astype(F32)], axis=1)
    qt_ref[...] = q_aug.T.astype(BF16)

    acc_ref[...] = jnp.zeros_like(acc_ref)
    m_ref[...] = jnp.full_like(m_ref, -jnp.inf)

    st_refs, p_refs = (st0_ref, st1_ref), (p0_ref, p1_ref)

    def scores(k_start, slot, diag_offset=None):
        st = jnp.dot(kaug_ref[pl.ds(k_start, tk), :], qt_ref[...], preferred_element_type=F32)
        if diag_offset is not None:
            kr = lax.broadcasted_iota(jnp.int32, (tk, tq), 0) + diag_offset
            qc = lax.broadcasted_iota(jnp.int32, (tk, tq), 1)
            st = jnp.where(kr <= qc, st, -jnp.inf)
        st_refs[slot][...] = st
        bmax_ref[slot] = jnp.max(st, axis=0, keepdims=True)

    def softmax(slot):
        m_prev = m_ref[...]
        m_new = jnp.maximum(m_prev, bmax_ref[slot])
        m_ref[...] = m_new
        p_refs[slot][...] = jnp.exp2(st_refs[slot][...] - m_new).astype(BF16)
        alpha_ref[slot] = jnp.exp2(m_prev - m_new)

    def values(k_start, slot):
        acc_ref[...] = alpha_ref[slot] * acc_ref[...] + jnp.dot(
            vt_ref[:, pl.ds(k_start, tk)], p_refs[slot][...], preferred_element_type=F32)

    n_diag = tq // tk
    assert n_diag == 2
    d0 = pl.multiple_of(qi * tq, tk)
    d1 = pl.multiple_of(qi * tq + tk, tk)
    n_full = qi * n_diag

    def full_start(i):
        return pl.multiple_of(jnp.minimum(i, jnp.maximum(n_full - 1, 0)) * tk, tk)

    scores(d0, 0, 0)
    scores(d1, 1, tk)
    softmax(0)
    scores(full_start(0), 0)
    softmax(1)
    values(d0, 0)

    def pair(j, carry):
        i0 = 2 * j
        prev = jnp.where(j == 0, d1, (i0 - 1) * tk)
        scores(full_start(i0 + 1), 1)
        softmax(0)
        values(pl.multiple_of(prev, tk), 1)
        scores(full_start(i0 + 2), 0)
        softmax(1)
        values(pl.multiple_of(i0 * tk, tk), 0)
        return carry

    lax.fori_loop(0, qi, pair, 0)
    last = jnp.where(qi == 0, d1, (n_full - 1) * tk)
    values(pl.multiple_of(last, tk), 1)

    o_ref[...] = (acc_ref[:dh, :] / acc_ref[dh:dh + 1, :]).T.astype(o_ref.dtype)


def _fox_attention(qkv, c, n_heads):
    s = qkv.shape[0]
    dh = FOX_HEAD_DIM
    tq = min(ATT_TQ, s)
    hh = n_heads
    return pl.pallas_call(
        _fox_attn_kernel,
        grid=(hh, s // tq),
        in_specs=[pl.BlockSpec((tq, dh), lambda h, i: (i, h)),
                  pl.BlockSpec((s, dh), lambda h, i: (0, hh + h)),
                  pl.BlockSpec((s, dh), lambda h, i: (0, 2 * hh + h)),
                  pl.BlockSpec((s, LANES), lambda h, i: (0, 0))],
        out_specs=pl.BlockSpec((tq, dh), lambda h, i: (i, h)),
        out_shape=jax.ShapeDtypeStruct((s, hh * dh), BF16),
        scratch_shapes=[pltpu.VMEM((s, 2 * dh), BF16),
                        pltpu.VMEM((dh + 16, s), BF16),
                        pltpu.VMEM((2 * dh, tq), BF16),
                        pltpu.VMEM((ATT_TK, tq), F32),
                        pltpu.VMEM((ATT_TK, tq), F32),
                        pltpu.VMEM((ATT_TK, tq), BF16),
                        pltpu.VMEM((ATT_TK, tq), BF16),
                        pltpu.VMEM((2, 1, tq), F32),
                        pltpu.VMEM((2, 1, tq), F32),
                        pltpu.VMEM((dh + 16, tq), F32),
                        pltpu.VMEM((1, tq), F32)],
        compiler_params=_params("arbitrary", "arbitrary"),
        name="fox_attention",
    )(qkv, qkv, qkv, c)


def _first_lane_eq(vals, target, lane):
    return jnp.min(jnp.where(vals == target, lane, LANES), axis=1, keepdims=True)


def _route_rows(x, w_ref, b_ref, carry_ref):
    tm = x.shape[0]
    neg = -jnp.inf
    logits = _dot_hi_lo(x, w_ref) + b_ref[...]
    lane = lax.broadcasted_iota(jnp.int32, (tm, LANES), 1)

    gl = jnp.where(lane < N_GROUPS, logits, neg)
    gmax = jnp.max(gl, axis=1, keepdims=True)
    gsum = jnp.sum(jnp.exp(gl - gmax), axis=1, keepdims=True)
    grp_p = 1.0 / gsum
    grp = _first_lane_eq(gl, gmax, lane)

    lo = N_GROUPS + grp * EXPERTS_PER_GROUP
    el = jnp.where((lane >= lo) & (lane < lo + EXPERTS_PER_GROUP), logits, neg)
    emax = jnp.max(el, axis=1, keepdims=True)
    esum = jnp.sum(jnp.exp(el - emax), axis=1, keepdims=True)
    idx1 = _first_lane_eq(el, emax, lane)
    el2 = jnp.where(lane == idx1, neg, el)
    emax2 = jnp.max(el2, axis=1, keepdims=True)
    idx2 = _first_lane_eq(el2, emax2, lane)
    p1 = 1.0 / esum
    p2 = jnp.exp(emax2 - emax) / esum
    psum = p1 + p2
    g1 = grp_p * (p1 / psum)
    g2 = grp_p * (p2 / psum)

    oh1 = lane == idx1
    oh2 = lane == idx2
    both = jnp.where(oh1 | oh2, 1.0, 0.0)
    before = jnp.dot(_tri(tm, strict=True), both.astype(BF16), preferred_element_type=F32)
    before = before + carry_ref[...]
    r1 = jnp.sum(jnp.where(oh1, before, 0.0), axis=1, keepdims=True)
    r2 = jnp.sum(jnp.where(oh2, before, 0.0), axis=1, keepdims=True)
    carry_ref[...] = carry_ref[...] + jnp.sum(both, axis=0, keepdims=True)

    e1 = (idx1 - N_GROUPS).astype(F32)
    e2 = (idx2 - N_GROUPS).astype(F32)
    out = jnp.where(lane == 0, e1, 0.0)
    out = jnp.where(lane == 1, e2, out)
    out = jnp.where(lane == 2, g1, out)
    out = jnp.where(lane == 3, g2, out)
    out = jnp.where(lane == 4, r1, out)
    return jnp.where(lane == 5, r2, out)


def _proj_ln_route_kernel(o_ref, w_ref, x_ref, g_ref, b_ref, wr_ref, br_ref,
                          y_ref, yp_ref, route_ref, cnt_ref, carry_ref):
    @pl.when(pl.program_id(0) == 0)
    def _():
        carry_ref[...] = jnp.zeros_like(carry_ref)

    mix = jnp.dot(o_ref[...], w_ref[...], preferred_element_type=F32)
    y = _layer_norm_rows(DEEPNORM_ALPHA * x_ref[...] + mix, g_ref[...], b_ref[...])
    y_ref[...] = y
    _store_rows_as_lines(yp_ref, 0, _pack_rows(y))
    route_ref[...] = _route_rows(y, wr_ref, br_ref, carry_ref)
    cnt_ref[...] = carry_ref[...]


def _proj_ln_route(o, w, x, g, b, w_group, b_group, w_expert, b_expert):
    s, d = x.shape
    kd = o.shape[1]
    tm = min(ROW_TILE, s)
    whl = _pack_hi_lo(jnp.concatenate([w_group.T, w_expert.T], axis=0))
    nb = N_GROUPS + N_EXPERTS
    bias = jnp.zeros((1, LANES), F32).at[0, :nb].set(jnp.concatenate([b_group, b_expert]))
    return pl.pallas_call(
        _proj_ln_route_kernel,
        grid=(s // tm,),
        in_specs=[pl.BlockSpec((tm, kd), lambda i: (i, 0)),
                  pl.BlockSpec((kd, d), lambda i: (0, 0)),
                  pl.BlockSpec((tm, d), lambda i: (i, 0)),
                  pl.BlockSpec((1, d), lambda i: (0, 0)),
                  pl.BlockSpec((1, d), lambda i: (0, 0)),
                  pl.BlockSpec((2 * LANES, d), lambda i: (0, 0)),
                  pl.BlockSpec((1, LANES), lambda i: (0, 0))],
        out_specs=[pl.BlockSpec((tm, d), lambda i: (i, 0)),
                   pl.BlockSpec((tm * (d // 2 // LANES), LANES), lambda i: (i, 0)),
                   pl.BlockSpec((tm, LANES), lambda i: (i, 0)),
                   pl.BlockSpec((1, LANES), lambda i: (0, 0))],
        out_shape=[jax.ShapeDtypeStruct((s, d), F32),
                   jax.ShapeDtypeStruct((s * (d // 2 // LANES), LANES), jnp.uint32),
                   jax.ShapeDtypeStruct((s, LANES), F32),
                   jax.ShapeDtypeStruct((1, LANES), F32)],
        scratch_shapes=[pltpu.VMEM((1, LANES), F32)],
        compiler_params=_params("arbitrary"),
        name="proj_ln_route",
    )(o, w, x, g.reshape(1, d), b.reshape(1, d), whl, bias)


def _dispatch_kernel(dest_ref, cnt_ref, pstart_ref, nv_ref, x_ref, xs_hbm, zero_ref, sem, zsem, *, nt):
    i = pl.program_id(0)
    tm = x_ref.shape[0] // nt
    bm = MOE_BLOCK
    n_exp = cnt_ref.shape[0]
    n_blocks = xs_hbm.shape[0] // (bm * nt)

    def lines(row, n_rows=1):
        return pl.ds(pl.multiple_of(row * nt, nt), n_rows * nt)

    def pad_copies(e, fn):
        cnt = cnt_ref[e]
        pos = pstart_ref[e] + cnt

        def row(r, carry):
            fn(pltpu.make_async_copy(zero_ref.at[lines(0), :], xs_hbm.at[lines(pos + r), :], zsem))
            return carry
        lax.fori_loop(0, (bm - cnt % bm) % bm, row, 0)

    def tail_copy(g):
        return pltpu.make_async_copy(zero_ref, xs_hbm.at[lines(g * bm, bm), :], zsem)

    def for_all_fills(fn):
        def per_expert(e, carry):
            pad_copies(e, fn)
            return carry
        lax.fori_loop(0, n_exp, per_expert, 0)

        def per_tail(g, carry):
            fn(tail_copy(g))
            return carry
        lax.fori_loop(nv_ref[0], n_blocks, per_tail, 0)

    @pl.when(i == 0)
    def _():
        zero_ref[...] = jnp.zeros_like(zero_ref)
        for_all_fills(lambda c: c.start())

    base = i * (TOP_K * tm)

    def body(r, carry):
        for k in range(TOP_K):
            dst = dest_ref[base + TOP_K * r + k]
            pltpu.make_async_copy(x_ref.at[lines(r), :], xs_hbm.at[lines(dst), :], sem).start(priority=k)
        return carry
    lax.fori_loop(0, tm, body, 0, unroll=8)

    @pl.when(i == 0)
    def _():
        for_all_fills(lambda c: c.wait())

    for _ in range(TOP_K):
        pltpu.make_async_copy(x_ref, xs_hbm.at[lines(0, tm), :], sem).wait()


def _dispatch(xp, n_rows, dest, cnt, pstart, n_valid, n_slots):
    nt = xp.shape[0] // n_rows
    tm = min(ROW_TILE, n_rows)
    grid_spec = pltpu.PrefetchScalarGridSpec(
        num_scalar_prefetch=4,
        grid=(n_rows // tm,),
        in_specs=[pl.BlockSpec((tm * nt, LANES), lambda i, *_: (i, 0))],
        out_specs=pl.BlockSpec(memory_space=pl.ANY),
        scratch_shapes=[pltpu.VMEM((MOE_BLOCK * nt, LANES), xp.dtype),
                        pltpu.SemaphoreType.DMA,
                        pltpu.SemaphoreType.DMA],
    )
    return pl.pallas_call(
        functools.partial(_dispatch_kernel, nt=nt),
        grid_spec=grid_spec,
        out_shape=jax.ShapeDtypeStruct((n_slots * nt, LANES), xp.dtype),
        compiler_params=_params("arbitrary"),
        name="moe_dispatch",
    )(dest, cnt, pstart, n_valid, xp)


def _expert_kernel(nblk_ref, bstart_ref, nv_ref, xs_hbm, wg_hbm, wu_hbm, wd_hbm, y_hbm,
                   xin_ref, yout_ref, wgf_ref, wuf_ref, wdf_ref, wgb_ref, wub_ref, wdb_ref,
                   in_sem, out_sem, w_sem, *, layer):
    e = pl.program_id(0)
    n_exp = pl.num_programs(0)
    n_valid = nv_ref[0]
    bm = MOE_BLOCK
    blk_lines = xin_ref.shape[1]
    nt = blk_lines // bm
    n_blocks = y_hbm.shape[0] // blk_lines
    ns = MOE_W_SLOTS

    half = wd_hbm.shape[2] // 2

    def w_copies(ex, slot):
        return (pltpu.make_async_copy(wg_hbm.at[layer, ex], wgf_ref.at[slot], w_sem.at[slot]),
                pltpu.make_async_copy(wu_hbm.at[layer, ex], wuf_ref.at[slot], w_sem.at[slot]),
                pltpu.make_async_copy(wd_hbm.at[layer, ex, pl.ds(0, half)],
                                      wdf_ref.at[slot, pl.ds(0, half)], w_sem.at[slot]),
                pltpu.make_async_copy(wd_hbm.at[layer, ex, pl.ds(half, half)],
                                      wdf_ref.at[slot, pl.ds(half, half)], w_sem.at[slot]))

    def start_weights(ex):
        exc = jnp.minimum(ex, n_exp - 1)

        @pl.when((ex < n_exp) & (nblk_ref[exc] > 0))
        def _():
            for piece, c in enumerate(w_copies(exc, exc % ns)):
                c.start(priority=piece % 2)

    @pl.when(e == 0)
    def _():
        for ahead in range(ns - 1):
            start_weights(ahead)

    start_weights(e + ns - 1)

    def block_lines(g):
        return pl.ds(pl.multiple_of(g * blk_lines, blk_lines), blk_lines)

    def in_copy(g, slot):
        return pltpu.make_async_copy(xs_hbm.at[block_lines(g), :], xin_ref.at[slot], in_sem.at[slot])

    def out_copy(g, slot):
        return pltpu.make_async_copy(yout_ref.at[slot], y_hbm.at[block_lines(g), :], out_sem.at[slot])

    @pl.when((e == 0) & (n_valid > 0))
    def _():
        in_copy(0, 0).start(priority=0)

    nb = nblk_ref[e]
    g0 = bstart_ref[e]

    @pl.when(nb > 0)
    def _():
        w_slot = e % ns
        for c in w_copies(e, w_slot):
            c.wait()
        wgb_ref[...] = wgf_ref[w_slot].astype(BF16)
        wub_ref[...] = wuf_ref[w_slot].astype(BF16)
        wdb_ref[...] = wdf_ref[w_slot].astype(BF16)

        def block(j, carry):
            g = g0 + j
            slot = g % 2
            in_copy(g, slot).wait()

            @pl.when(g + 1 < n_valid)
            def _():
                in_copy(g + 1, 1 - slot).start(priority=0)

            @pl.when(g >= 2)
            def _():
                out_copy(g - 2, slot).wait()

            xb = _unpack_rows(_load_rows_from_lines(xin_ref.at[slot], 0, bm, nt)).astype(BF16)
            gate = jnp.dot(xb, wgb_ref[...], preferred_element_type=F32)
            up = jnp.dot(xb, wub_ref[...], preferred_element_type=F32)
            hid = (gate * jax.nn.sigmoid(gate) * up).astype(BF16)
            y = jnp.dot(hid, wdb_ref[...], preferred_element_type=F32)
            _store_rows_as_lines(yout_ref.at[slot], 0, _pack_rows(y))
            out_copy(g, slot).start(priority=1)
            return carry
        lax.fori_loop(0, nb, block, 0)

    @pl.when(e == pl.num_programs(0) - 1)
    def _():
        for back in (2, 1):
            @pl.when(n_valid >= back)
            def _(back=back):
                out_copy(n_valid - back, (n_valid - back) % 2).wait()

        yout_ref[0] = jnp.zeros(yout_ref.shape[1:], yout_ref.dtype)

        def fill(g, carry):
            out_copy(g, 0).start()
            return carry
        lax.fori_loop(n_valid, n_blocks, fill, 0)

        def drain(g, carry):
            out_copy(g, 0).wait()
            return carry
        lax.fori_loop(n_valid, n_blocks, drain, 0)


def _experts(xs, nblk, bstart, n_valid, w_gate, w_up, w_down, layer):
    _, n_exp, d, f = w_gate.shape
    bm = MOE_BLOCK
    nt = d // 2 // LANES
    ns = MOE_W_SLOTS
    grid_spec = pltpu.PrefetchScalarGridSpec(
        num_scalar_prefetch=3,
        grid=(n_exp,),
        in_specs=[pl.BlockSpec(memory_space=pl.ANY)] * 4,
        out_specs=pl.BlockSpec(memory_space=pl.ANY),
        scratch_shapes=[pltpu.VMEM((2, bm * nt, LANES), jnp.uint32),
                        pltpu.VMEM((2, bm * nt, LANES), jnp.uint32),
                        pltpu.VMEM((ns, d, f), F32),
                        pltpu.VMEM((ns, d, f), F32),
                        pltpu.VMEM((ns, f, d), F32),
                        pltpu.VMEM((d, f), BF16),
                        pltpu.VMEM((d, f), BF16),
                        pltpu.VMEM((f, d), BF16),
                        pltpu.SemaphoreType.DMA((2,)),
                        pltpu.SemaphoreType.DMA((2,)),
                        pltpu.SemaphoreType.DMA((ns,))],
    )
    return pl.pallas_call(
        functools.partial(_expert_kernel, layer=layer),
        grid_spec=grid_spec,
        out_shape=jax.ShapeDtypeStruct(xs.shape, jnp.uint32),
        compiler_params=_params("arbitrary"),
        name="moe_experts",
    )(nblk, bstart, n_valid, xs, w_gate, w_up, w_down)


def _combine_ln_kernel(dest_ref, y_hbm, x_ref, route_ref, g_ref, b_ref, o_ref, yg_ref, sem):
    i = pl.program_id(0)
    n = pl.num_programs(0)
    tm = x_ref.shape[0]
    nt = yg_ref.shape[1] // (TOP_K * tm)

    def lines(row, n_rows=1):
        return pl.ds(pl.multiple_of(row * nt, nt), n_rows * nt)

    def start_gather(step, slot):
        base = step * (TOP_K * tm)

        def body(r, carry):
            for k in range(TOP_K):
                src = dest_ref[base + TOP_K * r + k]
                pltpu.make_async_copy(y_hbm.at[lines(src), :], yg_ref.at[slot, lines(k * tm + r), :],
                                      sem.at[slot]).start(priority=k)
            return carry
        lax.fori_loop(0, tm, body, 0, unroll=8)

    def wait_gather(slot):
        pltpu.make_async_copy(y_hbm.at[lines(0, TOP_K * tm), :], yg_ref.at[slot], sem.at[slot]).wait()

    @pl.when(i == 0)
    def _():
        start_gather(0, 0)

    @pl.when(i + 1 < n)
    def _():
        start_gather(i + 1, (i + 1) % 2)

    slot = i % 2
    wait_gather(slot)
    route = route_ref[...]
    lane = lax.broadcasted_iota(jnp.int32, route.shape, 1)
    g1 = jnp.sum(jnp.where(lane == 2, route, 0.0), axis=1, keepdims=True)
    g2 = jnp.sum(jnp.where(lane == 3, route, 0.0), axis=1, keepdims=True)
    y1 = _unpack_rows(_load_rows_from_lines(yg_ref.at[slot], 0, tm, nt))
    y2 = _unpack_rows(_load_rows_from_lines(yg_ref.at[slot], tm * nt, tm, nt))
    z = DEEPNORM_ALPHA * x_ref[...] + (y1 * g1 + y2 * g2)
    o_ref[...] = _layer_norm_rows(z, g_ref[...], b_ref[...])


def _combine_ln(y, dest, x, route, g, b):
    t, d = x.shape
    tm = min(COMB_TM, t)
    grid_spec = pltpu.PrefetchScalarGridSpec(
        num_scalar_prefetch=1,
        grid=(t // tm,),
        in_specs=[pl.BlockSpec(memory_space=pl.ANY),
                  pl.BlockSpec((tm, d), lambda i, dest: (i, 0)),
                  pl.BlockSpec((tm, LANES), lambda i, dest: (i, 0)),
                  pl.BlockSpec((1, d), lambda i, dest: (0, 0)),
                  pl.BlockSpec((1, d), lambda i, dest: (0, 0))],
        out_specs=pl.BlockSpec((tm, d), lambda i, dest: (i, 0)),
        scratch_shapes=[pltpu.VMEM((2, TOP_K * tm * (d // 2 // LANES), LANES), y.dtype),
                        pltpu.SemaphoreType.DMA((2,))],
    )
    return pl.pallas_call(
        _combine_ln_kernel,
        grid_spec=grid_spec,
        out_shape=jax.ShapeDtypeStruct((t, d), F32),
        compiler_params=_params("arbitrary"),
        name="moe_combine_ln",
    )(dest, y, x, route, g.reshape(1, d), b.reshape(1, d))


def _moe_ffn_ln(x, xp, route, counts, w_gate, w_up, w_down, layer, ln_g, ln_b):
    t, d = x.shape
    bm = MOE_BLOCK
    eid = route[:, 0:TOP_K].astype(jnp.int32)
    rank = route[:, 4:4 + TOP_K].astype(jnp.int32)
    cnt = counts[0, N_GROUPS:N_GROUPS + N_EXPERTS].astype(jnp.int32)
    nblk = (cnt + bm - 1) // bm
    bend = jnp.cumsum(nblk)
    bstart = bend - nblk
    pstart = bstart * bm
    onehot = eid[:, :, None] == jnp.arange(N_EXPERTS, dtype=jnp.int32)
    dest = (jnp.sum(jnp.where(onehot, pstart, 0), axis=-1) + rank).reshape(-1)
    n_valid = bend[-1:]
    n_slots = (-(-t * TOP_K // bm) + N_EXPERTS) * bm
    xs = _dispatch(xp, t, dest, cnt, pstart, n_valid, n_slots)
    y = _experts(xs, nblk, bstart, n_valid, w_gate, w_up, w_down, layer)
    return _combine_ln(y, dest, x, route, ln_g, ln_b)


def _gla_gate_kernel(x_ref, wl_ref, wu_ref, b_ref, la_ref):
    g_low = _dot_hi_lo(x_ref[...], wl_ref)
    n = wu_ref.shape[1] // 2
    gh, gl = _split2(g_low)
    a = jnp.dot(gh, wu_ref[...], preferred_element_type=F32)
    c = jnp.dot(gl, wu_ref[:, :n], preferred_element_type=F32)
    logit = a[:, :n] + a[:, n:] + c + b_ref[...]
    la_ref[...] = _log_sigmoid(logit) / GLA_GATE_TAU


def _gla_gates(x, w_low_t, w_gate_up, b_gate):
    s, d = x.shape
    rank, dk = w_gate_up.shape
    ts = min(ROW_TILE, s)
    wl = _pack_hi_lo(w_low_t)
    wu_pad = jnp.zeros((LANES, dk), F32).at[:rank].set(w_gate_up)
    wu = jnp.concatenate(_split2(wu_pad), axis=1)
    return pl.pallas_call(
        _gla_gate_kernel,
        grid=(s // ts,),
        in_specs=[pl.BlockSpec((ts, d), lambda i: (i, 0)),
                  pl.BlockSpec((2 * LANES, d), lambda i: (0, 0)),
                  pl.BlockSpec((LANES, 2 * dk), lambda i: (0, 0)),
                  pl.BlockSpec((1, dk), lambda i: (0, 0))],
        out_specs=pl.BlockSpec((ts, dk), lambda i: (i, 0)),
        out_shape=jax.ShapeDtypeStruct((s, dk), F32),
        compiler_params=_params("parallel"),
        name="gla_gates",
    )(x, wl, wu, b_gate.reshape(1, dk))


def _gla_kernel(q_ref, k_ref, v_ref, la_ref, r_ref, g_ref, o_ref, state_ref):
    @pl.when(pl.program_id(0) == 0)
    def _():
        state_ref[...] = jnp.zeros_like(state_ref)

    rows = q_ref.shape[0]
    nh, dk, dv = state_ref.shape
    cs = GLA_CHUNK
    tri = _tri(cs)
    rr = lax.broadcasted_iota(jnp.int32, (cs, cs), 0)
    cc = lax.broadcasted_iota(jnp.int32, (cs, cs), 1)
    causal = rr >= cc
    for ci in range(rows // cs):
        sl = pl.ds(ci * cs, cs)
        for h in range(nh):
            kc = pl.ds(h * dk, dk)
            vc = pl.ds(h * dv, dv)
            b = _cumsum_rows(la_ref[sl, kc], tri)
            b_mid = b[cs // 2 - 1:cs // 2, :]
            b_last = b[cs - 1:cs, :]
            q = q_ref[sl, kc].astype(F32) * (dk ** -0.5)
            k = k_ref[sl, kc].astype(F32)
            v = v_ref[sl, vc]
            qa = (q * jnp.exp(b - b_mid)).astype(BF16)
            ka = (k * jnp.exp(b_mid - b)).astype(BF16)
            a = jnp.where(causal, _dot_nt(qa, ka), 0.0)
            o = jnp.dot(a.astype(BF16), v, preferred_element_type=F32)
            state = state_ref[h]
            o = o + jnp.dot((q * jnp.exp(b)).astype(BF16), state.astype(BF16), preferred_element_type=F32)
            k_end_t = (k * jnp.exp(b_last - b)).T.astype(BF16)
            decay = jnp.exp(b.T[:, cs - 1:cs])
            state_ref[h] = decay * state + jnp.dot(k_end_t, v, preferred_element_type=F32)
            o = o * lax.rsqrt(jnp.mean(o * o, axis=-1, keepdims=True) + RMS_EPS) * g_ref[...]
            r = r_ref[sl, vc].astype(F32)
            o_ref[sl, vc] = (o * (r * jax.nn.sigmoid(r))).astype(o_ref.dtype)


def _gla(proj, la, norm_g):
    s = proj.shape[0]
    dk_all = la.shape[1]
    nh = GLA_HEADS
    dk = dk_all // nh
    dv = norm_g.shape[0]
    dv_all = nh * dv
    assert 2 * dk_all == dv_all
    rows = min(GLA_ROWS, s)
    return pl.pallas_call(
        _gla_kernel,
        grid=(s // rows,),
        in_specs=[pl.BlockSpec((rows, dk_all), lambda i: (i, 0)),
                  pl.BlockSpec((rows, dk_all), lambda i: (i, 1)),
                  pl.BlockSpec((rows, dv_all), lambda i: (i, 1)),
                  pl.BlockSpec((rows, dk_all), lambda i: (i, 0)),
                  pl.BlockSpec((rows, dv_all), lambda i: (i, 2)),
                  pl.BlockSpec((1, dv), lambda i: (0, 0))],
        out_specs=pl.BlockSpec((rows, dv_all), lambda i: (i, 0)),
        out_shape=jax.ShapeDtypeStruct((s, dv_all), BF16),
        scratch_shapes=[pltpu.VMEM((nh, dk, dv), F32)],
        compiler_params=_params("arbitrary"),
        name="gla_chunks",
    )(proj, proj, proj, la, proj, norm_g.reshape(1, dv))


def kernel(x, fox_w_in, fox_b_f, fox_w_o, gla_w_in, gla_w_gate_up, gla_b_gate, gla_norm_g, gla_w_o,
           ln_mix_g, ln_mix_b, ln_ffn_g, ln_ffn_b, moe_w_group, moe_b_group, moe_w_expert,
           moe_b_expert, moe_w_gate, moe_w_up, moe_w_down):
    bsz, s, d = x.shape
    outs = []
    for bi in range(bsz):
        xt = x[bi]
        for i in range(DEPTH):
            j = i // 2
            if i % 2 == 0:
                w_in_t = fox_w_in[j].T
                qkv = _matmul_nt(xt, w_in_t, 3 * d, BF16)
                c = _fox_gates(xt, w_in_t[3 * d:], fox_b_f[j])
                o = _fox_attention(qkv, c, FOX_HEADS)
                w_o = fox_w_o[j]
            else:
                w_in_t = gla_w_in[j].T
                n_main = w_in_t.shape[0] - gla_w_gate_up.shape[1]
                proj = _matmul_nt(xt, w_in_t, n_main, BF16)
                la = _gla_gates(xt, w_in_t[n_main:], gla_w_gate_up[j], gla_b_gate[j])
                o = _gla(proj, la, gla_norm_g[j])
                w_o = gla_w_o[j]
            xt, xp, route, counts = _proj_ln_route(o, w_o.astype(BF16), xt, ln_mix_g[i], ln_mix_b[i], moe_w_group[i],
                                                   moe_b_group[i], moe_w_expert[i], moe_b_expert[i])
            xt = _moe_ffn_ln(xt, xp, route, counts, moe_w_gate, moe_w_up, moe_w_down, i, ln_ffn_g[i], ln_ffn_b[i])
        outs.append(xt)
    return outs[0].reshape(1, s, d) if bsz == 1 else jnp.stack(outs, axis=0)
```

```python
import functools

import jax
import jax.numpy as jnp
from jax import lax
from jax.experimental import pallas as pl
from jax.experimental.pallas import tpu as pltpu

F32 = jnp.float32
BF16 = jnp.bfloat16

DEPTH = 2
FOX_HEADS = 16
FOX_HEAD_DIM = 128
GLA_HEADS = 4
GLA_CHUNK = 64
GLA_GATE_TAU = 16.0
N_GROUPS = 8
EXPERTS_PER_GROUP = 8
N_EXPERTS = N_GROUPS * EXPERTS_PER_GROUP
TOP_K = 2
DEEPNORM_ALPHA = (2 * DEPTH) ** 0.25
LN_EPS = 1e-5
RMS_EPS = 1e-6

LANES = 128
VMEM_LIMIT = 56 * 2**20

MM_TM, MM_TN = 1024, 1024
ROW_TILE = 512
ATT_TQ, ATT_TK = 1024, 512
GLA_ROWS = 256
MOE_BLOCK = 128
MOE_W_SLOTS = 3
COMB_TM = 256


def _params(*sem):
    return pltpu.CompilerParams(dimension_semantics=sem, vmem_limit_bytes=VMEM_LIMIT)


def _split2(a):
    hi = a.astype(BF16)
    lo = (a - hi.astype(F32)).astype(BF16)
    return hi, lo


def _split3(a):
    hi = a.astype(BF16)
    r = a - hi.astype(F32)
    mid = r.astype(BF16)
    lo = (r - mid.astype(F32)).astype(BF16)
    return hi, mid, lo


def _pack_hi_lo(w_t, n_pad=LANES):
    n, k = w_t.shape
    wp = jnp.zeros((n_pad, k), F32).at[:n].set(w_t)
    hi, lo = _split2(wp)
    return jnp.concatenate([hi, lo], axis=0)


def _dot_nt(a, b):
    return lax.dot_general(a, b, (((1,), (1,)), ((), ())), preferred_element_type=F32)


def _dot_hi_lo(x, whl_ref, n_pad=LANES):
    xh, xl = _split2(x)
    a = _dot_nt(xh, whl_ref[...])
    b = _dot_nt(xl, whl_ref[:n_pad, :])
    return a[:, :n_pad] + a[:, n_pad:] + b


def _pack_rows(a):
    half = a.shape[1] // 2
    hi = lax.bitcast_convert_type(a[:, :half].astype(BF16).astype(F32), jnp.uint32)
    lo = lax.bitcast_convert_type(a[:, half:].astype(BF16).astype(F32), jnp.uint32)
    return hi | (lo >> 16)


def _unpack_rows(w):
    hi = lax.bitcast_convert_type(w & jnp.uint32(0xFFFF0000), F32)
    lo = lax.bitcast_convert_type(w << 16, F32)
    return jnp.concatenate([hi, lo], axis=1)


def _store_rows_as_lines(ref, first_line, packed):
    n_rows, width = packed.shape
    nt = width // LANES
    for j in range(nt):
        ref[pl.ds(first_line + j, n_rows, stride=nt), :] = packed[:, j * LANES:(j + 1) * LANES]


def _load_rows_from_lines(ref, first_line, n_rows, nt):
    return jnp.concatenate([ref[pl.ds(first_line + j, n_rows, stride=nt), :] for j in range(nt)], axis=1)


def _log_sigmoid(x):
    return -(jnp.maximum(-x, 0.0) + jnp.log1p(jnp.exp(-jnp.abs(x))))


def _cumsum_rows(a, incl_tri):
    n = a.shape[1]
    parts = jnp.concatenate(_split3(a), axis=1)
    c = jnp.dot(incl_tri, parts, preferred_element_type=F32)
    return c[:, :n] + c[:, n:2 * n] + c[:, 2 * n:]


def _tri(n, strict=False):
    r = lax.broadcasted_iota(jnp.int32, (n, n), 0)
    c = lax.broadcasted_iota(jnp.int32, (n, n), 1)
    return jnp.where((r > c) if strict else (r >= c), 1.0, 0.0).astype(BF16)


def _layer_norm_rows(z, g, b):
    mu = jnp.mean(z, axis=-1, keepdims=True)
    d = z - mu
    var = jnp.mean(d * d, axis=-1, keepdims=True)
    return d * lax.rsqrt(var + LN_EPS) * g + b


def _mm_kernel(x_ref, w_ref, o_ref, wb_ref):
    @pl.when(pl.program_id(1) == 0)
    def _():
        wb_ref[...] = w_ref[...].astype(BF16)

    o_ref[...] = lax.dot_general(x_ref[...].astype(BF16), wb_ref[...], (((1,), (1,)), ((), ())),
                                 preferred_element_type=F32).astype(o_ref.dtype)


def _matmul_nt(x, w_t, n, out_dtype):
    m, k = x.shape
    tm, tn = min(MM_TM, m), min(MM_TN, n)
    return pl.pallas_call(
        _mm_kernel,
        grid=(n // tn, m // tm),
        in_specs=[pl.BlockSpec((tm, k), lambda j, i: (i, 0)),
                  pl.BlockSpec((tn, k), lambda j, i: (j, 0))],
        out_specs=pl.BlockSpec((tm, tn), lambda j, i: (i, j)),
        out_shape=jax.ShapeDtypeStruct((m, n), out_dtype),
        scratch_shapes=[pltpu.VMEM((tn, k), BF16)],
        compiler_params=_params("arbitrary", "arbitrary"),
        name="dense_proj",
    )(x, w_t)


def _fox_gate_kernel(x_ref, w_ref, b_ref, c_ref, carry_ref):
    @pl.when(pl.program_id(0) == 0)
    def _():
        carry_ref[...] = jnp.zeros_like(carry_ref)

    ts = x_ref.shape[0]
    logits = _dot_hi_lo(x_ref[...], w_ref) + b_ref[...]
    log_f = _log_sigmoid(logits)
    c = _cumsum_rows(log_f, _tri(ts)) + carry_ref[...]
    carry_ref[...] = c[ts - 1:ts, :]
    c_ref[...] = c


def _fox_gates(x, w_f_t, b_f):
    s, d = x.shape
    h = w_f_t.shape[0]
    ts = min(ROW_TILE, s)
    whl = _pack_hi_lo(w_f_t)
    bias = jnp.zeros((1, LANES), F32).at[0, :h].set(b_f)
    return pl.pallas_call(
        _fox_gate_kernel,
        grid=(s // ts,),
        in_specs=[pl.BlockSpec((ts, d), lambda i: (i, 0)),
                  pl.BlockSpec((2 * LANES, d), lambda i: (0, 0)),
                  pl.BlockSpec((1, LANES), lambda i: (0, 0))],
        out_specs=pl.BlockSpec((ts, LANES), lambda i: (i, 0)),
        out_shape=jax.ShapeDtypeStruct((s, LANES), F32),
        scratch_shapes=[pltpu.VMEM((1, LANES), F32)],
        compiler_params=_params("arbitrary"),
        name="fox_gates",
    )(x, whl, bias)


def _fox_attn_kernel(q_ref, k_ref, v_ref, c_ref, o_ref, kaug_ref, vt_ref, qt_ref, st0_ref, st1_ref,
                     p0_ref, p1_ref, alpha_ref, bmax_ref, acc_ref, m_ref):
    h = pl.program_id(0)
    qi = pl.program_id(1)
    tq, dh = q_ref.shape
    s_len = k_ref.shape[0]
    tk = ATT_TK
    log2e = 1.4426950408889634
    scale = dh ** -0.5 * log2e

    def head_column(rows, n):
        lane = lax.broadcasted_iota(jnp.int32, (n, LANES), 1)
        return log2e * jnp.sum(jnp.where(lane == h, c_ref[rows, :], 0.0), axis=1, keepdims=True)

    def bias_columns(col, first, n):
        lane = lax.broadcasted_iota(jnp.int32, (n, LANES), 1)
        hi, mid, lo = _split3(col)
        ones_first = 3 - first
        out = jnp.where((lane >= ones_first) & (lane < ones_first + 3), 1.0, 0.0)
        out = jnp.where(lane == first, hi.astype(F32), out)
        out = jnp.where(lane == first + 1, mid.astype(F32), out)
        return jnp.where(lane == first + 2, lo.astype(F32), out).astype(BF16)

    @pl.when(qi == 0)
    def _():
        ones_row = jnp.where(lax.broadcasted_iota(jnp.int32, (16, tk), 0) == 0, 1.0, 0.0).astype(BF16)

        def build(ci, carry):
            rows = pl.ds(pl.multiple_of(ci * tk, tk), tk)
            kaug_ref[rows, :dh] = k_ref[rows, :]
            kaug_ref[rows, dh:] = bias_columns(-head_column(rows, tk), 0, tk)
            vt_ref[:dh, rows] = v_ref[rows, :].astype(F32).T.astype(BF16)
            vt_ref[dh:, rows] = ones_row
            return carry
        lax.fori_loop(0, s_len // tk, build, 0)

    q_rows = pl.ds(pl.multiple_of(qi * tq, tq), tq)
    q_aug = jnp.concatenate([q_ref[...].astype(F32) * scale,
                             bias_columns(head_column(q_rows, tq), 3, tq).astype(F32)], axis=1)
    qt_ref[...] = q_aug.T.astype(BF16)

    acc_ref[...] = jnp.zeros_like(acc_ref)
    m_ref[...] = jnp.full_like(m_ref, -jnp.inf)

    st_refs, p_refs = (st0_ref, st1_ref), (p0_ref, p1_ref)

    def scores(k_start, slot, diag_offset=None):
        st = jnp.dot(kaug_ref[pl.ds(k_start, tk), :], qt_ref[...], preferred_element_type=F32)
        if diag_offset is not None:
            kr = lax.broadcasted_iota(jnp.int32, (tk, tq), 0) + diag_offset
            qc = lax.broadcasted_iota(jnp.int32, (tk, tq), 1)
            st = jnp.where(kr <= qc, st, -jnp.inf)
        st_refs[slot][...] = st
        bmax_ref[slot] = jnp.max(st, axis=0, keepdims=True)

    def softmax(slot):
        m_prev = m_ref[...]
        m_new = jnp.maximum(m_prev, bmax_ref[slot])
        m_ref[...] = m_new
        p_refs[slot][...] = jnp.exp2(st_refs[slot][...] - m_new).astype(BF16)
        alpha_ref[slot] = jnp.exp2(m_prev - m_new)

    def values(k_start, slot):
        acc_ref[...] = alpha_ref[slot] * acc_ref[...] + jnp.dot(
            vt_ref[:, pl.ds(k_start, tk)], p_refs[slot][...], preferred_element_type=F32)

    n_diag = tq // tk
    assert n_diag == 2
    d0 = pl.multiple_of(qi * tq, tk)
    d1 = pl.multiple_of(qi * tq + tk, tk)
    n_full = qi * n_diag

    def full_start(i):
        return pl.multiple_of(jnp.minimum(i, jnp.maximum(n_full - 1, 0)) * tk, tk)

    scores(d0, 0, 0)
    scores(d1, 1, tk)
    softmax(0)
    scores(full_start(0), 0)
    softmax(1)
    values(d0, 0)

    def pair(j, carry):
        i0 = 2 * j
        prev = jnp.where(j == 0, d1, (i0 - 1) * tk)
        scores(full_start(i0 + 1), 1)
        softmax(0)
        values(pl.multiple_of(prev, tk), 1)
        scores(full_start(i0 + 2), 0)
        softmax(1)
        values(pl.multiple_of(i0 * tk, tk), 0)
        return carry

    lax.fori_loop(0, qi, pair, 0)
    last = jnp.where(qi == 0, d1, (n_full - 1) * tk)
    values(pl.multiple_of(last, tk), 1)

    o_ref[...] = (acc_ref[:dh, :] / acc_ref[dh:dh + 1, :]).T.astype(o_ref.dtype)


def _fox_attention(qkv, c, n_heads):
    s = qkv.shape[0]
    dh = FOX_HEAD_DIM
    tq = min(ATT_TQ, s)
    hh = n_heads
    return pl.pallas_call(
        _fox_attn_kernel,
        grid=(hh, s // tq),
        in_specs=[pl.BlockSpec((tq, dh), lambda h, i: (i, h)),
                  pl.BlockSpec((s, dh), lambda h, i: (0, hh + h)),
                  pl.BlockSpec((s, dh), lambda h, i: (0, 2 * hh + h)),
                  pl.BlockSpec((s, LANES), lambda h, i: (0, 0))],
        out_specs=pl.BlockSpec((tq, dh), lambda h, i: (i, h)),
        out_shape=jax.ShapeDtypeStruct((s, hh * dh), BF16),
        scratch_shapes=[pltpu.VMEM((s, 2 * dh), BF16),
                        pltpu.VMEM((dh + 16, s), BF16),
                        pltpu.VMEM((2 * dh, tq), BF16),
                        pltpu.VMEM((ATT_TK, tq), F32),
                        pltpu.VMEM((ATT_TK, tq), F32),
                        pltpu.VMEM((ATT_TK, tq), BF16),
                        pltpu.VMEM((ATT_TK, tq), BF16),
                        pltpu.VMEM((2, 1, tq), F32),
                        pltpu.VMEM((2, 1, tq), F32),
                        pltpu.VMEM((dh + 16, tq), F32),
                        pltpu.VMEM((1, tq), F32)],
        compiler_params=_params("arbitrary", "arbitrary"),
        name="fox_attention",
    )(qkv, qkv, qkv, c)


def _first_lane_eq(vals, target, lane):
    return jnp.min(jnp.where(vals == target, lane, LANES), axis=1, keepdims=True)


def _route_rows(x, w_ref, b_ref, carry_ref):
    tm = x.shape[0]
    neg = -jnp.inf
    logits = _dot_hi_lo(x, w_ref) + b_ref[...]
    lane = lax.broadcasted_iota(jnp.int32, (tm, LANES), 1)

    gl = jnp.where(lane < N_GROUPS, logits, neg)
    gmax = jnp.max(gl, axis=1, keepdims=True)
    gsum = jnp.sum(jnp.exp(gl - gmax), axis=1, keepdims=True)
    grp_p = 1.0 / gsum
    grp = _first_lane_eq(gl, gmax, lane)

    lo = N_GROUPS + grp * EXPERTS_PER_GROUP
    el = jnp.where((lane >= lo) & (lane < lo + EXPERTS_PER_GROUP), logits, neg)
    emax = jnp.max(el, axis=1, keepdims=True)
    esum = jnp.sum(jnp.exp(el - emax), axis=1, keepdims=True)
    idx1 = _first_lane_eq(el, emax, lane)
    el2 = jnp.where(lane == idx1, neg, el)
    emax2 = jnp.max(el2, axis=1, keepdims=True)
    idx2 = _first_lane_eq(el2, emax2, lane)
    p1 = 1.0 / esum
    p2 = jnp.exp(emax2 - emax) / esum
    psum = p1 + p2
    g1 = grp_p * (p1 / psum)
    g2 = grp_p * (p2 / psum)

    oh1 = lane == idx1
    oh2 = lane == idx2
    both = jnp.where(oh1 | oh2, 1.0, 0.0)
    before = jnp.dot(_tri(tm, strict=True), both.astype(BF16), preferred_element_type=F32)
    before = before + carry_ref[...]
    r1 = jnp.sum(jnp.where(oh1, before, 0.0), axis=1, keepdims=True)
    r2 = jnp.sum(jnp.where(oh2, before, 0.0), axis=1, keepdims=True)
    carry_ref[...] = carry_ref[...] + jnp.sum(both, axis=0, keepdims=True)

    e1 = (idx1 - N_GROUPS).astype(F32)
    e2 = (idx2 - N_GROUPS).astype(F32)
    out = jnp.where(lane == 0, e1, 0.0)
    out = jnp.where(lane == 1, e2, out)
    out = jnp.where(lane == 2, g1, out)
    out = jnp.where(lane == 3, g2, out)
    out = jnp.where(lane == 4, r1, out)
    return jnp.where(lane == 5, r2, out)


def _proj_ln_route_kernel(o_ref, w_ref, x_ref, g_ref, b_ref, wr_ref, br_ref,
                          y_ref, yp_ref, route_ref, cnt_ref, carry_ref):
    @pl.when(pl.program_id(0) == 0)
    def _():
        carry_ref[...] = jnp.zeros_like(carry_ref)

    mix = jnp.dot(o_ref[...], w_ref[...], preferred_element_type=F32)
    y = _layer_norm_rows(DEEPNORM_ALPHA * x_ref[...] + mix, g_ref[...], b_ref[...])
    y_ref[...] = y
    _store_rows_as_lines(yp_ref, 0, _pack_rows(y))
    route_ref[...] = _route_rows(y, wr_ref, br_ref, carry_ref)
    cnt_ref[...] = carry_ref[...]


def _proj_ln_route(o, w, x, g, b, w_group, b_group, w_expert, b_expert):
    s, d = x.shape
    kd = o.shape[1]
    tm = min(ROW_TILE, s)
    whl = _pack_hi_lo(jnp.concatenate([w_group.T, w_expert.T], axis=0))
    nb = N_GROUPS + N_EXPERTS
    bias = jnp.zeros((1, LANES), F32).at[0, :nb].set(jnp.concatenate([b_group, b_expert]))
    return pl.pallas_call(
        _proj_ln_route_kernel,
        grid=(s // tm,),
        in_specs=[pl.BlockSpec((tm, kd), lambda i: (i, 0)),
                  pl.BlockSpec((kd, d), lambda i: (0, 0)),
                  pl.BlockSpec((tm, d), lambda i: (i, 0)),
                  pl.BlockSpec((1, d), lambda i: (0, 0)),
                  pl.BlockSpec((1, d), lambda i: (0, 0)),
                  pl.BlockSpec((2 * LANES, d), lambda i: (0, 0)),
                  pl.BlockSpec((1, LANES), lambda i: (0, 0))],
        out_specs=[pl.BlockSpec((tm, d), lambda i: (i, 0)),
                   pl.BlockSpec((tm * (d // 2 // LANES), LANES), lambda i: (i, 0)),
                   pl.BlockSpec((tm, LANES), lambda i: (i, 0)),
                   pl.BlockSpec((1, LANES), lambda i: (0, 0))],
        out_shape=[jax.ShapeDtypeStruct((s, d), F32),
                   jax.ShapeDtypeStruct((s * (d // 2 // LANES), LANES), jnp.uint32),
                   jax.ShapeDtypeStruct((s, LANES), F32),
                   jax.ShapeDtypeStruct((1, LANES), F32)],
        scratch_shapes=[pltpu.VMEM((1, LANES), F32)],
        compiler_params=_params("arbitrary"),
        name="proj_ln_route",
    )(o, w, x, g.reshape(1, d), b.reshape(1, d), whl, bias)


def _dispatch_kernel(dest_ref, cnt_ref, pstart_ref, nv_ref, x_ref, xs_hbm, zero_ref, sem, zsem, *, nt):
    i = pl.program_id(0)
    tm = x_ref.shape[0] // nt
    bm = MOE_BLOCK
    n_exp = cnt_ref.shape[0]
    n_blocks = xs_hbm.shape[0] // (bm * nt)

    def lines(row, n_rows=1):
        return pl.ds(pl.multiple_of(row * nt, nt), n_rows * nt)

    def pad_copies(e, fn):
        cnt = cnt_ref[e]
        pos = pstart_ref[e] + cnt

        def row(r, carry):
            fn(pltpu.make_async_copy(zero_ref.at[lines(0), :], xs_hbm.at[lines(pos + r), :], zsem))
            return carry
        lax.fori_loop(0, (bm - cnt % bm) % bm, row, 0)

    def tail_copy(g):
        return pltpu.make_async_copy(zero_ref, xs_hbm.at[lines(g * bm, bm), :], zsem)

    def for_all_fills(fn):
        def per_expert(e, carry):
            pad_copies(e, fn)
            return carry
        lax.fori_loop(0, n_exp, per_expert, 0)

        def per_tail(g, carry):
            fn(tail_copy(g))
            return carry
        lax.fori_loop(nv_ref[0], n_blocks, per_tail, 0)

    @pl.when(i == 0)
    def _():
        zero_ref[...] = jnp.zeros_like(zero_ref)
        for_all_fills(lambda c: c.start())

    base = i * (TOP_K * tm)

    def body(r, carry):
        for k in range(TOP_K):
            dst = dest_ref[base + TOP_K * r + k]
            pltpu.make_async_copy(x_ref.at[lines(r), :], xs_hbm.at[lines(dst), :], sem).start(priority=k)
        return carry
    lax.fori_loop(0, tm, body, 0, unroll=8)

    @pl.when(i == 0)
    def _():
        for_all_fills(lambda c: c.wait())

    for _ in range(TOP_K):
        pltpu.make_async_copy(x_ref, xs_hbm.at[lines(0, tm), :], sem).wait()


def _dispatch(xp, n_rows, dest, cnt, pstart, n_valid, n_slots):
    nt = xp.shape[0] // n_rows
    tm = min(ROW_TILE, n_rows)
    grid_spec = pltpu.PrefetchScalarGridSpec(
        num_scalar_prefetch=4,
        grid=(n_rows // tm,),
        in_specs=[pl.BlockSpec((tm * nt, LANES), lambda i, *_: (i, 0))],
        out_specs=pl.BlockSpec(memory_space=pl.ANY),
        scratch_shapes=[pltpu.VMEM((MOE_BLOCK * nt, LANES), xp.dtype),
                        pltpu.SemaphoreType.DMA,
                        pltpu.SemaphoreType.DMA],
    )
    return pl.pallas_call(
        functools.partial(_dispatch_kernel, nt=nt),
        grid_spec=grid_spec,
        out_shape=jax.ShapeDtypeStruct((n_slots * nt, LANES), xp.dtype),
        compiler_params=_params("arbitrary"),
        name="moe_dispatch",
    )(dest, cnt, pstart, n_valid, xp)


def _expert_kernel(nblk_ref, bstart_ref, nv_ref, xs_hbm, wg_hbm, wu_hbm, wd_hbm, y_hbm,
                   xin_ref, yout_ref, wgf_ref, wuf_ref, wdf_ref, wgb_ref, wub_ref, wdb_ref,
                   in_sem, out_sem, w_sem, *, layer):
    e = pl.program_id(0)
    n_exp = pl.num_programs(0)
    n_valid = nv_ref[0]
    bm = MOE_BLOCK
    blk_lines = xin_ref.shape[1]
    nt = blk_lines // bm
    n_blocks = y_hbm.shape[0] // blk_lines
    ns = MOE_W_SLOTS

    half = wd_hbm.shape[2] // 2

    def w_copies(ex, slot):
        return (pltpu.make_async_copy(wg_hbm.at[layer, ex], wgf_ref.at[slot], w_sem.at[slot]),
                pltpu.make_async_copy(wu_hbm.at[layer, ex], wuf_ref.at[slot], w_sem.at[slot]),
                pltpu.make_async_copy(wd_hbm.at[layer, ex, pl.ds(0, half)],
                                      wdf_ref.at[slot, pl.ds(0, half)], w_sem.at[slot]),
                pltpu.make_async_copy(wd_hbm.at[layer, ex, pl.ds(half, half)],
                                      wdf_ref.at[slot, pl.ds(half, half)], w_sem.at[slot]))

    def start_weights(ex):
        exc = jnp.minimum(ex, n_exp - 1)

        @pl.when((ex < n_exp) & (nblk_ref[exc] > 0))
        def _():
            for piece, c in enumerate(w_copies(exc, exc % ns)):
                c.start(priority=piece % 2)

    @pl.when(e == 0)
    def _():
        for ahead in range(ns - 1):
            start_weights(ahead)

    start_weights(e + ns - 1)

    def block_lines(g):
        return pl.ds(pl.multiple_of(g * blk_lines, blk_lines), blk_lines)

    def in_copy(g, slot):
        return pltpu.make_async_copy(xs_hbm.at[block_lines(g), :], xin_ref.at[slot], in_sem.at[slot])

    def out_copy(g, slot):
        return pltpu.make_async_copy(yout_ref.at[slot], y_hbm.at[block_lines(g), :], out_sem.at[slot])

    @pl.when((e == 0) & (n_valid > 0))
    def _():
        in_copy(0, 0).start(priority=0)

    nb = nblk_ref[e]
    g0 = bstart_ref[e]

    @pl.when(nb > 0)
    def _():
        w_slot = e % ns
        for c in w_copies(e, w_slot):
            c.wait()
        wgb_ref[...] = wgf_ref[w_slot].astype(BF16)
        wub_ref[...] = wuf_ref[w_slot].astype(BF16)
        wdb_ref[...] = wdf_ref[w_slot].astype(BF16)

        def block(j, carry):
            g = g0 + j
            slot = g % 2
            in_copy(g, slot).wait()

            @pl.when(g + 1 < n_valid)
            def _():
                in_copy(g + 1, 1 - slot).start(priority=0)

            @pl.when(g >= 2)
            def _():
                out_copy(g - 2, slot).wait()

            xb = _unpack_rows(_load_rows_from_lines(xin_ref.at[slot], 0, bm, nt)).astype(BF16)
            gate = jnp.dot(xb, wgb_ref[...], preferred_element_type=F32)
            up = jnp.dot(xb, wub_ref[...], preferred_element_type=F32)
            hid = (gate * jax.nn.sigmoid(gate) * up).astype(BF16)
            y = jnp.dot(hid, wdb_ref[...], preferred_element_type=F32)
            _store_rows_as_lines(yout_ref.at[slot], 0, _pack_rows(y))
            out_copy(g, slot).start(priority=1)
            return carry
        lax.fori_loop(0, nb, block, 0)

    @pl.when(e == pl.num_programs(0) - 1)
    def _():
        for back in (2, 1):
            @pl.when(n_valid >= back)
            def _(back=back):
                out_copy(n_valid - back, (n_valid - back) % 2).wait()

        yout_ref[0] = jnp.zeros(yout_ref.shape[1:], yout_ref.dtype)

        def fill(g, carry):
            out_copy(g, 0).start()
            return carry
        lax.fori_loop(n_valid, n_blocks, fill, 0)

        def drain(g, carry):
            out_copy(g, 0).wait()
            return carry
        lax.fori_loop(n_valid, n_blocks, drain, 0)


def _experts(xs, nblk, bstart, n_valid, w_gate, w_up, w_down, layer):
    _, n_exp, d, f = w_gate.shape
    bm = MOE_BLOCK
    nt = d // 2 // LANES
    ns = MOE_W_SLOTS
    grid_spec = pltpu.PrefetchScalarGridSpec(
        num_scalar_prefetch=3,
        grid=(n_exp,),
        in_specs=[pl.BlockSpec(memory_space=pl.ANY)] * 4,
        out_specs=pl.BlockSpec(memory_space=pl.ANY),
        scratch_shapes=[pltpu.VMEM((2, bm * nt, LANES), jnp.uint32),
                        pltpu.VMEM((2, bm * nt, LANES), jnp.uint32),
                        pltpu.VMEM((ns, d, f), F32),
                        pltpu.VMEM((ns, d, f), F32),
                        pltpu.VMEM((ns, f, d), F32),
                        pltpu.VMEM((d, f), BF16),
                        pltpu.VMEM((d, f), BF16),
                        pltpu.VMEM((f, d), BF16),
                        pltpu.SemaphoreType.DMA((2,)),
                        pltpu.SemaphoreType.DMA((2,)),
                        pltpu.SemaphoreType.DMA((ns,))],
    )
    return pl.pallas_call(
        functools.partial(_expert_kernel, layer=layer),
        grid_spec=grid_spec,
        out_shape=jax.ShapeDtypeStruct(xs.shape, jnp.uint32),
        compiler_params=_params("arbitrary"),
        name="moe_experts",
    )(nblk, bstart, n_valid, xs, w_gate, w_up, w_down)


def _combine_ln_kernel(dest_ref, y_hbm, x_ref, route_ref, g_ref, b_ref, o_ref, yg_ref, sem):
    i = pl.program_id(0)
    n = pl.num_programs(0)
    tm = x_ref.shape[0]
    nt = yg_ref.shape[1] // (TOP_K * tm)

    def lines(row, n_rows=1):
        return pl.ds(pl.multiple_of(row * nt, nt), n_rows * nt)

    def start_gather(step, slot):
        base = step * (TOP_K * tm)

        def body(r, carry):
            for k in range(TOP_K):
                src = dest_ref[base + TOP_K * r + k]
                pltpu.make_async_copy(y_hbm.at[lines(src), :], yg_ref.at[slot, lines(k * tm + r), :],
                                      sem.at[slot]).start(priority=k)
            return carry
        lax.fori_loop(0, tm, body, 0, unroll=8)

    def wait_gather(slot):
        pltpu.make_async_copy(y_hbm.at[lines(0, TOP_K * tm), :], yg_ref.at[slot], sem.at[slot]).wait()

    @pl.when(i == 0)
    def _():
        start_gather(0, 0)

    @pl.when(i + 1 < n)
    def _():
        start_gather(i + 1, (i + 1) % 2)

    slot = i % 2
    wait_gather(slot)
    route = route_ref[...]
    lane = lax.broadcasted_iota(jnp.int32, route.shape, 1)
    g1 = jnp.sum(jnp.where(lane == 2, route, 0.0), axis=1, keepdims=True)
    g2 = jnp.sum(jnp.where(lane == 3, route, 0.0), axis=1, keepdims=True)
    y1 = _unpack_rows(_load_rows_from_lines(yg_ref.at[slot], 0, tm, nt))
    y2 = _unpack_rows(_load_rows_from_lines(yg_ref.at[slot], tm * nt, tm, nt))
    z = DEEPNORM_ALPHA * x_ref[...] + (y1 * g1 + y2 * g2)
    o_ref[...] = _layer_norm_rows(z, g_ref[...], b_ref[...])


def _combine_ln(y, dest, x, route, g, b):
    t, d = x.shape
    tm = min(COMB_TM, t)
    grid_spec = pltpu.PrefetchScalarGridSpec(
        num_scalar_prefetch=1,
        grid=(t // tm,),
        in_specs=[pl.BlockSpec(memory_space=pl.ANY),
                  pl.BlockSpec((tm, d), lambda i, dest: (i, 0)),
                  pl.BlockSpec((tm, LANES), lambda i, dest: (i, 0)),
                  pl.BlockSpec((1, d), lambda i, dest: (0, 0)),
                  pl.BlockSpec((1, d), lambda i, dest: (0, 0))],
        out_specs=pl.BlockSpec((tm, d), lambda i, dest: (i, 0)),
        scratch_shapes=[pltpu.VMEM((2, TOP_K * tm * (d // 2 // LANES), LANES), y.dtype),
                        pltpu.SemaphoreType.DMA((2,))],
    )
    return pl.pallas_call(
        _combine_ln_kernel,
        grid_spec=grid_spec,
        out_shape=jax.ShapeDtypeStruct((t, d), F32),
        compiler_params=_params("arbitrary"),
        name="moe_combine_ln",
    )(dest, y, x, route, g.reshape(1, d), b.reshape(1, d))


def _moe_ffn_ln(x, xp, route, counts, w_gate, w_up, w_down, layer, ln_g, ln_b):
    t, d = x.shape
    bm = MOE_BLOCK
    eid = route[:, 0:TOP_K].astype(jnp.int32)
    rank = route[:, 4:4 + TOP_K].astype(jnp.int32)
    cnt = counts[0, N_GROUPS:N_GROUPS + N_EXPERTS].astype(jnp.int32)
    nblk = (cnt + bm - 1) // bm
    bend = jnp.cumsum(nblk)
    bstart = bend - nblk
    pstart = bstart * bm
    onehot = eid[:, :, None] == jnp.arange(N_EXPERTS, dtype=jnp.int32)
    dest = (jnp.sum(jnp.where(onehot, pstart, 0), axis=-1) + rank).reshape(-1)
    n_valid = bend[-1:]
    n_slots = (-(-t * TOP_K // bm) + N_EXPERTS) * bm
    xs = _dispatch(xp, t, dest, cnt, pstart, n_valid, n_slots)
    y = _experts(xs, nblk, bstart, n_valid, w_gate, w_up, w_down, layer)
    return _combine_ln(y, dest, x, route, ln_g, ln_b)


def _gla_gate_kernel(x_ref, wl_ref, wu_ref, b_ref, la_ref):
    g_low = _dot_hi_lo(x_ref[...], wl_ref)
    n = wu_ref.shape[1] // 2
    gh, gl = _split2(g_low)
    a = jnp.dot(gh, wu_ref[...], preferred_element_type=F32)
    c = jnp.dot(gl, wu_ref[:, :n], preferred_element_type=F32)
    logit = a[:, :n] + a[:, n:] + c + b_ref[...]
    la_ref[...] = _log_sigmoid(logit) / GLA_GATE_TAU


def _gla_gates(x, w_low_t, w_gate_up, b_gate):
    s, d = x.shape
    rank, dk = w_gate_up.shape
    ts = min(ROW_TILE, s)
    wl = _pack_hi_lo(w_low_t)
    wu_pad = jnp.zeros((LANES, dk), F32).at[:rank].set(w_gate_up)
    wu = jnp.concatenate(_split2(wu_pad), axis=1)
    return pl.pallas_call(
        _gla_gate_kernel,
        grid=(s // ts,),
        in_specs=[pl.BlockSpec((ts, d), lambda i: (i, 0)),
                  pl.BlockSpec((2 * LANES, d), lambda i: (0, 0)),
                  pl.BlockSpec((LANES, 2 * dk), lambda i: (0, 0)),
                  pl.BlockSpec((1, dk), lambda i: (0, 0))],
        out_specs=pl.BlockSpec((ts, dk), lambda i: (i, 0)),
        out_shape=jax.ShapeDtypeStruct((s, dk), F32),
        compiler_params=_params("parallel"),
        name="gla_gates",
    )(x, wl, wu, b_gate.reshape(1, dk))


def _gla_kernel(q_ref, k_ref, v_ref, la_ref, r_ref, g_ref, o_ref, state_ref):
    @pl.when(pl.program_id(0) == 0)
    def _():
        state_ref[...] = jnp.zeros_like(state_ref)

    rows = q_ref.shape[0]
    nh, dk, dv = state_ref.shape
    cs = GLA_CHUNK
    n_chunks = rows // cs
    rr = lax.broadcasted_iota(jnp.int32, (rows, rows), 0)
    cc = lax.broadcasted_iota(jnp.int32, (rows, rows), 1)
    causal = (rr // cs == cc // cs) & (rr >= cc)
    tri = jnp.where(causal, 1.0, 0.0).astype(BF16)
    lane_chunk = lax.broadcasted_iota(jnp.int32, (dk, rows), 1) // cs

    def chunk_row(a, i):
        return jnp.concatenate([jnp.broadcast_to(a[c * cs + i:c * cs + i + 1, :], (cs, a.shape[1]))
                                for c in range(n_chunks)], axis=0)

    for h in range(nh):
        kc = pl.ds(h * dk, dk)
        vc = pl.ds(h * dv, dv)
        b = _cumsum_rows(la_ref[:, kc], tri)
        b_mid = chunk_row(b, cs // 2 - 1)
        b_last = chunk_row(b, cs - 1)
        q = q_ref[:, kc].astype(F32) * (dk ** -0.5)
        k = k_ref[:, kc].astype(F32)
        v = v_ref[:, vc]
        qa = (q * jnp.exp(b - b_mid)).astype(BF16)
        ka = (k * jnp.exp(b_mid - b)).astype(BF16)
        a = jnp.where(causal, _dot_nt(qa, ka), 0.0)
        o_intra = jnp.dot(a.astype(BF16), v, preferred_element_type=F32)
        q_inter = (q * jnp.exp(b)).astype(BF16)
        k_end_t = (k * jnp.exp(b_last - b)).T
        b_t = b.T
        state = state_ref[h]
        outs = []
        for c in range(n_chunks):
            sl = slice(c * cs, (c + 1) * cs)
            outs.append(o_intra[sl] + jnp.dot(q_inter[sl], state.astype(BF16), preferred_element_type=F32))
            kv = jnp.dot(jnp.where(lane_chunk == c, k_end_t, 0.0).astype(BF16), v,
                         preferred_element_type=F32)
            decay = jnp.exp(b_t[:, (c + 1) * cs - 1:(c + 1) * cs])
            state = decay * state + kv
        state_ref[h] = state
        o = jnp.concatenate(outs, axis=0)
        o = o * lax.rsqrt(jnp.mean(o * o, axis=-1, keepdims=True) + RMS_EPS) * g_ref[...]
        r = r_ref[:, vc].astype(F32)
        o_ref[:, vc] = (o * (r * jax.nn.sigmoid(r))).astype(o_ref.dtype)


def _gla(proj, la, norm_g):
    s = proj.shape[0]
    dk_all = la.shape[1]
    nh = GLA_HEADS
    dk = dk_all // nh
    dv = norm_g.shape[0]
    dv_all = nh * dv
    assert 2 * dk_all == dv_all
    rows = min(GLA_ROWS, s)
    return pl.pallas_call(
        _gla_kernel,
        grid=(s // rows,),
        in_specs=[pl.BlockSpec((rows, dk_all), lambda i: (i, 0)),
                  pl.BlockSpec((rows, dk_all), lambda i: (i, 1)),
                  pl.BlockSpec((rows, dv_all), lambda i: (i, 1)),
                  pl.BlockSpec((rows, dk_all), lambda i: (i, 0)),
                  pl.BlockSpec((rows, dv_all), lambda i: (i, 2)),
                  pl.BlockSpec((1, dv), lambda i: (0, 0))],
        out_specs=pl.BlockSpec((rows, dv_all), lambda i: (i, 0)),
        out_shape=jax.ShapeDtypeStruct((s, dv_all), BF16),
        scratch_shapes=[pltpu.VMEM((nh, dk, dv), F32)],
        compiler_params=_params("arbitrary"),
        name="gla_chunks",
    )(proj, proj, proj, la, proj, norm_g.reshape(1, dv))


def kernel(x, fox_w_in, fox_b_f, fox_w_o, gla_w_in, gla_w_gate_up, gla_b_gate, gla_norm_g, gla_w_o,
           ln_mix_g, ln_mix_b, ln_ffn_g, ln_ffn_b, moe_w_group, moe_b_group, moe_w_expert,
           moe_b_expert, moe_w_gate, moe_w_up, moe_w_down):
    bsz, s, d = x.shape
    outs = []
    for bi in range(bsz):
        xt = x[bi]
        for i in range(DEPTH):
            j = i // 2
            if i % 2 == 0:
                w_in_t = fox_w_in[j].T
                qkv = _matmul_nt(xt, w_in_t, 3 * d, BF16)
                c = _fox_gates(xt, w_in_t[3 * d:], fox_b_f[j])
                o = _fox_attention(qkv, c, FOX_HEADS)
                w_o = fox_w_o[j]
            else:
                w_in_t = gla_w_in[j].T
                n_main = w_in_t.shape[0] - gla_w_gate_up.shape[1]
                proj = _matmul_nt(xt, w_in_t, n_main, BF16)
                la = _gla_gates(xt, w_in_t[n_main:], gla_w_gate_up[j], gla_b_gate[j])
                o = _gla(proj, la, gla_norm_g[j])
                w_o = gla_w_o[j]
            xt, xp, route, counts = _proj_ln_route(o, w_o.astype(BF16), xt, ln_mix_g[i], ln_mix_b[i], moe_w_group[i],
                                                   moe_b_group[i], moe_w_expert[i], moe_b_expert[i])
            xt = _moe_ffn_ln(xt, xp, route, counts, moe_w_gate, moe_w_up, moe_w_down, i, ln_ffn_g[i], ln_ffn_b[i])
        outs.append(xt)
    return outs[0].reshape(1, s, d) if bsz == 1 else jnp.stack(outs, axis=0)
```

```python
import functools

import jax
import jax.numpy as jnp
from jax import lax
from jax.experimental import pallas as pl
from jax.experimental.pallas import tpu as pltpu

F32 = jnp.float32
BF16 = jnp.bfloat16

DEPTH = 2
FOX_HEADS = 16
FOX_HEAD_DIM = 128
GLA_HEADS = 4
GLA_CHUNK = 64
GLA_GATE_TAU = 16.0
N_GROUPS = 8
EXPERTS_PER_GROUP = 8
N_EXPERTS = N_GROUPS * EXPERTS_PER_GROUP
TOP_K = 2
DEEPNORM_ALPHA = (2 * DEPTH) ** 0.25
LN_EPS = 1e-5
RMS_EPS = 1e-6

LANES = 128
VMEM_LIMIT = 56 * 2**20

MM_TM, MM_TN = 1024, 1024
ROW_TILE = 512
ATT_TQ, ATT_TK = 1024, 512
GLA_ROWS = 256
MOE_BLOCK = 128
MOE_W_SLOTS = 3
COMB_TM = 256


def _params(*sem):
    return pltpu.CompilerParams(dimension_semantics=sem, vmem_limit_bytes=VMEM_LIMIT)


def _split2(a):
    hi = a.astype(BF16)
    lo = (a - hi.astype(F32)).astype(BF16)
    return hi, lo


def _split3(a):
    hi = a.astype(BF16)
    r = a - hi.astype(F32)
    mid = r.astype(BF16)
    lo = (r - mid.astype(F32)).astype(BF16)
    return hi, mid, lo


def _pack_hi_lo(w_t, n_pad=LANES):
    n, k = w_t.shape
    wp = jnp.zeros((n_pad, k), F32).at[:n].set(w_t)
    hi, lo = _split2(wp)
    return jnp.concatenate([hi, lo], axis=0)


def _dot_nt(a, b):
    return lax.dot_general(a, b, (((1,), (1,)), ((), ())), preferred_element_type=F32)


def _dot_hi_lo(x, whl_ref, n_pad=LANES):
    xh, xl = _split2(x)
    a = _dot_nt(xh, whl_ref[...])
    b = _dot_nt(xl, whl_ref[:n_pad, :])
    return a[:, :n_pad] + a[:, n_pad:] + b


def _pack_rows(a):
    half = a.shape[1] // 2
    hi = lax.bitcast_convert_type(a[:, :half].astype(BF16).astype(F32), jnp.uint32)
    lo = lax.bitcast_convert_type(a[:, half:].astype(BF16).astype(F32), jnp.uint32)
    return hi | (lo >> 16)


def _unpack_rows(w):
    hi = lax.bitcast_convert_type(w & jnp.uint32(0xFFFF0000), F32)
    lo = lax.bitcast_convert_type(w << 16, F32)
    return jnp.concatenate([hi, lo], axis=1)


def _store_rows_as_lines(ref, first_line, packed):
    n_rows, width = packed.shape
    nt = width // LANES
    for j in range(nt):
        ref[pl.ds(first_line + j, n_rows, stride=nt), :] = packed[:, j * LANES:(j + 1) * LANES]


def _load_rows_from_lines(ref, first_line, n_rows, nt):
    return jnp.concatenate([ref[pl.ds(first_line + j, n_rows, stride=nt), :] for j in range(nt)], axis=1)


def _log_sigmoid(x):
    return -(jnp.maximum(-x, 0.0) + jnp.log1p(jnp.exp(-jnp.abs(x))))


def _cumsum_rows(a, incl_tri):
    n = a.shape[1]
    parts = jnp.concatenate(_split3(a), axis=1)
    c = jnp.dot(incl_tri, parts, preferred_element_type=F32)
    return c[:, :n] + c[:, n:2 * n] + c[:, 2 * n:]


def _tri(n, strict=False):
    r = lax.broadcasted_iota(jnp.int32, (n, n), 0)
    c = lax.broadcasted_iota(jnp.int32, (n, n), 1)
    return jnp.where((r > c) if strict else (r >= c), 1.0, 0.0).astype(BF16)


def _layer_norm_rows(z, g, b):
    mu = jnp.mean(z, axis=-1, keepdims=True)
    d = z - mu
    var = jnp.mean(d * d, axis=-1, keepdims=True)
    return d * lax.rsqrt(var + LN_EPS) * g + b


def _mm_kernel(x_ref, w_ref, o_ref, wb_ref):
    @pl.when(pl.program_id(1) == 0)
    def _():
        wb_ref[...] = w_ref[...].astype(BF16)

    o_ref[...] = lax.dot_general(x_ref[...].astype(BF16), wb_ref[...], (((1,), (1,)), ((), ())),
                                 preferred_element_type=F32).astype(o_ref.dtype)


def _matmul_nt(x, w_t, n, out_dtype):
    m, k = x.shape
    tm, tn = min(MM_TM, m), min(MM_TN, n)
    return pl.pallas_call(
        _mm_kernel,
        grid=(n // tn, m // tm),
        in_specs=[pl.BlockSpec((tm, k), lambda j, i: (i, 0)),
                  pl.BlockSpec((tn, k), lambda j, i: (j, 0))],
        out_specs=pl.BlockSpec((tm, tn), lambda j, i: (i, j)),
        out_shape=jax.ShapeDtypeStruct((m, n), out_dtype),
        scratch_shapes=[pltpu.VMEM((tn, k), BF16)],
        compiler_params=_params("arbitrary", "arbitrary"),
        name="dense_proj",
    )(x, w_t)


def _fox_gate_kernel(x_ref, w_ref, b_ref, c_ref, carry_ref):
    @pl.when(pl.program_id(0) == 0)
    def _():
        carry_ref[...] = jnp.zeros_like(carry_ref)

    ts = x_ref.shape[0]
    logits = _dot_hi_lo(x_ref[...], w_ref) + b_ref[...]
    log_f = _log_sigmoid(logits)
    c = _cumsum_rows(log_f, _tri(ts)) + carry_ref[...]
    carry_ref[...] = c[ts - 1:ts, :]
    c_ref[...] = c


def _fox_gates(x, w_f_t, b_f):
    s, d = x.shape
    h = w_f_t.shape[0]
    ts = min(ROW_TILE, s)
    whl = _pack_hi_lo(w_f_t)
    bias = jnp.zeros((1, LANES), F32).at[0, :h].set(b_f)
    return pl.pallas_call(
        _fox_gate_kernel,
        grid=(s // ts,),
        in_specs=[pl.BlockSpec((ts, d), lambda i: (i, 0)),
                  pl.BlockSpec((2 * LANES, d), lambda i: (0, 0)),
                  pl.BlockSpec((1, LANES), lambda i: (0, 0))],
        out_specs=pl.BlockSpec((ts, LANES), lambda i: (i, 0)),
        out_shape=jax.ShapeDtypeStruct((s, LANES), F32),
        scratch_shapes=[pltpu.VMEM((1, LANES), F32)],
        compiler_params=_params("arbitrary"),
        name="fox_gates",
    )(x, whl, bias)


def _fox_attn_kernel(q_ref, k_ref, v_ref, c_ref, o_ref, kaug_ref, vt_ref, qt_ref, st0_ref, st1_ref,
                     p0_ref, p1_ref, alpha_ref, bmax_ref, acc_ref, m_ref):
    h = pl.program_id(0)
    qi = pl.program_id(1)
    tq, dh = q_ref.shape
    s_len = k_ref.shape[0]
    tk = ATT_TK
    log2e = 1.4426950408889634
    scale = dh ** -0.5 * log2e

    def head_column(rows, n):
        lane = lax.broadcasted_iota(jnp.int32, (n, LANES), 1)
        return log2e * jnp.sum(jnp.where(lane == h, c_ref[rows, :], 0.0), axis=1, keepdims=True)

    def bias_columns(col, first, n):
        lane = lax.broadcasted_iota(jnp.int32, (n, LANES), 1)
        hi, mid, lo = _split3(col)
        ones_first = 3 - first
        out = jnp.where((lane >= ones_first) & (lane < ones_first + 3), 1.0, 0.0)
        out = jnp.where(lane == first, hi.astype(F32), out)
        out = jnp.where(lane == first + 1, mid.astype(F32), out)
        return jnp.where(lane == first + 2, lo.astype(F32), out).astype(BF16)

    @pl.when(qi == 0)
    def _():
        ones_row = jnp.where(lax.broadcasted_iota(jnp.int32, (16, tk), 0) == 0, 1.0, 0.0).astype(BF16)

        def build(ci, carry):
            rows = pl.ds(pl.multiple_of(ci * tk, tk), tk)
            kaug_ref[rows, :dh] = k_ref[rows, :]
            kaug_ref[rows, dh:] = bias_columns(-head_column(rows, tk), 0, tk)
            vt_ref[:dh, rows] = v_ref[rows, :].astype(F32).T.astype(BF16)
            vt_ref[dh:, rows] = ones_row
            return carry
        lax.fori_loop(0, s_len // tk, build, 0)

    q_rows = pl.ds(pl.multiple_of(qi * tq, tq), tq)
    q_aug = jnp.concatenate([q_ref[...].astype(F32) * scale,
                             bias_columns(head_column(q_rows, tq), 3, tq).astype(F32)], axis=1)
    qt_ref[...] = q_aug.T.astype(BF16)

    acc_ref[...] = jnp.zeros_like(acc_ref)
    m_ref[...] = jnp.full_like(m_ref, -jnp.inf)

    st_refs, p_refs = (st0_ref, st1_ref), (p0_ref, p1_ref)

    def scores(k_start, slot, diag_offset=None):
        st = jnp.dot(kaug_ref[pl.ds(k_start, tk), :], qt_ref[...], preferred_element_type=F32)
        if diag_offset is not None:
            kr = lax.broadcasted_iota(jnp.int32, (tk, tq), 0) + diag_offset
            qc = lax.broadcasted_iota(jnp.int32, (tk, tq), 1)
            st = jnp.where(kr <= qc, st, -jnp.inf)
        st_refs[slot][...] = st
        bmax_ref[slot] = jnp.max(st, axis=0, keepdims=True)

    def softmax(slot):
        m_prev = m_ref[...]
        m_new = jnp.maximum(m_prev, bmax_ref[slot])
        m_ref[...] = m_new
        p_refs[slot][...] = jnp.exp2(st_refs[slot][...] - m_new).astype(BF16)
        alpha_ref[slot] = jnp.exp2(m_prev - m_new)

    def values(k_start, slot):
        acc_ref[...] = alpha_ref[slot] * acc_ref[...] + jnp.dot(
            vt_ref[:, pl.ds(k_start, tk)], p_refs[slot][...], preferred_element_type=F32)

    n_diag = tq // tk
    assert n_diag == 2
    d0 = pl.multiple_of(qi * tq, tk)
    d1 = pl.multiple_of(qi * tq + tk, tk)
    n_full = qi * n_diag

    def full_start(i):
        return pl.multiple_of(jnp.minimum(i, jnp.maximum(n_full - 1, 0)) * tk, tk)

    scores(d0, 0, 0)
    scores(d1, 1, tk)
    softmax(0)
    scores(full_start(0), 0)
    softmax(1)
    values(d0, 0)

    def pair(j, carry):
        i0 = 2 * j
        prev = jnp.where(j == 0, d1, (i0 - 1) * tk)
        scores(full_start(i0 + 1), 1)
        softmax(0)
        values(pl.multiple_of(prev, tk), 1)
        scores(full_start(i0 + 2), 0)
        softmax(1)
        values(pl.multiple_of(i0 * tk, tk), 0)
        return carry

    lax.fori_loop(0, qi, pair, 0)
    last = jnp.where(qi == 0, d1, (n_full - 1) * tk)
    values(pl.multiple_of(last, tk), 1)

    o_ref[...] = (acc_ref[:dh, :] / acc_ref[dh:dh + 1, :]).T.astype(o_ref.dtype)


def _fox_attention(qkv, c, n_heads):
    s = qkv.shape[0]
    dh = FOX_HEAD_DIM
    tq = min(ATT_TQ, s)
    hh = n_heads
    return pl.pallas_call(
        _fox_attn_kernel,
        grid=(hh, s // tq),
        in_specs=[pl.BlockSpec((tq, dh), lambda h, i: (i, h)),
                  pl.BlockSpec((s, dh), lambda h, i: (0, hh + h)),
                  pl.BlockSpec((s, dh), lambda h, i: (0, 2 * hh + h)),
                  pl.BlockSpec((s, LANES), lambda h, i: (0, 0))],
        out_specs=pl.BlockSpec((tq, dh), lambda h, i: (i, h)),
        out_shape=jax.ShapeDtypeStruct((s, hh * dh), BF16),
        scratch_shapes=[pltpu.VMEM((s, 2 * dh), BF16),
                        pltpu.VMEM((dh + 16, s), BF16),
                        pltpu.VMEM((2 * dh, tq), BF16),
                        pltpu.VMEM((ATT_TK, tq), F32),
                        pltpu.VMEM((ATT_TK, tq), F32),
                        pltpu.VMEM((ATT_TK, tq), BF16),
                        pltpu.VMEM((ATT_TK, tq), BF16),
                        pltpu.VMEM((2, 1, tq), F32),
                        pltpu.VMEM((2, 1, tq), F32),
                        pltpu.VMEM((dh + 16, tq), F32),
                        pltpu.VMEM((1, tq), F32)],
        compiler_params=_params("arbitrary", "arbitrary"),
        name="fox_attention",
    )(qkv, qkv, qkv, c)


def _first_lane_eq(vals, target, lane):
    return jnp.min(jnp.where(vals == target, lane, LANES), axis=1, keepdims=True)


def _route_rows(x, w_ref, b_ref, carry_ref):
    tm = x.shape[0]
    neg = -jnp.inf
    logits = _dot_hi_lo(x, w_ref) + b_ref[...]
    lane = lax.broadcasted_iota(jnp.int32, (tm, LANES), 1)

    gl = jnp.where(lane < N_GROUPS, logits, neg)
    gmax = jnp.max(gl, axis=1, keepdims=True)
    gsum = jnp.sum(jnp.exp(gl - gmax), axis=1, keepdims=True)
    grp_p = 1.0 / gsum
    grp = _first_lane_eq(gl, gmax, lane)

    lo = N_GROUPS + grp * EXPERTS_PER_GROUP
    el = jnp.where((lane >= lo) & (lane < lo + EXPERTS_PER_GROUP), logits, neg)
    emax = jnp.max(el, axis=1, keepdims=True)
    esum = jnp.sum(jnp.exp(el - emax), axis=1, keepdims=True)
    idx1 = _first_lane_eq(el, emax, lane)
    el2 = jnp.where(lane == idx1, neg, el)
    emax2 = jnp.max(el2, axis=1, keepdims=True)
    idx2 = _first_lane_eq(el2, emax2, lane)
    p1 = 1.0 / esum
    p2 = jnp.exp(emax2 - emax) / esum
    psum = p1 + p2
    g1 = grp_p * (p1 / psum)
    g2 = grp_p * (p2 / psum)

    oh1 = lane == idx1
    oh2 = lane == idx2
    both = jnp.where(oh1 | oh2, 1.0, 0.0)
    before = jnp.dot(_tri(tm, strict=True), both.astype(BF16), preferred_element_type=F32)
    before = before + carry_ref[...]
    r1 = jnp.sum(jnp.where(oh1, before, 0.0), axis=1, keepdims=True)
    r2 = jnp.sum(jnp.where(oh2, before, 0.0), axis=1, keepdims=True)
    carry_ref[...] = carry_ref[...] + jnp.sum(both, axis=0, keepdims=True)

    e1 = (idx1 - N_GROUPS).astype(F32)
    e2 = (idx2 - N_GROUPS).astype(F32)
    out = jnp.where(lane == 0, e1, 0.0)
    out = jnp.where(lane == 1, e2, out)
    out = jnp.where(lane == 2, g1, out)
    out = jnp.where(lane == 3, g2, out)
    out = jnp.where(lane == 4, r1, out)
    return jnp.where(lane == 5, r2, out)


def _proj_ln_route_kernel(o_ref, w_ref, x_ref, g_ref, b_ref, wr_ref, br_ref,
                          y_ref, yp_ref, route_ref, cnt_ref, carry_ref):
    @pl.when(pl.program_id(0) == 0)
    def _():
        carry_ref[...] = jnp.zeros_like(carry_ref)

    mix = jnp.dot(o_ref[...], w_ref[...], preferred_element_type=F32)
    y = _layer_norm_rows(DEEPNORM_ALPHA * x_ref[...] + mix, g_ref[...], b_ref[...])
    y_ref[...] = y
    _store_rows_as_lines(yp_ref, 0, _pack_rows(y))
    route_ref[...] = _route_rows(y, wr_ref, br_ref, carry_ref)
    cnt_ref[...] = carry_ref[...]


def _proj_ln_route(o, w, x, g, b, w_group, b_group, w_expert, b_expert):
    s, d = x.shape
    kd = o.shape[1]
    tm = min(ROW_TILE, s)
    whl = _pack_hi_lo(jnp.concatenate([w_group.T, w_expert.T], axis=0))
    nb = N_GROUPS + N_EXPERTS
    bias = jnp.zeros((1, LANES), F32).at[0, :nb].set(jnp.concatenate([b_group, b_expert]))
    return pl.pallas_call(
        _proj_ln_route_kernel,
        grid=(s // tm,),
        in_specs=[pl.BlockSpec((tm, kd), lambda i: (i, 0)),
                  pl.BlockSpec((kd, d), lambda i: (0, 0)),
                  pl.BlockSpec((tm, d), lambda i: (i, 0)),
                  pl.BlockSpec((1, d), lambda i: (0, 0)),
                  pl.BlockSpec((1, d), lambda i: (0, 0)),
                  pl.BlockSpec((2 * LANES, d), lambda i: (0, 0)),
                  pl.BlockSpec((1, LANES), lambda i: (0, 0))],
        out_specs=[pl.BlockSpec((tm, d), lambda i: (i, 0)),
                   pl.BlockSpec((tm * (d // 2 // LANES), LANES), lambda i: (i, 0)),
                   pl.BlockSpec((tm, LANES), lambda i: (i, 0)),
                   pl.BlockSpec((1, LANES), lambda i: (0, 0))],
        out_shape=[jax.ShapeDtypeStruct((s, d), F32),
                   jax.ShapeDtypeStruct((s * (d // 2 // LANES), LANES), jnp.uint32),
                   jax.ShapeDtypeStruct((s, LANES), F32),
                   jax.ShapeDtypeStruct((1, LANES), F32)],
        scratch_shapes=[pltpu.VMEM((1, LANES), F32)],
        compiler_params=_params("arbitrary"),
        name="proj_ln_route",
    )(o, w, x, g.reshape(1, d), b.reshape(1, d), whl, bias)


def _dispatch_kernel(dest_ref, cnt_ref, pstart_ref, nv_ref, x_ref, xs_hbm, zero_ref, sem, zsem, *, nt):
    i = pl.program_id(0)
    tm = x_ref.shape[0] // nt
    bm = MOE_BLOCK
    n_exp = cnt_ref.shape[0]
    n_blocks = xs_hbm.shape[0] // (bm * nt)

    def lines(row, n_rows=1):
        return pl.ds(pl.multiple_of(row * nt, nt), n_rows * nt)

    def pad_copies(e, fn):
        cnt = cnt_ref[e]
        pad = (bm - cnt % bm) % bm
        pos = pstart_ref[e] + cnt
        size = bm // 2
        while size >= 1:
            @pl.when((pad & size) != 0)
            def _(pos=pos, size=size):
                fn(pltpu.make_async_copy(zero_ref.at[lines(0, size), :], xs_hbm.at[lines(pos, size), :], zsem))
            pos = pos + (pad & size)
            size //= 2

    def tail_copy(g):
        return pltpu.make_async_copy(zero_ref, xs_hbm.at[lines(g * bm, bm), :], zsem)

    def for_all_fills(fn):
        def per_expert(e, carry):
            pad_copies(e, fn)
            return carry
        lax.fori_loop(0, n_exp, per_expert, 0)

        def per_tail(g, carry):
            fn(tail_copy(g))
            return carry
        lax.fori_loop(nv_ref[0], n_blocks, per_tail, 0)

    @pl.when(i == 0)
    def _():
        zero_ref[...] = jnp.zeros_like(zero_ref)
        for_all_fills(lambda c: c.start())

    base = i * (TOP_K * tm)

    def body(r, carry):
        for k in range(TOP_K):
            dst = dest_ref[base + TOP_K * r + k]
            pltpu.make_async_copy(x_ref.at[lines(r), :], xs_hbm.at[lines(dst), :], sem).start(priority=k)
        return carry
    lax.fori_loop(0, tm, body, 0, unroll=8)

    @pl.when(i == 0)
    def _():
        for_all_fills(lambda c: c.wait())

    for _ in range(TOP_K):
        pltpu.make_async_copy(x_ref, xs_hbm.at[lines(0, tm), :], sem).wait()


def _dispatch(xp, n_rows, dest, cnt, pstart, n_valid, n_slots):
    nt = xp.shape[0] // n_rows
    tm = min(ROW_TILE, n_rows)
    grid_spec = pltpu.PrefetchScalarGridSpec(
        num_scalar_prefetch=4,
        grid=(n_rows // tm,),
        in_specs=[pl.BlockSpec((tm * nt, LANES), lambda i, *_: (i, 0))],
        out_specs=pl.BlockSpec(memory_space=pl.ANY),
        scratch_shapes=[pltpu.VMEM((MOE_BLOCK * nt, LANES), xp.dtype),
                        pltpu.SemaphoreType.DMA,
                        pltpu.SemaphoreType.DMA],
    )
    return pl.pallas_call(
        functools.partial(_dispatch_kernel, nt=nt),
        grid_spec=grid_spec,
        out_shape=jax.ShapeDtypeStruct((n_slots * nt, LANES), xp.dtype),
        compiler_params=_params("arbitrary"),
        name="moe_dispatch",
    )(dest, cnt, pstart, n_valid, xp)


def _expert_kernel(nblk_ref, bstart_ref, nv_ref, xs_hbm, wg_hbm, wu_hbm, wd_hbm, y_hbm,
                   xin_ref, yout_ref, wgf_ref, wuf_ref, wdf_ref, wgb_ref, wub_ref, wdb_ref,
                   in_sem, out_sem, w_sem, *, layer):
    e = pl.program_id(0)
    n_exp = pl.num_programs(0)
    n_valid = nv_ref[0]
    bm = MOE_BLOCK
    blk_lines = xin_ref.shape[1]
    nt = blk_lines // bm
    n_blocks = y_hbm.shape[0] // blk_lines
    ns = MOE_W_SLOTS

    half = wd_hbm.shape[2] // 2

    def w_copies(ex, slot):
        return (pltpu.make_async_copy(wg_hbm.at[layer, ex], wgf_ref.at[slot], w_sem.at[slot]),
                pltpu.make_async_copy(wu_hbm.at[layer, ex], wuf_ref.at[slot], w_sem.at[slot]),
                pltpu.make_async_copy(wd_hbm.at[layer, ex, pl.ds(0, half)],
                                      wdf_ref.at[slot, pl.ds(0, half)], w_sem.at[slot]),
                pltpu.make_async_copy(wd_hbm.at[layer, ex, pl.ds(half, half)],
                                      wdf_ref.at[slot, pl.ds(half, half)], w_sem.at[slot]))

    def start_weights(ex):
        exc = jnp.minimum(ex, n_exp - 1)

        @pl.when((ex < n_exp) & (nblk_ref[exc] > 0))
        def _():
            for piece, c in enumerate(w_copies(exc, exc % ns)):
                c.start(priority=piece % 2)

    @pl.when(e == 0)
    def _():
        for ahead in range(ns - 1):
            start_weights(ahead)

    start_weights(e + ns - 1)

    def block_lines(g):
        return pl.ds(pl.multiple_of(g * blk_lines, blk_lines), blk_lines)

    def in_copy(g, slot):
        return pltpu.make_async_copy(xs_hbm.at[block_lines(g), :], xin_ref.at[slot], in_sem.at[slot])

    def out_copy(g, slot):
        return pltpu.make_async_copy(yout_ref.at[slot], y_hbm.at[block_lines(g), :], out_sem.at[slot])

    @pl.when((e == 0) & (n_valid > 0))
    def _():
        in_copy(0, 0).start(priority=0)

    nb = nblk_ref[e]
    g0 = bstart_ref[e]

    @pl.when(nb > 0)
    def _():
        w_slot = e % ns
        for c in w_copies(e, w_slot):
            c.wait()
        wgb_ref[...] = wgf_ref[w_slot].astype(BF16)
        wub_ref[...] = wuf_ref[w_slot].astype(BF16)
        wdb_ref[...] = wdf_ref[w_slot].astype(BF16)

        def block(j, carry):
            g = g0 + j
            slot = g % 2
            in_copy(g, slot).wait()

            @pl.when(g + 1 < n_valid)
            def _():
                in_copy(g + 1, 1 - slot).start(priority=0)

            @pl.when(g >= 2)
            def _():
                out_copy(g - 2, slot).wait()

            xb = _unpack_rows(_load_rows_from_lines(xin_ref.at[slot], 0, bm, nt)).astype(BF16)
            gate = jnp.dot(xb, wgb_ref[...], preferred_element_type=F32)
            up = jnp.dot(xb, wub_ref[...], preferred_element_type=F32)
            hid = (gate * jax.nn.sigmoid(gate) * up).astype(BF16)
            y = jnp.dot(hid, wdb_ref[...], preferred_element_type=F32)
            _store_rows_as_lines(yout_ref.at[slot], 0, _pack_rows(y))
            out_copy(g, slot).start(priority=1)
            return carry
        lax.fori_loop(0, nb, block, 0)

    @pl.when(e == pl.num_programs(0) - 1)
    def _():
        for back in (2, 1):
            @pl.when(n_valid >= back)
            def _(back=back):
                out_copy(n_valid - back, (n_valid - back) % 2).wait()

        yout_ref[0] = jnp.zeros(yout_ref.shape[1:], yout_ref.dtype)

        def fill(g, carry):
            out_copy(g, 0).start()
            return carry
        lax.fori_loop(n_valid, n_blocks, fill, 0)

        def drain(g, carry):
            out_copy(g, 0).wait()
            return carry
        lax.fori_loop(n_valid, n_blocks, drain, 0)


def _experts(xs, nblk, bstart, n_valid, w_gate, w_up, w_down, layer):
    _, n_exp, d, f = w_gate.shape
    bm = MOE_BLOCK
    nt = d // 2 // LANES
    ns = MOE_W_SLOTS
    grid_spec = pltpu.PrefetchScalarGridSpec(
        num_scalar_prefetch=3,
        grid=(n_exp,),
        in_specs=[pl.BlockSpec(memory_space=pl.ANY)] * 4,
        out_specs=pl.BlockSpec(memory_space=pl.ANY),
        scratch_shapes=[pltpu.VMEM((2, bm * nt, LANES), jnp.uint32),
                        pltpu.VMEM((2, bm * nt, LANES), jnp.uint32),
                        pltpu.VMEM((ns, d, f), F32),
                        pltpu.VMEM((ns, d, f), F32),
                        pltpu.VMEM((ns, f, d), F32),
                        pltpu.VMEM((d, f), BF16),
                        pltpu.VMEM((d, f), BF16),
                        pltpu.VMEM((f, d), BF16),
                        pltpu.SemaphoreType.DMA((2,)),
                        pltpu.SemaphoreType.DMA((2,)),
                        pltpu.SemaphoreType.DMA((ns,))],
    )
    return pl.pallas_call(
        functools.partial(_expert_kernel, layer=layer),
        grid_spec=grid_spec,
        out_shape=jax.ShapeDtypeStruct(xs.shape, jnp.uint32),
        compiler_params=_params("arbitrary"),
        name="moe_experts",
    )(nblk, bstart, n_valid, xs, w_gate, w_up, w_down)


def _combine_ln_kernel(dest_ref, y_hbm, x_ref, route_ref, g_ref, b_ref, o_ref, yg_ref, sem):
    i = pl.program_id(0)
    n = pl.num_programs(0)
    tm = x_ref.shape[0]
    nt = yg_ref.shape[1] // (TOP_K * tm)

    def lines(row, n_rows=1):
        return pl.ds(pl.multiple_of(row * nt, nt), n_rows * nt)

    def start_gather(step, slot):
        base = step * (TOP_K * tm)
        for r in range(tm):
            for k in range(TOP_K):
                src = dest_ref[base + TOP_K * r + k]
                pltpu.make_async_copy(y_hbm.at[lines(src), :], yg_ref.at[slot, lines(k * tm + r), :],
                                      sem.at[slot]).start(priority=k)

    def wait_gather(slot):
        pltpu.make_async_copy(y_hbm.at[lines(0, TOP_K * tm), :], yg_ref.at[slot], sem.at[slot]).wait()

    @pl.when(i == 0)
    def _():
        start_gather(0, 0)

    slot = i % 2
    wait_gather(slot)
    start_gather(jnp.minimum(i + 1, n - 1), 1 - slot)
    route = route_ref[...]
    lane = lax.broadcasted_iota(jnp.int32, route.shape, 1)
    g1 = jnp.sum(jnp.where(lane == 2, route, 0.0), axis=1, keepdims=True)
    g2 = jnp.sum(jnp.where(lane == 3, route, 0.0), axis=1, keepdims=True)
    y1 = _unpack_rows(_load_rows_from_lines(yg_ref.at[slot], 0, tm, nt))
    y2 = _unpack_rows(_load_rows_from_lines(yg_ref.at[slot], tm * nt, tm, nt))
    z = DEEPNORM_ALPHA * x_ref[...] + (y1 * g1 + y2 * g2)
    o_ref[...] = _layer_norm_rows(z, g_ref[...], b_ref[...])

    @pl.when(i == n - 1)
    def _():
        wait_gather(1 - slot)


def _combine_ln(y, dest, x, route, g, b):
    t, d = x.shape
    tm = min(COMB_TM, t)
    grid_spec = pltpu.PrefetchScalarGridSpec(
        num_scalar_prefetch=1,
        grid=(t // tm,),
        in_specs=[pl.BlockSpec(memory_space=pl.ANY),
                  pl.BlockSpec((tm, d), lambda i, dest: (i, 0)),
                  pl.BlockSpec((tm, LANES), lambda i, dest: (i, 0)),
                  pl.BlockSpec((1, d), lambda i, dest: (0, 0)),
                  pl.BlockSpec((1, d), lambda i, dest: (0, 0))],
        out_specs=pl.BlockSpec((tm, d), lambda i, dest: (i, 0)),
        scratch_shapes=[pltpu.VMEM((2, TOP_K * tm * (d // 2 // LANES), LANES), y.dtype),
                        pltpu.SemaphoreType.DMA((2,))],
    )
    return pl.pallas_call(
        _combine_ln_kernel,
        grid_spec=grid_spec,
        out_shape=jax.ShapeDtypeStruct((t, d), F32),
        compiler_params=_params("arbitrary"),
        name="moe_combine_ln",
    )(dest, y, x, route, g.reshape(1, d), b.reshape(1, d))


def _moe_ffn_ln(x, xp, route, counts, w_gate, w_up, w_down, layer, ln_g, ln_b):
    t, d = x.shape
    bm = MOE_BLOCK
    eid = route[:, 0:TOP_K].astype(jnp.int32)
    rank = route[:, 4:4 + TOP_K].astype(jnp.int32)
    cnt = counts[0, N_GROUPS:N_GROUPS + N_EXPERTS].astype(jnp.int32)
    nblk = (cnt + bm - 1) // bm
    bend = jnp.cumsum(nblk)
    bstart = bend - nblk
    pstart = bstart * bm
    onehot = eid[:, :, None] == jnp.arange(N_EXPERTS, dtype=jnp.int32)
    dest = (jnp.sum(jnp.where(onehot, pstart, 0), axis=-1) + rank).reshape(-1)
    n_valid = bend[-1:]
    n_slots = (-(-t * TOP_K // bm) + N_EXPERTS) * bm
    xs = _dispatch(xp, t, dest, cnt, pstart, n_valid, n_slots)
    y = _experts(xs, nblk, bstart, n_valid, w_gate, w_up, w_down, layer)
    return _combine_ln(y, dest, x, route, ln_g, ln_b)


def _gla_gate_kernel(x_ref, wl_ref, wu_ref, b_ref, la_ref):
    g_low = _dot_hi_lo(x_ref[...], wl_ref)
    n = wu_ref.shape[1] // 2
    gh, gl = _split2(g_low)
    a = jnp.dot(gh, wu_ref[...], preferred_element_type=F32)
    c = jnp.dot(gl, wu_ref[:, :n], preferred_element_type=F32)
    logit = a[:, :n] + a[:, n:] + c + b_ref[...]
    la_ref[...] = _log_sigmoid(logit) / GLA_GATE_TAU


def _gla_gates(x, w_low_t, w_gate_up, b_gate):
    s, d = x.shape
    rank, dk = w_gate_up.shape
    ts = min(ROW_TILE, s)
    wl = _pack_hi_lo(w_low_t)
    wu_pad = jnp.zeros((LANES, dk), F32).at[:rank].set(w_gate_up)
    wu = jnp.concatenate(_split2(wu_pad), axis=1)
    return pl.pallas_call(
        _gla_gate_kernel,
        grid=(s // ts,),
        in_specs=[pl.BlockSpec((ts, d), lambda i: (i, 0)),
                  pl.BlockSpec((2 * LANES, d), lambda i: (0, 0)),
                  pl.BlockSpec((LANES, 2 * dk), lambda i: (0, 0)),
                  pl.BlockSpec((1, dk), lambda i: (0, 0))],
        out_specs=pl.BlockSpec((ts, dk), lambda i: (i, 0)),
        out_shape=jax.ShapeDtypeStruct((s, dk), F32),
        compiler_params=_params("parallel"),
        name="gla_gates",
    )(x, wl, wu, b_gate.reshape(1, dk))


def _gla_kernel(q_ref, k_ref, v_ref, la_ref, r_ref, g_ref, o_ref, state_ref):
    @pl.when(pl.program_id(0) == 0)
    def _():
        state_ref[...] = jnp.zeros_like(state_ref)

    rows = q_ref.shape[0]
    nh, dk, dv = state_ref.shape
    cs = GLA_CHUNK
    n_chunks = rows // cs
    rr = lax.broadcasted_iota(jnp.int32, (rows, rows), 0)
    cc = lax.broadcasted_iota(jnp.int32, (rows, rows), 1)
    causal = (rr // cs == cc // cs) & (rr >= cc)
    tri = jnp.where(causal, 1.0, 0.0).astype(BF16)
    lane_chunk = lax.broadcasted_iota(jnp.int32, (dk, rows), 1) // cs

    def chunk_row(a, i):
        return jnp.concatenate([jnp.broadcast_to(a[c * cs + i:c * cs + i + 1, :], (cs, a.shape[1]))
                                for c in range(n_chunks)], axis=0)

    for h in range(nh):
        kc = pl.ds(h * dk, dk)
        vc = pl.ds(h * dv, dv)
        b = _cumsum_rows(la_ref[:, kc], tri)
        b_mid = chunk_row(b, cs // 2 - 1)
        b_last = chunk_row(b, cs - 1)
        q = q_ref[:, kc].astype(F32) * (dk ** -0.5)
        k = k_ref[:, kc].astype(F32)
        v = v_ref[:, vc]
        qa = (q * jnp.exp(b - b_mid)).astype(BF16)
        ka = (k * jnp.exp(b_mid - b)).astype(BF16)
        a = jnp.where(causal, _dot_nt(qa, ka), 0.0)
        o_intra = jnp.dot(a.astype(BF16), v, preferred_element_type=F32)
        q_inter = (q * jnp.exp(b)).astype(BF16)
        k_end_t = (k * jnp.exp(b_last - b)).T
        b_t = b.T
        state = state_ref[h]
        outs = []
        for c in range(n_chunks):
            sl = slice(c * cs, (c + 1) * cs)
            outs.append(o_intra[sl] + jnp.dot(q_inter[sl], state.astype(BF16), preferred_element_type=F32))
            kv = jnp.dot(jnp.where(lane_chunk == c, k_end_t, 0.0).astype(BF16), v,
                         preferred_element_type=F32)
            decay = jnp.exp(b_t[:, (c + 1) * cs - 1:(c + 1) * cs])
            state = decay * state + kv
        state_ref[h] = state
        o = jnp.concatenate(outs, axis=0)
        o = o * lax.rsqrt(jnp.mean(o * o, axis=-1, keepdims=True) + RMS_EPS) * g_ref[...]
        r = r_ref[:, vc].astype(F32)
        o_ref[:, vc] = (o * (r * jax.nn.sigmoid(r))).astype(o_ref.dtype)


def _gla(proj, la, norm_g):
    s = proj.shape[0]
    dk_all = la.shape[1]
    nh = GLA_HEADS
    dk = dk_all // nh
    dv = norm_g.shape[0]
    dv_all = nh * dv
    assert 2 * dk_all == dv_all
    rows = min(GLA_ROWS, s)
    return pl.pallas_call(
        _gla_kernel,
        grid=(s // rows,),
        in_specs=[pl.BlockSpec((rows, dk_all), lambda i: (i, 0)),
                  pl.BlockSpec((rows, dk_all), lambda i: (i, 1)),
                  pl.BlockSpec((rows, dv_all), lambda i: (i, 1)),
                  pl.BlockSpec((rows, dk_all), lambda i: (i, 0)),
                  pl.BlockSpec((rows, dv_all), lambda i: (i, 2)),
                  pl.BlockSpec((1, dv), lambda i: (0, 0))],
        out_specs=pl.BlockSpec((rows, dv_all), lambda i: (i, 0)),
        out_shape=jax.ShapeDtypeStruct((s, dv_all), BF16),
        scratch_shapes=[pltpu.VMEM((nh, dk, dv), F32)],
        compiler_params=_params("arbitrary"),
        name="gla_chunks",
    )(proj, proj, proj, la, proj, norm_g.reshape(1, dv))


def kernel(x, fox_w_in, fox_b_f, fox_w_o, gla_w_in, gla_w_gate_up, gla_b_gate, gla_norm_g, gla_w_o,
           ln_mix_g, ln_mix_b, ln_ffn_g, ln_ffn_b, moe_w_group, moe_b_group, moe_w_expert,
           moe_b_expert, moe_w_gate, moe_w_up, moe_w_down):
    bsz, s, d = x.shape
    outs = []
    for bi in range(bsz):
        xt = x[bi]
        for i in range(DEPTH):
            j = i // 2
            if i % 2 == 0:
                w_in_t = fox_w_in[j].T
                qkv = _matmul_nt(xt, w_in_t, 3 * d, BF16)
                c = _fox_gates(xt, w_in_t[3 * d:], fox_b_f[j])
                o = _fox_attention(qkv, c, FOX_HEADS)
                w_o = fox_w_o[j]
            else:
                w_in_t = gla_w_in[j].T
                n_main = w_in_t.shape[0] - gla_w_gate_up.shape[1]
                proj = _matmul_nt(xt, w_in_t, n_main, BF16)
                la = _gla_gates(xt, w_in_t[n_main:], gla_w_gate_up[j], gla_b_gate[j])
                o = _gla(proj, la, gla_norm_g[j])
                w_o = gla_w_o[j]
            xt, xp, route, counts = _proj_ln_route(o, w_o.astype(BF16), xt, ln_mix_g[i], ln_mix_b[i], moe_w_group[i],
                                                   moe_b_group[i], moe_w_expert[i], moe_b_expert[i])
            xt = _moe_ffn_ln(xt, xp, route, counts, moe_w_gate, moe_w_up, moe_w_down, i, ln_ffn_g[i], ln_ffn_b[i])
        outs.append(xt)
    return outs[0].reshape(1, s, d) if bsz == 1 else jnp.stack(outs, axis=0)
```

```python
import functools

import jax
import jax.numpy as jnp
from jax import lax
from jax.experimental import pallas as pl
from jax.experimental.pallas import tpu as pltpu

F32 = jnp.float32
BF16 = jnp.bfloat16

DEPTH = 2
FOX_HEADS = 16
FOX_HEAD_DIM = 128
GLA_HEADS = 4
GLA_CHUNK = 64
GLA_GATE_TAU = 16.0
N_GROUPS = 8
EXPERTS_PER_GROUP = 8
N_EXPERTS = N_GROUPS * EXPERTS_PER_GROUP
TOP_K = 2
DEEPNORM_ALPHA = (2 * DEPTH) ** 0.25
LN_EPS = 1e-5
RMS_EPS = 1e-6

LANES = 128
VMEM_LIMIT = 56 * 2**20

MM_TM, MM_TN = 1024, 1024
ROW_TILE = 512
ATT_TQ, ATT_TK = 1024, 512
GLA_ROWS = 256
MOE_BLOCK = 128
MOE_W_SLOTS = 3
DISPATCH_TM = 1024
COMB_TM = 512


def _params(*sem):
    return pltpu.CompilerParams(dimension_semantics=sem, vmem_limit_bytes=VMEM_LIMIT)


def _split2(a):
    hi = a.astype(BF16)
    lo = (a - hi.astype(F32)).astype(BF16)
    return hi, lo


def _split3(a):
    hi = a.astype(BF16)
    r = a - hi.astype(F32)
    mid = r.astype(BF16)
    lo = (r - mid.astype(F32)).astype(BF16)
    return hi, mid, lo


def _pack_hi_lo(w_t, n_pad=LANES):
    n, k = w_t.shape
    wp = jnp.zeros((n_pad, k), F32).at[:n].set(w_t)
    hi, lo = _split2(wp)
    return jnp.concatenate([hi, lo], axis=0)


def _dot_nt(a, b):
    return lax.dot_general(a, b, (((1,), (1,)), ((), ())), preferred_element_type=F32)


def _dot_hi_lo(x, whl_ref, n_pad=LANES):
    xh, xl = _split2(x)
    a = _dot_nt(xh, whl_ref[...])
    b = _dot_nt(xl, whl_ref[:n_pad, :])
    return a[:, :n_pad] + a[:, n_pad:] + b


def _pack_rows(a):
    half = a.shape[1] // 2
    hi = lax.bitcast_convert_type(a[:, :half].astype(BF16).astype(F32), jnp.uint32)
    lo = lax.bitcast_convert_type(a[:, half:].astype(BF16).astype(F32), jnp.uint32)
    return hi | (lo >> 16)


def _unpack_rows(w):
    hi = lax.bitcast_convert_type(w & jnp.uint32(0xFFFF0000), F32)
    lo = lax.bitcast_convert_type(w << 16, F32)
    return jnp.concatenate([hi, lo], axis=1)


def _store_rows_as_lines(ref, first_line, packed):
    n_rows, width = packed.shape
    nt = width // LANES
    for j in range(nt):
        ref[pl.ds(first_line + j, n_rows, stride=nt), :] = packed[:, j * LANES:(j + 1) * LANES]


def _load_rows_from_lines(ref, first_line, n_rows, nt):
    return jnp.concatenate([ref[pl.ds(first_line + j, n_rows, stride=nt), :] for j in range(nt)], axis=1)


def _log_sigmoid(x):
    return -(jnp.maximum(-x, 0.0) + jnp.log1p(jnp.exp(-jnp.abs(x))))


def _cumsum_rows(a, incl_tri):
    n = a.shape[1]
    parts = jnp.concatenate(_split3(a), axis=1)
    c = jnp.dot(incl_tri, parts, preferred_element_type=F32)
    return c[:, :n] + c[:, n:2 * n] + c[:, 2 * n:]


def _tri(n, strict=False):
    r = lax.broadcasted_iota(jnp.int32, (n, n), 0)
    c = lax.broadcasted_iota(jnp.int32, (n, n), 1)
    return jnp.where((r > c) if strict else (r >= c), 1.0, 0.0).astype(BF16)


def _layer_norm_rows(z, g, b):
    mu = jnp.mean(z, axis=-1, keepdims=True)
    d = z - mu
    var = jnp.mean(d * d, axis=-1, keepdims=True)
    return d * lax.rsqrt(var + LN_EPS) * g + b


def _mm_kernel(x_ref, w_ref, o_ref, wb_ref):
    @pl.when(pl.program_id(1) == 0)
    def _():
        wb_ref[...] = w_ref[...].astype(BF16)

    o_ref[...] = lax.dot_general(x_ref[...].astype(BF16), wb_ref[...], (((1,), (1,)), ((), ())),
                                 preferred_element_type=F32).astype(o_ref.dtype)


def _matmul_nt(x, w_t, n, out_dtype):
    m, k = x.shape
    tm, tn = min(MM_TM, m), min(MM_TN, n)
    return pl.pallas_call(
        _mm_kernel,
        grid=(n // tn, m // tm),
        in_specs=[pl.BlockSpec((tm, k), lambda j, i: (i, 0)),
                  pl.BlockSpec((tn, k), lambda j, i: (j, 0))],
        out_specs=pl.BlockSpec((tm, tn), lambda j, i: (i, j)),
        out_shape=jax.ShapeDtypeStruct((m, n), out_dtype),
        scratch_shapes=[pltpu.VMEM((tn, k), BF16)],
        compiler_params=_params("arbitrary", "arbitrary"),
        name="dense_proj",
    )(x, w_t)


def _fox_gate_kernel(x_ref, w_ref, b_ref, c_ref, carry_ref):
    @pl.when(pl.program_id(0) == 0)
    def _():
        carry_ref[...] = jnp.zeros_like(carry_ref)

    ts = x_ref.shape[0]
    logits = _dot_hi_lo(x_ref[...], w_ref) + b_ref[...]
    log_f = _log_sigmoid(logits)
    c = _cumsum_rows(log_f, _tri(ts)) + carry_ref[...]
    carry_ref[...] = c[ts - 1:ts, :]
    c_ref[...] = c


def _fox_gates(x, w_f_t, b_f):
    s, d = x.shape
    h = w_f_t.shape[0]
    ts = min(ROW_TILE, s)
    whl = _pack_hi_lo(w_f_t)
    bias = jnp.zeros((1, LANES), F32).at[0, :h].set(b_f)
    return pl.pallas_call(
        _fox_gate_kernel,
        grid=(s // ts,),
        in_specs=[pl.BlockSpec((ts, d), lambda i: (i, 0)),
                  pl.BlockSpec((2 * LANES, d), lambda i: (0, 0)),
                  pl.BlockSpec((1, LANES), lambda i: (0, 0))],
        out_specs=pl.BlockSpec((ts, LANES), lambda i: (i, 0)),
        out_shape=jax.ShapeDtypeStruct((s, LANES), F32),
        scratch_shapes=[pltpu.VMEM((1, LANES), F32)],
        compiler_params=_params("arbitrary"),
        name="fox_gates",
    )(x, whl, bias)


def _fox_attn_kernel(q_ref, k_ref, v_ref, c_ref, o_ref, kaug_ref, vt_ref, qt_ref, st0_ref, st1_ref,
                     p0_ref, p1_ref, alpha_ref, bmax_ref, acc_ref, m_ref):
    h = pl.program_id(0)
    qi = pl.program_id(1)
    tq, dh = q_ref.shape
    s_len = k_ref.shape[0]
    tk = ATT_TK
    log2e = 1.4426950408889634
    scale = dh ** -0.5 * log2e

    def head_column(rows, n):
        lane = lax.broadcasted_iota(jnp.int32, (n, LANES), 1)
        return log2e * jnp.sum(jnp.where(lane == h, c_ref[rows, :], 0.0), axis=1, keepdims=True)

    def bias_columns(col, first, n):
        lane = lax.broadcasted_iota(jnp.int32, (n, LANES), 1)
        hi, mid, lo = _split3(col)
        ones_first = 3 - first
        out = jnp.where((lane >= ones_first) & (lane < ones_first + 3), 1.0, 0.0)
        out = jnp.where(lane == first, hi.astype(F32), out)
        out = jnp.where(lane == first + 1, mid.astype(F32), out)
        return jnp.where(lane == first + 2, lo.astype(F32), out).astype(BF16)

    @pl.when(qi == 0)
    def _():
        ones_row = jnp.where(lax.broadcasted_iota(jnp.int32, (16, tk), 0) == 0, 1.0, 0.0).astype(BF16)

        def build(ci, carry):
            rows = pl.ds(pl.multiple_of(ci * tk, tk), tk)
            kaug_ref[rows, :dh] = k_ref[rows, :]
            kaug_ref[rows, dh:] = bias_columns(-head_column(rows, tk), 0, tk)
            vt_ref[:dh, rows] = v_ref[rows, :].astype(F32).T.astype(BF16)
            vt_ref[dh:, rows] = ones_row
            return carry
        lax.fori_loop(0, s_len // tk, build, 0)

    q_rows = pl.ds(pl.multiple_of(qi * tq, tq), tq)
    q_aug = jnp.concatenate([q_ref[...].astype(F32) * scale,
                             bias_columns(head_column(q_rows, tq), 3, tq).astype(F32)], axis=1)
    qt_ref[...] = q_aug.T.astype(BF16)

    acc_ref[...] = jnp.zeros_like(acc_ref)
    m_ref[...] = jnp.full_like(m_ref, -jnp.inf)

    st_refs, p_refs = (st0_ref, st1_ref), (p0_ref, p1_ref)

    def scores(k_start, slot, diag_offset=None):
        st = jnp.dot(kaug_ref[pl.ds(k_start, tk), :], qt_ref[...], preferred_element_type=F32)
        if diag_offset is not None:
            kr = lax.broadcasted_iota(jnp.int32, (tk, tq), 0) + diag_offset
            qc = lax.broadcasted_iota(jnp.int32, (tk, tq), 1)
            st = jnp.where(kr <= qc, st, -jnp.inf)
        st_refs[slot][...] = st
        bmax_ref[slot] = jnp.max(st, axis=0, keepdims=True)

    def softmax(slot):
        m_prev = m_ref[...]
        m_new = jnp.maximum(m_prev, bmax_ref[slot])
        m_ref[...] = m_new
        p_refs[slot][...] = jnp.exp2(st_refs[slot][...] - m_new).astype(BF16)
        alpha_ref[slot] = jnp.exp2(m_prev - m_new)

    def values(k_start, slot):
        acc_ref[...] = alpha_ref[slot] * acc_ref[...] + jnp.dot(
            vt_ref[:, pl.ds(k_start, tk)], p_refs[slot][...], preferred_element_type=F32)

    n_diag = tq // tk
    assert n_diag == 2
    d0 = pl.multiple_of(qi * tq, tk)
    d1 = pl.multiple_of(qi * tq + tk, tk)
    n_full = qi * n_diag

    def full_start(i):
        return pl.multiple_of(jnp.minimum(i, jnp.maximum(n_full - 1, 0)) * tk, tk)

    scores(d0, 0, 0)
    scores(d1, 1, tk)
    softmax(0)
    scores(full_start(0), 0)
    softmax(1)
    values(d0, 0)

    def pair(j, carry):
        i0 = 2 * j
        prev = jnp.where(j == 0, d1, (i0 - 1) * tk)
        scores(full_start(i0 + 1), 1)
        softmax(0)
        values(pl.multiple_of(prev, tk), 1)
        scores(full_start(i0 + 2), 0)
        softmax(1)
        values(pl.multiple_of(i0 * tk, tk), 0)
        return carry

    lax.fori_loop(0, qi, pair, 0)
    last = jnp.where(qi == 0, d1, (n_full - 1) * tk)
    values(pl.multiple_of(last, tk), 1)

    o_ref[...] = (acc_ref[:dh, :] / acc_ref[dh:dh + 1, :]).T.astype(o_ref.dtype)


def _fox_attention(qkv, c, n_heads):
    s = qkv.shape[0]
    dh = FOX_HEAD_DIM
    tq = min(ATT_TQ, s)
    hh = n_heads
    return pl.pallas_call(
        _fox_attn_kernel,
        grid=(hh, s // tq),
        in_specs=[pl.BlockSpec((tq, dh), lambda h, i: (i, h)),
                  pl.BlockSpec((s, dh), lambda h, i: (0, hh + h)),
                  pl.BlockSpec((s, dh), lambda h, i: (0, 2 * hh + h)),
                  pl.BlockSpec((s, LANES), lambda h, i: (0, 0))],
        out_specs=pl.BlockSpec((tq, dh), lambda h, i: (i, h)),
        out_shape=jax.ShapeDtypeStruct((s, hh * dh), BF16),
        scratch_shapes=[pltpu.VMEM((s, 2 * dh), BF16),
                        pltpu.VMEM((dh + 16, s), BF16),
                        pltpu.VMEM((2 * dh, tq), BF16),
                        pltpu.VMEM((ATT_TK, tq), F32),
                        pltpu.VMEM((ATT_TK, tq), F32),
                        pltpu.VMEM((ATT_TK, tq), BF16),
                        pltpu.VMEM((ATT_TK, tq), BF16),
                        pltpu.VMEM((2, 1, tq), F32),
                        pltpu.VMEM((2, 1, tq), F32),
                        pltpu.VMEM((dh + 16, tq), F32),
                        pltpu.VMEM((1, tq), F32)],
        compiler_params=_params("arbitrary", "arbitrary"),
        name="fox_attention",
    )(qkv, qkv, qkv, c)


def _first_lane_eq(vals, target, lane):
    return jnp.min(jnp.where(vals == target, lane, LANES), axis=1, keepdims=True)


def _route_rows(x, w_ref, b_ref, carry_ref):
    tm = x.shape[0]
    neg = -jnp.inf
    logits = _dot_hi_lo(x, w_ref) + b_ref[...]
    lane = lax.broadcasted_iota(jnp.int32, (tm, LANES), 1)

    gl = jnp.where(lane < N_GROUPS, logits, neg)
    gmax = jnp.max(gl, axis=1, keepdims=True)
    gsum = jnp.sum(jnp.exp(gl - gmax), axis=1, keepdims=True)
    grp_p = 1.0 / gsum
    grp = _first_lane_eq(gl, gmax, lane)

    lo = N_GROUPS + grp * EXPERTS_PER_GROUP
    el = jnp.where((lane >= lo) & (lane < lo + EXPERTS_PER_GROUP), logits, neg)
    emax = jnp.max(el, axis=1, keepdims=True)
    esum = jnp.sum(jnp.exp(el - emax), axis=1, keepdims=True)
    idx1 = _first_lane_eq(el, emax, lane)
    el2 = jnp.where(lane == idx1, neg, el)
    emax2 = jnp.max(el2, axis=1, keepdims=True)
    idx2 = _first_lane_eq(el2, emax2, lane)
    p1 = 1.0 / esum
    p2 = jnp.exp(emax2 - emax) / esum
    psum = p1 + p2
    g1 = grp_p * (p1 / psum)
    g2 = grp_p * (p2 / psum)

    oh1 = lane == idx1
    oh2 = lane == idx2
    both = jnp.where(oh1 | oh2, 1.0, 0.0)
    before = jnp.dot(_tri(tm, strict=True), both.astype(BF16), preferred_element_type=F32)
    before = before + carry_ref[...]
    r1 = jnp.sum(jnp.where(oh1, before, 0.0), axis=1, keepdims=True)
    r2 = jnp.sum(jnp.where(oh2, before, 0.0), axis=1, keepdims=True)
    carry_ref[...] = carry_ref[...] + jnp.sum(both, axis=0, keepdims=True)

    e1 = (idx1 - N_GROUPS).astype(F32)
    e2 = (idx2 - N_GROUPS).astype(F32)
    out = jnp.where(lane == 0, e1, 0.0)
    out = jnp.where(lane == 1, e2, out)
    out = jnp.where(lane == 2, g1, out)
    out = jnp.where(lane == 3, g2, out)
    out = jnp.where(lane == 4, r1, out)
    return jnp.where(lane == 5, r2, out)


def _proj_ln_route_kernel(o_ref, w_ref, x_ref, g_ref, b_ref, wr_ref, br_ref,
                          y_ref, yp_ref, route_ref, cnt_ref, carry_ref):
    @pl.when(pl.program_id(0) == 0)
    def _():
        carry_ref[...] = jnp.zeros_like(carry_ref)

    mix = jnp.dot(o_ref[...], w_ref[...], preferred_element_type=F32)
    y = _layer_norm_rows(DEEPNORM_ALPHA * x_ref[...] + mix, g_ref[...], b_ref[...])
    y_ref[...] = y
    _store_rows_as_lines(yp_ref, 0, _pack_rows(y))
    route_ref[...] = _route_rows(y, wr_ref, br_ref, carry_ref)
    cnt_ref[...] = carry_ref[...]


def _proj_ln_route(o, w, x, g, b, w_group, b_group, w_expert, b_expert):
    s, d = x.shape
    kd = o.shape[1]
    tm = min(ROW_TILE, s)
    whl = _pack_hi_lo(jnp.concatenate([w_group.T, w_expert.T], axis=0))
    nb = N_GROUPS + N_EXPERTS
    bias = jnp.zeros((1, LANES), F32).at[0, :nb].set(jnp.concatenate([b_group, b_expert]))
    return pl.pallas_call(
        _proj_ln_route_kernel,
        grid=(s // tm,),
        in_specs=[pl.BlockSpec((tm, kd), lambda i: (i, 0)),
                  pl.BlockSpec((kd, d), lambda i: (0, 0)),
                  pl.BlockSpec((tm, d), lambda i: (i, 0)),
                  pl.BlockSpec((1, d), lambda i: (0, 0)),
                  pl.BlockSpec((1, d), lambda i: (0, 0)),
                  pl.BlockSpec((2 * LANES, d), lambda i: (0, 0)),
                  pl.BlockSpec((1, LANES), lambda i: (0, 0))],
        out_specs=[pl.BlockSpec((tm, d), lambda i: (i, 0)),
                   pl.BlockSpec((tm * (d // 2 // LANES), LANES), lambda i: (i, 0)),
                   pl.BlockSpec((tm, LANES), lambda i: (i, 0)),
                   pl.BlockSpec((1, LANES), lambda i: (0, 0))],
        out_shape=[jax.ShapeDtypeStruct((s, d), F32),
                   jax.ShapeDtypeStruct((s * (d // 2 // LANES), LANES), jnp.uint32),
                   jax.ShapeDtypeStruct((s, LANES), F32),
                   jax.ShapeDtypeStruct((1, LANES), F32)],
        scratch_shapes=[pltpu.VMEM((1, LANES), F32)],
        compiler_params=_params("arbitrary"),
        name="proj_ln_route",
    )(o, w, x, g.reshape(1, d), b.reshape(1, d), whl, bias)


def _dispatch_kernel(dest_ref, cnt_ref, pstart_ref, nv_ref, x_ref, xs_hbm, zero_ref, sem, zsem, *, nt):
    i = pl.program_id(0)
    tm = x_ref.shape[0] // nt
    bm = MOE_BLOCK
    n_exp = cnt_ref.shape[0]
    n_blocks = xs_hbm.shape[0] // (bm * nt)

    def lines(row, n_rows=1):
        return pl.ds(pl.multiple_of(row * nt, nt), n_rows * nt)

    def pad_copies(e, fn):
        cnt = cnt_ref[e]
        pad = (bm - cnt % bm) % bm
        pos = pstart_ref[e] + cnt
        size = bm // 2
        while size >= 1:
            @pl.when((pad & size) != 0)
            def _(pos=pos, size=size):
                fn(pltpu.make_async_copy(zero_ref.at[lines(0, size), :], xs_hbm.at[lines(pos, size), :], zsem))
            pos = pos + (pad & size)
            size //= 2

    def tail_copy(g):
        return pltpu.make_async_copy(zero_ref, xs_hbm.at[lines(g * bm, bm), :], zsem)

    def for_all_fills(fn):
        def per_expert(e, carry):
            pad_copies(e, fn)
            return carry
        lax.fori_loop(0, n_exp, per_expert, 0)

        def per_tail(g, carry):
            fn(tail_copy(g))
            return carry
        lax.fori_loop(nv_ref[0], n_blocks, per_tail, 0)

    @pl.when(i == 0)
    def _():
        zero_ref[...] = jnp.zeros_like(zero_ref)
        for_all_fills(lambda c: c.start())

    base = i * (TOP_K * tm)

    def body(r, carry):
        for k in range(TOP_K):
            dst = dest_ref[base + TOP_K * r + k]
            pltpu.make_async_copy(x_ref.at[lines(r), :], xs_hbm.at[lines(dst), :], sem).start(priority=k)
        return carry
    lax.fori_loop(0, tm, body, 0, unroll=8)

    @pl.when(i == 0)
    def _():
        for_all_fills(lambda c: c.wait())

    for _ in range(TOP_K):
        pltpu.make_async_copy(x_ref, xs_hbm.at[lines(0, tm), :], sem).wait()


def _dispatch(xp, n_rows, dest, cnt, pstart, n_valid, n_slots):
    nt = xp.shape[0] // n_rows
    tm = min(DISPATCH_TM, n_rows)
    grid_spec = pltpu.PrefetchScalarGridSpec(
        num_scalar_prefetch=4,
        grid=(n_rows // tm,),
        in_specs=[pl.BlockSpec((tm * nt, LANES), lambda i, *_: (i, 0))],
        out_specs=pl.BlockSpec(memory_space=pl.ANY),
        scratch_shapes=[pltpu.VMEM((MOE_BLOCK * nt, LANES), xp.dtype),
                        pltpu.SemaphoreType.DMA,
                        pltpu.SemaphoreType.DMA],
    )
    return pl.pallas_call(
        functools.partial(_dispatch_kernel, nt=nt),
        grid_spec=grid_spec,
        out_shape=jax.ShapeDtypeStruct((n_slots * nt, LANES), xp.dtype),
        compiler_params=_params("arbitrary"),
        name="moe_dispatch",
    )(dest, cnt, pstart, n_valid, xp)


def _expert_kernel(nblk_ref, bstart_ref, nv_ref, xs_hbm, wg_hbm, wu_hbm, wd_hbm, y_hbm,
                   xin_ref, yout_ref, wgf_ref, wuf_ref, wdf_ref, wgb_ref, wub_ref, wdb_ref,
                   in_sem, out_sem, w_sem, *, layer):
    e = pl.program_id(0)
    n_exp = pl.num_programs(0)
    n_valid = nv_ref[0]
    bm = MOE_BLOCK
    blk_lines = xin_ref.shape[1]
    nt = blk_lines // bm
    n_blocks = y_hbm.shape[0] // blk_lines
    ns = MOE_W_SLOTS

    half = wd_hbm.shape[2] // 2

    def w_copies(ex, slot):
        return (pltpu.make_async_copy(wg_hbm.at[layer, ex], wgf_ref.at[slot], w_sem.at[slot]),
                pltpu.make_async_copy(wu_hbm.at[layer, ex], wuf_ref.at[slot], w_sem.at[slot]),
                pltpu.make_async_copy(wd_hbm.at[layer, ex, pl.ds(0, half)],
                                      wdf_ref.at[slot, pl.ds(0, half)], w_sem.at[slot]),
                pltpu.make_async_copy(wd_hbm.at[layer, ex, pl.ds(half, half)],
                                      wdf_ref.at[slot, pl.ds(half, half)], w_sem.at[slot]))

    def start_weights(ex):
        exc = jnp.minimum(ex, n_exp - 1)

        @pl.when((ex < n_exp) & (nblk_ref[exc] > 0))
        def _():
            for piece, c in enumerate(w_copies(exc, exc % ns)):
                c.start(priority=piece % 2)

    @pl.when(e == 0)
    def _():
        for ahead in range(ns - 1):
            start_weights(ahead)

    start_weights(e + ns - 1)

    def block_lines(g):
        return pl.ds(pl.multiple_of(g * blk_lines, blk_lines), blk_lines)

    def in_copy(g, slot):
        return pltpu.make_async_copy(xs_hbm.at[block_lines(g), :], xin_ref.at[slot], in_sem.at[slot])

    def out_copy(g, slot):
        return pltpu.make_async_copy(yout_ref.at[slot], y_hbm.at[block_lines(g), :], out_sem.at[slot])

    @pl.when((e == 0) & (n_valid > 0))
    def _():
        in_copy(0, 0).start(priority=0)

    nb = nblk_ref[e]
    g0 = bstart_ref[e]

    @pl.when(nb > 0)
    def _():
        w_slot = e % ns
        for c in w_copies(e, w_slot):
            c.wait()
        wgb_ref[...] = wgf_ref[w_slot].astype(BF16)
        wub_ref[...] = wuf_ref[w_slot].astype(BF16)
        wdb_ref[...] = wdf_ref[w_slot].astype(BF16)

        def block(j, carry):
            g = g0 + j
            slot = g % 2
            in_copy(g, slot).wait()

            @pl.when(g + 1 < n_valid)
            def _():
                in_copy(g + 1, 1 - slot).start(priority=0)

            @pl.when(g >= 2)
            def _():
                out_copy(g - 2, slot).wait()

            xb = _unpack_rows(_load_rows_from_lines(xin_ref.at[slot], 0, bm, nt)).astype(BF16)
            gate = jnp.dot(xb, wgb_ref[...], preferred_element_type=F32)
            up = jnp.dot(xb, wub_ref[...], preferred_element_type=F32)
            hid = (gate * jax.nn.sigmoid(gate) * up).astype(BF16)
            y = jnp.dot(hid, wdb_ref[...], preferred_element_type=F32)
            _store_rows_as_lines(yout_ref.at[slot], 0, _pack_rows(y))
            out_copy(g, slot).start(priority=1)
            return carry
        lax.fori_loop(0, nb, block, 0)

    @pl.when(e == pl.num_programs(0) - 1)
    def _():
        for back in (2, 1):
            @pl.when(n_valid >= back)
            def _(back=back):
                out_copy(n_valid - back, (n_valid - back) % 2).wait()

        yout_ref[0] = jnp.zeros(yout_ref.shape[1:], yout_ref.dtype)

        def fill(g, carry):
            out_copy(g, 0).start()
            return carry
        lax.fori_loop(n_valid, n_blocks, fill, 0)

        def drain(g, carry):
            out_copy(g, 0).wait()
            return carry
        lax.fori_loop(n_valid, n_blocks, drain, 0)


def _experts(xs, nblk, bstart, n_valid, w_gate, w_up, w_down, layer):
    _, n_exp, d, f = w_gate.shape
    bm = MOE_BLOCK
    nt = d // 2 // LANES
    ns = MOE_W_SLOTS
    grid_spec = pltpu.PrefetchScalarGridSpec(
        num_scalar_prefetch=3,
        grid=(n_exp,),
        in_specs=[pl.BlockSpec(memory_space=pl.ANY)] * 4,
        out_specs=pl.BlockSpec(memory_space=pl.ANY),
        scratch_shapes=[pltpu.VMEM((2, bm * nt, LANES), jnp.uint32),
                        pltpu.VMEM((2, bm * nt, LANES), jnp.uint32),
                        pltpu.VMEM((ns, d, f), F32),
                        pltpu.VMEM((ns, d, f), F32),
                        pltpu.VMEM((ns, f, d), F32),
                        pltpu.VMEM((d, f), BF16),
                        pltpu.VMEM((d, f), BF16),
                        pltpu.VMEM((f, d), BF16),
                        pltpu.SemaphoreType.DMA((2,)),
                        pltpu.SemaphoreType.DMA((2,)),
                        pltpu.SemaphoreType.DMA((ns,))],
    )
    return pl.pallas_call(
        functools.partial(_expert_kernel, layer=layer),
        grid_spec=grid_spec,
        out_shape=jax.ShapeDtypeStruct(xs.shape, jnp.uint32),
        compiler_params=_params("arbitrary"),
        name="moe_experts",
    )(nblk, bstart, n_valid, xs, w_gate, w_up, w_down)


def _combine_ln_kernel(dest_ref, y_hbm, x_ref, route_ref, g_ref, b_ref, o_ref, yg_ref, sem):
    i = pl.program_id(0)
    n = pl.num_programs(0)
    tm = x_ref.shape[0]
    nt = yg_ref.shape[1] // (TOP_K * tm)

    def lines(row, n_rows=1):
        return pl.ds(pl.multiple_of(row * nt, nt), n_rows * nt)

    def start_gather(step, slot):
        base = step * (TOP_K * tm)

        def body(r, carry):
            for k in range(TOP_K):
                src = dest_ref[base + TOP_K * r + k]
                pltpu.make_async_copy(y_hbm.at[lines(src), :], yg_ref.at[slot, lines(k * tm + r), :],
                                      sem.at[slot]).start(priority=k)
            return carry
        lax.fori_loop(0, tm, body, 0, unroll=8)

    def wait_gather(slot):
        pltpu.make_async_copy(y_hbm.at[lines(0, TOP_K * tm), :], yg_ref.at[slot], sem.at[slot]).wait()

    @pl.when(i == 0)
    def _():
        start_gather(0, 0)

    @pl.when(i + 1 < n)
    def _():
        start_gather(i + 1, (i + 1) % 2)

    slot = i % 2
    wait_gather(slot)
    route = route_ref[...]
    lane = lax.broadcasted_iota(jnp.int32, route.shape, 1)
    g1 = jnp.sum(jnp.where(lane == 2, route, 0.0), axis=1, keepdims=True)
    g2 = jnp.sum(jnp.where(lane == 3, route, 0.0), axis=1, keepdims=True)
    y1 = _unpack_rows(_load_rows_from_lines(yg_ref.at[slot], 0, tm, nt))
    y2 = _unpack_rows(_load_rows_from_lines(yg_ref.at[slot], tm * nt, tm, nt))
    z = DEEPNORM_ALPHA * x_ref[...] + (y1 * g1 + y2 * g2)
    o_ref[...] = _layer_norm_rows(z, g_ref[...], b_ref[...])


def _combine_ln(y, dest, x, route, g, b):
    t, d = x.shape
    tm = min(COMB_TM, t)
    grid_spec = pltpu.PrefetchScalarGridSpec(
        num_scalar_prefetch=1,
        grid=(t // tm,),
        in_specs=[pl.BlockSpec(memory_space=pl.ANY),
                  pl.BlockSpec((tm, d), lambda i, dest: (i, 0)),
                  pl.BlockSpec((tm, LANES), lambda i, dest: (i, 0)),
                  pl.BlockSpec((1, d), lambda i, dest: (0, 0)),
                  pl.BlockSpec((1, d), lambda i, dest: (0, 0))],
        out_specs=pl.BlockSpec((tm, d), lambda i, dest: (i, 0)),
        scratch_shapes=[pltpu.VMEM((2, TOP_K * tm * (d // 2 // LANES), LANES), y.dtype),
                        pltpu.SemaphoreType.DMA((2,))],
    )
    return pl.pallas_call(
        _combine_ln_kernel,
        grid_spec=grid_spec,
        out_shape=jax.ShapeDtypeStruct((t, d), F32),
        compiler_params=_params("arbitrary"),
        name="moe_combine_ln",
    )(dest, y, x, route, g.reshape(1, d), b.reshape(1, d))


def _moe_ffn_ln(x, xp, route, counts, w_gate, w_up, w_down, layer, ln_g, ln_b):
    t, d = x.shape
    bm = MOE_BLOCK
    eid = route[:, 0:TOP_K].astype(jnp.int32)
    rank = route[:, 4:4 + TOP_K].astype(jnp.int32)
    cnt = counts[0, N_GROUPS:N_GROUPS + N_EXPERTS].astype(jnp.int32)
    nblk = (cnt + bm - 1) // bm
    bend = jnp.cumsum(nblk)
    bstart = bend - nblk
    pstart = bstart * bm
    onehot = eid[:, :, None] == jnp.arange(N_EXPERTS, dtype=jnp.int32)
    dest = (jnp.sum(jnp.where(onehot, pstart, 0), axis=-1) + rank).reshape(-1)
    n_valid = bend[-1:]
    n_slots = (-(-t * TOP_K // bm) + N_EXPERTS) * bm
    xs = _dispatch(xp, t, dest, cnt, pstart, n_valid, n_slots)
    y = _experts(xs, nblk, bstart, n_valid, w_gate, w_up, w_down, layer)
    return _combine_ln(y, dest, x, route, ln_g, ln_b)


def _gla_gate_kernel(x_ref, wl_ref, wu_ref, b_ref, la_ref):
    g_low = _dot_hi_lo(x_ref[...], wl_ref)
    n = wu_ref.shape[1] // 2
    gh, gl = _split2(g_low)
    a = jnp.dot(gh, wu_ref[...], preferred_element_type=F32)
    c = jnp.dot(gl, wu_ref[:, :n], preferred_element_type=F32)
    logit = a[:, :n] + a[:, n:] + c + b_ref[...]
    la_ref[...] = _log_sigmoid(logit) / GLA_GATE_TAU


def _gla_gates(x, w_low_t, w_gate_up, b_gate):
    s, d = x.shape
    rank, dk = w_gate_up.shape
    ts = min(ROW_TILE, s)
    wl = _pack_hi_lo(w_low_t)
    wu_pad = jnp.zeros((LANES, dk), F32).at[:rank].set(w_gate_up)
    wu = jnp.concatenate(_split2(wu_pad), axis=1)
    return pl.pallas_call(
        _gla_gate_kernel,
        grid=(s // ts,),
        in_specs=[pl.BlockSpec((ts, d), lambda i: (i, 0)),
                  pl.BlockSpec((2 * LANES, d), lambda i: (0, 0)),
                  pl.BlockSpec((LANES, 2 * dk), lambda i: (0, 0)),
                  pl.BlockSpec((1, dk), lambda i: (0, 0))],
        out_specs=pl.BlockSpec((ts, dk), lambda i: (i, 0)),
        out_shape=jax.ShapeDtypeStruct((s, dk), F32),
        compiler_params=_params("parallel"),
        name="gla_gates",
    )(x, wl, wu, b_gate.reshape(1, dk))


def _gla_kernel(q_ref, k_ref, v_ref, la_ref, r_ref, g_ref, o_ref, state_ref):
    @pl.when(pl.program_id(0) == 0)
    def _():
        state_ref[...] = jnp.zeros_like(state_ref)

    rows = q_ref.shape[0]
    nh, dk, dv = state_ref.shape
    cs = GLA_CHUNK
    n_chunks = rows // cs
    rr = lax.broadcasted_iota(jnp.int32, (rows, rows), 0)
    cc = lax.broadcasted_iota(jnp.int32, (rows, rows), 1)
    causal = (rr // cs == cc // cs) & (rr >= cc)
    tri = jnp.where(causal, 1.0, 0.0).astype(BF16)
    lane_chunk = lax.broadcasted_iota(jnp.int32, (dk, rows), 1) // cs

    def chunk_row(a, i):
        return jnp.concatenate([jnp.broadcast_to(a[c * cs + i:c * cs + i + 1, :], (cs, a.shape[1]))
                                for c in range(n_chunks)], axis=0)

    for h in range(nh):
        kc = pl.ds(h * dk, dk)
        vc = pl.ds(h * dv, dv)
        b = _cumsum_rows(la_ref[:, kc], tri)
        b_mid = chunk_row(b, cs // 2 - 1)
        b_last = chunk_row(b, cs - 1)
        q = q_ref[:, kc].astype(F32) * (dk ** -0.5)
        k = k_ref[:, kc].astype(F32)
        v = v_ref[:, vc]
        qa = (q * jnp.exp(b - b_mid)).astype(BF16)
        ka = (k * jnp.exp(b_mid - b)).astype(BF16)
        a = jnp.where(causal, _dot_nt(qa, ka), 0.0)
        o_intra = jnp.dot(a.astype(BF16), v, preferred_element_type=F32)
        q_inter = (q * jnp.exp(b)).astype(BF16)
        k_end_t = (k * jnp.exp(b_last - b)).T
        b_t = b.T
        state = state_ref[h]
        outs = []
        for c in range(n_chunks):
            sl = slice(c * cs, (c + 1) * cs)
            outs.append(o_intra[sl] + jnp.dot(q_inter[sl], state.astype(BF16), preferred_element_type=F32))
            kv = jnp.dot(jnp.where(lane_chunk == c, k_end_t, 0.0).astype(BF16), v,
                         preferred_element_type=F32)
            decay = jnp.exp(b_t[:, (c + 1) * cs - 1:(c + 1) * cs])
            state = decay * state + kv
        state_ref[h] = state
        o = jnp.concatenate(outs, axis=0)
        o = o * lax.rsqrt(jnp.mean(o * o, axis=-1, keepdims=True) + RMS_EPS) * g_ref[...]
        r = r_ref[:, vc].astype(F32)
        o_ref[:, vc] = (o * (r * jax.nn.sigmoid(r))).astype(o_ref.dtype)


def _gla(proj, la, norm_g):
    s = proj.shape[0]
    dk_all = la.shape[1]
    nh = GLA_HEADS
    dk = dk_all // nh
    dv = norm_g.shape[0]
    dv_all = nh * dv
    assert 2 * dk_all == dv_all
    rows = min(GLA_ROWS, s)
    return pl.pallas_call(
        _gla_kernel,
        grid=(s // rows,),
        in_specs=[pl.BlockSpec((rows, dk_all), lambda i: (i, 0)),
                  pl.BlockSpec((rows, dk_all), lambda i: (i, 1)),
                  pl.BlockSpec((rows, dv_all), lambda i: (i, 1)),
                  pl.BlockSpec((rows, dk_all), lambda i: (i, 0)),
                  pl.BlockSpec((rows, dv_all), lambda i: (i, 2)),
                  pl.BlockSpec((1, dv), lambda i: (0, 0))],
        out_specs=pl.BlockSpec((rows, dv_all), lambda i: (i, 0)),
        out_shape=jax.ShapeDtypeStruct((s, dv_all), BF16),
        scratch_shapes=[pltpu.VMEM((nh, dk, dv), F32)],
        compiler_params=_params("arbitrary"),
        name="gla_chunks",
    )(proj, proj, proj, la, proj, norm_g.reshape(1, dv))


def kernel(x, fox_w_in, fox_b_f, fox_w_o, gla_w_in, gla_w_gate_up, gla_b_gate, gla_norm_g, gla_w_o,
           ln_mix_g, ln_mix_b, ln_ffn_g, ln_ffn_b, moe_w_group, moe_b_group, moe_w_expert,
           moe_b_expert, moe_w_gate, moe_w_up, moe_w_down):
    bsz, s, d = x.shape
    outs = []
    for bi in range(bsz):
        xt = x[bi]
        for i in range(DEPTH):
            j = i // 2
            if i % 2 == 0:
                w_in_t = fox_w_in[j].T
                qkv = _matmul_nt(xt, w_in_t, 3 * d, BF16)
                c = _fox_gates(xt, w_in_t[3 * d:], fox_b_f[j])
                o = _fox_attention(qkv, c, FOX_HEADS)
                w_o = fox_w_o[j]
            else:
                w_in_t = gla_w_in[j].T
                n_main = w_in_t.shape[0] - gla_w_gate_up.shape[1]
                proj = _matmul_nt(xt, w_in_t, n_main, BF16)
                la = _gla_gates(xt, w_in_t[n_main:], gla_w_gate_up[j], gla_b_gate[j])
                o = _gla(proj, la, gla_norm_g[j])
                w_o = gla_w_o[j]
            xt, xp, route, counts = _proj_ln_route(o, w_o.astype(BF16), xt, ln_mix_g[i], ln_mix_b[i], moe_w_group[i],
                                                   moe_b_group[i], moe_w_expert[i], moe_b_expert[i])
            xt = _moe_ffn_ln(xt, xp, route, counts, moe_w_gate, moe_w_up, moe_w_down, i, ln_ffn_g[i], ln_ffn_b[i])
        outs.append(xt)
    return outs[0].reshape(1, s, d) if bsz == 1 else jnp.stack(outs, axis=0)
```

```python
import functools

import jax
import jax.numpy as jnp
from jax import lax
from jax.experimental import pallas as pl
from jax.experimental.pallas import tpu as pltpu

F32 = jnp.float32
BF16 = jnp.bfloat16

DEPTH = 2
FOX_HEADS = 16
FOX_HEAD_DIM = 128
GLA_HEADS = 4
GLA_CHUNK = 64
GLA_GATE_TAU = 16.0
N_GROUPS = 8
EXPERTS_PER_GROUP = 8
N_EXPERTS = N_GROUPS * EXPERTS_PER_GROUP
TOP_K = 2
DEEPNORM_ALPHA = (2 * DEPTH) ** 0.25
LN_EPS = 1e-5
RMS_EPS = 1e-6

LANES = 128
VMEM_LIMIT = 56 * 2**20

MM_TM, MM_TN = 1024, 1024
ROW_TILE = 512
ATT_TQ, ATT_TK = 1024, 512
GLA_ROWS = 256
MOE_BLOCK = 128
MOE_W_SLOTS = 3
DISPATCH_TM = 1024
COMB_TM = 256


def _params(*sem):
    return pltpu.CompilerParams(dimension_semantics=sem, vmem_limit_bytes=VMEM_LIMIT)


def _split2(a):
    hi = a.astype(BF16)
    lo = (a - hi.astype(F32)).astype(BF16)
    return hi, lo


def _split3(a):
    hi = a.astype(BF16)
    r = a - hi.astype(F32)
    mid = r.astype(BF16)
    lo = (r - mid.astype(F32)).astype(BF16)
    return hi, mid, lo


def _pack_hi_lo(w_t, n_pad=LANES):
    n, k = w_t.shape
    wp = jnp.zeros((n_pad, k), F32).at[:n].set(w_t)
    hi, lo = _split2(wp)
    return jnp.concatenate([hi, lo], axis=0)


def _dot_nt(a, b):
    return lax.dot_general(a, b, (((1,), (1,)), ((), ())), preferred_element_type=F32)


def _dot_hi_lo(x, whl_ref, n_pad=LANES):
    xh, xl = _split2(x)
    a = _dot_nt(xh, whl_ref[...])
    b = _dot_nt(xl, whl_ref[:n_pad, :])
    return a[:, :n_pad] + a[:, n_pad:] + b


def _pack_rows(a):
    half = a.shape[1] // 2
    hi = lax.bitcast_convert_type(a[:, :half].astype(BF16).astype(F32), jnp.uint32)
    lo = lax.bitcast_convert_type(a[:, half:].astype(BF16).astype(F32), jnp.uint32)
    return hi | (lo >> 16)


def _unpack_rows(w):
    hi = lax.bitcast_convert_type(w & jnp.uint32(0xFFFF0000), F32)
    lo = lax.bitcast_convert_type(w << 16, F32)
    return jnp.concatenate([hi, lo], axis=1)


def _store_rows_as_lines(ref, first_line, packed):
    n_rows, width = packed.shape
    nt = width // LANES
    for j in range(nt):
        ref[pl.ds(first_line + j, n_rows, stride=nt), :] = packed[:, j * LANES:(j + 1) * LANES]


def _load_rows_from_lines(ref, first_line, n_rows, nt):
    return jnp.concatenate([ref[pl.ds(first_line + j, n_rows, stride=nt), :] for j in range(nt)], axis=1)


def _log_sigmoid(x):
    return -(jnp.maximum(-x, 0.0) + jnp.log1p(jnp.exp(-jnp.abs(x))))


def _cumsum_rows(a, incl_tri):
    n = a.shape[1]
    parts = jnp.concatenate(_split3(a), axis=1)
    c = jnp.dot(incl_tri, parts, preferred_element_type=F32)
    return c[:, :n] + c[:, n:2 * n] + c[:, 2 * n:]


def _tri(n, strict=False):
    r = lax.broadcasted_iota(jnp.int32, (n, n), 0)
    c = lax.broadcasted_iota(jnp.int32, (n, n), 1)
    return jnp.where((r > c) if strict else (r >= c), 1.0, 0.0).astype(BF16)


def _layer_norm_rows(z, g, b):
    mu = jnp.mean(z, axis=-1, keepdims=True)
    d = z - mu
    var = jnp.mean(d * d, axis=-1, keepdims=True)
    return d * lax.rsqrt(var + LN_EPS) * g + b


def _mm_kernel(x_ref, w_ref, o_ref, wb_ref):
    @pl.when(pl.program_id(1) == 0)
    def _():
        wb_ref[...] = w_ref[...].astype(BF16)

    o_ref[...] = lax.dot_general(x_ref[...].astype(BF16), wb_ref[...], (((1,), (1,)), ((), ())),
                                 preferred_element_type=F32).astype(o_ref.dtype)


def _matmul_nt(x, w_t, n, out_dtype):
    m, k = x.shape
    tm, tn = min(MM_TM, m), min(MM_TN, n)
    return pl.pallas_call(
        _mm_kernel,
        grid=(n // tn, m // tm),
        in_specs=[pl.BlockSpec((tm, k), lambda j, i: (i, 0)),
                  pl.BlockSpec((tn, k), lambda j, i: (j, 0))],
        out_specs=pl.BlockSpec((tm, tn), lambda j, i: (i, j)),
        out_shape=jax.ShapeDtypeStruct((m, n), out_dtype),
        scratch_shapes=[pltpu.VMEM((tn, k), BF16)],
        compiler_params=_params("arbitrary", "arbitrary"),
        name="dense_proj",
    )(x, w_t)


def _fox_gate_kernel(x_ref, w_ref, b_ref, c_ref, carry_ref):
    @pl.when(pl.program_id(0) == 0)
    def _():
        carry_ref[...] = jnp.zeros_like(carry_ref)

    ts = x_ref.shape[0]
    logits = _dot_hi_lo(x_ref[...], w_ref) + b_ref[...]
    log_f = _log_sigmoid(logits)
    c = _cumsum_rows(log_f, _tri(ts)) + carry_ref[...]
    carry_ref[...] = c[ts - 1:ts, :]
    c_ref[...] = c


def _fox_gates(x, w_f_t, b_f):
    s, d = x.shape
    h = w_f_t.shape[0]
    ts = min(ROW_TILE, s)
    whl = _pack_hi_lo(w_f_t)
    bias = jnp.zeros((1, LANES), F32).at[0, :h].set(b_f)
    return pl.pallas_call(
        _fox_gate_kernel,
        grid=(s // ts,),
        in_specs=[pl.BlockSpec((ts, d), lambda i: (i, 0)),
                  pl.BlockSpec((2 * LANES, d), lambda i: (0, 0)),
                  pl.BlockSpec((1, LANES), lambda i: (0, 0))],
        out_specs=pl.BlockSpec((ts, LANES), lambda i: (i, 0)),
        out_shape=jax.ShapeDtypeStruct((s, LANES), F32),
        scratch_shapes=[pltpu.VMEM((1, LANES), F32)],
        compiler_params=_params("arbitrary"),
        name="fox_gates",
    )(x, whl, bias)


def _fox_attn_kernel(q_ref, k_ref, v_ref, c_ref, o_ref, kaug_ref, vt_ref, qt_ref, st0_ref, st1_ref,
                     p0_ref, p1_ref, alpha_ref, bmax_ref, acc_ref, m_ref):
    h = pl.program_id(0)
    qi = pl.program_id(1)
    tq, dh = q_ref.shape
    s_len = k_ref.shape[0]
    tk = ATT_TK
    log2e = 1.4426950408889634
    scale = dh ** -0.5 * log2e

    def head_column(rows, n):
        lane = lax.broadcasted_iota(jnp.int32, (n, LANES), 1)
        return log2e * jnp.sum(jnp.where(lane == h, c_ref[rows, :], 0.0), axis=1, keepdims=True)

    def bias_columns(col, first, n):
        lane = lax.broadcasted_iota(jnp.int32, (n, LANES), 1)
        hi, mid, lo = _split3(col)
        ones_first = 3 - first
        out = jnp.where((lane >= ones_first) & (lane < ones_first + 3), 1.0, 0.0)
        out = jnp.where(lane == first, hi.astype(F32), out)
        out = jnp.where(lane == first + 1, mid.astype(F32), out)
        return jnp.where(lane == first + 2, lo.astype(F32), out).astype(BF16)

    @pl.when(qi == 0)
    def _():
        ones_row = jnp.where(lax.broadcasted_iota(jnp.int32, (16, tk), 0) == 0, 1.0, 0.0).astype(BF16)

        def build(ci, carry):
            rows = pl.ds(pl.multiple_of(ci * tk, tk), tk)
            kaug_ref[rows, :dh] = k_ref[rows, :]
            kaug_ref[rows, dh:] = bias_columns(-head_column(rows, tk), 0, tk)
            vt_ref[:dh, rows] = v_ref[rows, :].astype(F32).T.astype(BF16)
            vt_ref[dh:, rows] = ones_row
            return carry
        lax.fori_loop(0, s_len // tk, build, 0)

    q_rows = pl.ds(pl.multiple_of(qi * tq, tq), tq)
    q_aug = jnp.concatenate([q_ref[...].astype(F32) * scale,
                             bias_columns(head_column(q_rows, tq), 3, tq).astype(F32)], axis=1)
    qt_ref[...] = q_aug.T.astype(BF16)

    acc_ref[...] = jnp.zeros_like(acc_ref)
    m_ref[...] = jnp.full_like(m_ref, -jnp.inf)

    st_refs, p_refs = (st0_ref, st1_ref), (p0_ref, p1_ref)

    def scores(k_start, slot, diag_offset=None):
        st = jnp.dot(kaug_ref[pl.ds(k_start, tk), :], qt_ref[...], preferred_element_type=F32)
        if diag_offset is not None:
            kr = lax.broadcasted_iota(jnp.int32, (tk, tq), 0) + diag_offset
            qc = lax.broadcasted_iota(jnp.int32, (tk, tq), 1)
            st = jnp.where(kr <= qc, st, -jnp.inf)
        st_refs[slot][...] = st
        bmax_ref[slot] = jnp.max(st, axis=0, keepdims=True)

    def softmax(slot):
        m_prev = m_ref[...]
        m_new = jnp.maximum(m_prev, bmax_ref[slot])
        m_ref[...] = m_new
        p_refs[slot][...] = jnp.exp2(st_refs[slot][...] - m_new).astype(BF16)
        alpha_ref[slot] = jnp.exp2(m_prev - m_new)

    def values(k_start, slot):
        acc_ref[...] = alpha_ref[slot] * acc_ref[...] + jnp.dot(
            vt_ref[:, pl.ds(k_start, tk)], p_refs[slot][...], preferred_element_type=F32)

    n_diag = tq // tk
    assert n_diag == 2
    d0 = pl.multiple_of(qi * tq, tk)
    d1 = pl.multiple_of(qi * tq + tk, tk)
    n_full = qi * n_diag

    def full_start(i):
        return pl.multiple_of(jnp.minimum(i, jnp.maximum(n_full - 1, 0)) * tk, tk)

    scores(d0, 0, 0)
    scores(d1, 1, tk)
    softmax(0)
    scores(full_start(0), 0)
    softmax(1)
    values(d0, 0)

    def pair(j, carry):
        i0 = 2 * j
        prev = jnp.where(j == 0, d1, (i0 - 1) * tk)
        scores(full_start(i0 + 1), 1)
        softmax(0)
        values(pl.multiple_of(prev, tk), 1)
        scores(full_start(i0 + 2), 0)
        softmax(1)
        values(pl.multiple_of(i0 * tk, tk), 0)
        return carry

    lax.fori_loop(0, qi, pair, 0)
    last = jnp.where(qi == 0, d1, (n_full - 1) * tk)
    values(pl.multiple_of(last, tk), 1)

    o_ref[...] = (acc_ref[:dh, :] / acc_ref[dh:dh + 1, :]).T.astype(o_ref.dtype)


def _fox_attention(qkv, c, n_heads):
    s = qkv.shape[0]
    dh = FOX_HEAD_DIM
    tq = min(ATT_TQ, s)
    hh = n_heads
    return pl.pallas_call(
        _fox_attn_kernel,
        grid=(hh, s // tq),
        in_specs=[pl.BlockSpec((tq, dh), lambda h, i: (i, h)),
                  pl.BlockSpec((s, dh), lambda h, i: (0, hh + h)),
                  pl.BlockSpec((s, dh), lambda h, i: (0, 2 * hh + h)),
                  pl.BlockSpec((s, LANES), lambda h, i: (0, 0))],
        out_specs=pl.BlockSpec((tq, dh), lambda h, i: (i, h)),
        out_shape=jax.ShapeDtypeStruct((s, hh * dh), BF16),
        scratch_shapes=[pltpu.VMEM((s, 2 * dh), BF16),
                        pltpu.VMEM((dh + 16, s), BF16),
                        pltpu.VMEM((2 * dh, tq), BF16),
                        pltpu.VMEM((ATT_TK, tq), F32),
                        pltpu.VMEM((ATT_TK, tq), F32),
                        pltpu.VMEM((ATT_TK, tq), BF16),
                        pltpu.VMEM((ATT_TK, tq), BF16),
                        pltpu.VMEM((2, 1, tq), F32),
                        pltpu.VMEM((2, 1, tq), F32),
                        pltpu.VMEM((dh + 16, tq), F32),
                        pltpu.VMEM((1, tq), F32)],
        compiler_params=_params("arbitrary", "arbitrary"),
        name="fox_attention",
    )(qkv, qkv, qkv, c)


def _first_lane_eq(vals, target, lane):
    return jnp.min(jnp.where(vals == target, lane, LANES), axis=1, keepdims=True)


def _route_rows(x, w_ref, b_ref, carry_ref):
    tm = x.shape[0]
    neg = -jnp.inf
    logits = _dot_hi_lo(x, w_ref) + b_ref[...]
    lane = lax.broadcasted_iota(jnp.int32, (tm, LANES), 1)

    gl = jnp.where(lane < N_GROUPS, logits, neg)
    gmax = jnp.max(gl, axis=1, keepdims=True)
    gsum = jnp.sum(jnp.exp(gl - gmax), axis=1, keepdims=True)
    grp_p = 1.0 / gsum
    grp = _first_lane_eq(gl, gmax, lane)

    lo = N_GROUPS + grp * EXPERTS_PER_GROUP
    el = jnp.where((lane >= lo) & (lane < lo + EXPERTS_PER_GROUP), logits, neg)
    emax = jnp.max(el, axis=1, keepdims=True)
    esum = jnp.sum(jnp.exp(el - emax), axis=1, keepdims=True)
    idx1 = _first_lane_eq(el, emax, lane)
    el2 = jnp.where(lane == idx1, neg, el)
    emax2 = jnp.max(el2, axis=1, keepdims=True)
    idx2 = _first_lane_eq(el2, emax2, lane)
    p1 = 1.0 / esum
    p2 = jnp.exp(emax2 - emax) / esum
    psum = p1 + p2
    g1 = grp_p * (p1 / psum)
    g2 = grp_p * (p2 / psum)

    oh1 = lane == idx1
    oh2 = lane == idx2
    both = jnp.where(oh1 | oh2, 1.0, 0.0)
    before = jnp.dot(_tri(tm, strict=True), both.astype(BF16), preferred_element_type=F32)
    before = before + carry_ref[...]
    r1 = jnp.sum(jnp.where(oh1, before, 0.0), axis=1, keepdims=True)
    r2 = jnp.sum(jnp.where(oh2, before, 0.0), axis=1, keepdims=True)
    carry_ref[...] = carry_ref[...] + jnp.sum(both, axis=0, keepdims=True)

    e1 = (idx1 - N_GROUPS).astype(F32)
    e2 = (idx2 - N_GROUPS).astype(F32)
    out = jnp.where(lane == 0, e1, 0.0)
    out = jnp.where(lane == 1, e2, out)
    out = jnp.where(lane == 2, g1, out)
    out = jnp.where(lane == 3, g2, out)
    out = jnp.where(lane == 4, r1, out)
    return jnp.where(lane == 5, r2, out)


def _proj_ln_route_kernel(o_ref, w_ref, x_ref, g_ref, b_ref, wr_ref, br_ref,
                          y_ref, yp_ref, route_ref, cnt_ref, carry_ref):
    @pl.when(pl.program_id(0) == 0)
    def _():
        carry_ref[...] = jnp.zeros_like(carry_ref)

    mix = jnp.dot(o_ref[...], w_ref[...], preferred_element_type=F32)
    y = _layer_norm_rows(DEEPNORM_ALPHA * x_ref[...] + mix, g_ref[...], b_ref[...])
    y_ref[...] = y
    _store_rows_as_lines(yp_ref, 0, _pack_rows(y))
    route_ref[...] = _route_rows(y, wr_ref, br_ref, carry_ref)
    cnt_ref[...] = carry_ref[...]


def _proj_ln_route(o, w, x, g, b, w_group, b_group, w_expert, b_expert):
    s, d = x.shape
    kd = o.shape[1]
    tm = min(ROW_TILE, s)
    whl = _pack_hi_lo(jnp.concatenate([w_group.T, w_expert.T], axis=0))
    nb = N_GROUPS + N_EXPERTS
    bias = jnp.zeros((1, LANES), F32).at[0, :nb].set(jnp.concatenate([b_group, b_expert]))
    return pl.pallas_call(
        _proj_ln_route_kernel,
        grid=(s // tm,),
        in_specs=[pl.BlockSpec((tm, kd), lambda i: (i, 0)),
                  pl.BlockSpec((kd, d), lambda i: (0, 0)),
                  pl.BlockSpec((tm, d), lambda i: (i, 0)),
                  pl.BlockSpec((1, d), lambda i: (0, 0)),
                  pl.BlockSpec((1, d), lambda i: (0, 0)),
                  pl.BlockSpec((2 * LANES, d), lambda i: (0, 0)),
                  pl.BlockSpec((1, LANES), lambda i: (0, 0))],
        out_specs=[pl.BlockSpec((tm, d), lambda i: (i, 0)),
                   pl.BlockSpec((tm * (d // 2 // LANES), LANES), lambda i: (i, 0)),
                   pl.BlockSpec((tm, LANES), lambda i: (i, 0)),
                   pl.BlockSpec((1, LANES), lambda i: (0, 0))],
        out_shape=[jax.ShapeDtypeStruct((s, d), F32),
                   jax.ShapeDtypeStruct((s * (d // 2 // LANES), LANES), jnp.uint32),
                   jax.ShapeDtypeStruct((s, LANES), F32),
                   jax.ShapeDtypeStruct((1, LANES), F32)],
        scratch_shapes=[pltpu.VMEM((1, LANES), F32)],
        compiler_params=_params("arbitrary"),
        name="proj_ln_route",
    )(o, w, x, g.reshape(1, d), b.reshape(1, d), whl, bias)


def _dispatch_kernel(dest_ref, cnt_ref, pstart_ref, nv_ref, x_ref, xs_hbm, zero_ref, sem, zsem, *, nt):
    i = pl.program_id(0)
    tm = x_ref.shape[0] // nt
    bm = MOE_BLOCK
    n_exp = cnt_ref.shape[0]
    n_blocks = xs_hbm.shape[0] // (bm * nt)

    def lines(row, n_rows=1):
        return pl.ds(pl.multiple_of(row * nt, nt), n_rows * nt)

    def pad_copies(e, fn):
        cnt = cnt_ref[e]
        pad = (bm - cnt % bm) % bm
        pos = pstart_ref[e] + cnt
        size = bm // 2
        while size >= 1:
            @pl.when((pad & size) != 0)
            def _(pos=pos, size=size):
                fn(pltpu.make_async_copy(zero_ref.at[lines(0, size), :], xs_hbm.at[lines(pos, size), :], zsem))
            pos = pos + (pad & size)
            size //= 2

    def tail_copy(g):
        return pltpu.make_async_copy(zero_ref, xs_hbm.at[lines(g * bm, bm), :], zsem)

    def for_all_fills(fn):
        def per_expert(e, carry):
            pad_copies(e, fn)
            return carry
        lax.fori_loop(0, n_exp, per_expert, 0)

        def per_tail(g, carry):
            fn(tail_copy(g))
            return carry
        lax.fori_loop(nv_ref[0], n_blocks, per_tail, 0)

    @pl.when(i == 0)
    def _():
        zero_ref[...] = jnp.zeros_like(zero_ref)
        for_all_fills(lambda c: c.start())

    base = i * (TOP_K * tm)

    def body(r, carry):
        for k in range(TOP_K):
            dst = dest_ref[base + TOP_K * r + k]
            pltpu.make_async_copy(x_ref.at[lines(r), :], xs_hbm.at[lines(dst), :], sem).start(priority=k)
        return carry
    lax.fori_loop(0, tm, body, 0, unroll=8)

    @pl.when(i == 0)
    def _():
        for_all_fills(lambda c: c.wait())

    for _ in range(TOP_K):
        pltpu.make_async_copy(x_ref, xs_hbm.at[lines(0, tm), :], sem).wait()


def _dispatch(xp, n_rows, dest, cnt, pstart, n_valid, n_slots):
    nt = xp.shape[0] // n_rows
    tm = min(DISPATCH_TM, n_rows)
    grid_spec = pltpu.PrefetchScalarGridSpec(
        num_scalar_prefetch=4,
        grid=(n_rows // tm,),
        in_specs=[pl.BlockSpec((tm * nt, LANES), lambda i, *_: (i, 0))],
        out_specs=pl.BlockSpec(memory_space=pl.ANY),
        scratch_shapes=[pltpu.VMEM((MOE_BLOCK * nt, LANES), xp.dtype),
                        pltpu.SemaphoreType.DMA,
                        pltpu.SemaphoreType.DMA],
    )
    return pl.pallas_call(
        functools.partial(_dispatch_kernel, nt=nt),
        grid_spec=grid_spec,
        out_shape=jax.ShapeDtypeStruct((n_slots * nt, LANES), xp.dtype),
        compiler_params=_params("arbitrary"),
        name="moe_dispatch",
    )(dest, cnt, pstart, n_valid, xp)


def _expert_kernel(nblk_ref, bstart_ref, nv_ref, xs_hbm, wg_hbm, wu_hbm, wd_hbm, y_hbm,
                   xin_ref, yout_ref, wgf_ref, wuf_ref, wdf_ref, wgb_ref, wub_ref, wdb_ref,
                   in_sem, out_sem, w_sem, *, layer):
    e = pl.program_id(0)
    n_exp = pl.num_programs(0)
    n_valid = nv_ref[0]
    bm = MOE_BLOCK
    blk_lines = xin_ref.shape[1]
    nt = blk_lines // bm
    n_blocks = y_hbm.shape[0] // blk_lines
    ns = MOE_W_SLOTS

    half = wd_hbm.shape[2] // 2

    def w_copies(ex, slot):
        return (pltpu.make_async_copy(wg_hbm.at[layer, ex], wgf_ref.at[slot], w_sem.at[slot]),
                pltpu.make_async_copy(wu_hbm.at[layer, ex], wuf_ref.at[slot], w_sem.at[slot]),
                pltpu.make_async_copy(wd_hbm.at[layer, ex, pl.ds(0, half)],
                                      wdf_ref.at[slot, pl.ds(0, half)], w_sem.at[slot]),
                pltpu.make_async_copy(wd_hbm.at[layer, ex, pl.ds(half, half)],
                                      wdf_ref.at[slot, pl.ds(half, half)], w_sem.at[slot]))

    def start_weights(ex):
        exc = jnp.minimum(ex, n_exp - 1)

        @pl.when((ex < n_exp) & (nblk_ref[exc] > 0))
        def _():
            for piece, c in enumerate(w_copies(exc, exc % ns)):
                c.start(priority=piece % 2)

    @pl.when(e == 0)
    def _():
        for ahead in range(ns - 1):
            start_weights(ahead)

    start_weights(e + ns - 1)

    def block_lines(g):
        return pl.ds(pl.multiple_of(g * blk_lines, blk_lines), blk_lines)

    def in_copy(g, slot):
        return pltpu.make_async_copy(xs_hbm.at[block_lines(g), :], xin_ref.at[slot], in_sem.at[slot])

    def out_copy(g, slot):
        return pltpu.make_async_copy(yout_ref.at[slot], y_hbm.at[block_lines(g), :], out_sem.at[slot])

    @pl.when((e == 0) & (n_valid > 0))
    def _():
        in_copy(0, 0).start(priority=0)

    nb = nblk_ref[e]
    g0 = bstart_ref[e]

    @pl.when(nb > 0)
    def _():
        w_slot = e % ns
        for c in w_copies(e, w_slot):
            c.wait()
        wgb_ref[...] = wgf_ref[w_slot].astype(BF16)
        wub_ref[...] = wuf_ref[w_slot].astype(BF16)
        wdb_ref[...] = wdf_ref[w_slot].astype(BF16)

        def block(j, carry):
            g = g0 + j
            slot = g % 2
            in_copy(g, slot).wait()

            @pl.when(g + 1 < n_valid)
            def _():
                in_copy(g + 1, 1 - slot).start(priority=0)

            @pl.when(g >= 2)
            def _():
                out_copy(g - 2, slot).wait()

            xb = _unpack_rows(_load_rows_from_lines(xin_ref.at[slot], 0, bm, nt)).astype(BF16)
            gate = jnp.dot(xb, wgb_ref[...], preferred_element_type=F32)
            up = jnp.dot(xb, wub_ref[...], preferred_element_type=F32)
            hid = (gate * jax.nn.sigmoid(gate) * up).astype(BF16)
            y = jnp.dot(hid, wdb_ref[...], preferred_element_type=F32)
            _store_rows_as_lines(yout_ref.at[slot], 0, _pack_rows(y))
            out_copy(g, slot).start(priority=1)
            return carry
        lax.fori_loop(0, nb, block, 0)

    @pl.when(e == pl.num_programs(0) - 1)
    def _():
        for back in (2, 1):
            @pl.when(n_valid >= back)
            def _(back=back):
                out_copy(n_valid - back, (n_valid - back) % 2).wait()

        yout_ref[0] = jnp.zeros(yout_ref.shape[1:], yout_ref.dtype)

        def fill(g, carry):
            out_copy(g, 0).start()
            return carry
        lax.fori_loop(n_valid, n_blocks, fill, 0)

        def drain(g, carry):
            out_copy(g, 0).wait()
            return carry
        lax.fori_loop(n_valid, n_blocks, drain, 0)


def _experts(xs, nblk, bstart, n_valid, w_gate, w_up, w_down, layer):
    _, n_exp, d, f = w_gate.shape
    bm = MOE_BLOCK
    nt = d // 2 // LANES
    ns = MOE_W_SLOTS
    grid_spec = pltpu.PrefetchScalarGridSpec(
        num_scalar_prefetch=3,
        grid=(n_exp,),
        in_specs=[pl.BlockSpec(memory_space=pl.ANY)] * 4,
        out_specs=pl.BlockSpec(memory_space=pl.ANY),
        scratch_shapes=[pltpu.VMEM((2, bm * nt, LANES), jnp.uint32),
                        pltpu.VMEM((2, bm * nt, LANES), jnp.uint32),
                        pltpu.VMEM((ns, d, f), F32),
                        pltpu.VMEM((ns, d, f), F32),
                        pltpu.VMEM((ns, f, d), F32),
                        pltpu.VMEM((d, f), BF16),
                        pltpu.VMEM((d, f), BF16),
                        pltpu.VMEM((f, d), BF16),
                        pltpu.SemaphoreType.DMA((2,)),
                        pltpu.SemaphoreType.DMA((2,)),
                        pltpu.SemaphoreType.DMA((ns,))],
    )
    return pl.pallas_call(
        functools.partial(_expert_kernel, layer=layer),
        grid_spec=grid_spec,
        out_shape=jax.ShapeDtypeStruct(xs.shape, jnp.uint32),
        compiler_params=_params("arbitrary"),
        name="moe_experts",
    )(nblk, bstart, n_valid, xs, w_gate, w_up, w_down)


def _combine_ln_kernel(dest_ref, y_hbm, x_ref, route_ref, g_ref, b_ref, o_ref, yg_ref, sem):
    i = pl.program_id(0)
    n = pl.num_programs(0)
    tm = x_ref.shape[0]
    nt = yg_ref.shape[1] // (TOP_K * tm)

    def lines(row, n_rows=1):
        return pl.ds(pl.multiple_of(row * nt, nt), n_rows * nt)

    def start_gather(step, slot):
        base = step * (TOP_K * tm)

        def body(r, carry):
            for k in range(TOP_K):
                src = dest_ref[base + TOP_K * r + k]
                pltpu.make_async_copy(y_hbm.at[lines(src), :], yg_ref.at[slot, lines(k * tm + r), :],
                                      sem.at[slot]).start(priority=k)
            return carry
        lax.fori_loop(0, tm, body, 0, unroll=8)

    def wait_gather(slot):
        pltpu.make_async_copy(y_hbm.at[lines(0, TOP_K * tm), :], yg_ref.at[slot], sem.at[slot]).wait()

    @pl.when(i == 0)
    def _():
        start_gather(0, 0)

    @pl.when(i + 1 < n)
    def _():
        start_gather(i + 1, (i + 1) % 2)

    slot = i % 2
    wait_gather(slot)
    route = route_ref[...]
    lane = lax.broadcasted_iota(jnp.int32, route.shape, 1)
    g1 = jnp.sum(jnp.where(lane == 2, route, 0.0), axis=1, keepdims=True)
    g2 = jnp.sum(jnp.where(lane == 3, route, 0.0), axis=1, keepdims=True)
    y1 = _unpack_rows(_load_rows_from_lines(yg_ref.at[slot], 0, tm, nt))
    y2 = _unpack_rows(_load_rows_from_lines(yg_ref.at[slot], tm * nt, tm, nt))
    z = DEEPNORM_ALPHA * x_ref[...] + (y1 * g1 + y2 * g2)
    o_ref[...] = _layer_norm_rows(z, g_ref[...], b_ref[...])


def _combine_ln(y, dest, x, route, g, b):
    t, d = x.shape
    tm = min(COMB_TM, t)
    grid_spec = pltpu.PrefetchScalarGridSpec(
        num_scalar_prefetch=1,
        grid=(t // tm,),
        in_specs=[pl.BlockSpec(memory_space=pl.ANY),
                  pl.BlockSpec((tm, d), lambda i, dest: (i, 0)),
                  pl.BlockSpec((tm, LANES), lambda i, dest: (i, 0)),
                  pl.BlockSpec((1, d), lambda i, dest: (0, 0)),
                  pl.BlockSpec((1, d), lambda i, dest: (0, 0))],
        out_specs=pl.BlockSpec((tm, d), lambda i, dest: (i, 0)),
        scratch_shapes=[pltpu.VMEM((2, TOP_K * tm * (d // 2 // LANES), LANES), y.dtype),
                        pltpu.SemaphoreType.DMA((2,))],
    )
    return pl.pallas_call(
        _combine_ln_kernel,
        grid_spec=grid_spec,
        out_shape=jax.ShapeDtypeStruct((t, d), F32),
        compiler_params=_params("arbitrary"),
        name="moe_combine_ln",
    )(dest, y, x, route, g.reshape(1, d), b.reshape(1, d))


def _moe_ffn_ln(x, xp, route, counts, w_gate, w_up, w_down, layer, ln_g, ln_b):
    t, d = x.shape
    bm = MOE_BLOCK
    eid = route[:, 0:TOP_K].astype(jnp.int32)
    rank = route[:, 4:4 + TOP_K].astype(jnp.int32)
    cnt = counts[0, N_GROUPS:N_GROUPS + N_EXPERTS].astype(jnp.int32)
    nblk = (cnt + bm - 1) // bm
    bend = jnp.cumsum(nblk)
    bstart = bend - nblk
    pstart = bstart * bm
    onehot = eid[:, :, None] == jnp.arange(N_EXPERTS, dtype=jnp.int32)
    dest = (jnp.sum(jnp.where(onehot, pstart, 0), axis=-1) + rank).reshape(-1)
    n_valid = bend[-1:]
    n_slots = (-(-t * TOP_K // bm) + N_EXPERTS) * bm
    xs = _dispatch(xp, t, dest, cnt, pstart, n_valid, n_slots)
    y = _experts(xs, nblk, bstart, n_valid, w_gate, w_up, w_down, layer)
    return _combine_ln(y, dest, x, route, ln_g, ln_b)


def _gla_log_alpha(x, wl_ref, wu_ref, b_ref):
    g_low = _dot_hi_lo(x, wl_ref)
    n = wu_ref.shape[1] // 2
    gh, gl = _split2(g_low)
    a = jnp.dot(gh, wu_ref[...], preferred_element_type=F32)
    c = jnp.dot(gl, wu_ref[:, :n], preferred_element_type=F32)
    return _log_sigmoid(a[:, :n] + a[:, n:] + c + b_ref[...]) / GLA_GATE_TAU


def _gla_kernel(q_ref, k_ref, v_ref, r_ref, x_ref, wl_ref, wu_ref, bg_ref, g_ref, o_ref, state_ref):
    @pl.when(pl.program_id(0) == 0)
    def _():
        state_ref[...] = jnp.zeros_like(state_ref)

    log_alpha = _gla_log_alpha(x_ref[...], wl_ref, wu_ref, bg_ref)

    rows = q_ref.shape[0]
    nh, dk, dv = state_ref.shape
    cs = GLA_CHUNK
    n_chunks = rows // cs
    rr = lax.broadcasted_iota(jnp.int32, (rows, rows), 0)
    cc = lax.broadcasted_iota(jnp.int32, (rows, rows), 1)
    causal = (rr // cs == cc // cs) & (rr >= cc)
    tri = jnp.where(causal, 1.0, 0.0).astype(BF16)
    lane_chunk = lax.broadcasted_iota(jnp.int32, (dk, rows), 1) // cs

    def chunk_row(a, i):
        return jnp.concatenate([jnp.broadcast_to(a[c * cs + i:c * cs + i + 1, :], (cs, a.shape[1]))
                                for c in range(n_chunks)], axis=0)

    for h in range(nh):
        kc = pl.ds(h * dk, dk)
        vc = pl.ds(h * dv, dv)
        b = _cumsum_rows(log_alpha[:, h * dk:(h + 1) * dk], tri)
        b_mid = chunk_row(b, cs // 2 - 1)
        b_last = chunk_row(b, cs - 1)
        q = q_ref[:, kc].astype(F32) * (dk ** -0.5)
        k = k_ref[:, kc].astype(F32)
        v = v_ref[:, vc]
        qa = (q * jnp.exp(b - b_mid)).astype(BF16)
        ka = (k * jnp.exp(b_mid - b)).astype(BF16)
        a = jnp.where(causal, _dot_nt(qa, ka), 0.0)
        o_intra = jnp.dot(a.astype(BF16), v, preferred_element_type=F32)
        q_inter = (q * jnp.exp(b)).astype(BF16)
        k_end_t = (k * jnp.exp(b_last - b)).T
        b_t = b.T
        state = state_ref[h]
        outs = []
        for c in range(n_chunks):
            sl = slice(c * cs, (c + 1) * cs)
            outs.append(o_intra[sl] + jnp.dot(q_inter[sl], state.astype(BF16), preferred_element_type=F32))
            kv = jnp.dot(jnp.where(lane_chunk == c, k_end_t, 0.0).astype(BF16), v,
                         preferred_element_type=F32)
            decay = jnp.exp(b_t[:, (c + 1) * cs - 1:(c + 1) * cs])
            state = decay * state + kv
        state_ref[h] = state
        o = jnp.concatenate(outs, axis=0)
        o = o * lax.rsqrt(jnp.mean(o * o, axis=-1, keepdims=True) + RMS_EPS) * g_ref[...]
        r = r_ref[:, vc].astype(F32)
        o_ref[:, vc] = (o * (r * jax.nn.sigmoid(r))).astype(o_ref.dtype)


def _gla(proj, x, w_low_t, w_gate_up, b_gate, norm_g):
    s, d = x.shape
    rank, dk_all = w_gate_up.shape
    nh = GLA_HEADS
    dk = dk_all // nh
    dv = norm_g.shape[0]
    dv_all = nh * dv
    assert 2 * dk_all == dv_all
    rows = min(GLA_ROWS, s)
    wl = _pack_hi_lo(w_low_t)
    wu_pad = jnp.zeros((LANES, dk_all), F32).at[:rank].set(w_gate_up)
    wu = jnp.concatenate(_split2(wu_pad), axis=1)
    return pl.pallas_call(
        _gla_kernel,
        grid=(s // rows,),
        in_specs=[pl.BlockSpec((rows, dk_all), lambda i: (i, 0)),
                  pl.BlockSpec((rows, dk_all), lambda i: (i, 1)),
                  pl.BlockSpec((rows, dv_all), lambda i: (i, 1)),
                  pl.BlockSpec((rows, dv_all), lambda i: (i, 2)),
                  pl.BlockSpec((rows, d), lambda i: (i, 0)),
                  pl.BlockSpec((2 * LANES, d), lambda i: (0, 0)),
                  pl.BlockSpec((LANES, 2 * dk_all), lambda i: (0, 0)),
                  pl.BlockSpec((1, dk_all), lambda i: (0, 0)),
                  pl.BlockSpec((1, dv), lambda i: (0, 0))],
        out_specs=pl.BlockSpec((rows, dv_all), lambda i: (i, 0)),
        out_shape=jax.ShapeDtypeStruct((s, dv_all), BF16),
        scratch_shapes=[pltpu.VMEM((nh, dk, dv), F32)],
        compiler_params=_params("arbitrary"),
        name="gla_chunks",
    )(proj, proj, proj, proj, x, wl, wu, b_gate.reshape(1, dk_all), norm_g.reshape(1, dv))


def kernel(x, fox_w_in, fox_b_f, fox_w_o, gla_w_in, gla_w_gate_up, gla_b_gate, gla_norm_g, gla_w_o,
           ln_mix_g, ln_mix_b, ln_ffn_g, ln_ffn_b, moe_w_group, moe_b_group, moe_w_expert,
           moe_b_expert, moe_w_gate, moe_w_up, moe_w_down):
    bsz, s, d = x.shape
    outs = []
    for bi in range(bsz):
        xt = x[bi]
        for i in range(DEPTH):
            j = i // 2
            if i % 2 == 0:
                w_in_t = fox_w_in[j].T
                qkv = _matmul_nt(xt, w_in_t, 3 * d, BF16)
                c = _fox_gates(xt, w_in_t[3 * d:], fox_b_f[j])
                o = _fox_attention(qkv, c, FOX_HEADS)
                w_o = fox_w_o[j]
            else:
                w_in_t = gla_w_in[j].T
                n_main = w_in_t.shape[0] - gla_w_gate_up.shape[1]
                proj = _matmul_nt(xt, w_in_t, n_main, BF16)
                o = _gla(proj, xt, w_in_t[n_main:], gla_w_gate_up[j], gla_b_gate[j], gla_norm_g[j])
                w_o = gla_w_o[j]
            xt, xp, route, counts = _proj_ln_route(o, w_o.astype(BF16), xt, ln_mix_g[i], ln_mix_b[i], moe_w_group[i],
                                                   moe_b_group[i], moe_w_expert[i], moe_b_expert[i])
            xt = _moe_ffn_ln(xt, xp, route, counts, moe_w_gate, moe_w_up, moe_w_down, i, ln_ffn_g[i], ln_ffn_b[i])
        outs.append(xt)
    return outs[0].reshape(1, s, d) if bsz == 1 else jnp.stack(outs, axis=0)
```

```python
import functools

import jax
import jax.numpy as jnp
from jax import lax
from jax.experimental import pallas as pl
from jax.experimental.pallas import tpu as pltpu

F32 = jnp.float32
BF16 = jnp.bfloat16

DEPTH = 2
FOX_HEADS = 16
FOX_HEAD_DIM = 128
GLA_HEADS = 4
GLA_CHUNK = 64
GLA_GATE_TAU = 16.0
N_GROUPS = 8
EXPERTS_PER_GROUP = 8
N_EXPERTS = N_GROUPS * EXPERTS_PER_GROUP
TOP_K = 2
DEEPNORM_ALPHA = (2 * DEPTH) ** 0.25
LN_EPS = 1e-5
RMS_EPS = 1e-6

LANES = 128
VMEM_LIMIT = 56 * 2**20

MM_TM, MM_TN = 1024, 1024
ROW_TILE = 512
ATT_TQ, ATT_TK = 1024, 512
ATT_GROUP = 256
GLA_ROWS = 256
MOE_BLOCK = 128
MOE_W_SLOTS = 3
DISPATCH_TM = 1024
COMB_TM = 256


def _params(*sem):
    return pltpu.CompilerParams(dimension_semantics=sem, vmem_limit_bytes=VMEM_LIMIT)


def _split2(a):
    hi = a.astype(BF16)
    lo = (a - hi.astype(F32)).astype(BF16)
    return hi, lo


def _split3(a):
    hi = a.astype(BF16)
    r = a - hi.astype(F32)
    mid = r.astype(BF16)
    lo = (r - mid.astype(F32)).astype(BF16)
    return hi, mid, lo


def _pack_hi_lo(w_t, n_pad=LANES):
    n, k = w_t.shape
    wp = jnp.zeros((n_pad, k), F32).at[:n].set(w_t)
    hi, lo = _split2(wp)
    return jnp.concatenate([hi, lo], axis=0)


def _dot_nt(a, b):
    return lax.dot_general(a, b, (((1,), (1,)), ((), ())), preferred_element_type=F32)


def _dot_hi_lo(x, whl_ref, n_pad=LANES):
    xh, xl = _split2(x)
    a = _dot_nt(xh, whl_ref[...])
    b = _dot_nt(xl, whl_ref[:n_pad, :])
    return a[:, :n_pad] + a[:, n_pad:] + b


def _pack_rows(a):
    half = a.shape[1] // 2
    hi = lax.bitcast_convert_type(a[:, :half].astype(BF16).astype(F32), jnp.uint32)
    lo = lax.bitcast_convert_type(a[:, half:].astype(BF16).astype(F32), jnp.uint32)
    return hi | (lo >> 16)


def _unpack_rows(w):
    hi = lax.bitcast_convert_type(w & jnp.uint32(0xFFFF0000), F32)
    lo = lax.bitcast_convert_type(w << 16, F32)
    return jnp.concatenate([hi, lo], axis=1)


def _store_rows_as_lines(ref, first_line, packed):
    n_rows, width = packed.shape
    nt = width // LANES
    for j in range(nt):
        ref[pl.ds(first_line + j, n_rows, stride=nt), :] = packed[:, j * LANES:(j + 1) * LANES]


def _load_rows_from_lines(ref, first_line, n_rows, nt):
    return jnp.concatenate([ref[pl.ds(first_line + j, n_rows, stride=nt), :] for j in range(nt)], axis=1)


def _log_sigmoid(x):
    return -(jnp.maximum(-x, 0.0) + jnp.log1p(jnp.exp(-jnp.abs(x))))


def _cumsum_rows(a, incl_tri):
    n = a.shape[1]
    parts = jnp.concatenate(_split3(a), axis=1)
    c = jnp.dot(incl_tri, parts, preferred_element_type=F32)
    return c[:, :n] + c[:, n:2 * n] + c[:, 2 * n:]


def _tri(n, strict=False):
    r = lax.broadcasted_iota(jnp.int32, (n, n), 0)
    c = lax.broadcasted_iota(jnp.int32, (n, n), 1)
    return jnp.where((r > c) if strict else (r >= c), 1.0, 0.0).astype(BF16)


def _layer_norm_rows(z, g, b):
    mu = jnp.mean(z, axis=-1, keepdims=True)
    d = z - mu
    var = jnp.mean(d * d, axis=-1, keepdims=True)
    return d * lax.rsqrt(var + LN_EPS) * g + b


def _mm_kernel(x_ref, w_ref, o_ref, wb_ref):
    @pl.when(pl.program_id(1) == 0)
    def _():
        wb_ref[...] = w_ref[...].astype(BF16)

    o_ref[...] = lax.dot_general(x_ref[...].astype(BF16), wb_ref[...], (((1,), (1,)), ((), ())),
                                 preferred_element_type=F32).astype(o_ref.dtype)


def _matmul_nt(x, w_t, n, out_dtype):
    m, k = x.shape
    tm, tn = min(MM_TM, m), min(MM_TN, n)
    return pl.pallas_call(
        _mm_kernel,
        grid=(n // tn, m // tm),
        in_specs=[pl.BlockSpec((tm, k), lambda j, i: (i, 0)),
                  pl.BlockSpec((tn, k), lambda j, i: (j, 0))],
        out_specs=pl.BlockSpec((tm, tn), lambda j, i: (i, j)),
        out_shape=jax.ShapeDtypeStruct((m, n), out_dtype),
        scratch_shapes=[pltpu.VMEM((tn, k), BF16)],
        compiler_params=_params("arbitrary", "arbitrary"),
        name="dense_proj",
    )(x, w_t)


def _fox_gate_kernel(x_ref, w_ref, b_ref, c_ref, carry_ref):
    @pl.when(pl.program_id(0) == 0)
    def _():
        carry_ref[...] = jnp.zeros_like(carry_ref)

    ts = x_ref.shape[0]
    logits = _dot_hi_lo(x_ref[...], w_ref) + b_ref[...]
    log_f = _log_sigmoid(logits)
    c = _cumsum_rows(log_f, _tri(ts)) + carry_ref[...]
    carry_ref[...] = c[ts - 1:ts, :]
    c_ref[...] = c


def _fox_gates(x, w_f_t, b_f):
    s, d = x.shape
    h = w_f_t.shape[0]
    ts = min(ROW_TILE, s)
    whl = _pack_hi_lo(w_f_t)
    bias = jnp.zeros((1, LANES), F32).at[0, :h].set(b_f)
    return pl.pallas_call(
        _fox_gate_kernel,
        grid=(s // ts,),
        in_specs=[pl.BlockSpec((ts, d), lambda i: (i, 0)),
                  pl.BlockSpec((2 * LANES, d), lambda i: (0, 0)),
                  pl.BlockSpec((1, LANES), lambda i: (0, 0))],
        out_specs=pl.BlockSpec((ts, LANES), lambda i: (i, 0)),
        out_shape=jax.ShapeDtypeStruct((s, LANES), F32),
        scratch_shapes=[pltpu.VMEM((1, LANES), F32)],
        compiler_params=_params("arbitrary"),
        name="fox_gates",
    )(x, whl, bias)


def _fox_attn_kernel(q_ref, k_ref, v_ref, c_ref, o_ref, kaug_ref, vt_ref, qt_ref, st0_ref, st1_ref,
                     p0_ref, p1_ref, alpha_ref, bmax_ref, acc_ref, m_ref):
    h = pl.program_id(0)
    qi = pl.program_id(1)
    tq, dh = q_ref.shape
    s_len = k_ref.shape[0]
    tk = ATT_TK
    log2e = 1.4426950408889634
    scale = dh ** -0.5 * log2e

    def head_column(rows, n):
        lane = lax.broadcasted_iota(jnp.int32, (n, LANES), 1)
        return log2e * jnp.sum(jnp.where(lane == h, c_ref[rows, :], 0.0), axis=1, keepdims=True)

    def bias_columns(col, first, n):
        lane = lax.broadcasted_iota(jnp.int32, (n, LANES), 1)
        hi, mid, lo = _split3(col)
        ones_first = 3 - first
        out = jnp.where((lane >= ones_first) & (lane < ones_first + 3), 1.0, 0.0)
        out = jnp.where(lane == first, hi.astype(F32), out)
        out = jnp.where(lane == first + 1, mid.astype(F32), out)
        return jnp.where(lane == first + 2, lo.astype(F32), out).astype(BF16)

    @pl.when(qi == 0)
    def _():
        ones_row = jnp.where(lax.broadcasted_iota(jnp.int32, (16, tk), 0) == 0, 1.0, 0.0).astype(BF16)

        def build(ci, carry):
            rows = pl.ds(pl.multiple_of(ci * tk, tk), tk)
            kaug_ref[rows, :dh] = k_ref[rows, :]
            kaug_ref[rows, dh:] = bias_columns(-head_column(rows, tk), 0, tk)
            vt_ref[:dh, rows] = v_ref[rows, :].astype(F32).T.astype(BF16)
            vt_ref[dh:, rows] = ones_row
            return carry
        lax.fori_loop(0, s_len // tk, build, 0)

    q_rows = pl.ds(pl.multiple_of(qi * tq, tq), tq)
    q_aug = jnp.concatenate([q_ref[...].astype(F32) * scale,
                             bias_columns(head_column(q_rows, tq), 3, tq).astype(F32)], axis=1)
    qt_ref[...] = q_aug.T.astype(BF16)

    acc_ref[...] = jnp.zeros_like(acc_ref)
    m_ref[...] = jnp.full_like(m_ref, -jnp.inf)

    st_refs, p_refs = (st0_ref, st1_ref), (p0_ref, p1_ref)

    gw = ATT_GROUP
    groups = [pl.ds(g * gw, gw) for g in range(tq // gw)]

    def scores(k_start, slot, diag_offset=None, lanes=None):
        for gl in ([lanes] if lanes is not None else groups):
            st = jnp.dot(kaug_ref[pl.ds(k_start, tk), :], qt_ref[:, gl], preferred_element_type=F32)
            if diag_offset is not None:
                kr = lax.broadcasted_iota(jnp.int32, (tk, gw), 0) + diag_offset
                qc = lax.broadcasted_iota(jnp.int32, (tk, gw), 1) + gl.start
                st = jnp.where(kr <= qc, st, -jnp.inf)
            st_refs[slot][:, gl] = st
            bmax_ref[slot, :, gl] = jnp.max(st, axis=0, keepdims=True)

    def softmax(slot, lanes=None):
        for gl in ([lanes] if lanes is not None else groups):
            m_prev = m_ref[:, gl]
            m_new = jnp.maximum(m_prev, bmax_ref[slot, :, gl])
            m_ref[:, gl] = m_new
            p_refs[slot][:, gl] = jnp.exp2(st_refs[slot][:, gl] - m_new).astype(BF16)
            alpha_ref[slot, :, gl] = jnp.exp2(m_prev - m_new)

    def values(k_start, slot, lanes=None):
        for gl in ([lanes] if lanes is not None else groups):
            acc_ref[:, gl] = alpha_ref[slot, :, gl] * acc_ref[:, gl] + jnp.dot(
                vt_ref[:, pl.ds(k_start, tk)], p_refs[slot][:, gl], preferred_element_type=F32)

    n_diag = tq // tk
    assert n_diag == 2
    d0 = pl.multiple_of(qi * tq, tk)
    d1 = pl.multiple_of(qi * tq + tk, tk)
    n_full = qi * n_diag

    def full_start(i):
        return pl.multiple_of(jnp.minimum(i, jnp.maximum(n_full - 1, 0)) * tk, tk)

    scores(d0, 0, 0)
    scores(d1, 1, tk)
    softmax(0)
    scores(full_start(0), 0)
    softmax(1)
    values(d0, 0)

    def pair(j, carry):
        i0 = 2 * j
        prev = jnp.where(j == 0, d1, (i0 - 1) * tk)
        for gl in groups:
            scores(full_start(i0 + 1), 1, lanes=gl)
            softmax(0, lanes=gl)
            values(pl.multiple_of(prev, tk), 1, lanes=gl)
        for gl in groups:
            scores(full_start(i0 + 2), 0, lanes=gl)
            softmax(1, lanes=gl)
            values(pl.multiple_of(i0 * tk, tk), 0, lanes=gl)
        return carry

    lax.fori_loop(0, qi, pair, 0)
    last = jnp.where(qi == 0, d1, (n_full - 1) * tk)
    values(pl.multiple_of(last, tk), 1)

    o_ref[...] = (acc_ref[:dh, :] / acc_ref[dh:dh + 1, :]).T.astype(o_ref.dtype)


def _fox_attention(qkv, c, n_heads):
    s = qkv.shape[0]
    dh = FOX_HEAD_DIM
    tq = min(ATT_TQ, s)
    hh = n_heads
    return pl.pallas_call(
        _fox_attn_kernel,
        grid=(hh, s // tq),
        in_specs=[pl.BlockSpec((tq, dh), lambda h, i: (i, h)),
                  pl.BlockSpec((s, dh), lambda h, i: (0, hh + h)),
                  pl.BlockSpec((s, dh), lambda h, i: (0, 2 * hh + h)),
                  pl.BlockSpec((s, LANES), lambda h, i: (0, 0))],
        out_specs=pl.BlockSpec((tq, dh), lambda h, i: (i, h)),
        out_shape=jax.ShapeDtypeStruct((s, hh * dh), BF16),
        scratch_shapes=[pltpu.VMEM((s, 2 * dh), BF16),
                        pltpu.VMEM((dh + 16, s), BF16),
                        pltpu.VMEM((2 * dh, tq), BF16),
                        pltpu.VMEM((ATT_TK, tq), F32),
                        pltpu.VMEM((ATT_TK, tq), F32),
                        pltpu.VMEM((ATT_TK, tq), BF16),
                        pltpu.VMEM((ATT_TK, tq), BF16),
                        pltpu.VMEM((2, 1, tq), F32),
                        pltpu.VMEM((2, 1, tq), F32),
                        pltpu.VMEM((dh + 16, tq), F32),
                        pltpu.VMEM((1, tq), F32)],
        compiler_params=_params("arbitrary", "arbitrary"),
        name="fox_attention",
    )(qkv, qkv, qkv, c)


def _first_lane_eq(vals, target, lane):
    return jnp.min(jnp.where(vals == target, lane, LANES), axis=1, keepdims=True)


def _route_rows(x, w_ref, b_ref, carry_ref):
    tm = x.shape[0]
    neg = -jnp.inf
    logits = _dot_hi_lo(x, w_ref) + b_ref[...]
    lane = lax.broadcasted_iota(jnp.int32, (tm, LANES), 1)

    gl = jnp.where(lane < N_GROUPS, logits, neg)
    gmax = jnp.max(gl, axis=1, keepdims=True)
    gsum = jnp.sum(jnp.exp(gl - gmax), axis=1, keepdims=True)
    grp_p = 1.0 / gsum
    grp = _first_lane_eq(gl, gmax, lane)

    lo = N_GROUPS + grp * EXPERTS_PER_GROUP
    el = jnp.where((lane >= lo) & (lane < lo + EXPERTS_PER_GROUP), logits, neg)
    emax = jnp.max(el, axis=1, keepdims=True)
    esum = jnp.sum(jnp.exp(el - emax), axis=1, keepdims=True)
    idx1 = _first_lane_eq(el, emax, lane)
    el2 = jnp.where(lane == idx1, neg, el)
    emax2 = jnp.max(el2, axis=1, keepdims=True)
    idx2 = _first_lane_eq(el2, emax2, lane)
    p1 = 1.0 / esum
    p2 = jnp.exp(emax2 - emax) / esum
    psum = p1 + p2
    g1 = grp_p * (p1 / psum)
    g2 = grp_p * (p2 / psum)

    oh1 = lane == idx1
    oh2 = lane == idx2
    both = jnp.where(oh1 | oh2, 1.0, 0.0)
    before = jnp.dot(_tri(tm, strict=True), both.astype(BF16), preferred_element_type=F32)
    before = before + carry_ref[...]
    r1 = jnp.sum(jnp.where(oh1, before, 0.0), axis=1, keepdims=True)
    r2 = jnp.sum(jnp.where(oh2, before, 0.0), axis=1, keepdims=True)
    carry_ref[...] = carry_ref[...] + jnp.sum(both, axis=0, keepdims=True)

    e1 = (idx1 - N_GROUPS).astype(F32)
    e2 = (idx2 - N_GROUPS).astype(F32)
    out = jnp.where(lane == 0, e1, 0.0)
    out = jnp.where(lane == 1, e2, out)
    out = jnp.where(lane == 2, g1, out)
    out = jnp.where(lane == 3, g2, out)
    out = jnp.where(lane == 4, r1, out)
    return jnp.where(lane == 5, r2, out)


def _proj_ln_route_kernel(o_ref, w_ref, x_ref, g_ref, b_ref, wr_ref, br_ref,
                          y_ref, yp_ref, route_ref, cnt_ref, carry_ref):
    @pl.when(pl.program_id(0) == 0)
    def _():
        carry_ref[...] = jnp.zeros_like(carry_ref)

    mix = jnp.dot(o_ref[...], w_ref[...], preferred_element_type=F32)
    y = _layer_norm_rows(DEEPNORM_ALPHA * x_ref[...] + mix, g_ref[...], b_ref[...])
    y_ref[...] = y
    _store_rows_as_lines(yp_ref, 0, _pack_rows(y))
    route_ref[...] = _route_rows(y, wr_ref, br_ref, carry_ref)
    cnt_ref[...] = carry_ref[...]


def _proj_ln_route(o, w, x, g, b, w_group, b_group, w_expert, b_expert):
    s, d = x.shape
    kd = o.shape[1]
    tm = min(ROW_TILE, s)
    whl = _pack_hi_lo(jnp.concatenate([w_group.T, w_expert.T], axis=0))
    nb = N_GROUPS + N_EXPERTS
    bias = jnp.zeros((1, LANES), F32).at[0, :nb].set(jnp.concatenate([b_group, b_expert]))
    return pl.pallas_call(
        _proj_ln_route_kernel,
        grid=(s // tm,),
        in_specs=[pl.BlockSpec((tm, kd), lambda i: (i, 0)),
                  pl.BlockSpec((kd, d), lambda i: (0, 0)),
                  pl.BlockSpec((tm, d), lambda i: (i, 0)),
                  pl.BlockSpec((1, d), lambda i: (0, 0)),
                  pl.BlockSpec((1, d), lambda i: (0, 0)),
                  pl.BlockSpec((2 * LANES, d), lambda i: (0, 0)),
                  pl.BlockSpec((1, LANES), lambda i: (0, 0))],
        out_specs=[pl.BlockSpec((tm, d), lambda i: (i, 0)),
                   pl.BlockSpec((tm * (d // 2 // LANES), LANES), lambda i: (i, 0)),
                   pl.BlockSpec((tm, LANES), lambda i: (i, 0)),
                   pl.BlockSpec((1, LANES), lambda i: (0, 0))],
        out_shape=[jax.ShapeDtypeStruct((s, d), F32),
                   jax.ShapeDtypeStruct((s * (d // 2 // LANES), LANES), jnp.uint32),
                   jax.ShapeDtypeStruct((s, LANES), F32),
                   jax.ShapeDtypeStruct((1, LANES), F32)],
        scratch_shapes=[pltpu.VMEM((1, LANES), F32)],
        compiler_params=_params("arbitrary"),
        name="proj_ln_route",
    )(o, w, x, g.reshape(1, d), b.reshape(1, d), whl, bias)


def _dispatch_kernel(dest_ref, cnt_ref, pstart_ref, nv_ref, x_ref, xs_hbm, zero_ref, sem, zsem, *, nt):
    i = pl.program_id(0)
    tm = x_ref.shape[0] // nt
    bm = MOE_BLOCK
    n_exp = cnt_ref.shape[0]
    n_blocks = xs_hbm.shape[0] // (bm * nt)

    def lines(row, n_rows=1):
        return pl.ds(pl.multiple_of(row * nt, nt), n_rows * nt)

    def pad_copies(e, fn):
        cnt = cnt_ref[e]
        pad = (bm - cnt % bm) % bm
        pos = pstart_ref[e] + cnt
        size = bm // 2
        while size >= 1:
            @pl.when((pad & size) != 0)
            def _(pos=pos, size=size):
                fn(pltpu.make_async_copy(zero_ref.at[lines(0, size), :], xs_hbm.at[lines(pos, size), :], zsem))
            pos = pos + (pad & size)
            size //= 2

    def tail_copy(g):
        return pltpu.make_async_copy(zero_ref, xs_hbm.at[lines(g * bm, bm), :], zsem)

    def for_all_fills(fn):
        def per_expert(e, carry):
            pad_copies(e, fn)
            return carry
        lax.fori_loop(0, n_exp, per_expert, 0)

        def per_tail(g, carry):
            fn(tail_copy(g))
            return carry
        lax.fori_loop(nv_ref[0], n_blocks, per_tail, 0)

    @pl.when(i == 0)
    def _():
        zero_ref[...] = jnp.zeros_like(zero_ref)
        for_all_fills(lambda c: c.start())

    base = i * (TOP_K * tm)

    def body(r, carry):
        for k in range(TOP_K):
            dst = dest_ref[base + TOP_K * r + k]
            pltpu.make_async_copy(x_ref.at[lines(r), :], xs_hbm.at[lines(dst), :], sem).start(priority=k)
        return carry
    lax.fori_loop(0, tm, body, 0, unroll=8)

    @pl.when(i == 0)
    def _():
        for_all_fills(lambda c: c.wait())

    for _ in range(TOP_K):
        pltpu.make_async_copy(x_ref, xs_hbm.at[lines(0, tm), :], sem).wait()


def _dispatch(xp, n_rows, dest, cnt, pstart, n_valid, n_slots):
    nt = xp.shape[0] // n_rows
    tm = min(DISPATCH_TM, n_rows)
    grid_spec = pltpu.PrefetchScalarGridSpec(
        num_scalar_prefetch=4,
        grid=(n_rows // tm,),
        in_specs=[pl.BlockSpec((tm * nt, LANES), lambda i, *_: (i, 0))],
        out_specs=pl.BlockSpec(memory_space=pl.ANY),
        scratch_shapes=[pltpu.VMEM((MOE_BLOCK * nt, LANES), xp.dtype),
                        pltpu.SemaphoreType.DMA,
                        pltpu.SemaphoreType.DMA],
    )
    return pl.pallas_call(
        functools.partial(_dispatch_kernel, nt=nt),
        grid_spec=grid_spec,
        out_shape=jax.ShapeDtypeStruct((n_slots * nt, LANES), xp.dtype),
        compiler_params=_params("arbitrary"),
        name="moe_dispatch",
    )(dest, cnt, pstart, n_valid, xp)


def _expert_kernel(nblk_ref, bstart_ref, nv_ref, xs_hbm, wg_hbm, wu_hbm, wd_hbm, y_hbm,
                   xin_ref, yout_ref, wgf_ref, wuf_ref, wdf_ref, wgb_ref, wub_ref, wdb_ref,
                   in_sem, out_sem, w_sem, *, layer):
    e = pl.program_id(0)
    n_exp = pl.num_programs(0)
    n_valid = nv_ref[0]
    bm = MOE_BLOCK
    blk_lines = xin_ref.shape[1]
    nt = blk_lines // bm
    n_blocks = y_hbm.shape[0] // blk_lines
    ns = MOE_W_SLOTS

    half = wd_hbm.shape[2] // 2

    def w_copies(ex, slot):
        return (pltpu.make_async_copy(wg_hbm.at[layer, ex], wgf_ref.at[slot], w_sem.at[slot]),
                pltpu.make_async_copy(wu_hbm.at[layer, ex], wuf_ref.at[slot], w_sem.at[slot]),
                pltpu.make_async_copy(wd_hbm.at[layer, ex, pl.ds(0, half)],
                                      wdf_ref.at[slot, pl.ds(0, half)], w_sem.at[slot]),
                pltpu.make_async_copy(wd_hbm.at[layer, ex, pl.ds(half, half)],
                                      wdf_ref.at[slot, pl.ds(half, half)], w_sem.at[slot]))

    def start_weights(ex):
        exc = jnp.minimum(ex, n_exp - 1)

        @pl.when((ex < n_exp) & (nblk_ref[exc] > 0))
        def _():
            for piece, c in enumerate(w_copies(exc, exc % ns)):
                c.start(priority=piece % 2)

    @pl.when(e == 0)
    def _():
        for ahead in range(ns - 1):
            start_weights(ahead)

    start_weights(e + ns - 1)

    def block_lines(g):
        return pl.ds(pl.multiple_of(g * blk_lines, blk_lines), blk_lines)

    def in_copy(g, slot):
        return pltpu.make_async_copy(xs_hbm.at[block_lines(g), :], xin_ref.at[slot], in_sem.at[slot])

    def out_copy(g, slot):
        return pltpu.make_async_copy(yout_ref.at[slot], y_hbm.at[block_lines(g), :], out_sem.at[slot])

    @pl.when((e == 0) & (n_valid > 0))
    def _():
        in_copy(0, 0).start(priority=0)

    nb = nblk_ref[e]
    g0 = bstart_ref[e]

    @pl.when(nb > 0)
    def _():
        w_slot = e % ns
        for c in w_copies(e, w_slot):
            c.wait()
        wgb_ref[...] = wgf_ref[w_slot].astype(BF16)
        wub_ref[...] = wuf_ref[w_slot].astype(BF16)
        wdb_ref[...] = wdf_ref[w_slot].astype(BF16)

        def block(j, carry):
            g = g0 + j
            slot = g % 2
            in_copy(g, slot).wait()

            @pl.when(g + 1 < n_valid)
            def _():
                in_copy(g + 1, 1 - slot).start(priority=0)

            @pl.when(g >= 2)
            def _():
                out_copy(g - 2, slot).wait()

            xb = _unpack_rows(_load_rows_from_lines(xin_ref.at[slot], 0, bm, nt)).astype(BF16)
            gate = jnp.dot(xb, wgb_ref[...], preferred_element_type=F32)
            up = jnp.dot(xb, wub_ref[...], preferred_element_type=F32)
            hid = (gate * jax.nn.sigmoid(gate) * up).astype(BF16)
            y = jnp.dot(hid, wdb_ref[...], preferred_element_type=F32)
            _store_rows_as_lines(yout_ref.at[slot], 0, _pack_rows(y))
            out_copy(g, slot).start(priority=1)
            return carry
        lax.fori_loop(0, nb, block, 0)

    @pl.when(e == pl.num_programs(0) - 1)
    def _():
        for back in (2, 1):
            @pl.when(n_valid >= back)
            def _(back=back):
                out_copy(n_valid - back, (n_valid - back) % 2).wait()

        yout_ref[0] = jnp.zeros(yout_ref.shape[1:], yout_ref.dtype)

        def fill(g, carry):
            out_copy(g, 0).start()
            return carry
        lax.fori_loop(n_valid, n_blocks, fill, 0)

        def drain(g, carry):
            out_copy(g, 0).wait()
            return carry
        lax.fori_loop(n_valid, n_blocks, drain, 0)


def _experts(xs, nblk, bstart, n_valid, w_gate, w_up, w_down, layer):
    _, n_exp, d, f = w_gate.shape
    bm = MOE_BLOCK
    nt = d // 2 // LANES
    ns = MOE_W_SLOTS
    grid_spec = pltpu.PrefetchScalarGridSpec(
        num_scalar_prefetch=3,
        grid=(n_exp,),
        in_specs=[pl.BlockSpec(memory_space=pl.ANY)] * 4,
        out_specs=pl.BlockSpec(memory_space=pl.ANY),
        scratch_shapes=[pltpu.VMEM((2, bm * nt, LANES), jnp.uint32),
                        pltpu.VMEM((2, bm * nt, LANES), jnp.uint32),
                        pltpu.VMEM((ns, d, f), F32),
                        pltpu.VMEM((ns, d, f), F32),
                        pltpu.VMEM((ns, f, d), F32),
                        pltpu.VMEM((d, f), BF16),
                        pltpu.VMEM((d, f), BF16),
                        pltpu.VMEM((f, d), BF16),
                        pltpu.SemaphoreType.DMA((2,)),
                        pltpu.SemaphoreType.DMA((2,)),
                        pltpu.SemaphoreType.DMA((ns,))],
    )
    return pl.pallas_call(
        functools.partial(_expert_kernel, layer=layer),
        grid_spec=grid_spec,
        out_shape=jax.ShapeDtypeStruct(xs.shape, jnp.uint32),
        compiler_params=_params("arbitrary"),
        name="moe_experts",
    )(nblk, bstart, n_valid, xs, w_gate, w_up, w_down)


def _combine_ln_kernel(dest_ref, y_hbm, x_ref, route_ref, g_ref, b_ref, o_ref, yg_ref, sem):
    i = pl.program_id(0)
    n = pl.num_programs(0)
    tm = x_ref.shape[0]
    nt = yg_ref.shape[1] // (TOP_K * tm)

    def lines(row, n_rows=1):
        return pl.ds(pl.multiple_of(row * nt, nt), n_rows * nt)

    def start_gather(step, slot):
        base = step * (TOP_K * tm)

        def body(r, carry):
            for k in range(TOP_K):
                src = dest_ref[base + TOP_K * r + k]
                pltpu.make_async_copy(y_hbm.at[lines(src), :], yg_ref.at[slot, lines(k * tm + r), :],
                                      sem.at[slot]).start(priority=k)
            return carry
        lax.fori_loop(0, tm, body, 0, unroll=8)

    def wait_gather(slot):
        pltpu.make_async_copy(y_hbm.at[lines(0, TOP_K * tm), :], yg_ref.at[slot], sem.at[slot]).wait()

    @pl.when(i == 0)
    def _():
        start_gather(0, 0)

    @pl.when(i + 1 < n)
    def _():
        start_gather(i + 1, (i + 1) % 2)

    slot = i % 2
    wait_gather(slot)
    route = route_ref[...]
    lane = lax.broadcasted_iota(jnp.int32, route.shape, 1)
    g1 = jnp.sum(jnp.where(lane == 2, route, 0.0), axis=1, keepdims=True)
    g2 = jnp.sum(jnp.where(lane == 3, route, 0.0), axis=1, keepdims=True)
    y1 = _unpack_rows(_load_rows_from_lines(yg_ref.at[slot], 0, tm, nt))
    y2 = _unpack_rows(_load_rows_from_lines(yg_ref.at[slot], tm * nt, tm, nt))
    z = DEEPNORM_ALPHA * x_ref[...] + (y1 * g1 + y2 * g2)
    o_ref[...] = _layer_norm_rows(z, g_ref[...], b_ref[...])


def _combine_ln(y, dest, x, route, g, b):
    t, d = x.shape
    tm = min(COMB_TM, t)
    grid_spec = pltpu.PrefetchScalarGridSpec(
        num_scalar_prefetch=1,
        grid=(t // tm,),
        in_specs=[pl.BlockSpec(memory_space=pl.ANY),
                  pl.BlockSpec((tm, d), lambda i, dest: (i, 0)),
                  pl.BlockSpec((tm, LANES), lambda i, dest: (i, 0)),
                  pl.BlockSpec((1, d), lambda i, dest: (0, 0)),
                  pl.BlockSpec((1, d), lambda i, dest: (0, 0))],
        out_specs=pl.BlockSpec((tm, d), lambda i, dest: (i, 0)),
        scratch_shapes=[pltpu.VMEM((2, TOP_K * tm * (d // 2 // LANES), LANES), y.dtype),
                        pltpu.SemaphoreType.DMA((2,))],
    )
    return pl.pallas_call(
        _combine_ln_kernel,
        grid_spec=grid_spec,
        out_shape=jax.ShapeDtypeStruct((t, d), F32),
        compiler_params=_params("arbitrary"),
        name="moe_combine_ln",
    )(dest, y, x, route, g.reshape(1, d), b.reshape(1, d))


def _moe_ffn_ln(x, xp, route, counts, w_gate, w_up, w_down, layer, ln_g, ln_b):
    t, d = x.shape
    bm = MOE_BLOCK
    eid = route[:, 0:TOP_K].astype(jnp.int32)
    rank = route[:, 4:4 + TOP_K].astype(jnp.int32)
    cnt = counts[0, N_GROUPS:N_GROUPS + N_EXPERTS].astype(jnp.int32)
    nblk = (cnt + bm - 1) // bm
    bend = jnp.cumsum(nblk)
    bstart = bend - nblk
    pstart = bstart * bm
    onehot = eid[:, :, None] == jnp.arange(N_EXPERTS, dtype=jnp.int32)
    dest = (jnp.sum(jnp.where(onehot, pstart, 0), axis=-1) + rank).reshape(-1)
    n_valid = bend[-1:]
    n_slots = (-(-t * TOP_K // bm) + N_EXPERTS) * bm
    xs = _dispatch(xp, t, dest, cnt, pstart, n_valid, n_slots)
    y = _experts(xs, nblk, bstart, n_valid, w_gate, w_up, w_down, layer)
    return _combine_ln(y, dest, x, route, ln_g, ln_b)


def _gla_log_alpha(x, wl_ref, wu_ref, b_ref):
    g_low = _dot_hi_lo(x, wl_ref)
    n = wu_ref.shape[1] // 2
    gh, gl = _split2(g_low)
    a = jnp.dot(gh, wu_ref[...], preferred_element_type=F32)
    c = jnp.dot(gl, wu_ref[:, :n], preferred_element_type=F32)
    return _log_sigmoid(a[:, :n] + a[:, n:] + c + b_ref[...]) / GLA_GATE_TAU


def _gla_kernel(q_ref, k_ref, v_ref, r_ref, x_ref, wl_ref, wu_ref, bg_ref, g_ref, o_ref, state_ref):
    @pl.when(pl.program_id(0) == 0)
    def _():
        state_ref[...] = jnp.zeros_like(state_ref)

    log_alpha = _gla_log_alpha(x_ref[...], wl_ref, wu_ref, bg_ref)

    rows = q_ref.shape[0]
    nh, dk, dv = state_ref.shape
    cs = GLA_CHUNK
    n_chunks = rows // cs
    rr = lax.broadcasted_iota(jnp.int32, (rows, rows), 0)
    cc = lax.broadcasted_iota(jnp.int32, (rows, rows), 1)
    causal = (rr // cs == cc // cs) & (rr >= cc)
    tri = jnp.where(causal, 1.0, 0.0).astype(BF16)
    lane_chunk = lax.broadcasted_iota(jnp.int32, (dk, rows), 1) // cs

    def chunk_row(a, i):
        return jnp.concatenate([jnp.broadcast_to(a[c * cs + i:c * cs + i + 1, :], (cs, a.shape[1]))
                                for c in range(n_chunks)], axis=0)

    for h in range(nh):
        kc = pl.ds(h * dk, dk)
        vc = pl.ds(h * dv, dv)
        b = _cumsum_rows(log_alpha[:, h * dk:(h + 1) * dk], tri)
        b_mid = chunk_row(b, cs // 2 - 1)
        b_last = chunk_row(b, cs - 1)
        q = q_ref[:, kc].astype(F32) * (dk ** -0.5)
        k = k_ref[:, kc].astype(F32)
        v = v_ref[:, vc]
        qa = (q * jnp.exp(b - b_mid)).astype(BF16)
        ka = (k * jnp.exp(b_mid - b)).astype(BF16)
        a = jnp.where(causal, _dot_nt(qa, ka), 0.0)
        o_intra = jnp.dot(a.astype(BF16), v, preferred_element_type=F32)
        q_inter = (q * jnp.exp(b)).astype(BF16)
        k_end_t = (k * jnp.exp(b_last - b)).T
        b_t = b.T
        state = state_ref[h]
        outs = []
        for c in range(n_chunks):
            sl = slice(c * cs, (c + 1) * cs)
            outs.append(o_intra[sl] + jnp.dot(q_inter[sl], state.astype(BF16), preferred_element_type=F32))
            kv = jnp.dot(jnp.where(lane_chunk == c, k_end_t, 0.0).astype(BF16), v,
                         preferred_element_type=F32)
            decay = jnp.exp(b_t[:, (c + 1) * cs - 1:(c + 1) * cs])
            state = decay * state + kv
        state_ref[h] = state
        o = jnp.concatenate(outs, axis=0)
        o = o * lax.rsqrt(jnp.mean(o * o, axis=-1, keepdims=True) + RMS_EPS) * g_ref[...]
        r = r_ref[:, vc].astype(F32)
        o_ref[:, vc] = (o * (r * jax.nn.sigmoid(r))).astype(o_ref.dtype)


def _gla(proj, x, w_low_t, w_gate_up, b_gate, norm_g):
    s, d = x.shape
    rank, dk_all = w_gate_up.shape
    nh = GLA_HEADS
    dk = dk_all // nh
    dv = norm_g.shape[0]
    dv_all = nh * dv
    assert 2 * dk_all == dv_all
    rows = min(GLA_ROWS, s)
    wl = _pack_hi_lo(w_low_t)
    wu_pad = jnp.zeros((LANES, dk_all), F32).at[:rank].set(w_gate_up)
    wu = jnp.concatenate(_split2(wu_pad), axis=1)
    return pl.pallas_call(
        _gla_kernel,
        grid=(s // rows,),
        in_specs=[pl.BlockSpec((rows, dk_all), lambda i: (i, 0)),
                  pl.BlockSpec((rows, dk_all), lambda i: (i, 1)),
                  pl.BlockSpec((rows, dv_all), lambda i: (i, 1)),
                  pl.BlockSpec((rows, dv_all), lambda i: (i, 2)),
                  pl.BlockSpec((rows, d), lambda i: (i, 0)),
                  pl.BlockSpec((2 * LANES, d), lambda i: (0, 0)),
                  pl.BlockSpec((LANES, 2 * dk_all), lambda i: (0, 0)),
                  pl.BlockSpec((1, dk_all), lambda i: (0, 0)),
                  pl.BlockSpec((1, dv), lambda i: (0, 0))],
        out_specs=pl.BlockSpec((rows, dv_all), lambda i: (i, 0)),
        out_shape=jax.ShapeDtypeStruct((s, dv_all), BF16),
        scratch_shapes=[pltpu.VMEM((nh, dk, dv), F32)],
        compiler_params=_params("arbitrary"),
        name="gla_chunks",
    )(proj, proj, proj, proj, x, wl, wu, b_gate.reshape(1, dk_all), norm_g.reshape(1, dv))


def kernel(x, fox_w_in, fox_b_f, fox_w_o, gla_w_in, gla_w_gate_up, gla_b_gate, gla_norm_g, gla_w_o,
           ln_mix_g, ln_mix_b, ln_ffn_g, ln_ffn_b, moe_w_group, moe_b_group, moe_w_expert,
           moe_b_expert, moe_w_gate, moe_w_up, moe_w_down):
    bsz, s, d = x.shape
    outs = []
    for bi in range(bsz):
        xt = x[bi]
        for i in range(DEPTH):
            j = i // 2
            if i % 2 == 0:
                w_in_t = fox_w_in[j].T
                qkv = _matmul_nt(xt, w_in_t, 3 * d, BF16)
                c = _fox_gates(xt, w_in_t[3 * d:], fox_b_f[j])
                o = _fox_attention(qkv, c, FOX_HEADS)
                w_o = fox_w_o[j]
            else:
                w_in_t = gla_w_in[j].T
                n_main = w_in_t.shape[0] - gla_w_gate_up.shape[1]
                proj = _matmul_nt(xt, w_in_t, n_main, BF16)
                o = _gla(proj, xt, w_in_t[n_main:], gla_w_gate_up[j], gla_b_gate[j], gla_norm_g[j])
                w_o = gla_w_o[j]
            xt, xp, route, counts = _proj_ln_route(o, w_o.astype(BF16), xt, ln_mix_g[i], ln_mix_b[i], moe_w_group[i],
                                                   moe_b_group[i], moe_w_expert[i], moe_b_expert[i])
            xt = _moe_ffn_ln(xt, xp, route, counts, moe_w_gate, moe_w_up, moe_w_down, i, ln_ffn_g[i], ln_ffn_b[i])
        outs.append(xt)
    return outs[0].reshape(1, s, d) if bsz == 1 else jnp.stack(outs, axis=0)
```

```python
import functools

import jax
import jax.numpy as jnp
from jax import lax
from jax.experimental import pallas as pl
from jax.experimental.pallas import tpu as pltpu

F32 = jnp.float32
BF16 = jnp.bfloat16

DEPTH = 2
FOX_HEADS = 16
FOX_HEAD_DIM = 128
GLA_HEADS = 4
GLA_CHUNK = 64
GLA_GATE_TAU = 16.0
N_GROUPS = 8
EXPERTS_PER_GROUP = 8
N_EXPERTS = N_GROUPS * EXPERTS_PER_GROUP
TOP_K = 2
DEEPNORM_ALPHA = (2 * DEPTH) ** 0.25
LN_EPS = 1e-5
RMS_EPS = 1e-6

LANES = 128
VMEM_LIMIT = 56 * 2**20

MM_TM, MM_TN = 1024, 1024
ROW_TILE = 512
ATT_TQ, ATT_TK = 1024, 512
ATT_GROUP = 256
GLA_ROWS = 256
MOE_BLOCK = 128
MOE_W_SLOTS = 3
DISPATCH_TM = 1024
COMB_TM = 256


def _params(*sem):
    return pltpu.CompilerParams(dimension_semantics=sem, vmem_limit_bytes=VMEM_LIMIT)


def _split2(a):
    hi = a.astype(BF16)
    lo = (a - hi.astype(F32)).astype(BF16)
    return hi, lo


def _split3(a):
    hi = a.astype(BF16)
    r = a - hi.astype(F32)
    mid = r.astype(BF16)
    lo = (r - mid.astype(F32)).astype(BF16)
    return hi, mid, lo


def _pack_hi_lo(w_t, n_pad=LANES):
    n, k = w_t.shape
    wp = jnp.zeros((n_pad, k), F32).at[:n].set(w_t)
    hi, lo = _split2(wp)
    return jnp.concatenate([hi, lo], axis=0)


def _dot_nt(a, b):
    return lax.dot_general(a, b, (((1,), (1,)), ((), ())), preferred_element_type=F32)


def _dot_hi_lo(x, whl_ref, n_pad=LANES):
    xh, xl = _split2(x)
    a = _dot_nt(xh, whl_ref[...])
    b = _dot_nt(xl, whl_ref[:n_pad, :])
    return a[:, :n_pad] + a[:, n_pad:] + b


def _pack_rows(a):
    half = a.shape[1] // 2
    hi = lax.bitcast_convert_type(a[:, :half].astype(BF16).astype(F32), jnp.uint32)
    lo = lax.bitcast_convert_type(a[:, half:].astype(BF16).astype(F32), jnp.uint32)
    return hi | (lo >> 16)


def _unpack_rows(w):
    hi = lax.bitcast_convert_type(w & jnp.uint32(0xFFFF0000), F32)
    lo = lax.bitcast_convert_type(w << 16, F32)
    return jnp.concatenate([hi, lo], axis=1)


def _store_rows_as_lines(ref, first_line, packed):
    n_rows, width = packed.shape
    nt = width // LANES
    for j in range(nt):
        ref[pl.ds(first_line + j, n_rows, stride=nt), :] = packed[:, j * LANES:(j + 1) * LANES]


def _load_rows_from_lines(ref, first_line, n_rows, nt):
    return jnp.concatenate([ref[pl.ds(first_line + j, n_rows, stride=nt), :] for j in range(nt)], axis=1)


def _log_sigmoid(x):
    return -(jnp.maximum(-x, 0.0) + jnp.log1p(jnp.exp(-jnp.abs(x))))


def _cumsum_rows(a, incl_tri):
    n = a.shape[1]
    parts = jnp.concatenate(_split3(a), axis=1)
    c = jnp.dot(incl_tri, parts, preferred_element_type=F32)
    return c[:, :n] + c[:, n:2 * n] + c[:, 2 * n:]


def _tri(n, strict=False):
    r = lax.broadcasted_iota(jnp.int32, (n, n), 0)
    c = lax.broadcasted_iota(jnp.int32, (n, n), 1)
    return jnp.where((r > c) if strict else (r >= c), 1.0, 0.0).astype(BF16)


def _layer_norm_rows(z, g, b):
    mu = jnp.mean(z, axis=-1, keepdims=True)
    d = z - mu
    var = jnp.mean(d * d, axis=-1, keepdims=True)
    return d * lax.rsqrt(var + LN_EPS) * g + b


def _mm_kernel(x_hbm, w_ref, o_ref, xbuf_ref, wb_ref, sem):
    j, i = pl.program_id(0), pl.program_id(1)
    nj, ni = pl.num_programs(0), pl.num_programs(1)
    tm = xbuf_ref.shape[1]
    half = tm // 2
    step = j * ni + i

    def x_copies(tile, slot):
        return [pltpu.make_async_copy(x_hbm.at[pl.ds(pl.multiple_of(tile * tm + h * half, half), half), :],
                                      xbuf_ref.at[slot, pl.ds(h * half, half), :], sem.at[slot])
                for h in range(2)]

    @pl.when(step == 0)
    def _():
        for h, c in enumerate(x_copies(0, 0)):
            c.start(priority=h)

    @pl.when(step + 1 < nj * ni)
    def _():
        for h, c in enumerate(x_copies((i + 1) % ni, (step + 1) % 2)):
            c.start(priority=h)

    @pl.when(i == 0)
    def _():
        wb_ref[...] = w_ref[...].astype(BF16)

    slot = step % 2
    for c in x_copies(i, slot):
        c.wait()
    o_ref[...] = _dot_nt(xbuf_ref[slot].astype(BF16), wb_ref[...]).astype(o_ref.dtype)


def _matmul_nt(x, w_t, n, out_dtype):
    m, k = x.shape
    tm, tn = min(MM_TM, m), min(MM_TN, n)
    return pl.pallas_call(
        _mm_kernel,
        grid=(n // tn, m // tm),
        in_specs=[pl.BlockSpec(memory_space=pl.ANY),
                  pl.BlockSpec((tn, k), lambda j, i: (j, 0))],
        out_specs=pl.BlockSpec((tm, tn), lambda j, i: (i, j)),
        out_shape=jax.ShapeDtypeStruct((m, n), out_dtype),
        scratch_shapes=[pltpu.VMEM((2, tm, k), F32),
                        pltpu.VMEM((tn, k), BF16),
                        pltpu.SemaphoreType.DMA((2,))],
        compiler_params=_params("arbitrary", "arbitrary"),
        name="dense_proj",
    )(x, w_t)


def _fox_gate_kernel(x_ref, w_ref, b_ref, c_ref, carry_ref):
    @pl.when(pl.program_id(0) == 0)
    def _():
        carry_ref[...] = jnp.zeros_like(carry_ref)

    ts = x_ref.shape[0]
    logits = _dot_hi_lo(x_ref[...], w_ref) + b_ref[...]
    log_f = _log_sigmoid(logits)
    c = _cumsum_rows(log_f, _tri(ts)) + carry_ref[...]
    carry_ref[...] = c[ts - 1:ts, :]
    c_ref[...] = c


def _fox_gates(x, w_f_t, b_f):
    s, d = x.shape
    h = w_f_t.shape[0]
    ts = min(ROW_TILE, s)
    whl = _pack_hi_lo(w_f_t)
    bias = jnp.zeros((1, LANES), F32).at[0, :h].set(b_f)
    return pl.pallas_call(
        _fox_gate_kernel,
        grid=(s // ts,),
        in_specs=[pl.BlockSpec((ts, d), lambda i: (i, 0)),
                  pl.BlockSpec((2 * LANES, d), lambda i: (0, 0)),
                  pl.BlockSpec((1, LANES), lambda i: (0, 0))],
        out_specs=pl.BlockSpec((ts, LANES), lambda i: (i, 0)),
        out_shape=jax.ShapeDtypeStruct((s, LANES), F32),
        scratch_shapes=[pltpu.VMEM((1, LANES), F32)],
        compiler_params=_params("arbitrary"),
        name="fox_gates",
    )(x, whl, bias)


def _fox_attn_kernel(q_ref, k_ref, v_ref, c_ref, o_ref, kaug_ref, vt_ref, qt_ref, st0_ref, st1_ref,
                     p0_ref, p1_ref, alpha_ref, bmax_ref, acc_ref, m_ref):
    h = pl.program_id(0)
    qi = pl.program_id(1)
    tq, dh = q_ref.shape
    s_len = k_ref.shape[0]
    tk = ATT_TK
    log2e = 1.4426950408889634
    scale = dh ** -0.5 * log2e

    def head_column(rows, n):
        lane = lax.broadcasted_iota(jnp.int32, (n, LANES), 1)
        return log2e * jnp.sum(jnp.where(lane == h, c_ref[rows, :], 0.0), axis=1, keepdims=True)

    def bias_columns(col, first, n):
        lane = lax.broadcasted_iota(jnp.int32, (n, LANES), 1)
        hi, mid, lo = _split3(col)
        ones_first = 3 - first
        out = jnp.where((lane >= ones_first) & (lane < ones_first + 3), 1.0, 0.0)
        out = jnp.where(lane == first, hi.astype(F32), out)
        out = jnp.where(lane == first + 1, mid.astype(F32), out)
        return jnp.where(lane == first + 2, lo.astype(F32), out).astype(BF16)

    @pl.when(qi == 0)
    def _():
        ones_row = jnp.where(lax.broadcasted_iota(jnp.int32, (16, tk), 0) == 0, 1.0, 0.0).astype(BF16)

        def build(ci, carry):
            rows = pl.ds(pl.multiple_of(ci * tk, tk), tk)
            kaug_ref[rows, :dh] = k_ref[rows, :]
            kaug_ref[rows, dh:] = bias_columns(-head_column(rows, tk), 0, tk)
            vt_ref[:dh, rows] = v_ref[rows, :].astype(F32).T.astype(BF16)
            vt_ref[dh:, rows] = ones_row
            return carry
        lax.fori_loop(0, s_len // tk, build, 0)

    q_rows = pl.ds(pl.multiple_of(qi * tq, tq), tq)
    q_aug = jnp.concatenate([q_ref[...].astype(F32) * scale,
                             bias_columns(head_column(q_rows, tq), 3, tq).astype(F32)], axis=1)
    qt_ref[...] = q_aug.T.astype(BF16)

    acc_ref[...] = jnp.zeros_like(acc_ref)
    m_ref[...] = jnp.full_like(m_ref, -jnp.inf)

    st_refs, p_refs = (st0_ref, st1_ref), (p0_ref, p1_ref)

    gw = ATT_GROUP
    groups = [pl.ds(g * gw, gw) for g in range(tq // gw)]

    def scores(k_start, slot, diag_offset=None, lanes=None):
        for gl in ([lanes] if lanes is not None else groups):
            st = jnp.dot(kaug_ref[pl.ds(k_start, tk), :], qt_ref[:, gl], preferred_element_type=F32)
            if diag_offset is not None:
                kr = lax.broadcasted_iota(jnp.int32, (tk, gw), 0) + diag_offset
                qc = lax.broadcasted_iota(jnp.int32, (tk, gw), 1) + gl.start
                st = jnp.where(kr <= qc, st, -jnp.inf)
            st_refs[slot][:, gl] = st
            bmax_ref[slot, :, gl] = jnp.max(st, axis=0, keepdims=True)

    def softmax(slot, lanes=None):
        for gl in ([lanes] if lanes is not None else groups):
            m_prev = m_ref[:, gl]
            m_new = jnp.maximum(m_prev, bmax_ref[slot, :, gl])
            m_ref[:, gl] = m_new
            p_refs[slot][:, gl] = jnp.exp2(st_refs[slot][:, gl] - m_new).astype(BF16)
            alpha_ref[slot, :, gl] = jnp.exp2(m_prev - m_new)

    def values(k_start, slot, lanes=None):
        for gl in ([lanes] if lanes is not None else groups):
            acc_ref[:, gl] = alpha_ref[slot, :, gl] * acc_ref[:, gl] + jnp.dot(
                vt_ref[:, pl.ds(k_start, tk)], p_refs[slot][:, gl], preferred_element_type=F32)

    n_diag = tq // tk
    assert n_diag == 2
    d0 = pl.multiple_of(qi * tq, tk)
    d1 = pl.multiple_of(qi * tq + tk, tk)
    n_full = qi * n_diag

    def full_start(i):
        return pl.multiple_of(jnp.minimum(i, jnp.maximum(n_full - 1, 0)) * tk, tk)

    scores(d0, 0, 0)
    scores(d1, 1, tk)
    softmax(0)
    scores(full_start(0), 0)
    softmax(1)
    values(d0, 0)

    def pair(j, carry):
        i0 = 2 * j
        prev = jnp.where(j == 0, d1, (i0 - 1) * tk)
        for gl in groups:
            scores(full_start(i0 + 1), 1, lanes=gl)
            softmax(0, lanes=gl)
            values(pl.multiple_of(prev, tk), 1, lanes=gl)
        for gl in groups:
            scores(full_start(i0 + 2), 0, lanes=gl)
            softmax(1, lanes=gl)
            values(pl.multiple_of(i0 * tk, tk), 0, lanes=gl)
        return carry

    lax.fori_loop(0, qi, pair, 0)
    last = jnp.where(qi == 0, d1, (n_full - 1) * tk)
    values(pl.multiple_of(last, tk), 1)

    o_ref[...] = (acc_ref[:dh, :] / acc_ref[dh:dh + 1, :]).T.astype(o_ref.dtype)


def _fox_attention(qkv, c, n_heads):
    s = qkv.shape[0]
    dh = FOX_HEAD_DIM
    tq = min(ATT_TQ, s)
    hh = n_heads
    return pl.pallas_call(
        _fox_attn_kernel,
        grid=(hh, s // tq),
        in_specs=[pl.BlockSpec((tq, dh), lambda h, i: (i, h)),
                  pl.BlockSpec((s, dh), lambda h, i: (0, hh + h)),
                  pl.BlockSpec((s, dh), lambda h, i: (0, 2 * hh + h)),
                  pl.BlockSpec((s, LANES), lambda h, i: (0, 0))],
        out_specs=pl.BlockSpec((tq, dh), lambda h, i: (i, h)),
        out_shape=jax.ShapeDtypeStruct((s, hh * dh), BF16),
        scratch_shapes=[pltpu.VMEM((s, 2 * dh), BF16),
                        pltpu.VMEM((dh + 16, s), BF16),
                        pltpu.VMEM((2 * dh, tq), BF16),
                        pltpu.VMEM((ATT_TK, tq), F32),
                        pltpu.VMEM((ATT_TK, tq), F32),
                        pltpu.VMEM((ATT_TK, tq), BF16),
                        pltpu.VMEM((ATT_TK, tq), BF16),
                        pltpu.VMEM((2, 1, tq), F32),
                        pltpu.VMEM((2, 1, tq), F32),
                        pltpu.VMEM((dh + 16, tq), F32),
                        pltpu.VMEM((1, tq), F32)],
        compiler_params=_params("arbitrary", "arbitrary"),
        name="fox_attention",
    )(qkv, qkv, qkv, c)


def _first_lane_eq(vals, target, lane):
    return jnp.min(jnp.where(vals == target, lane, LANES), axis=1, keepdims=True)


def _route_rows(x, w_ref, b_ref, carry_ref):
    tm = x.shape[0]
    neg = -jnp.inf
    logits = _dot_hi_lo(x, w_ref) + b_ref[...]
    lane = lax.broadcasted_iota(jnp.int32, (tm, LANES), 1)

    gl = jnp.where(lane < N_GROUPS, logits, neg)
    gmax = jnp.max(gl, axis=1, keepdims=True)
    gsum = jnp.sum(jnp.exp(gl - gmax), axis=1, keepdims=True)
    grp_p = 1.0 / gsum
    grp = _first_lane_eq(gl, gmax, lane)

    lo = N_GROUPS + grp * EXPERTS_PER_GROUP
    el = jnp.where((lane >= lo) & (lane < lo + EXPERTS_PER_GROUP), logits, neg)
    emax = jnp.max(el, axis=1, keepdims=True)
    esum = jnp.sum(jnp.exp(el - emax), axis=1, keepdims=True)
    idx1 = _first_lane_eq(el, emax, lane)
    el2 = jnp.where(lane == idx1, neg, el)
    emax2 = jnp.max(el2, axis=1, keepdims=True)
    idx2 = _first_lane_eq(el2, emax2, lane)
    p1 = 1.0 / esum
    p2 = jnp.exp(emax2 - emax) / esum
    psum = p1 + p2
    g1 = grp_p * (p1 / psum)
    g2 = grp_p * (p2 / psum)

    oh1 = lane == idx1
    oh2 = lane == idx2
    both = jnp.where(oh1 | oh2, 1.0, 0.0)
    before = jnp.dot(_tri(tm, strict=True), both.astype(BF16), preferred_element_type=F32)
    before = before + carry_ref[...]
    r1 = jnp.sum(jnp.where(oh1, before, 0.0), axis=1, keepdims=True)
    r2 = jnp.sum(jnp.where(oh2, before, 0.0), axis=1, keepdims=True)
    carry_ref[...] = carry_ref[...] + jnp.sum(both, axis=0, keepdims=True)

    e1 = (idx1 - N_GROUPS).astype(F32)
    e2 = (idx2 - N_GROUPS).astype(F32)
    out = jnp.where(lane == 0, e1, 0.0)
    out = jnp.where(lane == 1, e2, out)
    out = jnp.where(lane == 2, g1, out)
    out = jnp.where(lane == 3, g2, out)
    out = jnp.where(lane == 4, r1, out)
    return jnp.where(lane == 5, r2, out)


def _proj_ln_route_kernel(o_ref, w_ref, x_ref, g_ref, b_ref, wr_ref, br_ref,
                          y_ref, yp_ref, route_ref, cnt_ref, carry_ref):
    @pl.when(pl.program_id(0) == 0)
    def _():
        carry_ref[...] = jnp.zeros_like(carry_ref)

    mix = jnp.dot(o_ref[...], w_ref[...], preferred_element_type=F32)
    y = _layer_norm_rows(DEEPNORM_ALPHA * x_ref[...] + mix, g_ref[...], b_ref[...])
    y_ref[...] = y
    _store_rows_as_lines(yp_ref, 0, _pack_rows(y))
    route_ref[...] = _route_rows(y, wr_ref, br_ref, carry_ref)
    cnt_ref[...] = carry_ref[...]


def _proj_ln_route(o, w, x, g, b, w_group, b_group, w_expert, b_expert):
    s, d = x.shape
    kd = o.shape[1]
    tm = min(ROW_TILE, s)
    whl = _pack_hi_lo(jnp.concatenate([w_group.T, w_expert.T], axis=0))
    nb = N_GROUPS + N_EXPERTS
    bias = jnp.zeros((1, LANES), F32).at[0, :nb].set(jnp.concatenate([b_group, b_expert]))
    return pl.pallas_call(
        _proj_ln_route_kernel,
        grid=(s // tm,),
        in_specs=[pl.BlockSpec((tm, kd), lambda i: (i, 0)),
                  pl.BlockSpec((kd, d), lambda i: (0, 0)),
                  pl.BlockSpec((tm, d), lambda i: (i, 0)),
                  pl.BlockSpec((1, d), lambda i: (0, 0)),
                  pl.BlockSpec((1, d), lambda i: (0, 0)),
                  pl.BlockSpec((2 * LANES, d), lambda i: (0, 0)),
                  pl.BlockSpec((1, LANES), lambda i: (0, 0))],
        out_specs=[pl.BlockSpec((tm, d), lambda i: (i, 0)),
                   pl.BlockSpec((tm * (d // 2 // LANES), LANES), lambda i: (i, 0)),
                   pl.BlockSpec((tm, LANES), lambda i: (i, 0)),
                   pl.BlockSpec((1, LANES), lambda i: (0, 0))],
        out_shape=[jax.ShapeDtypeStruct((s, d), F32),
                   jax.ShapeDtypeStruct((s * (d // 2 // LANES), LANES), jnp.uint32),
                   jax.ShapeDtypeStruct((s, LANES), F32),
                   jax.ShapeDtypeStruct((1, LANES), F32)],
        scratch_shapes=[pltpu.VMEM((1, LANES), F32)],
        compiler_params=_params("arbitrary"),
        name="proj_ln_route",
    )(o, w, x, g.reshape(1, d), b.reshape(1, d), whl, bias)


def _dispatch_kernel(dest_ref, cnt_ref, pstart_ref, nv_ref, x_ref, xs_hbm, zero_ref, sem, zsem, *, nt):
    i = pl.program_id(0)
    tm = x_ref.shape[0] // nt
    bm = MOE_BLOCK
    n_exp = cnt_ref.shape[0]
    n_blocks = xs_hbm.shape[0] // (bm * nt)

    def lines(row, n_rows=1):
        return pl.ds(pl.multiple_of(row * nt, nt), n_rows * nt)

    def pad_copies(e, fn):
        cnt = cnt_ref[e]
        pad = (bm - cnt % bm) % bm
        pos = pstart_ref[e] + cnt
        size = bm // 2
        while size >= 1:
            @pl.when((pad & size) != 0)
            def _(pos=pos, size=size):
                fn(pltpu.make_async_copy(zero_ref.at[lines(0, size), :], xs_hbm.at[lines(pos, size), :], zsem))
            pos = pos + (pad & size)
            size //= 2

    def tail_copy(g):
        return pltpu.make_async_copy(zero_ref, xs_hbm.at[lines(g * bm, bm), :], zsem)

    def for_all_fills(fn):
        def per_expert(e, carry):
            pad_copies(e, fn)
            return carry
        lax.fori_loop(0, n_exp, per_expert, 0)

        def per_tail(g, carry):
            fn(tail_copy(g))
            return carry
        lax.fori_loop(nv_ref[0], n_blocks, per_tail, 0)

    @pl.when(i == 0)
    def _():
        zero_ref[...] = jnp.zeros_like(zero_ref)
        for_all_fills(lambda c: c.start())

    base = i * (TOP_K * tm)

    def body(r, carry):
        for k in range(TOP_K):
            dst = dest_ref[base + TOP_K * r + k]
            pltpu.make_async_copy(x_ref.at[lines(r), :], xs_hbm.at[lines(dst), :], sem).start(priority=k)
        return carry
    lax.fori_loop(0, tm, body, 0, unroll=8)

    @pl.when(i == 0)
    def _():
        for_all_fills(lambda c: c.wait())

    for _ in range(TOP_K):
        pltpu.make_async_copy(x_ref, xs_hbm.at[lines(0, tm), :], sem).wait()


def _dispatch(xp, n_rows, dest, cnt, pstart, n_valid, n_slots):
    nt = xp.shape[0] // n_rows
    tm = min(DISPATCH_TM, n_rows)
    grid_spec = pltpu.PrefetchScalarGridSpec(
        num_scalar_prefetch=4,
        grid=(n_rows // tm,),
        in_specs=[pl.BlockSpec((tm * nt, LANES), lambda i, *_: (i, 0))],
        out_specs=pl.BlockSpec(memory_space=pl.ANY),
        scratch_shapes=[pltpu.VMEM((MOE_BLOCK * nt, LANES), xp.dtype),
                        pltpu.SemaphoreType.DMA,
                        pltpu.SemaphoreType.DMA],
    )
    return pl.pallas_call(
        functools.partial(_dispatch_kernel, nt=nt),
        grid_spec=grid_spec,
        out_shape=jax.ShapeDtypeStruct((n_slots * nt, LANES), xp.dtype),
        compiler_params=_params("arbitrary"),
        name="moe_dispatch",
    )(dest, cnt, pstart, n_valid, xp)


def _expert_kernel(nblk_ref, bstart_ref, nv_ref, xs_hbm, wg_hbm, wu_hbm, wd_hbm, y_hbm,
                   xin_ref, yout_ref, wgf_ref, wuf_ref, wdf_ref, wgb_ref, wub_ref, wdb_ref,
                   in_sem, out_sem, w_sem, *, layer):
    e = pl.program_id(0)
    n_exp = pl.num_programs(0)
    n_valid = nv_ref[0]
    bm = MOE_BLOCK
    blk_lines = xin_ref.shape[1]
    nt = blk_lines // bm
    n_blocks = y_hbm.shape[0] // blk_lines
    ns = MOE_W_SLOTS

    half = wd_hbm.shape[2] // 2

    def w_copies(ex, slot):
        return (pltpu.make_async_copy(wg_hbm.at[layer, ex], wgf_ref.at[slot], w_sem.at[slot]),
                pltpu.make_async_copy(wu_hbm.at[layer, ex], wuf_ref.at[slot], w_sem.at[slot]),
                pltpu.make_async_copy(wd_hbm.at[layer, ex, pl.ds(0, half)],
                                      wdf_ref.at[slot, pl.ds(0, half)], w_sem.at[slot]),
                pltpu.make_async_copy(wd_hbm.at[layer, ex, pl.ds(half, half)],
                                      wdf_ref.at[slot, pl.ds(half, half)], w_sem.at[slot]))

    def start_weights(ex):
        exc = jnp.minimum(ex, n_exp - 1)

        @pl.when((ex < n_exp) & (nblk_ref[exc] > 0))
        def _():
            for piece, c in enumerate(w_copies(exc, exc % ns)):
                c.start(priority=piece % 2)

    @pl.when(e == 0)
    def _():
        for ahead in range(ns - 1):
            start_weights(ahead)

    start_weights(e + ns - 1)

    def block_lines(g):
        return pl.ds(pl.multiple_of(g * blk_lines, blk_lines), blk_lines)

    def in_copy(g, slot):
        return pltpu.make_async_copy(xs_hbm.at[block_lines(g), :], xin_ref.at[slot], in_sem.at[slot])

    def out_copy(g, slot):
        return pltpu.make_async_copy(yout_ref.at[slot], y_hbm.at[block_lines(g), :], out_sem.at[slot])

    @pl.when((e == 0) & (n_valid > 0))
    def _():
        in_copy(0, 0).start(priority=0)

    nb = nblk_ref[e]
    g0 = bstart_ref[e]

    @pl.when(nb > 0)
    def _():
        w_slot = e % ns
        for c in w_copies(e, w_slot):
            c.wait()
        wgb_ref[...] = wgf_ref[w_slot].astype(BF16)
        wub_ref[...] = wuf_ref[w_slot].astype(BF16)
        wdb_ref[...] = wdf_ref[w_slot].astype(BF16)

        def block(j, carry):
            g = g0 + j
            slot = g % 2
            in_copy(g, slot).wait()

            @pl.when(g + 1 < n_valid)
            def _():
                in_copy(g + 1, 1 - slot).start(priority=0)

            @pl.when(g >= 2)
            def _():
                out_copy(g - 2, slot).wait()

            xb = _unpack_rows(_load_rows_from_lines(xin_ref.at[slot], 0, bm, nt)).astype(BF16)
            gate = jnp.dot(xb, wgb_ref[...], preferred_element_type=F32)
            up = jnp.dot(xb, wub_ref[...], preferred_element_type=F32)
            hid = (gate * jax.nn.sigmoid(gate) * up).astype(BF16)
            y = jnp.dot(hid, wdb_ref[...], preferred_element_type=F32)
            _store_rows_as_lines(yout_ref.at[slot], 0, _pack_rows(y))
            out_copy(g, slot).start(priority=1)
            return carry
        lax.fori_loop(0, nb, block, 0)

    @pl.when(e == pl.num_programs(0) - 1)
    def _():
        for back in (2, 1):
            @pl.when(n_valid >= back)
            def _(back=back):
                out_copy(n_valid - back, (n_valid - back) % 2).wait()

        yout_ref[0] = jnp.zeros(yout_ref.shape[1:], yout_ref.dtype)

        def fill(g, carry):
            out_copy(g, 0).start()
            return carry
        lax.fori_loop(n_valid, n_blocks, fill, 0)

        def drain(g, carry):
            out_copy(g, 0).wait()
            return carry
        lax.fori_loop(n_valid, n_blocks, drain, 0)


def _experts(xs, nblk, bstart, n_valid, w_gate, w_up, w_down, layer):
    _, n_exp, d, f = w_gate.shape
    bm = MOE_BLOCK
    nt = d // 2 // LANES
    ns = MOE_W_SLOTS
    grid_spec = pltpu.PrefetchScalarGridSpec(
        num_scalar_prefetch=3,
        grid=(n_exp,),
        in_specs=[pl.BlockSpec(memory_space=pl.ANY)] * 4,
        out_specs=pl.BlockSpec(memory_space=pl.ANY),
        scratch_shapes=[pltpu.VMEM((2, bm * nt, LANES), jnp.uint32),
                        pltpu.VMEM((2, bm * nt, LANES), jnp.uint32),
                        pltpu.VMEM((ns, d, f), F32),
                        pltpu.VMEM((ns, d, f), F32),
                        pltpu.VMEM((ns, f, d), F32),
                        pltpu.VMEM((d, f), BF16),
                        pltpu.VMEM((d, f), BF16),
                        pltpu.VMEM((f, d), BF16),
                        pltpu.SemaphoreType.DMA((2,)),
                        pltpu.SemaphoreType.DMA((2,)),
                        pltpu.SemaphoreType.DMA((ns,))],
    )
    return pl.pallas_call(
        functools.partial(_expert_kernel, layer=layer),
        grid_spec=grid_spec,
        out_shape=jax.ShapeDtypeStruct(xs.shape, jnp.uint32),
        compiler_params=_params("arbitrary"),
        name="moe_experts",
    )(nblk, bstart, n_valid, xs, w_gate, w_up, w_down)


def _combine_ln_kernel(dest_ref, y_hbm, x_ref, route_ref, g_ref, b_ref, o_ref, yg_ref, sem):
    i = pl.program_id(0)
    n = pl.num_programs(0)
    tm = x_ref.shape[0]
    nt = yg_ref.shape[1] // (TOP_K * tm)

    def lines(row, n_rows=1):
        return pl.ds(pl.multiple_of(row * nt, nt), n_rows * nt)

    def start_gather(step, slot):
        base = step * (TOP_K * tm)

        def body(r, carry):
            for k in range(TOP_K):
                src = dest_ref[base + TOP_K * r + k]
                pltpu.make_async_copy(y_hbm.at[lines(src), :], yg_ref.at[slot, lines(k * tm + r), :],
                                      sem.at[slot]).start(priority=k)
            return carry
        lax.fori_loop(0, tm, body, 0, unroll=8)

    def wait_gather(slot):
        pltpu.make_async_copy(y_hbm.at[lines(0, TOP_K * tm), :], yg_ref.at[slot], sem.at[slot]).wait()

    @pl.when(i == 0)
    def _():
        start_gather(0, 0)

    @pl.when(i + 1 < n)
    def _():
        start_gather(i + 1, (i + 1) % 2)

    slot = i % 2
    wait_gather(slot)
    route = route_ref[...]
    lane = lax.broadcasted_iota(jnp.int32, route.shape, 1)
    g1 = jnp.sum(jnp.where(lane == 2, route, 0.0), axis=1, keepdims=True)
    g2 = jnp.sum(jnp.where(lane == 3, route, 0.0), axis=1, keepdims=True)
    y1 = _unpack_rows(_load_rows_from_lines(yg_ref.at[slot], 0, tm, nt))
    y2 = _unpack_rows(_load_rows_from_lines(yg_ref.at[slot], tm * nt, tm, nt))
    z = DEEPNORM_ALPHA * x_ref[...] + (y1 * g1 + y2 * g2)
    o_ref[...] = _layer_norm_rows(z, g_ref[...], b_ref[...])


def _combine_ln(y, dest, x, route, g, b):
    t, d = x.shape
    tm = min(COMB_TM, t)
    grid_spec = pltpu.PrefetchScalarGridSpec(
        num_scalar_prefetch=1,
        grid=(t // tm,),
        in_specs=[pl.BlockSpec(memory_space=pl.ANY),
                  pl.BlockSpec((tm, d), lambda i, dest: (i, 0)),
                  pl.BlockSpec((tm, LANES), lambda i, dest: (i, 0)),
                  pl.BlockSpec((1, d), lambda i, dest: (0, 0)),
                  pl.BlockSpec((1, d), lambda i, dest: (0, 0))],
        out_specs=pl.BlockSpec((tm, d), lambda i, dest: (i, 0)),
        scratch_shapes=[pltpu.VMEM((2, TOP_K * tm * (d // 2 // LANES), LANES), y.dtype),
                        pltpu.SemaphoreType.DMA((2,))],
    )
    return pl.pallas_call(
        _combine_ln_kernel,
        grid_spec=grid_spec,
        out_shape=jax.ShapeDtypeStruct((t, d), F32),
        compiler_params=_params("arbitrary"),
        name="moe_combine_ln",
    )(dest, y, x, route, g.reshape(1, d), b.reshape(1, d))


def _moe_ffn_ln(x, xp, route, counts, w_gate, w_up, w_down, layer, ln_g, ln_b):
    t, d = x.shape
    bm = MOE_BLOCK
    eid = route[:, 0:TOP_K].astype(jnp.int32)
    rank = route[:, 4:4 + TOP_K].astype(jnp.int32)
    cnt = counts[0, N_GROUPS:N_GROUPS + N_EXPERTS].astype(jnp.int32)
    nblk = (cnt + bm - 1) // bm
    bend = jnp.cumsum(nblk)
    bstart = bend - nblk
    pstart = bstart * bm
    onehot = eid[:, :, None] == jnp.arange(N_EXPERTS, dtype=jnp.int32)
    dest = (jnp.sum(jnp.where(onehot, pstart, 0), axis=-1) + rank).reshape(-1)
    n_valid = bend[-1:]
    n_slots = (-(-t * TOP_K // bm) + N_EXPERTS) * bm
    xs = _dispatch(xp, t, dest, cnt, pstart, n_valid, n_slots)
    y = _experts(xs, nblk, bstart, n_valid, w_gate, w_up, w_down, layer)
    return _combine_ln(y, dest, x, route, ln_g, ln_b)


def _gla_log_alpha(x, wl_ref, wu_ref, b_ref):
    g_low = _dot_hi_lo(x, wl_ref)
    n = wu_ref.shape[1] // 2
    gh, gl = _split2(g_low)
    a = jnp.dot(gh, wu_ref[...], preferred_element_type=F32)
    c = jnp.dot(gl, wu_ref[:, :n], preferred_element_type=F32)
    return _log_sigmoid(a[:, :n] + a[:, n:] + c + b_ref[...]) / GLA_GATE_TAU


def _gla_kernel(q_ref, k_ref, v_ref, r_ref, x_ref, wl_ref, wu_ref, bg_ref, g_ref, o_ref, state_ref):
    @pl.when(pl.program_id(0) == 0)
    def _():
        state_ref[...] = jnp.zeros_like(state_ref)

    log_alpha = _gla_log_alpha(x_ref[...], wl_ref, wu_ref, bg_ref)

    rows = q_ref.shape[0]
    nh, dk, dv = state_ref.shape
    cs = GLA_CHUNK
    n_chunks = rows // cs
    rr = lax.broadcasted_iota(jnp.int32, (rows, rows), 0)
    cc = lax.broadcasted_iota(jnp.int32, (rows, rows), 1)
    causal = (rr // cs == cc // cs) & (rr >= cc)
    tri = jnp.where(causal, 1.0, 0.0).astype(BF16)
    lane_chunk = lax.broadcasted_iota(jnp.int32, (dk, rows), 1) // cs

    def chunk_row(a, i):
        return jnp.concatenate([jnp.broadcast_to(a[c * cs + i:c * cs + i + 1, :], (cs, a.shape[1]))
                                for c in range(n_chunks)], axis=0)

    heads = range(nh)
    kcs = [pl.ds(h * dk, dk) for h in heads]
    vcs = [pl.ds(h * dv, dv) for h in heads]
    bs = [_cumsum_rows(log_alpha[:, h * dk:(h + 1) * dk], tri) for h in heads]
    qs = [q_ref[:, kcs[h]].astype(F32) * (dk ** -0.5) for h in heads]
    ks = [k_ref[:, kcs[h]].astype(F32) for h in heads]
    vs = [v_ref[:, vcs[h]] for h in heads]
    b_mids = [chunk_row(bs[h], cs // 2 - 1) for h in heads]
    att = [jnp.where(causal, _dot_nt((qs[h] * jnp.exp(bs[h] - b_mids[h])).astype(BF16),
                                     (ks[h] * jnp.exp(b_mids[h] - bs[h])).astype(BF16)), 0.0) for h in heads]
    o_intra = [jnp.dot(att[h].astype(BF16), vs[h], preferred_element_type=F32) for h in heads]
    q_inter = [(qs[h] * jnp.exp(bs[h])).astype(BF16) for h in heads]
    k_end_t = [(ks[h] * jnp.exp(chunk_row(bs[h], cs - 1) - bs[h])).T for h in heads]
    b_t = [bs[h].T for h in heads]
    states = [state_ref[h] for h in heads]
    outs = [[] for _ in heads]
    for c in range(n_chunks):
        sl = slice(c * cs, (c + 1) * cs)
        for h in heads:
            outs[h].append(o_intra[h][sl] + jnp.dot(q_inter[h][sl], states[h].astype(BF16),
                                                    preferred_element_type=F32))
            kv = jnp.dot(jnp.where(lane_chunk == c, k_end_t[h], 0.0).astype(BF16), vs[h],
                         preferred_element_type=F32)
            decay = jnp.exp(b_t[h][:, (c + 1) * cs - 1:(c + 1) * cs])
            states[h] = decay * states[h] + kv
    for h in heads:
        state_ref[h] = states[h]
        o = jnp.concatenate(outs[h], axis=0)
        o = o * lax.rsqrt(jnp.mean(o * o, axis=-1, keepdims=True) + RMS_EPS) * g_ref[...]
        r = r_ref[:, vcs[h]].astype(F32)
        o_ref[:, vcs[h]] = (o * (r * jax.nn.sigmoid(r))).astype(o_ref.dtype)


def _gla(proj, x, w_low_t, w_gate_up, b_gate, norm_g):
    s, d = x.shape
    rank, dk_all = w_gate_up.shape
    nh = GLA_HEADS
    dk = dk_all // nh
    dv = norm_g.shape[0]
    dv_all = nh * dv
    assert 2 * dk_all == dv_all
    rows = min(GLA_ROWS, s)
    wl = _pack_hi_lo(w_low_t)
    wu_pad = jnp.zeros((LANES, dk_all), F32).at[:rank].set(w_gate_up)
    wu = jnp.concatenate(_split2(wu_pad), axis=1)
    return pl.pallas_call(
        _gla_kernel,
        grid=(s // rows,),
        in_specs=[pl.BlockSpec((rows, dk_all), lambda i: (i, 0)),
                  pl.BlockSpec((rows, dk_all), lambda i: (i, 1)),
                  pl.BlockSpec((rows, dv_all), lambda i: (i, 1)),
                  pl.BlockSpec((rows, dv_all), lambda i: (i, 2)),
                  pl.BlockSpec((rows, d), lambda i: (i, 0)),
                  pl.BlockSpec((2 * LANES, d), lambda i: (0, 0)),
                  pl.BlockSpec((LANES, 2 * dk_all), lambda i: (0, 0)),
                  pl.BlockSpec((1, dk_all), lambda i: (0, 0)),
                  pl.BlockSpec((1, dv), lambda i: (0, 0))],
        out_specs=pl.BlockSpec((rows, dv_all), lambda i: (i, 0)),
        out_shape=jax.ShapeDtypeStruct((s, dv_all), BF16),
        scratch_shapes=[pltpu.VMEM((nh, dk, dv), F32)],
        compiler_params=_params("arbitrary"),
        name="gla_chunks",
    )(proj, proj, proj, proj, x, wl, wu, b_gate.reshape(1, dk_all), norm_g.reshape(1, dv))


def kernel(x, fox_w_in, fox_b_f, fox_w_o, gla_w_in, gla_w_gate_up, gla_b_gate, gla_norm_g, gla_w_o,
           ln_mix_g, ln_mix_b, ln_ffn_g, ln_ffn_b, moe_w_group, moe_b_group, moe_w_expert,
           moe_b_expert, moe_w_gate, moe_w_up, moe_w_down):
    bsz, s, d = x.shape
    outs = []
    for bi in range(bsz):
        xt = x[bi]
        for i in range(DEPTH):
            j = i // 2
            if i % 2 == 0:
                w_in_t = fox_w_in[j].T
                qkv = _matmul_nt(xt, w_in_t, 3 * d, BF16)
                c = _fox_gates(xt, w_in_t[3 * d:], fox_b_f[j])
                o = _fox_attention(qkv, c, FOX_HEADS)
                w_o = fox_w_o[j]
            else:
                w_in_t = gla_w_in[j].T
                n_main = w_in_t.shape[0] - gla_w_gate_up.shape[1]
                proj = _matmul_nt(xt, w_in_t, n_main, BF16)
                o = _gla(proj, xt, w_in_t[n_main:], gla_w_gate_up[j], gla_b_gate[j], gla_norm_g[j])
                w_o = gla_w_o[j]
            xt, xp, route, counts = _proj_ln_route(o, w_o.astype(BF16), xt, ln_mix_g[i], ln_mix_b[i], moe_w_group[i],
                                                   moe_b_group[i], moe_w_expert[i], moe_b_expert[i])
            xt = _moe_ffn_ln(xt, xp, route, counts, moe_w_gate, moe_w_up, moe_w_down, i, ln_ffn_g[i], ln_ffn_b[i])
        outs.append(xt)
    return outs[0].reshape(1, s, d) if bsz == 1 else jnp.stack(outs, axis=0)
```

```python
import functools

import jax
import jax.numpy as jnp
from jax import lax
from jax.experimental import pallas as pl
from jax.experimental.pallas import tpu as pltpu

F32 = jnp.float32
BF16 = jnp.bfloat16

DEPTH = 2
FOX_HEADS = 16
FOX_HEAD_DIM = 128
GLA_HEADS = 4
GLA_CHUNK = 64
GLA_GATE_TAU = 16.0
N_GROUPS = 8
EXPERTS_PER_GROUP = 8
N_EXPERTS = N_GROUPS * EXPERTS_PER_GROUP
TOP_K = 2
DEEPNORM_ALPHA = (2 * DEPTH) ** 0.25
LN_EPS = 1e-5
RMS_EPS = 1e-6

LANES = 128
VMEM_LIMIT = 56 * 2**20

MM_TM, MM_TN = 1024, 1024
ROW_TILE = 512
ATT_TQ, ATT_TK = 1024, 512
ATT_GROUP = 256
GLA_ROWS = 256
MOE_BLOCK = 128
MOE_W_SLOTS = 3
MOE_W_PIECES = 4
DISPATCH_TM = 1024
COMB_TM = 256


def _params(*sem):
    return pltpu.CompilerParams(dimension_semantics=sem, vmem_limit_bytes=VMEM_LIMIT)


def _split2(a):
    hi = a.astype(BF16)
    lo = (a - hi.astype(F32)).astype(BF16)
    return hi, lo


def _split3(a):
    hi = a.astype(BF16)
    r = a - hi.astype(F32)
    mid = r.astype(BF16)
    lo = (r - mid.astype(F32)).astype(BF16)
    return hi, mid, lo


def _pack_hi_lo(w_t, n_pad=LANES):
    n, k = w_t.shape
    wp = jnp.zeros((n_pad, k), F32).at[:n].set(w_t)
    hi, lo = _split2(wp)
    return jnp.concatenate([hi, lo], axis=0)


def _dot_nt(a, b):
    return lax.dot_general(a, b, (((1,), (1,)), ((), ())), preferred_element_type=F32)


def _dot_hi_lo(x, whl_ref, n_pad=LANES):
    xh, xl = _split2(x)
    a = _dot_nt(xh, whl_ref[...])
    b = _dot_nt(xl, whl_ref[:n_pad, :])
    return a[:, :n_pad] + a[:, n_pad:] + b


def _pack_rows(a):
    half = a.shape[1] // 2
    hi = lax.bitcast_convert_type(a[:, :half].astype(BF16).astype(F32), jnp.uint32)
    lo = lax.bitcast_convert_type(a[:, half:].astype(BF16).astype(F32), jnp.uint32)
    return hi | (lo >> 16)


def _unpack_rows(w):
    hi = lax.bitcast_convert_type(w & jnp.uint32(0xFFFF0000), F32)
    lo = lax.bitcast_convert_type(w << 16, F32)
    return jnp.concatenate([hi, lo], axis=1)


def _store_rows_as_lines(ref, first_line, packed):
    n_rows, width = packed.shape
    nt = width // LANES
    for j in range(nt):
        ref[pl.ds(first_line + j, n_rows, stride=nt), :] = packed[:, j * LANES:(j + 1) * LANES]


def _load_rows_from_lines(ref, first_line, n_rows, nt):
    return jnp.concatenate([ref[pl.ds(first_line + j, n_rows, stride=nt), :] for j in range(nt)], axis=1)


def _log_sigmoid(x):
    return -(jnp.maximum(-x, 0.0) + jnp.log1p(jnp.exp(-jnp.abs(x))))


def _cumsum_rows(a, incl_tri):
    n = a.shape[1]
    parts = jnp.concatenate(_split3(a), axis=1)
    c = jnp.dot(incl_tri, parts, preferred_element_type=F32)
    return c[:, :n] + c[:, n:2 * n] + c[:, 2 * n:]


def _tri(n, strict=False):
    r = lax.broadcasted_iota(jnp.int32, (n, n), 0)
    c = lax.broadcasted_iota(jnp.int32, (n, n), 1)
    return jnp.where((r > c) if strict else (r >= c), 1.0, 0.0).astype(BF16)


def _layer_norm_rows(z, g, b):
    mu = jnp.mean(z, axis=-1, keepdims=True)
    d = z - mu
    var = jnp.mean(d * d, axis=-1, keepdims=True)
    return d * lax.rsqrt(var + LN_EPS) * g + b


def _mm_kernel(x_ref, w_ref, o_ref, wb_ref):
    @pl.when(pl.program_id(1) == 0)
    def _():
        wb_ref[...] = w_ref[...].astype(BF16)

    o_ref[...] = _dot_nt(x_ref[...].astype(BF16), wb_ref[...]).astype(o_ref.dtype)


def _matmul_nt(x, w_t, n, out_dtype):
    m, k = x.shape
    tm, tn = min(MM_TM, m), min(MM_TN, n)
    return pl.pallas_call(
        _mm_kernel,
        grid=(n // tn, m // tm),
        in_specs=[pl.BlockSpec((tm, k), lambda j, i: (i, 0)),
                  pl.BlockSpec((tn, k), lambda j, i: (j, 0))],
        out_specs=pl.BlockSpec((tm, tn), lambda j, i: (i, j)),
        out_shape=jax.ShapeDtypeStruct((m, n), out_dtype),
        scratch_shapes=[pltpu.VMEM((tn, k), BF16)],
        compiler_params=_params("arbitrary", "arbitrary"),
        name="dense_proj",
    )(x, w_t)


def _fox_gate_kernel(x_ref, w_ref, b_ref, c_ref, carry_ref):
    @pl.when(pl.program_id(0) == 0)
    def _():
        carry_ref[...] = jnp.zeros_like(carry_ref)

    ts = x_ref.shape[0]
    logits = _dot_hi_lo(x_ref[...], w_ref) + b_ref[...]
    log_f = _log_sigmoid(logits)
    c = _cumsum_rows(log_f, _tri(ts)) + carry_ref[...]
    carry_ref[...] = c[ts - 1:ts, :]
    c_ref[...] = c


def _fox_gates(x, w_f_t, b_f):
    s, d = x.shape
    h = w_f_t.shape[0]
    ts = min(ROW_TILE, s)
    whl = _pack_hi_lo(w_f_t)
    bias = jnp.zeros((1, LANES), F32).at[0, :h].set(b_f)
    return pl.pallas_call(
        _fox_gate_kernel,
        grid=(s // ts,),
        in_specs=[pl.BlockSpec((ts, d), lambda i: (i, 0)),
                  pl.BlockSpec((2 * LANES, d), lambda i: (0, 0)),
                  pl.BlockSpec((1, LANES), lambda i: (0, 0))],
        out_specs=pl.BlockSpec((ts, LANES), lambda i: (i, 0)),
        out_shape=jax.ShapeDtypeStruct((s, LANES), F32),
        scratch_shapes=[pltpu.VMEM((1, LANES), F32)],
        compiler_params=_params("arbitrary"),
        name="fox_gates",
    )(x, whl, bias)


def _fox_attn_kernel(q_ref, k_ref, v_ref, c_ref, o_ref, kaug_ref, vt_ref, qt_ref, st0_ref, st1_ref,
                     p0_ref, p1_ref, alpha_ref, bmax_ref, acc_ref, m_ref):
    h = pl.program_id(0)
    qi = pl.program_id(1)
    tq, dh = q_ref.shape
    s_len = k_ref.shape[0]
    tk = ATT_TK
    log2e = 1.4426950408889634
    scale = dh ** -0.5 * log2e

    def head_column(rows, n):
        lane = lax.broadcasted_iota(jnp.int32, (n, LANES), 1)
        return log2e * jnp.sum(jnp.where(lane == h, c_ref[rows, :], 0.0), axis=1, keepdims=True)

    def bias_columns(col, first, n):
        lane = lax.broadcasted_iota(jnp.int32, (n, LANES), 1)
        hi, mid, lo = _split3(col)
        ones_first = 3 - first
        out = jnp.where((lane >= ones_first) & (lane < ones_first + 3), 1.0, 0.0)
        out = jnp.where(lane == first, hi.astype(F32), out)
        out = jnp.where(lane == first + 1, mid.astype(F32), out)
        return jnp.where(lane == first + 2, lo.astype(F32), out).astype(BF16)

    @pl.when(qi == 0)
    def _():
        ones_row = jnp.where(lax.broadcasted_iota(jnp.int32, (16, tk), 0) == 0, 1.0, 0.0).astype(BF16)

        def build(ci, carry):
            rows = pl.ds(pl.multiple_of(ci * tk, tk), tk)
            kaug_ref[rows, :dh] = k_ref[rows, :]
            kaug_ref[rows, dh:] = bias_columns(-head_column(rows, tk), 0, tk)
            vt_ref[:dh, rows] = v_ref[rows, :].astype(F32).T.astype(BF16)
            vt_ref[dh:, rows] = ones_row
            return carry
        lax.fori_loop(0, s_len // tk, build, 0)

    q_rows = pl.ds(pl.multiple_of(qi * tq, tq), tq)
    q_aug = jnp.concatenate([q_ref[...].astype(F32) * scale,
                             bias_columns(head_column(q_rows, tq), 3, tq).astype(F32)], axis=1)
    qt_ref[...] = q_aug.T.astype(BF16)

    acc_ref[...] = jnp.zeros_like(acc_ref)
    m_ref[...] = jnp.full_like(m_ref, -jnp.inf)

    st_refs, p_refs = (st0_ref, st1_ref), (p0_ref, p1_ref)

    gw = ATT_GROUP
    groups = [pl.ds(g * gw, gw) for g in range(tq // gw)]

    def scores(k_start, slot, diag_offset=None, lanes=None):
        for gl in ([lanes] if lanes is not None else groups):
            st = jnp.dot(kaug_ref[pl.ds(k_start, tk), :], qt_ref[:, gl], preferred_element_type=F32)
            if diag_offset is not None:
                kr = lax.broadcasted_iota(jnp.int32, (tk, gw), 0) + diag_offset
                qc = lax.broadcasted_iota(jnp.int32, (tk, gw), 1) + gl.start
                st = jnp.where(kr <= qc, st, -jnp.inf)
            st_refs[slot][:, gl] = st
            bmax_ref[slot, :, gl] = jnp.max(st, axis=0, keepdims=True)

    def softmax(slot, lanes=None):
        for gl in ([lanes] if lanes is not None else groups):
            m_prev = m_ref[:, gl]
            m_new = jnp.maximum(m_prev, bmax_ref[slot, :, gl])
            m_ref[:, gl] = m_new
            p_refs[slot][:, gl] = jnp.exp2(st_refs[slot][:, gl] - m_new).astype(BF16)
            alpha_ref[slot, :, gl] = jnp.exp2(m_prev - m_new)

    def values(k_start, slot, lanes=None):
        for gl in ([lanes] if lanes is not None else groups):
            acc_ref[:, gl] = alpha_ref[slot, :, gl] * acc_ref[:, gl] + jnp.dot(
                vt_ref[:, pl.ds(k_start, tk)], p_refs[slot][:, gl], preferred_element_type=F32)

    n_diag = tq // tk
    assert n_diag == 2
    d0 = pl.multiple_of(qi * tq, tk)
    d1 = pl.multiple_of(qi * tq + tk, tk)
    n_full = qi * n_diag

    def full_start(i):
        return pl.multiple_of(jnp.minimum(i, jnp.maximum(n_full - 1, 0)) * tk, tk)

    scores(d0, 0, 0)
    scores(d1, 1, tk)
    softmax(0)
    scores(full_start(0), 0)
    softmax(1)
    values(d0, 0)

    def pair(j, carry):
        i0 = 2 * j
        prev = jnp.where(j == 0, d1, (i0 - 1) * tk)
        for gl in groups:
            scores(full_start(i0 + 1), 1, lanes=gl)
            softmax(0, lanes=gl)
            values(pl.multiple_of(prev, tk), 1, lanes=gl)
        for gl in groups:
            scores(full_start(i0 + 2), 0, lanes=gl)
            softmax(1, lanes=gl)
            values(pl.multiple_of(i0 * tk, tk), 0, lanes=gl)
        return carry

    lax.fori_loop(0, qi, pair, 0)
    last = jnp.where(qi == 0, d1, (n_full - 1) * tk)
    values(pl.multiple_of(last, tk), 1)

    o_ref[...] = (acc_ref[:dh, :] / acc_ref[dh:dh + 1, :]).T.astype(o_ref.dtype)


def _fox_attention(qkv, c, n_heads):
    s = qkv.shape[0]
    dh = FOX_HEAD_DIM
    tq = min(ATT_TQ, s)
    hh = n_heads
    return pl.pallas_call(
        _fox_attn_kernel,
        grid=(hh, s // tq),
        in_specs=[pl.BlockSpec((tq, dh), lambda h, i: (i, h)),
                  pl.BlockSpec((s, dh), lambda h, i: (0, hh + h)),
                  pl.BlockSpec((s, dh), lambda h, i: (0, 2 * hh + h)),
                  pl.BlockSpec((s, LANES), lambda h, i: (0, 0))],
        out_specs=pl.BlockSpec((tq, dh), lambda h, i: (i, h)),
        out_shape=jax.ShapeDtypeStruct((s, hh * dh), BF16),
        scratch_shapes=[pltpu.VMEM((s, 2 * dh), BF16),
                        pltpu.VMEM((dh + 16, s), BF16),
                        pltpu.VMEM((2 * dh, tq), BF16),
                        pltpu.VMEM((ATT_TK, tq), F32),
                        pltpu.VMEM((ATT_TK, tq), F32),
                        pltpu.VMEM((ATT_TK, tq), BF16),
                        pltpu.VMEM((ATT_TK, tq), BF16),
                        pltpu.VMEM((2, 1, tq), F32),
                        pltpu.VMEM((2, 1, tq), F32),
                        pltpu.VMEM((dh + 16, tq), F32),
                        pltpu.VMEM((1, tq), F32)],
        compiler_params=_params("arbitrary", "arbitrary"),
        name="fox_attention",
    )(qkv, qkv, qkv, c)


def _first_lane_eq(vals, target, lane):
    return jnp.min(jnp.where(vals == target, lane, LANES), axis=1, keepdims=True)


def _route_rows(x, w_ref, b_ref, carry_ref):
    tm = x.shape[0]
    neg = -jnp.inf
    logits = _dot_hi_lo(x, w_ref) + b_ref[...]
    lane = lax.broadcasted_iota(jnp.int32, (tm, LANES), 1)

    gl = jnp.where(lane < N_GROUPS, logits, neg)
    gmax = jnp.max(gl, axis=1, keepdims=True)
    gsum = jnp.sum(jnp.exp(gl - gmax), axis=1, keepdims=True)
    grp_p = 1.0 / gsum
    grp = _first_lane_eq(gl, gmax, lane)

    lo = N_GROUPS + grp * EXPERTS_PER_GROUP
    el = jnp.where((lane >= lo) & (lane < lo + EXPERTS_PER_GROUP), logits, neg)
    emax = jnp.max(el, axis=1, keepdims=True)
    esum = jnp.sum(jnp.exp(el - emax), axis=1, keepdims=True)
    idx1 = _first_lane_eq(el, emax, lane)
    el2 = jnp.where(lane == idx1, neg, el)
    emax2 = jnp.max(el2, axis=1, keepdims=True)
    idx2 = _first_lane_eq(el2, emax2, lane)
    p1 = 1.0 / esum
    p2 = jnp.exp(emax2 - emax) / esum
    psum = p1 + p2
    g1 = grp_p * (p1 / psum)
    g2 = grp_p * (p2 / psum)

    oh1 = lane == idx1
    oh2 = lane == idx2
    both = jnp.where(oh1 | oh2, 1.0, 0.0)
    before = jnp.dot(_tri(tm, strict=True), both.astype(BF16), preferred_element_type=F32)
    before = before + carry_ref[...]
    r1 = jnp.sum(jnp.where(oh1, before, 0.0), axis=1, keepdims=True)
    r2 = jnp.sum(jnp.where(oh2, before, 0.0), axis=1, keepdims=True)
    carry_ref[...] = carry_ref[...] + jnp.sum(both, axis=0, keepdims=True)

    e1 = (idx1 - N_GROUPS).astype(F32)
    e2 = (idx2 - N_GROUPS).astype(F32)
    out = jnp.where(lane == 0, e1, 0.0)
    out = jnp.where(lane == 1, e2, out)
    out = jnp.where(lane == 2, g1, out)
    out = jnp.where(lane == 3, g2, out)
    out = jnp.where(lane == 4, r1, out)
    return jnp.where(lane == 5, r2, out)


def _proj_ln_route_kernel(o_ref, w_ref, x_ref, g_ref, b_ref, wr_ref, br_ref,
                          y_ref, yp_ref, route_ref, cnt_ref, carry_ref):
    @pl.when(pl.program_id(0) == 0)
    def _():
        carry_ref[...] = jnp.zeros_like(carry_ref)

    mix = jnp.dot(o_ref[...], w_ref[...], preferred_element_type=F32)
    y = _layer_norm_rows(DEEPNORM_ALPHA * x_ref[...] + mix, g_ref[...], b_ref[...])
    y_ref[...] = y
    _store_rows_as_lines(yp_ref, 0, _pack_rows(y))
    route_ref[...] = _route_rows(y, wr_ref, br_ref, carry_ref)
    cnt_ref[...] = carry_ref[...]


def _proj_ln_route(o, w, x, g, b, w_group, b_group, w_expert, b_expert):
    s, d = x.shape
    kd = o.shape[1]
    tm = min(ROW_TILE, s)
    whl = _pack_hi_lo(jnp.concatenate([w_group.T, w_expert.T], axis=0))
    nb = N_GROUPS + N_EXPERTS
    bias = jnp.zeros((1, LANES), F32).at[0, :nb].set(jnp.concatenate([b_group, b_expert]))
    return pl.pallas_call(
        _proj_ln_route_kernel,
        grid=(s // tm,),
        in_specs=[pl.BlockSpec((tm, kd), lambda i: (i, 0)),
                  pl.BlockSpec((kd, d), lambda i: (0, 0)),
                  pl.BlockSpec((tm, d), lambda i: (i, 0)),
                  pl.BlockSpec((1, d), lambda i: (0, 0)),
                  pl.BlockSpec((1, d), lambda i: (0, 0)),
                  pl.BlockSpec((2 * LANES, d), lambda i: (0, 0)),
                  pl.BlockSpec((1, LANES), lambda i: (0, 0))],
        out_specs=[pl.BlockSpec((tm, d), lambda i: (i, 0)),
                   pl.BlockSpec((tm * (d // 2 // LANES), LANES), lambda i: (i, 0)),
                   pl.BlockSpec((tm, LANES), lambda i: (i, 0)),
                   pl.BlockSpec((1, LANES), lambda i: (0, 0))],
        out_shape=[jax.ShapeDtypeStruct((s, d), F32),
                   jax.ShapeDtypeStruct((s * (d // 2 // LANES), LANES), jnp.uint32),
                   jax.ShapeDtypeStruct((s, LANES), F32),
                   jax.ShapeDtypeStruct((1, LANES), F32)],
        scratch_shapes=[pltpu.VMEM((1, LANES), F32)],
        compiler_params=_params("arbitrary"),
        name="proj_ln_route",
    )(o, w, x, g.reshape(1, d), b.reshape(1, d), whl, bias)


def _dispatch_kernel(dest_ref, cnt_ref, pstart_ref, nv_ref, x_ref, xs_hbm, zero_ref, sem, zsem, *, nt):
    i = pl.program_id(0)
    tm = x_ref.shape[0] // nt
    bm = MOE_BLOCK
    n_exp = cnt_ref.shape[0]
    n_blocks = xs_hbm.shape[0] // (bm * nt)

    def lines(row, n_rows=1):
        return pl.ds(pl.multiple_of(row * nt, nt), n_rows * nt)

    def pad_copies(e, fn):
        cnt = cnt_ref[e]
        pad = (bm - cnt % bm) % bm
        pos = pstart_ref[e] + cnt
        size = bm // 2
        while size >= 1:
            @pl.when((pad & size) != 0)
            def _(pos=pos, size=size):
                fn(pltpu.make_async_copy(zero_ref.at[lines(0, size), :], xs_hbm.at[lines(pos, size), :], zsem))
            pos = pos + (pad & size)
            size //= 2

    def tail_copy(g):
        return pltpu.make_async_copy(zero_ref, xs_hbm.at[lines(g * bm, bm), :], zsem)

    def for_all_fills(fn):
        def per_expert(e, carry):
            pad_copies(e, fn)
            return carry
        lax.fori_loop(0, n_exp, per_expert, 0)

        def per_tail(g, carry):
            fn(tail_copy(g))
            return carry
        lax.fori_loop(nv_ref[0], n_blocks, per_tail, 0)

    @pl.when(i == 0)
    def _():
        zero_ref[...] = jnp.zeros_like(zero_ref)
        for_all_fills(lambda c: c.start())

    base = i * (TOP_K * tm)

    def body(r, carry):
        for k in range(TOP_K):
            dst = dest_ref[base + TOP_K * r + k]
            pltpu.make_async_copy(x_ref.at[lines(r), :], xs_hbm.at[lines(dst), :], sem).start(priority=k)
        return carry
    lax.fori_loop(0, tm, body, 0, unroll=8)

    @pl.when(i == 0)
    def _():
        for_all_fills(lambda c: c.wait())

    for _ in range(TOP_K):
        pltpu.make_async_copy(x_ref, xs_hbm.at[lines(0, tm), :], sem).wait()


def _dispatch(xp, n_rows, dest, cnt, pstart, n_valid, n_slots):
    nt = xp.shape[0] // n_rows
    tm = min(DISPATCH_TM, n_rows)
    grid_spec = pltpu.PrefetchScalarGridSpec(
        num_scalar_prefetch=4,
        grid=(n_rows // tm,),
        in_specs=[pl.BlockSpec((tm * nt, LANES), lambda i, *_: (i, 0))],
        out_specs=pl.BlockSpec(memory_space=pl.ANY),
        scratch_shapes=[pltpu.VMEM((MOE_BLOCK * nt, LANES), xp.dtype),
                        pltpu.SemaphoreType.DMA,
                        pltpu.SemaphoreType.DMA],
    )
    return pl.pallas_call(
        functools.partial(_dispatch_kernel, nt=nt),
        grid_spec=grid_spec,
        out_shape=jax.ShapeDtypeStruct((n_slots * nt, LANES), xp.dtype),
        compiler_params=_params("arbitrary"),
        name="moe_dispatch",
    )(dest, cnt, pstart, n_valid, xp)


def _expert_kernel(nblk_ref, bstart_ref, nv_ref, xs_hbm, wg_hbm, wu_hbm, wd_hbm, y_hbm,
                   xin_ref, yout_ref, wgf_ref, wuf_ref, wdf_ref, wgb_ref, wub_ref, wdb_ref,
                   in_sem, out_sem, w_sem, *, layer):
    e = pl.program_id(0)
    n_exp = pl.num_programs(0)
    n_valid = nv_ref[0]
    bm = MOE_BLOCK
    blk_lines = xin_ref.shape[1]
    nt = blk_lines // bm
    n_blocks = y_hbm.shape[0] // blk_lines
    ns = MOE_W_SLOTS

    def w_copies(ex, slot):
        copies = []
        for src, dst in ((wg_hbm, wgf_ref), (wu_hbm, wuf_ref), (wd_hbm, wdf_ref)):
            rows = src.shape[2] // MOE_W_PIECES
            for piece in range(MOE_W_PIECES):
                sl = pl.ds(piece * rows, rows)
                copies.append(pltpu.make_async_copy(src.at[layer, ex, sl], dst.at[slot, sl], w_sem.at[slot]))
        return copies

    def start_weights(ex):
        exc = jnp.minimum(ex, n_exp - 1)

        @pl.when((ex < n_exp) & (nblk_ref[exc] > 0))
        def _():
            for piece, c in enumerate(w_copies(exc, exc % ns)):
                c.start(priority=piece % 2)

    @pl.when(e == 0)
    def _():
        for ahead in range(ns - 1):
            start_weights(ahead)

    start_weights(e + ns - 1)

    def block_lines(g):
        return pl.ds(pl.multiple_of(g * blk_lines, blk_lines), blk_lines)

    def in_copy(g, slot):
        return pltpu.make_async_copy(xs_hbm.at[block_lines(g), :], xin_ref.at[slot], in_sem.at[slot])

    def out_copy(g, slot):
        return pltpu.make_async_copy(yout_ref.at[slot], y_hbm.at[block_lines(g), :], out_sem.at[slot])

    @pl.when((e == 0) & (n_valid > 0))
    def _():
        in_copy(0, 0).start(priority=0)

    nb = nblk_ref[e]
    g0 = bstart_ref[e]

    @pl.when(nb > 0)
    def _():
        w_slot = e % ns
        for c in w_copies(e, w_slot):
            c.wait()
        wgb_ref[...] = wgf_ref[w_slot].astype(BF16)
        wub_ref[...] = wuf_ref[w_slot].astype(BF16)
        wdb_ref[...] = wdf_ref[w_slot].astype(BF16)

        def block(j, carry):
            g = g0 + j
            slot = g % 2
            in_copy(g, slot).wait()

            @pl.when(g + 1 < n_valid)
            def _():
                in_copy(g + 1, 1 - slot).start(priority=0)

            @pl.when(g >= 2)
            def _():
                out_copy(g - 2, slot).wait()

            xb = _unpack_rows(_load_rows_from_lines(xin_ref.at[slot], 0, bm, nt)).astype(BF16)
            gate = jnp.dot(xb, wgb_ref[...], preferred_element_type=F32)
            up = jnp.dot(xb, wub_ref[...], preferred_element_type=F32)
            hid = (gate * jax.nn.sigmoid(gate) * up).astype(BF16)
            y = jnp.dot(hid, wdb_ref[...], preferred_element_type=F32)
            _store_rows_as_lines(yout_ref.at[slot], 0, _pack_rows(y))
            out_copy(g, slot).start(priority=1)
            return carry
        lax.fori_loop(0, nb, block, 0)

    @pl.when(e == pl.num_programs(0) - 1)
    def _():
        for back in (2, 1):
            @pl.when(n_valid >= back)
            def _(back=back):
                out_copy(n_valid - back, (n_valid - back) % 2).wait()

        yout_ref[0] = jnp.zeros(yout_ref.shape[1:], yout_ref.dtype)

        def fill(g, carry):
            out_copy(g, 0).start()
            return carry
        lax.fori_loop(n_valid, n_blocks, fill, 0)

        def drain(g, carry):
            out_copy(g, 0).wait()
            return carry
        lax.fori_loop(n_valid, n_blocks, drain, 0)


def _experts(xs, nblk, bstart, n_valid, w_gate, w_up, w_down, layer):
    _, n_exp, d, f = w_gate.shape
    bm = MOE_BLOCK
    nt = d // 2 // LANES
    ns = MOE_W_SLOTS
    grid_spec = pltpu.PrefetchScalarGridSpec(
        num_scalar_prefetch=3,
        grid=(n_exp,),
        in_specs=[pl.BlockSpec(memory_space=pl.ANY)] * 4,
        out_specs=pl.BlockSpec(memory_space=pl.ANY),
        scratch_shapes=[pltpu.VMEM((2, bm * nt, LANES), jnp.uint32),
                        pltpu.VMEM((2, bm * nt, LANES), jnp.uint32),
                        pltpu.VMEM((ns, d, f), F32),
                        pltpu.VMEM((ns, d, f), F32),
                        pltpu.VMEM((ns, f, d), F32),
                        pltpu.VMEM((d, f), BF16),
                        pltpu.VMEM((d, f), BF16),
                        pltpu.VMEM((f, d), BF16),
                        pltpu.SemaphoreType.DMA((2,)),
                        pltpu.SemaphoreType.DMA((2,)),
                        pltpu.SemaphoreType.DMA((ns,))],
    )
    return pl.pallas_call(
        functools.partial(_expert_kernel, layer=layer),
        grid_spec=grid_spec,
        out_shape=jax.ShapeDtypeStruct(xs.shape, jnp.uint32),
        compiler_params=_params("arbitrary"),
        name="moe_experts",
    )(nblk, bstart, n_valid, xs, w_gate, w_up, w_down)


def _combine_ln_kernel(dest_ref, y_hbm, x_ref, route_ref, g_ref, b_ref, o_ref, yg_ref, sem):
    i = pl.program_id(0)
    n = pl.num_programs(0)
    tm = x_ref.shape[0]
    nt = yg_ref.shape[1] // (TOP_K * tm)

    def lines(row, n_rows=1):
        return pl.ds(pl.multiple_of(row * nt, nt), n_rows * nt)

    def start_gather(step, slot):
        base = step * (TOP_K * tm)

        def body(r, carry):
            for k in range(TOP_K):
                src = dest_ref[base + TOP_K * r + k]
                pltpu.make_async_copy(y_hbm.at[lines(src), :], yg_ref.at[slot, lines(k * tm + r), :],
                                      sem.at[slot]).start(priority=k)
            return carry
        lax.fori_loop(0, tm, body, 0, unroll=8)

    def wait_gather(slot):
        pltpu.make_async_copy(y_hbm.at[lines(0, TOP_K * tm), :], yg_ref.at[slot], sem.at[slot]).wait()

    @pl.when(i == 0)
    def _():
        start_gather(0, 0)

    @pl.when(i + 1 < n)
    def _():
        start_gather(i + 1, (i + 1) % 2)

    slot = i % 2
    wait_gather(slot)
    route = route_ref[...]
    lane = lax.broadcasted_iota(jnp.int32, route.shape, 1)
    g1 = jnp.sum(jnp.where(lane == 2, route, 0.0), axis=1, keepdims=True)
    g2 = jnp.sum(jnp.where(lane == 3, route, 0.0), axis=1, keepdims=True)
    y1 = _unpack_rows(_load_rows_from_lines(yg_ref.at[slot], 0, tm, nt))
    y2 = _unpack_rows(_load_rows_from_lines(yg_ref.at[slot], tm * nt, tm, nt))
    z = DEEPNORM_ALPHA * x_ref[...] + (y1 * g1 + y2 * g2)
    o_ref[...] = _layer_norm_rows(z, g_ref[...], b_ref[...])


def _combine_ln(y, dest, x, route, g, b):
    t, d = x.shape
    tm = min(COMB_TM, t)
    grid_spec = pltpu.PrefetchScalarGridSpec(
        num_scalar_prefetch=1,
        grid=(t // tm,),
        in_specs=[pl.BlockSpec(memory_space=pl.ANY),
                  pl.BlockSpec((tm, d), lambda i, dest: (i, 0)),
                  pl.BlockSpec((tm, LANES), lambda i, dest: (i, 0)),
                  pl.BlockSpec((1, d), lambda i, dest: (0, 0)),
                  pl.BlockSpec((1, d), lambda i, dest: (0, 0))],
        out_specs=pl.BlockSpec((tm, d), lambda i, dest: (i, 0)),
        scratch_shapes=[pltpu.VMEM((2, TOP_K * tm * (d // 2 // LANES), LANES), y.dtype),
                        pltpu.SemaphoreType.DMA((2,))],
    )
    return pl.pallas_call(
        _combine_ln_kernel,
        grid_spec=grid_spec,
        out_shape=jax.ShapeDtypeStruct((t, d), F32),
        compiler_params=_params("arbitrary"),
        name="moe_combine_ln",
    )(dest, y, x, route, g.reshape(1, d), b.reshape(1, d))


def _moe_ffn_ln(x, xp, route, counts, w_gate, w_up, w_down, layer, ln_g, ln_b):
    t, d = x.shape
    bm = MOE_BLOCK
    eid = route[:, 0:TOP_K].astype(jnp.int32)
    rank = route[:, 4:4 + TOP_K].astype(jnp.int32)
    cnt = counts[0, N_GROUPS:N_GROUPS + N_EXPERTS].astype(jnp.int32)
    nblk = (cnt + bm - 1) // bm
    bend = jnp.cumsum(nblk)
    bstart = bend - nblk
    pstart = bstart * bm
    onehot = eid[:, :, None] == jnp.arange(N_EXPERTS, dtype=jnp.int32)
    dest = (jnp.sum(jnp.where(onehot, pstart, 0), axis=-1) + rank).reshape(-1)
    n_valid = bend[-1:]
    n_slots = (-(-t * TOP_K // bm) + N_EXPERTS) * bm
    xs = _dispatch(xp, t, dest, cnt, pstart, n_valid, n_slots)
    y = _experts(xs, nblk, bstart, n_valid, w_gate, w_up, w_down, layer)
    return _combine_ln(y, dest, x, route, ln_g, ln_b)


def _gla_log_alpha(x, wl_ref, wu_ref, b_ref):
    g_low = _dot_hi_lo(x, wl_ref)
    n = wu_ref.shape[1] // 2
    gh, gl = _split2(g_low)
    a = jnp.dot(gh, wu_ref[...], preferred_element_type=F32)
    c = jnp.dot(gl, wu_ref[:, :n], preferred_element_type=F32)
    return _log_sigmoid(a[:, :n] + a[:, n:] + c + b_ref[...]) / GLA_GATE_TAU


def _gla_kernel(q_ref, k_ref, v_ref, r_ref, x_ref, wl_ref, wu_ref, bg_ref, g_ref, o_ref, state_ref):
    @pl.when(pl.program_id(0) == 0)
    def _():
        state_ref[...] = jnp.zeros_like(state_ref)

    log_alpha = _gla_log_alpha(x_ref[...], wl_ref, wu_ref, bg_ref)

    rows = q_ref.shape[0]
    nh, dk, dv = state_ref.shape
    cs = GLA_CHUNK
    n_chunks = rows // cs
    rr = lax.broadcasted_iota(jnp.int32, (rows, rows), 0)
    cc = lax.broadcasted_iota(jnp.int32, (rows, rows), 1)
    causal = (rr // cs == cc // cs) & (rr >= cc)
    tri = jnp.where(causal, 1.0, 0.0).astype(BF16)
    lane_chunk = lax.broadcasted_iota(jnp.int32, (dk, rows), 1) // cs

    def chunk_row(a, i):
        return jnp.concatenate([jnp.broadcast_to(a[c * cs + i:c * cs + i + 1, :], (cs, a.shape[1]))
                                for c in range(n_chunks)], axis=0)

    heads = range(nh)
    kcs = [pl.ds(h * dk, dk) for h in heads]
    vcs = [pl.ds(h * dv, dv) for h in heads]
    bs = [_cumsum_rows(log_alpha[:, h * dk:(h + 1) * dk], tri) for h in heads]
    qs = [q_ref[:, kcs[h]].astype(F32) * (dk ** -0.5) for h in heads]
    ks = [k_ref[:, kcs[h]].astype(F32) for h in heads]
    vs = [v_ref[:, vcs[h]] for h in heads]
    b_mids = [chunk_row(bs[h], cs // 2 - 1) for h in heads]
    att = [jnp.where(causal, _dot_nt((qs[h] * jnp.exp(bs[h] - b_mids[h])).astype(BF16),
                                     (ks[h] * jnp.exp(b_mids[h] - bs[h])).astype(BF16)), 0.0) for h in heads]
    o_intra = [jnp.dot(att[h].astype(BF16), vs[h], preferred_element_type=F32) for h in heads]
    q_inter = [(qs[h] * jnp.exp(bs[h])).astype(BF16) for h in heads]
    k_end_t = [(ks[h] * jnp.exp(chunk_row(bs[h], cs - 1) - bs[h])).T for h in heads]
    b_t = [bs[h].T for h in heads]
    states = [state_ref[h] for h in heads]
    outs = [[] for _ in heads]
    for c in range(n_chunks):
        sl = slice(c * cs, (c + 1) * cs)
        for h in heads:
            outs[h].append(o_intra[h][sl] + jnp.dot(q_inter[h][sl], states[h].astype(BF16),
                                                    preferred_element_type=F32))
            kv = jnp.dot(jnp.where(lane_chunk == c, k_end_t[h], 0.0).astype(BF16), vs[h],
                         preferred_element_type=F32)
            decay = jnp.exp(b_t[h][:, (c + 1) * cs - 1:(c + 1) * cs])
            states[h] = decay * states[h] + kv
    for h in heads:
        state_ref[h] = states[h]
        o = jnp.concatenate(outs[h], axis=0)
        o = o * lax.rsqrt(jnp.mean(o * o, axis=-1, keepdims=True) + RMS_EPS) * g_ref[...]
        r = r_ref[:, vcs[h]].astype(F32)
        o_ref[:, vcs[h]] = (o * (r * jax.nn.sigmoid(r))).astype(o_ref.dtype)


def _gla(proj, x, w_low_t, w_gate_up, b_gate, norm_g):
    s, d = x.shape
    rank, dk_all = w_gate_up.shape
    nh = GLA_HEADS
    dk = dk_all // nh
    dv = norm_g.shape[0]
    dv_all = nh * dv
    assert 2 * dk_all == dv_all
    rows = min(GLA_ROWS, s)
    wl = _pack_hi_lo(w_low_t)
    wu_pad = jnp.zeros((LANES, dk_all), F32).at[:rank].set(w_gate_up)
    wu = jnp.concatenate(_split2(wu_pad), axis=1)
    return pl.pallas_call(
        _gla_kernel,
        grid=(s // rows,),
        in_specs=[pl.BlockSpec((rows, dk_all), lambda i: (i, 0)),
                  pl.BlockSpec((rows, dk_all), lambda i: (i, 1)),
                  pl.BlockSpec((rows, dv_all), lambda i: (i, 1)),
                  pl.BlockSpec((rows, dv_all), lambda i: (i, 2)),
                  pl.BlockSpec((rows, d), lambda i: (i, 0)),
                  pl.BlockSpec((2 * LANES, d), lambda i: (0, 0)),
                  pl.BlockSpec((LANES, 2 * dk_all), lambda i: (0, 0)),
                  pl.BlockSpec((1, dk_all), lambda i: (0, 0)),
                  pl.BlockSpec((1, dv), lambda i: (0, 0))],
        out_specs=pl.BlockSpec((rows, dv_all), lambda i: (i, 0)),
        out_shape=jax.ShapeDtypeStruct((s, dv_all), BF16),
        scratch_shapes=[pltpu.VMEM((nh, dk, dv), F32)],
        compiler_params=_params("arbitrary"),
        name="gla_chunks",
    )(proj, proj, proj, proj, x, wl, wu, b_gate.reshape(1, dk_all), norm_g.reshape(1, dv))


def kernel(x, fox_w_in, fox_b_f, fox_w_o, gla_w_in, gla_w_gate_up, gla_b_gate, gla_norm_g, gla_w_o,
           ln_mix_g, ln_mix_b, ln_ffn_g, ln_ffn_b, moe_w_group, moe_b_group, moe_w_expert,
           moe_b_expert, moe_w_gate, moe_w_up, moe_w_down):
    bsz, s, d = x.shape
    outs = []
    for bi in range(bsz):
        xt = x[bi]
        for i in range(DEPTH):
            j = i // 2
            if i % 2 == 0:
                w_in_t = fox_w_in[j].T
                qkv = _matmul_nt(xt, w_in_t, 3 * d, BF16)
                c = _fox_gates(xt, w_in_t[3 * d:], fox_b_f[j])
                o = _fox_attention(qkv, c, FOX_HEADS)
                w_o = fox_w_o[j]
            else:
                w_in_t = gla_w_in[j].T
                n_main = w_in_t.shape[0] - gla_w_gate_up.shape[1]
                proj = _matmul_nt(xt, w_in_t, n_main, BF16)
                o = _gla(proj, xt, w_in_t[n_main:], gla_w_gate_up[j], gla_b_gate[j], gla_norm_g[j])
                w_o = gla_w_o[j]
            xt, xp, route, counts = _proj_ln_route(o, w_o.astype(BF16), xt, ln_mix_g[i], ln_mix_b[i], moe_w_group[i],
                                                   moe_b_group[i], moe_w_expert[i], moe_b_expert[i])
            xt = _moe_ffn_ln(xt, xp, route, counts, moe_w_gate, moe_w_up, moe_w_down, i, ln_ffn_g[i], ln_ffn_b[i])
        outs.append(xt)
    return outs[0].reshape(1, s, d) if bsz == 1 else jnp.stack(outs, axis=0)
```

```python
import functools

import jax
import jax.numpy as jnp
from jax import lax
from jax.experimental import pallas as pl
from jax.experimental.pallas import tpu as pltpu

F32 = jnp.float32
BF16 = jnp.bfloat16

DEPTH = 2
FOX_HEADS = 16
FOX_HEAD_DIM = 128
GLA_HEADS = 4
GLA_CHUNK = 64
GLA_GATE_TAU = 16.0
N_GROUPS = 8
EXPERTS_PER_GROUP = 8
N_EXPERTS = N_GROUPS * EXPERTS_PER_GROUP
TOP_K = 2
DEEPNORM_ALPHA = (2 * DEPTH) ** 0.25
LN_EPS = 1e-5
RMS_EPS = 1e-6

LANES = 128
VMEM_LIMIT = 56 * 2**20

MM_TM, MM_TN = 1024, 1024
ROW_TILE = 512
ATT_TQ, ATT_TK = 1024, 512
ATT_GROUP = 256
GLA_ROWS = 256
MOE_BLOCK = 128
MOE_W_SLOTS = 3
MOE_W_PIECES = 4
DISPATCH_TM = 1024
COMB_TM = 256


def _params(*sem):
    return pltpu.CompilerParams(dimension_semantics=sem, vmem_limit_bytes=VMEM_LIMIT)


def _split2(a):
    hi = a.astype(BF16)
    lo = (a - hi.astype(F32)).astype(BF16)
    return hi, lo


def _split3(a):
    hi = a.astype(BF16)
    r = a - hi.astype(F32)
    mid = r.astype(BF16)
    lo = (r - mid.astype(F32)).astype(BF16)
    return hi, mid, lo


def _pack_hi_lo(w_t, n_pad=LANES):
    n, k = w_t.shape
    wp = jnp.zeros((n_pad, k), F32).at[:n].set(w_t)
    hi, lo = _split2(wp)
    return jnp.concatenate([hi, lo], axis=0)


def _dot_nt(a, b):
    return lax.dot_general(a, b, (((1,), (1,)), ((), ())), preferred_element_type=F32)


def _dot_hi_lo(x, whl_ref, n_pad=LANES):
    xh, xl = _split2(x)
    a = _dot_nt(xh, whl_ref[...])
    b = _dot_nt(xl, whl_ref[:n_pad, :])
    return a[:, :n_pad] + a[:, n_pad:] + b


def _pack_rows(a):
    half = a.shape[1] // 2
    hi = lax.bitcast_convert_type(a[:, :half].astype(BF16).astype(F32), jnp.uint32)
    lo = lax.bitcast_convert_type(a[:, half:].astype(BF16).astype(F32), jnp.uint32)
    return hi | (lo >> 16)


def _unpack_rows(w):
    hi = lax.bitcast_convert_type(w & jnp.uint32(0xFFFF0000), F32)
    lo = lax.bitcast_convert_type(w << 16, F32)
    return jnp.concatenate([hi, lo], axis=1)


def _store_rows_as_lines(ref, first_line, packed):
    n_rows, width = packed.shape
    nt = width // LANES
    for j in range(nt):
        ref[pl.ds(first_line + j, n_rows, stride=nt), :] = packed[:, j * LANES:(j + 1) * LANES]


def _load_rows_from_lines(ref, first_line, n_rows, nt):
    return jnp.concatenate([ref[pl.ds(first_line + j, n_rows, stride=nt), :] for j in range(nt)], axis=1)


def _log_sigmoid(x):
    return -(jnp.maximum(-x, 0.0) + jnp.log1p(jnp.exp(-jnp.abs(x))))


def _cumsum_rows(a, incl_tri):
    n = a.shape[1]
    parts = jnp.concatenate(_split3(a), axis=1)
    c = jnp.dot(incl_tri, parts, preferred_element_type=F32)
    return c[:, :n] + c[:, n:2 * n] + c[:, 2 * n:]


def _tri(n, strict=False):
    r = lax.broadcasted_iota(jnp.int32, (n, n), 0)
    c = lax.broadcasted_iota(jnp.int32, (n, n), 1)
    return jnp.where((r > c) if strict else (r >= c), 1.0, 0.0).astype(BF16)


def _layer_norm_rows(z, g, b):
    mu = jnp.mean(z, axis=-1, keepdims=True)
    d = z - mu
    var = jnp.mean(d * d, axis=-1, keepdims=True)
    return d * lax.rsqrt(var + LN_EPS) * g + b


def _mm_kernel(x_ref, w_ref, o_ref, wb_ref):
    @pl.when(pl.program_id(1) == 0)
    def _():
        wb_ref[...] = w_ref[...].astype(BF16)

    o_ref[...] = _dot_nt(x_ref[...].astype(BF16), wb_ref[...]).astype(o_ref.dtype)


def _matmul_nt(x, w_t, n, out_dtype):
    m, k = x.shape
    tm, tn = min(MM_TM, m), min(MM_TN, n)
    return pl.pallas_call(
        _mm_kernel,
        grid=(n // tn, m // tm),
        in_specs=[pl.BlockSpec((tm, k), lambda j, i: (i, 0)),
                  pl.BlockSpec((tn, k), lambda j, i: (j, 0))],
        out_specs=pl.BlockSpec((tm, tn), lambda j, i: (i, j)),
        out_shape=jax.ShapeDtypeStruct((m, n), out_dtype),
        scratch_shapes=[pltpu.VMEM((tn, k), BF16)],
        compiler_params=_params("arbitrary", "arbitrary"),
        name="dense_proj",
    )(x, w_t)


def _fox_gate_kernel(x_ref, w_ref, b_ref, c_ref, carry_ref):
    @pl.when(pl.program_id(0) == 0)
    def _():
        carry_ref[...] = jnp.zeros_like(carry_ref)

    ts = x_ref.shape[0]
    logits = _dot_hi_lo(x_ref[...], w_ref) + b_ref[...]
    log_f = _log_sigmoid(logits)
    c = _cumsum_rows(log_f, _tri(ts)) + carry_ref[...]
    carry_ref[...] = c[ts - 1:ts, :]
    c_ref[...] = c


def _fox_gates(x, w_f_t, b_f):
    s, d = x.shape
    h = w_f_t.shape[0]
    ts = min(ROW_TILE, s)
    whl = _pack_hi_lo(w_f_t)
    bias = jnp.zeros((1, LANES), F32).at[0, :h].set(b_f)
    return pl.pallas_call(
        _fox_gate_kernel,
        grid=(s // ts,),
        in_specs=[pl.BlockSpec((ts, d), lambda i: (i, 0)),
                  pl.BlockSpec((2 * LANES, d), lambda i: (0, 0)),
                  pl.BlockSpec((1, LANES), lambda i: (0, 0))],
        out_specs=pl.BlockSpec((ts, LANES), lambda i: (i, 0)),
        out_shape=jax.ShapeDtypeStruct((s, LANES), F32),
        scratch_shapes=[pltpu.VMEM((1, LANES), F32)],
        compiler_params=_params("arbitrary"),
        name="fox_gates",
    )(x, whl, bias)


def _fox_attn_kernel(q_ref, k_ref, v_ref, c_ref, o_ref, kaug_ref, vt_ref, qt_ref, st0_ref, st1_ref,
                     p0_ref, p1_ref, alpha_ref, bmax_ref, acc_ref, m_ref):
    h = pl.program_id(0)
    qi = pl.program_id(1)
    tq, dh = q_ref.shape
    s_len = k_ref.shape[0]
    tk = ATT_TK
    log2e = 1.4426950408889634
    scale = dh ** -0.5 * log2e

    def head_column(rows, n):
        lane = lax.broadcasted_iota(jnp.int32, (n, LANES), 1)
        return log2e * jnp.sum(jnp.where(lane == h, c_ref[rows, :], 0.0), axis=1, keepdims=True)

    def bias_columns(col, first, n):
        lane = lax.broadcasted_iota(jnp.int32, (n, LANES), 1)
        hi, mid, lo = _split3(col)
        ones_first = 3 - first
        out = jnp.where((lane >= ones_first) & (lane < ones_first + 3), 1.0, 0.0)
        out = jnp.where(lane == first, hi.astype(F32), out)
        out = jnp.where(lane == first + 1, mid.astype(F32), out)
        return jnp.where(lane == first + 2, lo.astype(F32), out).astype(BF16)

    @pl.when(qi == 0)
    def _():
        ones_row = jnp.where(lax.broadcasted_iota(jnp.int32, (16, tk), 0) == 0, 1.0, 0.0).astype(BF16)

        def build(ci, carry):
            rows = pl.ds(pl.multiple_of(ci * tk, tk), tk)
            kaug_ref[rows, :dh] = k_ref[rows, :]
            kaug_ref[rows, dh:] = bias_columns(-head_column(rows, tk), 0, tk)
            vt_ref[:dh, rows] = v_ref[rows, :].astype(F32).T.astype(BF16)
            vt_ref[dh:, rows] = ones_row
            return carry
        lax.fori_loop(0, s_len // tk, build, 0)

    q_rows = pl.ds(pl.multiple_of(qi * tq, tq), tq)
    q_aug = jnp.concatenate([q_ref[...].astype(F32) * scale,
                             bias_columns(head_column(q_rows, tq), 3, tq).astype(F32)], axis=1)
    qt_ref[...] = q_aug.T.astype(BF16)

    acc_ref[...] = jnp.zeros_like(acc_ref)
    m_ref[...] = jnp.full_like(m_ref, -jnp.inf)

    st_refs, p_refs = (st0_ref, st1_ref), (p0_ref, p1_ref)

    gw = ATT_GROUP
    groups = [pl.ds(g * gw, gw) for g in range(tq // gw)]

    def scores(k_start, slot, diag_offset=None, lanes=None):
        for gl in ([lanes] if lanes is not None else groups):
            if diag_offset is not None and gl.start + gw <= diag_offset:
                st_refs[slot][:, gl] = jnp.full((tk, gw), -jnp.inf, F32)
                bmax_ref[slot, :, gl] = jnp.full((1, gw), -jnp.inf, F32)
                continue
            st = jnp.dot(kaug_ref[pl.ds(k_start, tk), :], qt_ref[:, gl], preferred_element_type=F32)
            if diag_offset is not None and gl.start < diag_offset + tk - 1:
                kr = lax.broadcasted_iota(jnp.int32, (tk, gw), 0) + diag_offset
                qc = lax.broadcasted_iota(jnp.int32, (tk, gw), 1) + gl.start
                st = jnp.where(kr <= qc, st, -jnp.inf)
            st_refs[slot][:, gl] = st
            bmax_ref[slot, :, gl] = jnp.max(st, axis=0, keepdims=True)

    def softmax(slot, lanes=None):
        for gl in ([lanes] if lanes is not None else groups):
            m_prev = m_ref[:, gl]
            m_new = jnp.maximum(m_prev, bmax_ref[slot, :, gl])
            m_ref[:, gl] = m_new
            p_refs[slot][:, gl] = jnp.exp2(st_refs[slot][:, gl] - m_new).astype(BF16)
            alpha_ref[slot, :, gl] = jnp.exp2(m_prev - m_new)

    def values(k_start, slot, lanes=None):
        for gl in ([lanes] if lanes is not None else groups):
            acc_ref[:, gl] = alpha_ref[slot, :, gl] * acc_ref[:, gl] + jnp.dot(
                vt_ref[:, pl.ds(k_start, tk)], p_refs[slot][:, gl], preferred_element_type=F32)

    n_diag = tq // tk
    assert n_diag == 2
    d0 = pl.multiple_of(qi * tq, tk)
    d1 = pl.multiple_of(qi * tq + tk, tk)
    n_full = qi * n_diag

    def full_start(i):
        return pl.multiple_of(jnp.minimum(i, jnp.maximum(n_full - 1, 0)) * tk, tk)

    scores(d0, 0, 0)
    scores(d1, 1, tk)
    softmax(0)
    scores(full_start(0), 0)
    softmax(1)
    values(d0, 0)

    def pair(j, carry):
        i0 = 2 * j
        prev = jnp.where(j == 0, d1, (i0 - 1) * tk)
        for gl in groups:
            scores(full_start(i0 + 1), 1, lanes=gl)
            softmax(0, lanes=gl)
            values(pl.multiple_of(prev, tk), 1, lanes=gl)
        for gl in groups:
            scores(full_start(i0 + 2), 0, lanes=gl)
            softmax(1, lanes=gl)
            values(pl.multiple_of(i0 * tk, tk), 0, lanes=gl)
        return carry

    lax.fori_loop(0, qi, pair, 0)
    last = jnp.where(qi == 0, d1, (n_full - 1) * tk)
    values(pl.multiple_of(last, tk), 1)

    o_ref[...] = (acc_ref[:dh, :] / acc_ref[dh:dh + 1, :]).T.astype(o_ref.dtype)


def _fox_attention(qkv, c, n_heads):
    s = qkv.shape[0]
    dh = FOX_HEAD_DIM
    tq = min(ATT_TQ, s)
    hh = n_heads
    return pl.pallas_call(
        _fox_attn_kernel,
        grid=(hh, s // tq),
        in_specs=[pl.BlockSpec((tq, dh), lambda h, i: (i, h)),
                  pl.BlockSpec((s, dh), lambda h, i: (0, hh + h)),
                  pl.BlockSpec((s, dh), lambda h, i: (0, 2 * hh + h)),
                  pl.BlockSpec((s, LANES), lambda h, i: (0, 0))],
        out_specs=pl.BlockSpec((tq, dh), lambda h, i: (i, h)),
        out_shape=jax.ShapeDtypeStruct((s, hh * dh), BF16),
        scratch_shapes=[pltpu.VMEM((s, 2 * dh), BF16),
                        pltpu.VMEM((dh + 16, s), BF16),
                        pltpu.VMEM((2 * dh, tq), BF16),
                        pltpu.VMEM((ATT_TK, tq), F32),
                        pltpu.VMEM((ATT_TK, tq), F32),
                        pltpu.VMEM((ATT_TK, tq), BF16),
                        pltpu.VMEM((ATT_TK, tq), BF16),
                        pltpu.VMEM((2, 1, tq), F32),
                        pltpu.VMEM((2, 1, tq), F32),
                        pltpu.VMEM((dh + 16, tq), F32),
                        pltpu.VMEM((1, tq), F32)],
        compiler_params=_params("arbitrary", "arbitrary"),
        name="fox_attention",
    )(qkv, qkv, qkv, c)


def _first_lane_eq(vals, target, lane):
    return jnp.min(jnp.where(vals == target, lane, LANES), axis=1, keepdims=True)


def _route_rows(x, w_ref, b_ref, carry_ref):
    tm = x.shape[0]
    neg = -jnp.inf
    logits = _dot_hi_lo(x, w_ref) + b_ref[...]
    lane = lax.broadcasted_iota(jnp.int32, (tm, LANES), 1)

    gl = jnp.where(lane < N_GROUPS, logits, neg)
    gmax = jnp.max(gl, axis=1, keepdims=True)
    gsum = jnp.sum(jnp.exp(gl - gmax), axis=1, keepdims=True)
    grp_p = 1.0 / gsum
    grp = _first_lane_eq(gl, gmax, lane)

    lo = N_GROUPS + grp * EXPERTS_PER_GROUP
    el = jnp.where((lane >= lo) & (lane < lo + EXPERTS_PER_GROUP), logits, neg)
    emax = jnp.max(el, axis=1, keepdims=True)
    esum = jnp.sum(jnp.exp(el - emax), axis=1, keepdims=True)
    idx1 = _first_lane_eq(el, emax, lane)
    el2 = jnp.where(lane == idx1, neg, el)
    emax2 = jnp.max(el2, axis=1, keepdims=True)
    idx2 = _first_lane_eq(el2, emax2, lane)
    p1 = 1.0 / esum
    p2 = jnp.exp(emax2 - emax) / esum
    psum = p1 + p2
    g1 = grp_p * (p1 / psum)
    g2 = grp_p * (p2 / psum)

    oh1 = lane == idx1
    oh2 = lane == idx2
    both = jnp.where(oh1 | oh2, 1.0, 0.0)
    before = jnp.dot(_tri(tm, strict=True), both.astype(BF16), preferred_element_type=F32)
    before = before + carry_ref[...]
    r1 = jnp.sum(jnp.where(oh1, before, 0.0), axis=1, keepdims=True)
    r2 = jnp.sum(jnp.where(oh2, before, 0.0), axis=1, keepdims=True)
    carry_ref[...] = carry_ref[...] + jnp.sum(both, axis=0, keepdims=True)

    e1 = (idx1 - N_GROUPS).astype(F32)
    e2 = (idx2 - N_GROUPS).astype(F32)
    out = jnp.where(lane == 0, e1, 0.0)
    out = jnp.where(lane == 1, e2, out)
    out = jnp.where(lane == 2, g1, out)
    out = jnp.where(lane == 3, g2, out)
    out = jnp.where(lane == 4, r1, out)
    return jnp.where(lane == 5, r2, out)


def _proj_ln_route_kernel(o_ref, w_ref, x_ref, g_ref, b_ref, wr_ref, br_ref,
                          y_ref, yp_ref, route_ref, cnt_ref, carry_ref):
    @pl.when(pl.program_id(0) == 0)
    def _():
        carry_ref[...] = jnp.zeros_like(carry_ref)

    mix = jnp.dot(o_ref[...], w_ref[...], preferred_element_type=F32)
    y = _layer_norm_rows(DEEPNORM_ALPHA * x_ref[...] + mix, g_ref[...], b_ref[...])
    y_ref[...] = y
    _store_rows_as_lines(yp_ref, 0, _pack_rows(y))
    route_ref[...] = _route_rows(y, wr_ref, br_ref, carry_ref)
    cnt_ref[...] = carry_ref[...]


def _proj_ln_route(o, w, x, g, b, w_group, b_group, w_expert, b_expert):
    s, d = x.shape
    kd = o.shape[1]
    tm = min(ROW_TILE, s)
    whl = _pack_hi_lo(jnp.concatenate([w_group.T, w_expert.T], axis=0))
    nb = N_GROUPS + N_EXPERTS
    bias = jnp.zeros((1, LANES), F32).at[0, :nb].set(jnp.concatenate([b_group, b_expert]))
    return pl.pallas_call(
        _proj_ln_route_kernel,
        grid=(s // tm,),
        in_specs=[pl.BlockSpec((tm, kd), lambda i: (i, 0)),
                  pl.BlockSpec((kd, d), lambda i: (0, 0)),
                  pl.BlockSpec((tm, d), lambda i: (i, 0)),
                  pl.BlockSpec((1, d), lambda i: (0, 0)),
                  pl.BlockSpec((1, d), lambda i: (0, 0)),
                  pl.BlockSpec((2 * LANES, d), lambda i: (0, 0)),
                  pl.BlockSpec((1, LANES), lambda i: (0, 0))],
        out_specs=[pl.BlockSpec((tm, d), lambda i: (i, 0)),
                   pl.BlockSpec((tm * (d // 2 // LANES), LANES), lambda i: (i, 0)),
                   pl.BlockSpec((tm, LANES), lambda i: (i, 0)),
                   pl.BlockSpec((1, LANES), lambda i: (0, 0))],
        out_shape=[jax.ShapeDtypeStruct((s, d), F32),
                   jax.ShapeDtypeStruct((s * (d // 2 // LANES), LANES), jnp.uint32),
                   jax.ShapeDtypeStruct((s, LANES), F32),
                   jax.ShapeDtypeStruct((1, LANES), F32)],
        scratch_shapes=[pltpu.VMEM((1, LANES), F32)],
        compiler_params=_params("arbitrary"),
        name="proj_ln_route",
    )(o, w, x, g.reshape(1, d), b.reshape(1, d), whl, bias)


def _dispatch_kernel(dest_ref, cnt_ref, pstart_ref, nv_ref, x_ref, xs_hbm, zero_ref, sem, zsem, *, nt):
    i = pl.program_id(0)
    tm = x_ref.shape[0] // nt
    bm = MOE_BLOCK
    n_exp = cnt_ref.shape[0]
    n_blocks = xs_hbm.shape[0] // (bm * nt)

    def lines(row, n_rows=1):
        return pl.ds(pl.multiple_of(row * nt, nt), n_rows * nt)

    def pad_copies(e, fn):
        cnt = cnt_ref[e]
        pad = (bm - cnt % bm) % bm
        pos = pstart_ref[e] + cnt
        size = bm // 2
        while size >= 1:
            @pl.when((pad & size) != 0)
            def _(pos=pos, size=size):
                fn(pltpu.make_async_copy(zero_ref.at[lines(0, size), :], xs_hbm.at[lines(pos, size), :], zsem))
            pos = pos + (pad & size)
            size //= 2

    def tail_copy(g):
        return pltpu.make_async_copy(zero_ref, xs_hbm.at[lines(g * bm, bm), :], zsem)

    def for_all_fills(fn):
        def per_expert(e, carry):
            pad_copies(e, fn)
            return carry
        lax.fori_loop(0, n_exp, per_expert, 0)

        def per_tail(g, carry):
            fn(tail_copy(g))
            return carry
        lax.fori_loop(nv_ref[0], n_blocks, per_tail, 0)

    @pl.when(i == 0)
    def _():
        zero_ref[...] = jnp.zeros_like(zero_ref)
        for_all_fills(lambda c: c.start())

    base = i * (TOP_K * tm)

    def body(r, carry):
        for k in range(TOP_K):
            dst = dest_ref[base + TOP_K * r + k]
            pltpu.make_async_copy(x_ref.at[lines(r), :], xs_hbm.at[lines(dst), :], sem).start(priority=k)
        return carry
    lax.fori_loop(0, tm, body, 0, unroll=8)

    @pl.when(i == 0)
    def _():
        for_all_fills(lambda c: c.wait())

    for _ in range(TOP_K):
        pltpu.make_async_copy(x_ref, xs_hbm.at[lines(0, tm), :], sem).wait()


def _dispatch(xp, n_rows, dest, cnt, pstart, n_valid, n_slots):
    nt = xp.shape[0] // n_rows
    tm = min(DISPATCH_TM, n_rows)
    grid_spec = pltpu.PrefetchScalarGridSpec(
        num_scalar_prefetch=4,
        grid=(n_rows // tm,),
        in_specs=[pl.BlockSpec((tm * nt, LANES), lambda i, *_: (i, 0))],
        out_specs=pl.BlockSpec(memory_space=pl.ANY),
        scratch_shapes=[pltpu.VMEM((MOE_BLOCK * nt, LANES), xp.dtype),
                        pltpu.SemaphoreType.DMA,
                        pltpu.SemaphoreType.DMA],
    )
    return pl.pallas_call(
        functools.partial(_dispatch_kernel, nt=nt),
        grid_spec=grid_spec,
        out_shape=jax.ShapeDtypeStruct((n_slots * nt, LANES), xp.dtype),
        compiler_params=_params("arbitrary"),
        name="moe_dispatch",
    )(dest, cnt, pstart, n_valid, xp)


def _expert_kernel(nblk_ref, bstart_ref, nv_ref, xs_hbm, wg_hbm, wu_hbm, wd_hbm, y_hbm,
                   xin_ref, yout_ref, wgf_ref, wuf_ref, wdf_ref, wgb_ref, wub_ref, wdb_ref,
                   in_sem, out_sem, w_sem, *, layer):
    e = pl.program_id(0)
    n_exp = pl.num_programs(0)
    n_valid = nv_ref[0]
    bm = MOE_BLOCK
    blk_lines = xin_ref.shape[1]
    nt = blk_lines // bm
    n_blocks = y_hbm.shape[0] // blk_lines
    ns = MOE_W_SLOTS

    def w_copies(ex, slot):
        copies = []
        for src, dst in ((wg_hbm, wgf_ref), (wu_hbm, wuf_ref), (wd_hbm, wdf_ref)):
            rows = src.shape[2] // MOE_W_PIECES
            for piece in range(MOE_W_PIECES):
                sl = pl.ds(piece * rows, rows)
                copies.append(pltpu.make_async_copy(src.at[layer, ex, sl], dst.at[slot, sl], w_sem.at[slot]))
        return copies

    def start_weights(ex):
        exc = jnp.minimum(ex, n_exp - 1)

        @pl.when((ex < n_exp) & (nblk_ref[exc] > 0))
        def _():
            for piece, c in enumerate(w_copies(exc, exc % ns)):
                c.start(priority=piece % 2)

    @pl.when(e == 0)
    def _():
        for ahead in range(ns - 1):
            start_weights(ahead)

    start_weights(e + ns - 1)

    def block_lines(g):
        return pl.ds(pl.multiple_of(g * blk_lines, blk_lines), blk_lines)

    def in_copy(g, slot):
        return pltpu.make_async_copy(xs_hbm.at[block_lines(g), :], xin_ref.at[slot], in_sem.at[slot])

    def out_copy(g, slot):
        return pltpu.make_async_copy(yout_ref.at[slot], y_hbm.at[block_lines(g), :], out_sem.at[slot])

    @pl.when((e == 0) & (n_valid > 0))
    def _():
        in_copy(0, 0).start(priority=0)

    nb = nblk_ref[e]
    g0 = bstart_ref[e]

    @pl.when(nb > 0)
    def _():
        w_slot = e % ns
        for c in w_copies(e, w_slot):
            c.wait()
        wgb_ref[...] = wgf_ref[w_slot].astype(BF16)
        wub_ref[...] = wuf_ref[w_slot].astype(BF16)
        wdb_ref[...] = wdf_ref[w_slot].astype(BF16)

        def block(j, carry):
            g = g0 + j
            slot = g % 2
            in_copy(g, slot).wait()

            @pl.when(g + 1 < n_valid)
            def _():
                in_copy(g + 1, 1 - slot).start(priority=0)

            @pl.when(g >= 2)
            def _():
                out_copy(g - 2, slot).wait()

            xb = _unpack_rows(_load_rows_from_lines(xin_ref.at[slot], 0, bm, nt)).astype(BF16)
            gate = jnp.dot(xb, wgb_ref[...], preferred_element_type=F32)
            up = jnp.dot(xb, wub_ref[...], preferred_element_type=F32)
            hid = (gate * jax.nn.sigmoid(gate) * up).astype(BF16)
            y = jnp.dot(hid, wdb_ref[...], preferred_element_type=F32)
            _store_rows_as_lines(yout_ref.at[slot], 0, _pack_rows(y))
            out_copy(g, slot).start(priority=1)
            return carry
        lax.fori_loop(0, nb, block, 0)

    @pl.when(e == pl.num_programs(0) - 1)
    def _():
        for back in (2, 1):
            @pl.when(n_valid >= back)
            def _(back=back):
                out_copy(n_valid - back, (n_valid - back) % 2).wait()

        yout_ref[0] = jnp.zeros(yout_ref.shape[1:], yout_ref.dtype)

        def fill(g, carry):
            out_copy(g, 0).start()
            return carry
        lax.fori_loop(n_valid, n_blocks, fill, 0)

        def drain(g, carry):
            out_copy(g, 0).wait()
            return carry
        lax.fori_loop(n_valid, n_blocks, drain, 0)


def _experts(xs, nblk, bstart, n_valid, w_gate, w_up, w_down, layer):
    _, n_exp, d, f = w_gate.shape
    bm = MOE_BLOCK
    nt = d // 2 // LANES
    ns = MOE_W_SLOTS
    grid_spec = pltpu.PrefetchScalarGridSpec(
        num_scalar_prefetch=3,
        grid=(n_exp,),
        in_specs=[pl.BlockSpec(memory_space=pl.ANY)] * 4,
        out_specs=pl.BlockSpec(memory_space=pl.ANY),
        scratch_shapes=[pltpu.VMEM((2, bm * nt, LANES), jnp.uint32),
                        pltpu.VMEM((2, bm * nt, LANES), jnp.uint32),
                        pltpu.VMEM((ns, d, f), F32),
                        pltpu.VMEM((ns, d, f), F32),
                        pltpu.VMEM((ns, f, d), F32),
                        pltpu.VMEM((d, f), BF16),
                        pltpu.VMEM((d, f), BF16),
                        pltpu.VMEM((f, d), BF16),
                        pltpu.SemaphoreType.DMA((2,)),
                        pltpu.SemaphoreType.DMA((2,)),
                        pltpu.SemaphoreType.DMA((ns,))],
    )
    return pl.pallas_call(
        functools.partial(_expert_kernel, layer=layer),
        grid_spec=grid_spec,
        out_shape=jax.ShapeDtypeStruct(xs.shape, jnp.uint32),
        compiler_params=_params("arbitrary"),
        name="moe_experts",
    )(nblk, bstart, n_valid, xs, w_gate, w_up, w_down)


def _combine_ln_kernel(dest_ref, y_hbm, x_ref, route_ref, g_ref, b_ref, o_ref, yg_ref, sem):
    i = pl.program_id(0)
    n = pl.num_programs(0)
    tm = x_ref.shape[0]
    nt = yg_ref.shape[1] // (TOP_K * tm)

    def lines(row, n_rows=1):
        return pl.ds(pl.multiple_of(row * nt, nt), n_rows * nt)

    def start_gather(step, slot):
        base = step * (TOP_K * tm)

        def body(r, carry):
            for k in range(TOP_K):
                src = dest_ref[base + TOP_K * r + k]
                pltpu.make_async_copy(y_hbm.at[lines(src), :], yg_ref.at[slot, lines(k * tm + r), :],
                                      sem.at[slot]).start(priority=k)
            return carry
        lax.fori_loop(0, tm, body, 0, unroll=8)

    def wait_gather(slot):
        pltpu.make_async_copy(y_hbm.at[lines(0, TOP_K * tm), :], yg_ref.at[slot], sem.at[slot]).wait()

    @pl.when(i == 0)
    def _():
        start_gather(0, 0)

    @pl.when(i + 1 < n)
    def _():
        start_gather(i + 1, (i + 1) % 2)

    slot = i % 2
    wait_gather(slot)
    route = route_ref[...]
    lane = lax.broadcasted_iota(jnp.int32, route.shape, 1)
    g1 = jnp.sum(jnp.where(lane == 2, route, 0.0), axis=1, keepdims=True)
    g2 = jnp.sum(jnp.where(lane == 3, route, 0.0), axis=1, keepdims=True)
    y1 = _unpack_rows(_load_rows_from_lines(yg_ref.at[slot], 0, tm, nt))
    y2 = _unpack_rows(_load_rows_from_lines(yg_ref.at[slot], tm * nt, tm, nt))
    z = DEEPNORM_ALPHA * x_ref[...] + (y1 * g1 + y2 * g2)
    o_ref[...] = _layer_norm_rows(z, g_ref[...], b_ref[...])


def _combine_ln(y, dest, x, route, g, b):
    t, d = x.shape
    tm = min(COMB_TM, t)
    grid_spec = pltpu.PrefetchScalarGridSpec(
        num_scalar_prefetch=1,
        grid=(t // tm,),
        in_specs=[pl.BlockSpec(memory_space=pl.ANY),
                  pl.BlockSpec((tm, d), lambda i, dest: (i, 0)),
                  pl.BlockSpec((tm, LANES), lambda i, dest: (i, 0)),
                  pl.BlockSpec((1, d), lambda i, dest: (0, 0)),
                  pl.BlockSpec((1, d), lambda i, dest: (0, 0))],
        out_specs=pl.BlockSpec((tm, d), lambda i, dest: (i, 0)),
        scratch_shapes=[pltpu.VMEM((2, TOP_K * tm * (d // 2 // LANES), LANES), y.dtype),
                        pltpu.SemaphoreType.DMA((2,))],
    )
    return pl.pallas_call(
        _combine_ln_kernel,
        grid_spec=grid_spec,
        out_shape=jax.ShapeDtypeStruct((t, d), F32),
        compiler_params=_params("arbitrary"),
        name="moe_combine_ln",
    )(dest, y, x, route, g.reshape(1, d), b.reshape(1, d))


def _moe_ffn_ln(x, xp, route, counts, w_gate, w_up, w_down, layer, ln_g, ln_b):
    t, d = x.shape
    bm = MOE_BLOCK
    eid = route[:, 0:TOP_K].astype(jnp.int32)
    rank = route[:, 4:4 + TOP_K].astype(jnp.int32)
    cnt = counts[0, N_GROUPS:N_GROUPS + N_EXPERTS].astype(jnp.int32)
    nblk = (cnt + bm - 1) // bm
    bend = jnp.cumsum(nblk)
    bstart = bend - nblk
    pstart = bstart * bm
    onehot = eid[:, :, None] == jnp.arange(N_EXPERTS, dtype=jnp.int32)
    dest = (jnp.sum(jnp.where(onehot, pstart, 0), axis=-1) + rank).reshape(-1)
    n_valid = bend[-1:]
    n_slots = (-(-t * TOP_K // bm) + N_EXPERTS) * bm
    xs = _dispatch(xp, t, dest, cnt, pstart, n_valid, n_slots)
    y = _experts(xs, nblk, bstart, n_valid, w_gate, w_up, w_down, layer)
    return _combine_ln(y, dest, x, route, ln_g, ln_b)


def _gla_log_alpha(x, wl_ref, wu_ref, b_ref):
    g_low = _dot_hi_lo(x, wl_ref)
    n = wu_ref.shape[1] // 2
    gh, gl = _split2(g_low)
    a = jnp.dot(gh, wu_ref[...], preferred_element_type=F32)
    c = jnp.dot(gl, wu_ref[:, :n], preferred_element_type=F32)
    return _log_sigmoid(a[:, :n] + a[:, n:] + c + b_ref[...]) / GLA_GATE_TAU


def _gla_kernel(q_ref, k_ref, v_ref, r_ref, x_ref, wl_ref, wu_ref, bg_ref, g_ref, o_ref, state_ref):
    @pl.when(pl.program_id(0) == 0)
    def _():
        state_ref[...] = jnp.zeros_like(state_ref)

    log_alpha = _gla_log_alpha(x_ref[...], wl_ref, wu_ref, bg_ref)

    rows = q_ref.shape[0]
    nh, dk, dv = state_ref.shape
    cs = GLA_CHUNK
    n_chunks = rows // cs
    rr = lax.broadcasted_iota(jnp.int32, (rows, rows), 0)
    cc = lax.broadcasted_iota(jnp.int32, (rows, rows), 1)
    causal = (rr // cs == cc // cs) & (rr >= cc)
    tri = jnp.where(causal, 1.0, 0.0).astype(BF16)
    lane_chunk = lax.broadcasted_iota(jnp.int32, (dk, rows), 1) // cs

    def chunk_row(a, i):
        return jnp.concatenate([jnp.broadcast_to(a[c * cs + i:c * cs + i + 1, :], (cs, a.shape[1]))
                                for c in range(n_chunks)], axis=0)

    heads = range(nh)
    kcs = [pl.ds(h * dk, dk) for h in heads]
    vcs = [pl.ds(h * dv, dv) for h in heads]
    bs = [_cumsum_rows(log_alpha[:, h * dk:(h + 1) * dk], tri) for h in heads]
    qs = [q_ref[:, kcs[h]].astype(F32) * (dk ** -0.5) for h in heads]
    ks = [k_ref[:, kcs[h]].astype(F32) for h in heads]
    vs = [v_ref[:, vcs[h]] for h in heads]
    b_mids = [chunk_row(bs[h], cs // 2 - 1) for h in heads]
    att = [jnp.where(causal, _dot_nt((qs[h] * jnp.exp(bs[h] - b_mids[h])).astype(BF16),
                                     (ks[h] * jnp.exp(b_mids[h] - bs[h])).astype(BF16)), 0.0) for h in heads]
    o_intra = [jnp.dot(att[h].astype(BF16), vs[h], preferred_element_type=F32) for h in heads]
    q_inter = [(qs[h] * jnp.exp(bs[h])).astype(BF16) for h in heads]
    k_end_t = [(ks[h] * jnp.exp(chunk_row(bs[h], cs - 1) - bs[h])).T for h in heads]
    b_t = [bs[h].T for h in heads]
    states = [state_ref[h] for h in heads]
    outs = [[] for _ in heads]
    for c in range(n_chunks):
        sl = slice(c * cs, (c + 1) * cs)
        for h in heads:
            outs[h].append(o_intra[h][sl] + jnp.dot(q_inter[h][sl], states[h].astype(BF16),
                                                    preferred_element_type=F32))
            kv = jnp.dot(jnp.where(lane_chunk == c, k_end_t[h], 0.0).astype(BF16), vs[h],
                         preferred_element_type=F32)
            decay = jnp.exp(b_t[h][:, (c + 1) * cs - 1:(c + 1) * cs])
            states[h] = decay * states[h] + kv
    for h in heads:
        state_ref[h] = states[h]
        o = jnp.concatenate(outs[h], axis=0)
        o = o * lax.rsqrt(jnp.mean(o * o, axis=-1, keepdims=True) + RMS_EPS) * g_ref[...]
        r = r_ref[:, vcs[h]].astype(F32)
        o_ref[:, vcs[h]] = (o * (r * jax.nn.sigmoid(r))).astype(o_ref.dtype)


def _gla(proj, x, w_low_t, w_gate_up, b_gate, norm_g):
    s, d = x.shape
    rank, dk_all = w_gate_up.shape
    nh = GLA_HEADS
    dk = dk_all // nh
    dv = norm_g.shape[0]
    dv_all = nh * dv
    assert 2 * dk_all == dv_all
    rows = min(GLA_ROWS, s)
    wl = _pack_hi_lo(w_low_t)
    wu_pad = jnp.zeros((LANES, dk_all), F32).at[:rank].set(w_gate_up)
    wu = jnp.concatenate(_split2(wu_pad), axis=1)
    return pl.pallas_call(
        _gla_kernel,
        grid=(s // rows,),
        in_specs=[pl.BlockSpec((rows, dk_all), lambda i: (i, 0)),
                  pl.BlockSpec((rows, dk_all), lambda i: (i, 1)),
                  pl.BlockSpec((rows, dv_all), lambda i: (i, 1)),
                  pl.BlockSpec((rows, dv_all), lambda i: (i, 2)),
                  pl.BlockSpec((rows, d), lambda i: (i, 0)),
                  pl.BlockSpec((2 * LANES, d), lambda i: (0, 0)),
                  pl.BlockSpec((LANES, 2 * dk_all), lambda i: (0, 0)),
                  pl.BlockSpec((1, dk_all), lambda i: (0, 0)),
                  pl.BlockSpec((1, dv), lambda i: (0, 0))],
        out_specs=pl.BlockSpec((rows, dv_all), lambda i: (i, 0)),
        out_shape=jax.ShapeDtypeStruct((s, dv_all), BF16),
        scratch_shapes=[pltpu.VMEM((nh, dk, dv), F32)],
        compiler_params=_params("arbitrary"),
        name="gla_chunks",
    )(proj, proj, proj, proj, x, wl, wu, b_gate.reshape(1, dk_all), norm_g.reshape(1, dv))


def kernel(x, fox_w_in, fox_b_f, fox_w_o, gla_w_in, gla_w_gate_up, gla_b_gate, gla_norm_g, gla_w_o,
           ln_mix_g, ln_mix_b, ln_ffn_g, ln_ffn_b, moe_w_group, moe_b_group, moe_w_expert,
           moe_b_expert, moe_w_gate, moe_w_up, moe_w_down):
    bsz, s, d = x.shape
    outs = []
    for bi in range(bsz):
        xt = x[bi]
        for i in range(DEPTH):
            j = i // 2
            if i % 2 == 0:
                w_in_t = fox_w_in[j].T
                qkv = _matmul_nt(xt, w_in_t, 3 * d, BF16)
                c = _fox_gates(xt, w_in_t[3 * d:], fox_b_f[j])
                o = _fox_attention(qkv, c, FOX_HEADS)
                w_o = fox_w_o[j]
            else:
                w_in_t = gla_w_in[j].T
                n_main = w_in_t.shape[0] - gla_w_gate_up.shape[1]
                proj = _matmul_nt(xt, w_in_t, n_main, BF16)
                o = _gla(proj, xt, w_in_t[n_main:], gla_w_gate_up[j], gla_b_gate[j], gla_norm_g[j])
                w_o = gla_w_o[j]
            xt, xp, route, counts = _proj_ln_route(o, w_o.astype(BF16), xt, ln_mix_g[i], ln_mix_b[i], moe_w_group[i],
                                                   moe_b_group[i], moe_w_expert[i], moe_b_expert[i])
            xt = _moe_ffn_ln(xt, xp, route, counts, moe_w_gate, moe_w_up, moe_w_down, i, ln_ffn_g[i], ln_ffn_b[i])
        outs.append(xt)
    return outs[0].reshape(1, s, d) if bsz == 1 else jnp.stack(outs, axis=0)
```
